```python
import math
import jax, jax.numpy as jnp
from jax import lax
import numpy as np

D_MODEL = 2048
BATCH = 4
SEQ = 4096
DEPTH = 2

HEAD_DIM = 128
N_HEADS_BRANCH = 4
BRANCH_WIDTH = N_HEADS_BRANCH * HEAD_DIM
N_BRANCHES = 4
MIX_WIDTH = N_BRANCHES * BRANCH_WIDTH
MLA_Q_RANK = 384
MLA_KV_RANK = 256
MLA_NOPE = 128
MLA_ROPE = 64
MLA_V = 128
ROPE_THETA = 10000.0
DILATED_PATTERNS = ((128, 1), (512, 4), (2048, 16))
IDX_HEADS = 16
IDX_DIM = 64
TOPK_MAX = 256
DIFF_QK_DIM = HEAD_DIM // 2
REL_BUCKETS = 32
REL_MAX_DIST = 2048
N_BIAS_HEADS = 3 * N_HEADS_BRANCH
Q_BLOCK = 128
NORM_EPS = 1e-6
NEG_INF = -1e30

IN_SPLITS = (
    ("a_cq", MLA_Q_RANK), ("a_ckv", MLA_KV_RANK), ("a_krope", MLA_ROPE),
    ("b_q", BRANCH_WIDTH), ("b_k", BRANCH_WIDTH), ("b_v", BRANCH_WIDTH),
    ("c_q", BRANCH_WIDTH), ("c_k", BRANCH_WIDTH), ("c_v", BRANCH_WIDTH),
    ("c_qidx", IDX_HEADS * IDX_DIM), ("c_kidx", IDX_DIM), ("c_widx", IDX_HEADS),
    ("d_q", BRANCH_WIDTH), ("d_k", BRANCH_WIDTH), ("d_v", BRANCH_WIDTH),
    ("gate", MIX_WIDTH),
)
IN_WIDTH = 3 * MLA_ROPE + MLA_Q_RANK + MLA_KV_RANK - 2 * MLA_ROPE + 9 * BRANCH_WIDTH + IDX_HEADS * IDX_DIM + IDX_DIM + IDX_HEADS + MIX_WIDTH

kernel_name = "hybrid_mla_dilated_dsa_diff_trunk"


def rmsnorm(x, g):
    xf = x.astype(jnp.float32)
    y = xf * lax.rsqrt(jnp.mean(xf * xf, axis=-1, keepdims=True) + NORM_EPS)
    return (y * g.astype(jnp.float32)).astype(x.dtype)


def split_columns(proj):
    parts, start = {}, 0
    for name, width in IN_SPLITS:
        parts[name] = proj[..., start:start + width]
        start += width
    return parts


def t5_bucket(n):
    max_exact = REL_BUCKETS // 2
    nf = jnp.maximum(n, max_exact).astype(jnp.float32)
    large = max_exact + (jnp.log(nf / max_exact) / math.log(REL_MAX_DIST / max_exact)
                         * (REL_BUCKETS - max_exact)).astype(jnp.int32)
    large = jnp.minimum(large, REL_BUCKETS - 1)
    return jnp.where(n < max_exact, n, large)


def apply_rope(x, pos):
    half = x.shape[-1] // 2
    inv = ROPE_THETA ** (-jnp.arange(half, dtype=jnp.float32) / half)
    ang = pos.astype(jnp.float32)[:, None] * inv[None, :]
    cos = jnp.cos(ang)[None, :, None, :].astype(x.dtype)
    sin = jnp.sin(ang)[None, :, None, :].astype(x.dtype)
    x1, x2 = x[..., :half], x[..., half:]
    return jnp.concatenate([x1 * cos - x2 * sin, x2 * cos + x1 * sin], axis=-1)


def to_blocks(t):
    b, s = t.shape[:2]
    return jnp.swapaxes(t.reshape((b, s // Q_BLOCK, Q_BLOCK) + t.shape[2:]), 0, 1)


def from_blocks(t):
    t = jnp.swapaxes(t, 0, 1)
    return t.reshape((t.shape[0], t.shape[1] * t.shape[2]) + t.shape[3:])


def causal_block_attention(q, k, v, scale):
    s = q.shape[1]
    kpos = jnp.arange(s)

    def one_block(args):
        q_blk, bi = args
        qpos = bi * Q_BLOCK + jnp.arange(Q_BLOCK)
        logits = jnp.einsum('bqhd,bshd->bqhs', q_blk, k).astype(jnp.float32) * scale
        logits = jnp.where((kpos[None, :] <= qpos[:, None])[None, :, None, :], logits, NEG_INF)
        p = jax.nn.softmax(logits, axis=-1).astype(v.dtype)
        return jnp.einsum('bqhs,bshd->bqhd', p, v)

    out = lax.map(one_block, (to_blocks(q), jnp.arange(s // Q_BLOCK)))
    return from_blocks(out)


def dilated_mixture_attention(q, k, v, bias_tab):
    s, dh = q.shape[1], q.shape[-1]
    scale = dh ** -0.5
    patterns = []
    for window, dil in DILATED_PATTERNS:
        offs = dil * jnp.arange(window // dil + 1)
        bias = bias_tab[t5_bucket(offs)].T
        patterns.append((offs, bias))

    def one_block(args):
        q_blk, bi = args
        qpos = bi * Q_BLOCK + jnp.arange(Q_BLOCK)
        lses, outs = [], []
        for offs, bias in patterns:
            kidx = qpos[:, None] - offs[None, :]
            valid = kidx >= 0
            kidx = jnp.maximum(kidx, 0)
            k_g = k[:, kidx]
            v_g = v[:, kidx]
            logits = jnp.einsum('bqhd,bqnhd->bqhn', q_blk, k_g).astype(jnp.float32) * scale
            logits = logits + bias.astype(jnp.float32)[None, None]
            logits = jnp.where(valid[None, :, None, :], logits, NEG_INF)
            lse = jax.nn.logsumexp(logits, axis=-1)
            p = jnp.exp(logits - lse[..., None]).astype(v.dtype)
            outs.append(jnp.einsum('bqhn,bqnhd->bqhd', p, v_g))
            lses.append(lse)
        wts = jax.nn.softmax(jnp.stack(lses, axis=-1), axis=-1).astype(v.dtype)
        return sum(wts[..., g, None] * outs[g] for g in range(len(outs)))

    out = lax.map(one_block, (to_blocks(q), jnp.arange(s // Q_BLOCK)))
    return from_blocks(out)


def dsa_attention(q, k, v, q_idx, k_idx, w_idx, bias_tab):
    s, dh = q.shape[1], q.shape[-1]
    scale = dh ** -0.5
    n_sel = min(TOPK_MAX, s // 4)
    kpos = jnp.arange(s)

    def one_block(args):
        q_blk, qi_blk, wi_blk, bi = args
        qpos = bi * Q_BLOCK + jnp.arange(Q_BLOCK)
        causal = kpos[None, :] <= qpos[:, None]
        dots = jnp.einsum('bqhd,bsd->bqhs', qi_blk, k_idx).astype(jnp.float32) * IDX_DIM ** -0.5
        score = jnp.einsum('bqhs,bqh->bqs', jax.nn.relu(dots),
                           wi_blk.astype(jnp.float32) * IDX_HEADS ** -0.5)
        score = jnp.where(causal[None], score, NEG_INF)
        _, sel = lax.top_k(score, n_sel)
        valid = sel <= qpos[None, :, None]
        k_sel = jax.vmap(lambda kb, ib: kb[ib])(k, sel)
        v_sel = jax.vmap(lambda vb, ib: vb[ib])(v, sel)
        bias = bias_tab[t5_bucket(jnp.maximum(qpos[None, :, None] - sel, 0))]
        logits = jnp.einsum('bqhd,bqkhd->bqhk', q_blk, k_sel).astype(jnp.float32) * scale
        logits = logits + jnp.transpose(bias, (0, 1, 3, 2)).astype(jnp.float32)
        logits = jnp.where(valid[:, :, None, :], logits, NEG_INF)
        p = jax.nn.softmax(logits, axis=-1).astype(v.dtype)
        return jnp.einsum('bqhk,bqkhd->bqhd', p, v_sel)

    out = lax.map(one_block, (to_blocks(q), to_blocks(q_idx), to_blocks(w_idx),
                              jnp.arange(s // Q_BLOCK)))
    return from_blocks(out)


def diff_attention(q1, q2, k1, k2, v, lam, bias_tab):
    s, dq = q1.shape[1], q1.shape[-1]
    scale = dq ** -0.5
    kpos = jnp.arange(s)

    def one_block(args):
        q1b, q2b, bi = args
        qpos = bi * Q_BLOCK + jnp.arange(Q_BLOCK)
        causal = (kpos[None, :] <= qpos[:, None])[None, :, None, :]
        rel = jnp.maximum(qpos[:, None] - kpos[None, :], 0)
        bias = jnp.transpose(bias_tab[t5_bucket(rel)], (0, 2, 1))[None].astype(jnp.float32)
        l1 = jnp.einsum('bqhd,bshd->bqhs', q1b, k1).astype(jnp.float32) * scale + bias
        l2 = jnp.einsum('bqhd,bshd->bqhs', q2b, k2).astype(jnp.float32) * scale + bias
        p1 = jax.nn.softmax(jnp.where(causal, l1, NEG_INF), axis=-1)
        p2 = jax.nn.softmax(jnp.where(causal, l2, NEG_INF), axis=-1)
        a = (p1 - lam * p2).astype(v.dtype)
        return jnp.einsum('bqhs,bshd->bqhd', a, v)

    out = lax.map(one_block, (to_blocks(q1), to_blocks(q2), jnp.arange(s // Q_BLOCK)))
    return from_blocks(out)


def setup_inputs(seed: int = 0) -> dict:
    key = jax.random.key(seed)
    ks = jax.random.split(key, 20)
    f32 = jnp.float32

    def nrm(k, shape, s):
        return jax.random.normal(k, shape, f32) * s

    def gain(k, shape):
        return 1.0 + 0.05 * jax.random.normal(k, shape, f32)

    return {
        "x": nrm(ks[0], (BATCH, SEQ, D_MODEL), 1.0),
        "c": nrm(ks[1], (BATCH, D_MODEL), 1.0),
        "w_ada": nrm(ks[2], (DEPTH, D_MODEL, 3 * D_MODEL), 0.5 * D_MODEL ** -0.5),
        "b_ada": nrm(ks[3], (DEPTH, 3 * D_MODEL), 0.02),
        "g_pre": gain(ks[4], (DEPTH, D_MODEL)),
        "g_post": gain(ks[5], (DEPTH, D_MODEL)),
        "w_in": nrm(ks[6], (DEPTH, D_MODEL, IN_WIDTH), D_MODEL ** -0.5),
        "g_q_a": gain(ks[7], (DEPTH, MLA_Q_RANK)),
        "w_uq_a": nrm(ks[8], (DEPTH, MLA_Q_RANK, N_HEADS_BRANCH * (MLA_NOPE + MLA_ROPE)), MLA_Q_RANK ** -0.5),
        "g_kv_a": gain(ks[9], (DEPTH, MLA_KV_RANK)),
        "w_ukv_a": nrm(ks[10], (DEPTH, MLA_KV_RANK, N_HEADS_BRANCH * (MLA_NOPE + MLA_V)), MLA_KV_RANK ** -0.5),
        "lam_q1": nrm(ks[11], (DEPTH, DIFF_QK_DIM), 0.1),
        "lam_k1": nrm(ks[12], (DEPTH, DIFF_QK_DIM), 0.1),
        "lam_q2": nrm(ks[13], (DEPTH, DIFF_QK_DIM), 0.1),
        "lam_k2": nrm(ks[14], (DEPTH, DIFF_QK_DIM), 0.1),
        "g_sub_d": gain(ks[15], (DEPTH, HEAD_DIM)),
        "w_out": nrm(ks[16], (DEPTH, MIX_WIDTH, D_MODEL), MIX_WIDTH ** -0.5),
        "rel_bias": nrm(ks[17], (REL_BUCKETS, N_BIAS_HEADS), 0.5),
    }


def reference(x, c, w_ada, b_ada, g_pre, g_post, w_in, g_q_a, w_uq_a, g_kv_a, w_ukv_a,
              lam_q1, lam_k1, lam_q2, lam_k2, g_sub_d, w_out, rel_bias):
    b, s, _ = x.shape
    nh = N_HEADS_BRANCH
    pos = jnp.arange(s)

    def heads(t):
        return t.reshape(b, s, nh, -1)

    for li in range(DEPTH):
        mod = jax.nn.silu(c) @ w_ada[li] + b_ada[li]
        shift, scale, gate = jnp.split(mod, 3, axis=-1)
        h = rmsnorm(x, g_pre[li]) * (1.0 + scale[:, None, :]) + shift[:, None, :]
        p = split_columns(h @ w_in[li])

        q_a = (rmsnorm(p["a_cq"], g_q_a[li]) @ w_uq_a[li]).reshape(b, s, nh, MLA_NOPE + MLA_ROPE)
        kv_a = (rmsnorm(p["a_ckv"], g_kv_a[li]) @ w_ukv_a[li]).reshape(b, s, nh, MLA_NOPE + MLA_V)
        q_a = jnp.concatenate([q_a[..., :MLA_NOPE], apply_rope(q_a[..., MLA_NOPE:], pos)], axis=-1)
        k_rope = apply_rope(p["a_krope"][:, :, None, :], pos)
        k_a = jnp.concatenate([kv_a[..., :MLA_NOPE],
                               jnp.broadcast_to(k_rope, (b, s, nh, MLA_ROPE))], axis=-1)
        out_a = causal_block_attention(q_a, k_a, kv_a[..., MLA_NOPE:], (MLA_NOPE + MLA_ROPE) ** -0.5)

        out_b = dilated_mixture_attention(heads(p["b_q"]), heads(p["b_k"]), heads(p["b_v"]),
                                          rel_bias[:, 0:nh])

        out_c = dsa_attention(heads(p["c_q"]), heads(p["c_k"]), heads(p["c_v"]),
                              p["c_qidx"].reshape(b, s, IDX_HEADS, IDX_DIM), p["c_kidx"], p["c_widx"],
                              rel_bias[:, nh:2 * nh])

        lam_init = 0.8 - 0.6 * math.exp(-0.3 * li)
        lam = (jnp.exp(jnp.sum(lam_q1[li] * lam_k1[li]).astype(jnp.float32))
               - jnp.exp(jnp.sum(lam_q2[li] * lam_k2[li]).astype(jnp.float32)) + lam_init)
        q_d, k_d = heads(p["d_q"]), heads(p["d_k"])
        out_d = diff_attention(q_d[..., :DIFF_QK_DIM], q_d[..., DIFF_QK_DIM:],
                               k_d[..., :DIFF_QK_DIM], k_d[..., DIFF_QK_DIM:],
                               heads(p["d_v"]), lam, rel_bias[:, 2 * nh:3 * nh])
        out_d = rmsnorm(out_d, g_sub_d[li]) * (1.0 - lam_init)

        mixed = jnp.concatenate([o.reshape(b, s, BRANCH_WIDTH) for o in (out_a, out_b, out_c, out_d)],
                                axis=-1) * jax.nn.silu(p["gate"])
        y = mixed @ w_out[li]
        x = x + gate[:, None, :] * rmsnorm(y, g_post[li])
    return x
```

```python
import functools
import math

import numpy as np
import jax
import jax.numpy as jnp
from jax import lax
from jax.experimental import pallas as pl
from jax.experimental.pallas import tpu as pltpu

F32 = jnp.float32
BF16 = jnp.bfloat16

HEAD_DIM = 128
N_HEADS = 4
BRANCH_WIDTH = N_HEADS * HEAD_DIM
MIX_WIDTH = 4 * BRANCH_WIDTH
MLA_Q_RANK = 384
MLA_KV_RANK = 256
MLA_NOPE = 128
MLA_ROPE = 64
ROPE_THETA = 10000.0
DILATED_PATTERNS = ((128, 1), (512, 4), (2048, 16))
IDX_HEADS = 16
IDX_DIM = 64
TOPK_MAX = 256
REL_BUCKETS = 32
REL_MAX_DIST = 2048
NORM_EPS = 1e-6
NEG = -1e30

OFF_QIDX = 0
OFF_A = 1024
OFF_KIDX = OFF_A + 896
OFF_B = 2048
OFF_C = 3584
OFF_D = 5120
OFF_GATE = 6656
OFF_WIDX = 8704
PROJ_WIDTH = 8832

ATT_TILE = 256
VMEM_LIMIT = 56 * 1024 * 1024


def _cparams(*sem):
    return pltpu.CompilerParams(dimension_semantics=sem, vmem_limit_bytes=VMEM_LIMIT)


def _ada_kernel(c_ref, w_ref, b_ref, o_ref):
    c = c_ref[...]
    a = c * jax.nn.sigmoid(c)
    o_ref[...] = jnp.dot(a, w_ref[...], preferred_element_type=F32,
                         precision=lax.Precision.HIGHEST) + b_ref[...]


def _ada_mod(c, w_ada, b_ada):
    depth, d, n = w_ada.shape
    b = c.shape[0]
    tn = 768
    return pl.pallas_call(
        _ada_kernel,
        out_shape=jax.ShapeDtypeStruct((depth, b, n), F32),
        grid=(depth, n // tn),
        in_specs=[pl.BlockSpec((b, d), lambda l, j: (0, 0)),
                  pl.BlockSpec((None, d, tn), lambda l, j: (l, 0, j)),
                  pl.BlockSpec((None, 1, tn), lambda l, j: (l, 0, j))],
        out_specs=pl.BlockSpec((None, b, tn), lambda l, j: (l, 0, j)),
        compiler_params=_cparams("arbitrary", "arbitrary"),
        name="ada_mod",
    )(c, w_ada, b_ada.reshape(depth, 1, n))


def _prenorm_kernel(x_ref, g_ref, shift_ref, scale_ref, o_ref):
    x = x_ref[...]
    y = x * lax.rsqrt(jnp.mean(x * x, axis=-1, keepdims=True) + NORM_EPS) * g_ref[...]
    o_ref[...] = (y * (1.0 + scale_ref[...]) + shift_ref[...]).astype(BF16)


def _prenorm(x, g_pre, mod3):
    b, s, d = x.shape
    tm = min(512, s)
    return pl.pallas_call(
        _prenorm_kernel,
        out_shape=jax.ShapeDtypeStruct((b, s, d), BF16),
        grid=(b, s // tm),
        in_specs=[pl.BlockSpec((None, tm, d), lambda bi, i: (bi, i, 0)),
                  pl.BlockSpec((1, d), lambda bi, i: (0, 0)),
                  pl.BlockSpec((None, 1, d), lambda bi, i: (bi, 0, 0)),
                  pl.BlockSpec((None, 1, d), lambda bi, i: (bi, 0, 1))],
        out_specs=pl.BlockSpec((None, tm, d), lambda bi, i: (bi, i, 0)),
        compiler_params=_cparams("arbitrary", "arbitrary"),
        name="prenorm",
    )(x, g_pre.reshape(1, d), mod3, mod3)


def _matmul_kernel(a_ref, w_ref, o_ref):
    o_ref[...] = jnp.dot(a_ref[...], w_ref[...], preferred_element_type=F32).astype(o_ref.dtype)


def _in_proj(h2d, w):
    m, k = h2d.shape
    n = w.shape[1]
    tm = min(512, m)
    tn = n // 3
    return pl.pallas_call(
        _matmul_kernel,
        out_shape=jax.ShapeDtypeStruct((m, n), BF16),
        grid=(n // tn, m // tm),
        in_specs=[pl.BlockSpec((tm, k), lambda j, i: (i, 0)),
                  pl.BlockSpec((k, tn), lambda j, i: (0, j))],
        out_specs=pl.BlockSpec((tm, tn), lambda j, i: (i, j)),
        compiler_params=_cparams("arbitrary", "arbitrary"),
        name="in_proj",
    )(h2d, w)


def _rms(x, g):
    return x * lax.rsqrt(jnp.mean(x * x, axis=-1, keepdims=True) + NORM_EPS) * g


def _mla_prep_kernel(p_ref, cos_ref, sin_ref, gq_ref, gkv_ref, wq_ref, wkv_ref, q_ref, k_ref, v_ref):
    cos = cos_ref[...]
    sin = sin_ref[...]
    cq = _rms(p_ref[:, 0:MLA_Q_RANK].astype(F32), gq_ref[...]).astype(BF16)
    ckv = _rms(p_ref[:, MLA_Q_RANK:MLA_Q_RANK + MLA_KV_RANK].astype(F32), gkv_ref[...]).astype(BF16)
    q = jnp.dot(cq, wq_ref[...], preferred_element_type=F32)
    kv = jnp.dot(ckv, wkv_ref[...], preferred_element_type=F32)
    k_rope = (p_ref[:, 640:768].astype(F32) * cos + p_ref[:, 768:896].astype(F32) * sin).astype(BF16)
    for h in range(N_HEADS):
        qh = q[:, h * 384:(h + 1) * 384]
        q_ref[:, h * 256:h * 256 + 128] = qh[:, 0:128].astype(BF16)
        q_ref[:, h * 256 + 128:(h + 1) * 256] = (qh[:, 128:256] * cos + qh[:, 256:384] * sin).astype(BF16)
        k_ref[:, h * 256:h * 256 + 128] = kv[:, h * 256:h * 256 + 128].astype(BF16)
        k_ref[:, h * 256 + 128:(h + 1) * 256] = k_rope
        v_ref[:, h * 128:(h + 1) * 128] = kv[:, h * 256 + 128:(h + 1) * 256].astype(BF16)


def _mla_prep(proj, cos_t, sin_t, g_q, g_kv, wq, wkv):
    b, s, _ = proj.shape
    tm = min(512, s)
    const = lambda bi, i: (0, 0)
    return pl.pallas_call(
        _mla_prep_kernel,
        out_shape=(jax.ShapeDtypeStruct((b, s, N_HEADS * 256), BF16),
                   jax.ShapeDtypeStruct((b, s, N_HEADS * 256), BF16),
                   jax.ShapeDtypeStruct((b, s, BRANCH_WIDTH), BF16)),
        grid=(b, s // tm),
        in_specs=[pl.BlockSpec((None, tm, 1024), lambda bi, i: (bi, i, OFF_A // 1024)),
                  pl.BlockSpec((tm, 128), lambda bi, i: (i, 0)),
                  pl.BlockSpec((tm, 128), lambda bi, i: (i, 0)),
                  pl.BlockSpec((1, MLA_Q_RANK), const),
                  pl.BlockSpec((1, MLA_KV_RANK), const),
                  pl.BlockSpec(wq.shape, const),
                  pl.BlockSpec(wkv.shape, const)],
        out_specs=(pl.BlockSpec((None, tm, N_HEADS * 256), lambda bi, i: (bi, i, 0)),
                   pl.BlockSpec((None, tm, N_HEADS * 256), lambda bi, i: (bi, i, 0)),
                   pl.BlockSpec((None, tm, BRANCH_WIDTH), lambda bi, i: (bi, i, 0))),
        compiler_params=_cparams("arbitrary", "arbitrary"),
        name="mla_prep",
    )(proj, cos_t, sin_t, g_q.reshape(1, -1), g_kv.reshape(1, -1), wq, wkv)


def _qk(q, k):
    return lax.dot_general(q, k, (((1,), (1,)), ((), ())), preferred_element_type=F32)


def _online(carry, s, v):
    m, l, acc = carry
    m_new = jnp.maximum(m, jnp.max(s, axis=-1, keepdims=True))
    alpha = jnp.exp(m - m_new)
    p = jnp.exp(s - m_new)
    l = alpha * l + jnp.sum(p, axis=-1, keepdims=True)
    acc = alpha * acc + jnp.dot(p.astype(BF16), v, preferred_element_type=F32)
    return m_new, l, acc


def _init_carry(tq, dv):
    return (jnp.full((tq, 1), NEG, F32), jnp.zeros((tq, 1), F32), jnp.zeros((tq, dv), F32))


def _silu(g):
    return g * jax.nn.sigmoid(g)


def _tile(ref, kj, t, c0, width):
    return ref[pl.ds(pl.multiple_of(kj * t, t), t), c0:c0 + width]


def _attn_a_kernel(q_ref, k_ref, v_ref, g_ref, o_ref, *, t):
    i = pl.program_id(1)
    row = lax.broadcasted_iota(jnp.int32, (t, t), 0)
    col = lax.broadcasted_iota(jnp.int32, (t, t), 1)
    causal = col <= row
    for h in range(N_HEADS):
        q = q_ref[:, h * 256:(h + 1) * 256]

        def logits(kj, q=q, h=h):
            return _qk(q, _tile(k_ref, kj, t, h * 256, 256))

        def off_diag(kj, carry, h=h, logits=logits):
            return _online(carry, logits(kj), _tile(v_ref, kj, t, h * 128, 128))

        carry = lax.fori_loop(0, i, off_diag, _init_carry(t, HEAD_DIM))
        s = jnp.where(causal, logits(i), NEG)
        _, l, acc = _online(carry, s, _tile(v_ref, i, t, h * 128, 128))
        gate = g_ref[:, h * 128:(h + 1) * 128].astype(F32)
        o_ref[:, h * 128:(h + 1) * 128] = (acc / l * _silu(gate)).astype(BF16)


def _attn_band_kernel(q_ref, k_ref, v_ref, g_ref, bank_ref, o_ref, *, t, nd):
    i = pl.program_id(1)
    lo = jnp.maximum(i - (nd - 1), 0)
    for h in range(N_HEADS):
        q = q_ref[:, h * 128:(h + 1) * 128]

        def step(kj, carry, q=q, h=h):
            s = _qk(q, _tile(k_ref, kj, t, h * 128, 128)) + bank_ref[h, i - kj]
            return _online(carry, s, _tile(v_ref, kj, t, h * 128, 128))

        _, l, acc = lax.fori_loop(lo, i + 1, step, _init_carry(t, HEAD_DIM))
        gate = g_ref[:, h * 128:(h + 1) * 128].astype(F32)
        o_ref[:, h * 128:(h + 1) * 128] = (acc / l * _silu(gate)).astype(BF16)


def _attn_sel_kernel(far_ref, q_ref, k_ref, v_ref, g_ref, bank_ref, sel_ref, o_ref, *, t, nd):
    i = pl.program_id(1)
    n_far = jnp.maximum(i - (nd - 1), 0)
    for h in range(N_HEADS):
        q = q_ref[:, h * 128:(h + 1) * 128]
        far = far_ref[h]

        def logits(kj, q=q, h=h):
            sel = sel_ref[:, pl.ds(pl.multiple_of(kj * t, t), t)].astype(F32)
            return _qk(q, _tile(k_ref, kj, t, h * 128, 128)) + sel

        def far_step(kj, carry, h=h, far=far, logits=logits):
            return _online(carry, logits(kj) + far, _tile(v_ref, kj, t, h * 128, 128))

        def near_step(kj, carry, h=h, logits=logits):
            return _online(carry, logits(kj) + bank_ref[h, i - kj], _tile(v_ref, kj, t, h * 128, 128))

        carry = lax.fori_loop(0, n_far, far_step, _init_carry(t, HEAD_DIM))
        _, l, acc = lax.fori_loop(n_far, i + 1, near_step, carry)
        gate = g_ref[:, h * 128:(h + 1) * 128].astype(F32)
        o_ref[:, h * 128:(h + 1) * 128] = (acc / l * _silu(gate)).astype(BF16)


def _attn_diff_kernel(far_ref, q_ref, k_ref, v_ref, g_ref, bank_ref, lam_ref, gsub_ref, o_ref,
                      *, t, nd, lam_init):
    i = pl.program_id(1)
    n_far = jnp.maximum(i - (nd - 1), 0)
    lam_v = lam_ref[...]
    lam = (jnp.exp(jnp.sum(lam_v[0:1] * lam_v[1:2], axis=-1, keepdims=True))
           - jnp.exp(jnp.sum(lam_v[2:3] * lam_v[3:4], axis=-1, keepdims=True)) + lam_init)
    lane = lax.broadcasted_iota(jnp.int32, (t, HEAD_DIM), 1)
    for h in range(N_HEADS):
        q = q_ref[:, h * 128:(h + 1) * 128]
        q1 = jnp.where(lane < 64, q, jnp.zeros_like(q))
        q2 = jnp.where(lane >= 64, q, jnp.zeros_like(q))
        far = far_ref[h]

        def step(kj, carry, bias, q1=q1, q2=q2, h=h):
            c1, c2 = carry
            k = _tile(k_ref, kj, t, h * 128, 128)
            v = _tile(v_ref, kj, t, h * 128, 128)
            return _online(c1, _qk(q1, k) + bias, v), _online(c2, _qk(q2, k) + bias, v)

        def far_step(kj, carry, far=far, step=step):
            return step(kj, carry, far)

        def near_step(kj, carry, h=h, step=step):
            return step(kj, carry, bank_ref[h, i - kj])

        init = (_init_carry(t, HEAD_DIM), _init_carry(t, HEAD_DIM))
        carry = lax.fori_loop(0, n_far, far_step, init)
        (_, l1, a1), (_, l2, a2) = lax.fori_loop(n_far, i + 1, near_step, carry)
        o = a1 / l1 - lam * (a2 / l2)
        o = _rms(o, gsub_ref[...]) * (1.0 - lam_init)
        gate = g_ref[:, h * 128:(h + 1) * 128].astype(F32)
        o_ref[:, h * 128:(h + 1) * 128] = (o * _silu(gate)).astype(BF16)


def _attention(kind, q_arr, k_arr, v_arr, proj, q_blk, gate_blk, extra_in=(), extra_specs=(),
               smem_in=(), **kw):
    b, s, _ = proj.shape
    t = min(ATT_TILE, s)
    dk = 256 if kind == "a" else 128
    qw, kw_ = N_HEADS * dk, N_HEADS * dk
    k_blk = 0 if kind == "a" else q_blk + 1
    v_blk = 0 if kind == "a" else q_blk + 2
    body = {"a": _attn_a_kernel, "band": _attn_band_kernel, "sel": _attn_sel_kernel,
            "diff": _attn_diff_kernel}[kind]
    in_specs = [pl.BlockSpec(memory_space=pltpu.SMEM) for _ in smem_in]
    in_specs += [pl.BlockSpec((None, t, qw), lambda bi, i: (bi, i, q_blk)),
                 pl.BlockSpec((None, s, kw_), lambda bi, i: (bi, 0, k_blk)),
                 pl.BlockSpec((None, s, BRANCH_WIDTH), lambda bi, i: (bi, 0, v_blk)),
                 pl.BlockSpec((None, t, BRANCH_WIDTH), lambda bi, i: (bi, i, gate_blk))]
    in_specs += list(extra_specs)
    return pl.pallas_call(
        functools.partial(body, t=t, **kw),
        out_shape=jax.ShapeDtypeStruct((b, s, BRANCH_WIDTH), BF16),
        grid=(b, s // t),
        in_specs=in_specs,
        out_specs=pl.BlockSpec((None, t, BRANCH_WIDTH), lambda bi, i: (bi, i, 0)),
        compiler_params=_cparams("arbitrary", "arbitrary"),
        name="attn_" + kind,
    )(*smem_in, q_arr, k_arr, v_arr, proj, *extra_in)


def _select_kernel(qi_ref, ki_ref, wi_ref, o_ref, key_ref, *, tq, kc, n_sel):
    i = pl.program_id(1)
    s_len = o_ref.shape[1]
    n_ch = (i * tq + tq + kc - 1) // kc
    lane = lax.broadcasted_iota(jnp.int32, (tq, 128), 1)
    w = wi_ref[...].astype(F32)
    q_pairs = []
    for j in range(IDX_HEADS // 2):
        q2 = qi_ref[:, j * 128:(j + 1) * 128]
        q_pairs.append((jnp.where(lane < 64, q2, jnp.zeros_like(q2)),
                        jnp.where(lane >= 64, q2, jnp.zeros_like(q2))))
    qpos = i * tq + lax.broadcasted_iota(jnp.int32, (tq, kc), 0)
    kiota = lax.broadcasted_iota(jnp.int32, (tq, kc), 1)

    def score_chunk(c, _):
        k = ki_ref[pl.ds(pl.multiple_of(c * kc, kc), kc), :]
        acc = jnp.zeros((tq, kc), F32)
        for j in range(IDX_HEADS // 2):
            for half in range(2):
                hh = 2 * j + half
                acc = acc + jnp.maximum(_qk(q_pairs[j][half], k), 0.0) * w[:, hh:hh + 1]
        acc = jnp.where(c * kc + kiota <= qpos, acc + 0.0, NEG)
        bits = pltpu.bitcast(acc, jnp.int32)
        key_ref[:, pl.ds(pl.multiple_of(c * kc, kc), kc)] = bits ^ ((bits >> 31) & jnp.int32(0x7FFFFFFF))
        return 0

    lax.fori_loop(0, n_ch, score_chunk, 0)

    def count_ge(cand):
        def body(c, part):
            keys = key_ref[:, pl.ds(pl.multiple_of(c * kc, kc), kc)]
            for u in range(kc // 128):
                part = part + jnp.where(keys[:, u * 128:(u + 1) * 128] >= cand, 1, 0)
            return part
        part = lax.fori_loop(0, n_ch, body, jnp.zeros((tq, 128), jnp.int32))
        return jnp.sum(part, axis=-1, keepdims=True)

    int_min = jnp.int32(-2 ** 31)
    thr = jnp.where(count_ge(jnp.zeros((tq, 1), jnp.int32)) >= n_sel, jnp.int32(0), int_min)

    def bit_step(b, thr):
        cand = thr + (jnp.int32(1) << (30 - b))
        return jnp.where(count_ge(cand) >= n_sel, cand, thr)

    thr = lax.fori_loop(0, 31, bit_step, thr)

    def emit(c, _):
        sl = pl.ds(pl.multiple_of(c * kc, kc), kc)
        o_ref[:, sl] = jnp.where(key_ref[:, sl] >= thr, 0.0, NEG).astype(BF16)
        return 0

    lax.fori_loop(0, n_ch, emit, 0)

    def blank(c, _):
        o_ref[:, pl.ds(pl.multiple_of(c * kc, kc), kc)] = jnp.full((tq, kc), NEG, BF16)
        return 0

    lax.fori_loop(n_ch, s_len // kc, blank, 0)


def _select(proj):
    b, s, _ = proj.shape
    tq = min(128, s)
    kc = min(512, s)
    n_sel = min(TOPK_MAX, s // 4)
    return pl.pallas_call(
        functools.partial(_select_kernel, tq=tq, kc=kc, n_sel=n_sel),
        out_shape=jax.ShapeDtypeStruct((b, s, s), BF16),
        grid=(b, s // tq),
        in_specs=[pl.BlockSpec((None, tq, 1024), lambda bi, i: (bi, i, OFF_QIDX // 1024)),
                  pl.BlockSpec((None, s, 128), lambda bi, i: (bi, 0, OFF_KIDX // 128)),
                  pl.BlockSpec((None, tq, 128), lambda bi, i: (bi, i, OFF_WIDX // 128))],
        out_specs=pl.BlockSpec((None, tq, s), lambda bi, i: (bi, i, 0)),
        scratch_shapes=[pltpu.VMEM((tq, s), jnp.int32)],
        compiler_params=_cparams("arbitrary", "arbitrary"),
        name="idx_select",
    )(proj, proj, proj)


def _out_kernel(a_ref, b_ref, c_ref, d_ref, w_ref, x_ref, gate_ref, g_ref, o_ref):
    y = jnp.dot(a_ref[...], w_ref[0:512, :], preferred_element_type=F32)
    y += jnp.dot(b_ref[...], w_ref[512:1024, :], preferred_element_type=F32)
    y += jnp.dot(c_ref[...], w_ref[1024:1536, :], preferred_element_type=F32)
    y += jnp.dot(d_ref[...], w_ref[1536:2048, :], preferred_element_type=F32)
    o_ref[...] = x_ref[...] + gate_ref[...] * _rms(y, g_ref[...])


def _out_proj(outs, w_out, x, mod3, g_post):
    b, s, d = x.shape
    tm = min(512, s)
    mix = lambda bi, i: (bi, i, 0)
    return pl.pallas_call(
        _out_kernel,
        out_shape=jax.ShapeDtypeStruct((b, s, d), F32),
        grid=(b, s // tm),
        in_specs=[pl.BlockSpec((None, tm, BRANCH_WIDTH), mix)] * 4
        + [pl.BlockSpec(w_out.shape, lambda bi, i: (0, 0)),
           pl.BlockSpec((None, tm, d), mix),
           pl.BlockSpec((None, 1, d), lambda bi, i: (bi, 0, 2)),
           pl.BlockSpec((1, d), lambda bi, i: (0, 0))],
        out_specs=pl.BlockSpec((None, tm, d), mix),
        compiler_params=_cparams("arbitrary", "arbitrary"),
        name="out_proj",
    )(*outs, w_out, x, mod3, g_post.reshape(1, d))


def _bucket_np(n):
    max_exact = REL_BUCKETS // 2
    nf = np.maximum(n, max_exact).astype(np.float32)
    large = max_exact + (np.log(nf / np.float32(max_exact)) / np.float32(math.log(REL_MAX_DIST / max_exact))
                         * np.float32(REL_BUCKETS - max_exact)).astype(np.int32)
    return np.where(n < max_exact, n, np.minimum(large, REL_BUCKETS - 1)).astype(np.int32)


def _toeplitz_offsets(nd, t):
    d = np.arange(nd)[:, None, None]
    return t * d + np.arange(t)[None, :, None] - np.arange(t)[None, None, :]


def _near_tiles(t, nq):
    last = int(np.argmax(_bucket_np(np.arange(2 * REL_MAX_DIST)) == REL_BUCKETS - 1))
    return min(nq, (last + t - 1) // t + 1)


def _bias_bank(tab, nd, t):
    off = _toeplitz_offsets(nd, t)
    vals = tab[_bucket_np(np.maximum(off, 0))]
    vals = jnp.where((off >= 0)[..., None], vals, NEG)
    return jnp.transpose(vals, (3, 0, 1, 2))


def _band_bank(tab, nd, t):
    off = _toeplitz_offsets(nd, t)
    mult = np.zeros(off.shape, np.float32)
    for window, dil in DILATED_PATTERNS:
        mult += ((off >= 0) & (off <= window) & (off % dil == 0)).astype(np.float32)
    logm = np.log(np.maximum(mult, 1.0)).astype(np.float32)
    vals = tab[_bucket_np(np.maximum(off, 0))] + logm[..., None]
    vals = jnp.where((mult > 0)[..., None], vals, NEG)
    return jnp.transpose(vals, (3, 0, 1, 2))


def _rope_tables(s):
    half = MLA_ROPE // 2
    inv = ROPE_THETA ** (-jnp.arange(half, dtype=F32) / half)
    ang = jnp.arange(s, dtype=F32)[:, None] * inv[None, :]
    z = jnp.zeros((s, 128 - MLA_ROPE), F32)
    cos, sin = jnp.cos(ang), jnp.sin(ang)
    return jnp.concatenate([cos, cos, z], axis=-1), jnp.concatenate([sin, sin, z], axis=-1)


def _rot_cols(w):
    half = w.shape[-1] // 2
    return jnp.concatenate([-w[..., half:], w[..., :half]], axis=-1)


def _layout_w_in(w):
    d = w.shape[0]
    names = (("a_cq", 384), ("a_ckv", 256), ("a_krope", 64), ("b_q", 512), ("b_k", 512), ("b_v", 512),
             ("c_q", 512), ("c_k", 512), ("c_v", 512), ("c_qidx", 1024), ("c_kidx", 64), ("c_widx", 16),
             ("d_q", 512), ("d_k", 512), ("d_v", 512), ("gate", 2048))
    p, start = {}, 0
    for name, width in names:
        p[name] = w[:, start:start + width]
        start += width
    z = lambda n: jnp.zeros((d, n), w.dtype)
    cols = [p["c_qidx"] * IDX_DIM ** -0.5,
            p["a_cq"], p["a_ckv"], p["a_krope"], z(64), _rot_cols(p["a_krope"]), z(64), p["c_kidx"], p["c_kidx"],
            p["b_q"] * HEAD_DIM ** -0.5, p["b_k"], p["b_v"],
            p["c_q"] * HEAD_DIM ** -0.5, p["c_k"], p["c_v"],
            p["d_q"] * (HEAD_DIM // 2) ** -0.5, p["d_k"], p["d_v"],
            p["gate"],
            p["c_widx"] * IDX_HEADS ** -0.5, z(112)]
    out = jnp.concatenate(cols, axis=1).astype(BF16)
    assert out.shape[1] == PROJ_WIDTH
    return out


def _layout_w_uq(w):
    r = w.shape[0]
    w = w.reshape(r, N_HEADS, MLA_NOPE + MLA_ROPE) * (MLA_NOPE + MLA_ROPE) ** -0.5
    z = jnp.zeros((r, N_HEADS, 128 - MLA_ROPE), w.dtype)
    rope = w[..., MLA_NOPE:]
    return jnp.concatenate([w[..., :MLA_NOPE], rope, z, _rot_cols(rope), z], axis=-1).reshape(r, -1).astype(BF16)


def kernel(x, c, w_ada, b_ada, g_pre, g_post, w_in, g_q_a, w_uq_a, g_kv_a, w_ukv_a,
           lam_q1, lam_k1, lam_q2, lam_k2, g_sub_d, w_out, rel_bias):
    b, s, d = x.shape
    depth = w_ada.shape[0]
    t = min(ATT_TILE, s)
    nq = s // t
    nd_bias = _near_tiles(t, nq)
    nd_band = min(nq, DILATED_PATTERNS[-1][0] // t + 1)

    cos_t, sin_t = _rope_tables(s)
    bank_b = _band_bank(rel_bias[:, 0:N_HEADS], nd_band, t)
    bank_c = _bias_bank(rel_bias[:, N_HEADS:2 * N_HEADS], nd_bias, t)
    bank_d = _bias_bank(rel_bias[:, 2 * N_HEADS:3 * N_HEADS], nd_bias, t)
    far_c = rel_bias[REL_BUCKETS - 1, N_HEADS:2 * N_HEADS]
    far_d = rel_bias[REL_BUCKETS - 1, 2 * N_HEADS:3 * N_HEADS]
    bank_spec = lambda nd: pl.BlockSpec((N_HEADS, nd, t, t), lambda bi, i: (0, 0, 0, 0))

    mod = _ada_mod(c, w_ada, b_ada)
    for li in range(depth):
        mod3 = mod[li].reshape(b, 1, 3 * d)
        h = _prenorm(x, g_pre[li], mod3)
        proj = _in_proj(h.reshape(b * s, d), _layout_w_in(w_in[li])).reshape(b, s, PROJ_WIDTH)

        q_a, k_a, v_a = _mla_prep(proj, cos_t, sin_t, g_q_a[li], g_kv_a[li],
                                  _layout_w_uq(w_uq_a[li]), w_ukv_a[li].astype(BF16))
        gate0 = OFF_GATE // BRANCH_WIDTH
        out_a = _attention("a", q_a, k_a, v_a, proj, 0, gate0)
        out_b = _attention("band", proj, proj, proj, proj, OFF_B // BRANCH_WIDTH, gate0 + 1,
                           extra_in=(bank_b,), extra_specs=(bank_spec(nd_band),), nd=nd_band)
        sel = _select(proj)
        out_c = _attention("sel", proj, proj, proj, proj, OFF_C // BRANCH_WIDTH, gate0 + 2,
                           extra_in=(bank_c, sel),
                           extra_specs=(bank_spec(nd_bias), pl.BlockSpec((None, t, s), lambda bi, i: (bi, i, 0))),
                           smem_in=(far_c,), nd=nd_bias)
        lam_init = 0.8 - 0.6 * math.exp(-0.3 * li)
        lam_vecs = jnp.stack([lam_q1[li], lam_k1[li], lam_q2[li], lam_k2[li]])
        out_d = _attention("diff", proj, proj, proj, proj, OFF_D // BRANCH_WIDTH, gate0 + 3,
                           extra_in=(bank_d, lam_vecs, g_sub_d[li].reshape(1, HEAD_DIM)),
                           extra_specs=(bank_spec(nd_bias),
                                        pl.BlockSpec(lam_vecs.shape, lambda bi, i: (0, 0)),
                                        pl.BlockSpec((1, HEAD_DIM), lambda bi, i: (0, 0))),
                           smem_in=(far_d,), nd=nd_bias, lam_init=lam_init)
        x = _out_proj((out_a, out_b, out_c, out_d), w_out[li].astype(BF16), x, mod3, g_post[li])
    return x
```

```python
import functools
import math

import numpy as np
import jax
import jax.numpy as jnp
from jax import lax
from jax.experimental import pallas as pl
from jax.experimental.pallas import tpu as pltpu

F32 = jnp.float32
BF16 = jnp.bfloat16

HEAD_DIM = 128
N_HEADS = 4
BRANCH_WIDTH = N_HEADS * HEAD_DIM
MIX_WIDTH = 4 * BRANCH_WIDTH
MLA_Q_RANK = 384
MLA_KV_RANK = 256
MLA_NOPE = 128
MLA_ROPE = 64
ROPE_THETA = 10000.0
DILATED_PATTERNS = ((128, 1), (512, 4), (2048, 16))
IDX_HEADS = 16
IDX_DIM = 64
TOPK_MAX = 256
REL_BUCKETS = 32
REL_MAX_DIST = 2048
NORM_EPS = 1e-6
NEG = -1e30
LOG2E = math.log2(math.e)

OFF_QIDX = 0
OFF_A = 1024
OFF_KIDX = OFF_A + 896
OFF_B = 2048
OFF_C = 3584
OFF_D = 5120
OFF_GATE = 6656
OFF_WIDX = 8704
PROJ_WIDTH = 8832

ATT_TILE = 512
VMEM_LIMIT = 56 * 1024 * 1024


def _cparams(*sem):
    return pltpu.CompilerParams(dimension_semantics=sem, vmem_limit_bytes=VMEM_LIMIT)


def _ada_kernel(c_ref, w_ref, b_ref, o_ref):
    c = c_ref[...]
    a = c * jax.nn.sigmoid(c)
    o_ref[...] = jnp.dot(a, w_ref[...], preferred_element_type=F32,
                         precision=lax.Precision.HIGHEST) + b_ref[...]


def _ada_mod(c, w_ada, b_ada):
    depth, d, n = w_ada.shape
    b = c.shape[0]
    tn = 768
    return pl.pallas_call(
        _ada_kernel,
        out_shape=jax.ShapeDtypeStruct((depth, b, n), F32),
        grid=(depth, n // tn),
        in_specs=[pl.BlockSpec((b, d), lambda l, j: (0, 0)),
                  pl.BlockSpec((None, d, tn), lambda l, j: (l, 0, j)),
                  pl.BlockSpec((None, 1, tn), lambda l, j: (l, 0, j))],
        out_specs=pl.BlockSpec((None, b, tn), lambda l, j: (l, 0, j)),
        compiler_params=_cparams("arbitrary", "arbitrary"),
        name="ada_mod",
    )(c, w_ada, b_ada.reshape(depth, 1, n))


def _prenorm_kernel(x_ref, g_ref, shift_ref, scale_ref, o_ref):
    x = x_ref[...]
    y = x * lax.rsqrt(jnp.mean(x * x, axis=-1, keepdims=True) + NORM_EPS) * g_ref[...]
    o_ref[...] = (y * (1.0 + scale_ref[...]) + shift_ref[...]).astype(BF16)


def _prenorm(x, g_pre, mod3):
    b, s, d = x.shape
    tm = min(512, s)
    return pl.pallas_call(
        _prenorm_kernel,
        out_shape=jax.ShapeDtypeStruct((b, s, d), BF16),
        grid=(b, s // tm),
        in_specs=[pl.BlockSpec((None, tm, d), lambda bi, i: (bi, i, 0)),
                  pl.BlockSpec((1, d), lambda bi, i: (0, 0)),
                  pl.BlockSpec((None, 1, d), lambda bi, i: (bi, 0, 0)),
                  pl.BlockSpec((None, 1, d), lambda bi, i: (bi, 0, 1))],
        out_specs=pl.BlockSpec((None, tm, d), lambda bi, i: (bi, i, 0)),
        compiler_params=_cparams("arbitrary", "arbitrary"),
        name="prenorm",
    )(x, g_pre.reshape(1, d), mod3, mod3)


def _matmul_kernel(a_ref, w_ref, o_ref):
    o_ref[...] = jnp.dot(a_ref[...], w_ref[...], preferred_element_type=F32).astype(o_ref.dtype)


def _in_proj(h2d, w):
    m, k = h2d.shape
    n = w.shape[1]
    tm = min(512, m)
    tn = n // 3
    return pl.pallas_call(
        _matmul_kernel,
        out_shape=jax.ShapeDtypeStruct((m, n), BF16),
        grid=(n // tn, m // tm),
        in_specs=[pl.BlockSpec((tm, k), lambda j, i: (i, 0)),
                  pl.BlockSpec((k, tn), lambda j, i: (0, j))],
        out_specs=pl.BlockSpec((tm, tn), lambda j, i: (i, j)),
        compiler_params=_cparams("arbitrary", "arbitrary"),
        name="in_proj",
    )(h2d, w)


def _rms(x, g):
    return x * lax.rsqrt(jnp.mean(x * x, axis=-1, keepdims=True) + NORM_EPS) * g


def _mla_prep_kernel(p_ref, cos_ref, sin_ref, gq_ref, gkv_ref, wq_ref, wkv_ref, q_ref, k_ref, v_ref):
    cos = cos_ref[...]
    sin = sin_ref[...]
    cq = _rms(p_ref[:, 0:MLA_Q_RANK].astype(F32), gq_ref[...]).astype(BF16)
    ckv = _rms(p_ref[:, MLA_Q_RANK:MLA_Q_RANK + MLA_KV_RANK].astype(F32), gkv_ref[...]).astype(BF16)
    q = jnp.dot(cq, wq_ref[...], preferred_element_type=F32)
    kv = jnp.dot(ckv, wkv_ref[...], preferred_element_type=F32)
    k_rope = (p_ref[:, 640:768].astype(F32) * cos + p_ref[:, 768:896].astype(F32) * sin).astype(BF16)
    for h in range(N_HEADS):
        qh = q[:, h * 384:(h + 1) * 384]
        q_ref[:, h * 256:h * 256 + 128] = qh[:, 0:128].astype(BF16)
        q_ref[:, h * 256 + 128:(h + 1) * 256] = (qh[:, 128:256] * cos + qh[:, 256:384] * sin).astype(BF16)
        k_ref[:, h * 256:h * 256 + 128] = kv[:, h * 256:h * 256 + 128].astype(BF16)
        k_ref[:, h * 256 + 128:(h + 1) * 256] = k_rope
        v_ref[:, h * 128:(h + 1) * 128] = kv[:, h * 256 + 128:(h + 1) * 256].astype(BF16)


def _mla_prep(proj, cos_t, sin_t, g_q, g_kv, wq, wkv):
    b, s, _ = proj.shape
    tm = min(512, s)
    const = lambda bi, i: (0, 0)
    return pl.pallas_call(
        _mla_prep_kernel,
        out_shape=(jax.ShapeDtypeStruct((b, s, N_HEADS * 256), BF16),
                   jax.ShapeDtypeStruct((b, s, N_HEADS * 256), BF16),
                   jax.ShapeDtypeStruct((b, s, BRANCH_WIDTH), BF16)),
        grid=(b, s // tm),
        in_specs=[pl.BlockSpec((None, tm, 1024), lambda bi, i: (bi, i, OFF_A // 1024)),
                  pl.BlockSpec((tm, 128), lambda bi, i: (i, 0)),
                  pl.BlockSpec((tm, 128), lambda bi, i: (i, 0)),
                  pl.BlockSpec((1, MLA_Q_RANK), const),
                  pl.BlockSpec((1, MLA_KV_RANK), const),
                  pl.BlockSpec(wq.shape, const),
                  pl.BlockSpec(wkv.shape, const)],
        out_specs=(pl.BlockSpec((None, tm, N_HEADS * 256), lambda bi, i: (bi, i, 0)),
                   pl.BlockSpec((None, tm, N_HEADS * 256), lambda bi, i: (bi, i, 0)),
                   pl.BlockSpec((None, tm, BRANCH_WIDTH), lambda bi, i: (bi, i, 0))),
        compiler_params=_cparams("arbitrary", "arbitrary"),
        name="mla_prep",
    )(proj, cos_t, sin_t, g_q.reshape(1, -1), g_kv.reshape(1, -1), wq, wkv)


def _bucket_np(n):
    max_exact = REL_BUCKETS // 2
    nf = np.maximum(n, max_exact).astype(np.float32)
    large = max_exact + (np.log(nf / np.float32(max_exact)) / np.float32(math.log(REL_MAX_DIST / max_exact))
                         * np.float32(REL_BUCKETS - max_exact)).astype(np.int32)
    return np.where(n < max_exact, n, np.minimum(large, REL_BUCKETS - 1)).astype(np.int32)


def _bucket_starts():
    buckets = _bucket_np(np.arange(2 * REL_MAX_DIST))
    return [int(np.argmax(buckets >= b)) for b in range(REL_BUCKETS)]


def _bank_kernel(tab_ref, o_ref, *, t, d_min, band):
    d = pl.program_id(0) + d_min
    dist = (t * d + lax.broadcasted_iota(jnp.int32, (t, t), 0) - lax.broadcasted_iota(jnp.int32, (t, t), 1))
    starts = _bucket_starts()
    ge = [dist >= starts[b] for b in range(1, REL_BUCKETS)]
    if band:
        mult = jnp.zeros((t, t), jnp.int32)
        for window, dil in DILATED_PATTERNS:
            mult += jnp.where((dist >= 0) & (dist <= window) & ((dist & (dil - 1)) == 0), 1, 0)
        logm = jnp.where(mult == 3, math.log(3.0), jnp.where(mult == 2, math.log(2.0), 0.0))
        keep = mult > 0
    else:
        logm = 0.0
        keep = dist >= 0
    for h in range(o_ref.shape[0]):
        val = jnp.full((t, t), tab_ref[0, h], F32)
        for b in range(1, REL_BUCKETS):
            val = jnp.where(ge[b - 1], tab_ref[b, h], val)
        o_ref[h] = jnp.where(keep, (val + logm) * LOG2E, NEG)


def _bank(tab, n_tables, t, d_min, band):
    nh = tab.shape[1]
    return pl.pallas_call(
        functools.partial(_bank_kernel, t=t, d_min=d_min, band=band),
        out_shape=jax.ShapeDtypeStruct((nh, n_tables, t, t), F32),
        grid=(n_tables,),
        in_specs=[pl.BlockSpec(memory_space=pltpu.SMEM)],
        out_specs=pl.BlockSpec((nh, None, t, t), lambda j: (0, j, 0, 0)),
        compiler_params=_cparams("arbitrary"),
        name="bank_band" if band else "bank_bias",
    )(tab)


def _first_far_diagonal(t):
    last = _bucket_starts()[REL_BUCKETS - 1]
    return -(-(last + t - 1) // t)


def _qk(q, k):
    return lax.dot_general(q, k, (((1,), (1,)), ((), ())), preferred_element_type=F32)


def _flash_init(m_ref, acc_ref):
    m_ref[...] = jnp.full(m_ref.shape, NEG, F32)
    acc_ref[...] = jnp.zeros(acc_ref.shape, F32)


def _flash_update(slot, s, v, m_ref, acc_ref):
    m_prev = m_ref[slot]
    m_new = jnp.maximum(m_prev, jnp.max(s, axis=-1, keepdims=True))
    alpha = jnp.exp2(m_prev - m_new)
    p = jnp.concatenate([jnp.exp2(s[:, j * 128:(j + 1) * 128] - m_new) for j in range(s.shape[1] // 128)],
                        axis=1).astype(BF16)
    v_ones = jnp.concatenate([v, jnp.ones_like(v)], axis=1)
    acc_ref[slot] = (jnp.concatenate([alpha, alpha], axis=1) * acc_ref[slot]
                     + jnp.dot(p, v_ones, preferred_element_type=F32))
    m_ref[slot] = m_new


def _flash_result(slot, acc_ref):
    acc = acc_ref[slot]
    return acc[:, :HEAD_DIM] / acc[:, HEAD_DIM:]


def _silu(g):
    return g * jax.nn.sigmoid(g)


def _tile(ref, kj, t, c0, width):
    return ref[pl.ds(pl.multiple_of(kj * t, t), t), c0:c0 + width]


def _bank_tile(bank_ref, h, i, kj):
    base = 2 * (i - kj) + 1
    top = jnp.concatenate([bank_ref[h, base], bank_ref[h, base - 1]], axis=1)
    bot = jnp.concatenate([bank_ref[h, base + 1], bank_ref[h, base]], axis=1)
    return jnp.concatenate([top, bot], axis=0)


def _emit(o_ref, g_ref, h, o):
    gate = g_ref[:, h * 128:(h + 1) * 128].astype(F32)
    o_ref[:, h * 128:(h + 1) * 128] = (o * _silu(gate)).astype(BF16)


def _walk(lo, hi, step, *args):
    def body(kj, carry):
        step(kj, *args)
        return carry
    lax.fori_loop(lo, hi, body, 0)


def _attn_a_kernel(q_ref, k_ref, v_ref, g_ref, o_ref, m_ref, acc_ref, *, t):
    i = pl.program_id(1)
    _flash_init(m_ref, acc_ref)

    def step(kj, masked):
        s_all = [_qk(q_ref[:, h * 256:(h + 1) * 256], _tile(k_ref, kj, t, h * 256, 256))
                 for h in range(N_HEADS)]
        for h in range(N_HEADS):
            s = s_all[h]
            if masked:
                causal = (lax.broadcasted_iota(jnp.int32, (t, t), 1)
                          <= lax.broadcasted_iota(jnp.int32, (t, t), 0))
                s = jnp.where(causal, s, NEG)
            _flash_update(h, s, _tile(v_ref, kj, t, h * 128, 128), m_ref, acc_ref)

    _walk(0, i, step, False)
    step(i, True)
    for h in range(N_HEADS):
        _emit(o_ref, g_ref, h, _flash_result(h, acc_ref))


def _attn_band_kernel(q_ref, k_ref, v_ref, g_ref, bank_ref, o_ref, m_ref, acc_ref, *, t, near):
    i = pl.program_id(1)
    _flash_init(m_ref, acc_ref)

    def step(kj):
        s_all = [_qk(q_ref[:, h * 128:(h + 1) * 128], _tile(k_ref, kj, t, h * 128, 128))
                 for h in range(N_HEADS)]
        for h in range(N_HEADS):
            s = s_all[h] + _bank_tile(bank_ref, h, i, kj)
            _flash_update(h, s, _tile(v_ref, kj, t, h * 128, 128), m_ref, acc_ref)

    _walk(jnp.maximum(i - (near - 1), 0), i + 1, step)
    for h in range(N_HEADS):
        _emit(o_ref, g_ref, h, _flash_result(h, acc_ref))


def _attn_sel_kernel(far_ref, q_ref, k_ref, v_ref, g_ref, bank_ref, sel_ref, o_ref, m_ref, acc_ref,
                     *, t, near):
    i = pl.program_id(1)
    n_far = jnp.maximum(i - (near - 1), 0)
    _flash_init(m_ref, acc_ref)

    def step(kj, is_near):
        s_all = [_qk(q_ref[:, h * 128:(h + 1) * 128], _tile(k_ref, kj, t, h * 128, 128))
                 for h in range(N_HEADS)]
        sel = sel_ref[:, pl.ds(pl.multiple_of(kj * t, t), t)].astype(F32)
        for h in range(N_HEADS):
            bias = _bank_tile(bank_ref, h, i, kj) if is_near else far_ref[h] * LOG2E
            _flash_update(h, s_all[h] + (sel + bias), _tile(v_ref, kj, t, h * 128, 128), m_ref, acc_ref)

    _walk(0, n_far, step, False)
    _walk(n_far, i + 1, step, True)
    for h in range(N_HEADS):
        _emit(o_ref, g_ref, h, _flash_result(h, acc_ref))


def _attn_diff_kernel(far_ref, q_ref, k_ref, v_ref, g_ref, bank_ref, lam_ref, gsub_ref, o_ref,
                      q1_ref, q2_ref, m_ref, acc_ref, *, t, near, lam_init):
    i = pl.program_id(1)
    n_far = jnp.maximum(i - (near - 1), 0)
    _flash_init(m_ref, acc_ref)
    lane = lax.broadcasted_iota(jnp.int32, q_ref.shape, 1) & (HEAD_DIM - 1)
    q = q_ref[...]
    q1_ref[...] = jnp.where(lane < HEAD_DIM // 2, q, jnp.zeros_like(q))
    q2_ref[...] = jnp.where(lane >= HEAD_DIM // 2, q, jnp.zeros_like(q))

    def step(kj, is_near):
        s_all = []
        for h in range(N_HEADS):
            k = _tile(k_ref, kj, t, h * 128, 128)
            s_all.append(_qk(q1_ref[:, h * 128:(h + 1) * 128], k))
            s_all.append(_qk(q2_ref[:, h * 128:(h + 1) * 128], k))
        for h in range(N_HEADS):
            v = _tile(v_ref, kj, t, h * 128, 128)
            bias = _bank_tile(bank_ref, h, i, kj) if is_near else far_ref[h] * LOG2E
            _flash_update(2 * h, s_all[2 * h] + bias, v, m_ref, acc_ref)
            _flash_update(2 * h + 1, s_all[2 * h + 1] + bias, v, m_ref, acc_ref)

    _walk(0, n_far, step, False)
    _walk(n_far, i + 1, step, True)
    lam_v = lam_ref[...]
    lam = (jnp.exp(jnp.sum(lam_v[0:1] * lam_v[1:2], axis=-1, keepdims=True))
           - jnp.exp(jnp.sum(lam_v[2:3] * lam_v[3:4], axis=-1, keepdims=True)) + lam_init)
    for h in range(N_HEADS):
        o = _flash_result(2 * h, acc_ref) - lam * _flash_result(2 * h + 1, acc_ref)
        _emit(o_ref, g_ref, h, _rms(o, gsub_ref[...]) * (1.0 - lam_init))


def _attention(kind, q_arr, k_arr, v_arr, proj, q_blk, gate_blk, extra_in=(), extra_specs=(),
               smem_in=(), **kw):
    b, s, _ = proj.shape
    t = min(ATT_TILE, s)
    dk = 256 if kind == "a" else 128
    qw = N_HEADS * dk
    k_blk = 0 if kind == "a" else q_blk + 1
    v_blk = 0 if kind == "a" else q_blk + 2
    body = {"a": _attn_a_kernel, "band": _attn_band_kernel, "sel": _attn_sel_kernel,
            "diff": _attn_diff_kernel}[kind]
    slots = 2 * N_HEADS if kind == "diff" else N_HEADS
    scratch = [pltpu.VMEM((slots, t, HEAD_DIM), F32), pltpu.VMEM((slots, t, 2 * HEAD_DIM), F32)]
    if kind == "diff":
        scratch = [pltpu.VMEM((t, qw), BF16), pltpu.VMEM((t, qw), BF16)] + scratch
    in_specs = [pl.BlockSpec(memory_space=pltpu.SMEM) for _ in smem_in]
    in_specs += [pl.BlockSpec((None, t, qw), lambda bi, i: (bi, i, q_blk)),
                 pl.BlockSpec((None, s, qw), lambda bi, i: (bi, 0, k_blk)),
                 pl.BlockSpec((None, s, BRANCH_WIDTH), lambda bi, i: (bi, 0, v_blk)),
                 pl.BlockSpec((None, t, BRANCH_WIDTH), lambda bi, i: (bi, i, gate_blk))]
    in_specs += list(extra_specs)
    return pl.pallas_call(
        functools.partial(body, t=t, **kw),
        out_shape=jax.ShapeDtypeStruct((b, s, BRANCH_WIDTH), BF16),
        grid=(b, s // t),
        in_specs=in_specs,
        out_specs=pl.BlockSpec((None, t, BRANCH_WIDTH), lambda bi, i: (bi, i, 0)),
        scratch_shapes=scratch,
        compiler_params=_cparams("arbitrary", "arbitrary"),
        name="attn_" + kind,
    )(*smem_in, q_arr, k_arr, v_arr, proj, *extra_in)


def _select_kernel(qi_ref, ki_ref, wi_ref, o_ref, key_ref, *, tq, kc, n_sel):
    i = pl.program_id(1)
    s_len = o_ref.shape[1]
    n_ch = (i * tq + tq + kc - 1) // kc
    lane = lax.broadcasted_iota(jnp.int32, (tq, 128), 1)
    w = wi_ref[...].astype(F32)
    q_pairs = []
    for j in range(IDX_HEADS // 2):
        q2 = qi_ref[:, j * 128:(j + 1) * 128]
        q_pairs.append((jnp.where(lane < 64, q2, jnp.zeros_like(q2)),
                        jnp.where(lane >= 64, q2, jnp.zeros_like(q2))))
    qpos = i * tq + lax.broadcasted_iota(jnp.int32, (tq, kc), 0)
    kiota = lax.broadcasted_iota(jnp.int32, (tq, kc), 1)

    def score_chunk(c, _):
        k = ki_ref[pl.ds(pl.multiple_of(c * kc, kc), kc), :]
        acc = jnp.zeros((tq, kc), F32)
        for j in range(IDX_HEADS // 2):
            for half in range(2):
                hh = 2 * j + half
                acc = acc + jnp.maximum(_qk(q_pairs[j][half], k), 0.0) * w[:, hh:hh + 1]
        acc = jnp.where(c * kc + kiota <= qpos, acc + 0.0, NEG)
        bits = pltpu.bitcast(acc, jnp.int32)
        key_ref[:, pl.ds(pl.multiple_of(c * kc, kc), kc)] = bits ^ ((bits >> 31) & jnp.int32(0x7FFFFFFF))
        return 0

    lax.fori_loop(0, n_ch, score_chunk, 0)

    def count_ge(cand):
        def body(c, part):
            keys = key_ref[:, pl.ds(pl.multiple_of(c * kc, kc), kc)]
            for u in range(kc // 128):
                part = part + jnp.where(keys[:, u * 128:(u + 1) * 128] >= cand, 1, 0)
            return part
        part = lax.fori_loop(0, n_ch, body, jnp.zeros((tq, 128), jnp.int32))
        return jnp.sum(part, axis=-1, keepdims=True)

    int_min = jnp.int32(-2 ** 31)
    thr = jnp.where(count_ge(jnp.zeros((tq, 1), jnp.int32)) >= n_sel, jnp.int32(0), int_min)

    def bit_step(b, thr):
        cand = thr + (jnp.int32(1) << (30 - b))
        return jnp.where(count_ge(cand) >= n_sel, cand, thr)

    thr = lax.fori_loop(0, 31, bit_step, thr)

    def emit(c, _):
        sl = pl.ds(pl.multiple_of(c * kc, kc), kc)
        o_ref[:, sl] = jnp.where(key_ref[:, sl] >= thr, 0.0, NEG).astype(BF16)
        return 0

    lax.fori_loop(0, n_ch, emit, 0)

    def blank(c, _):
        o_ref[:, pl.ds(pl.multiple_of(c * kc, kc), kc)] = jnp.full((tq, kc), NEG, BF16)
        return 0

    lax.fori_loop(n_ch, s_len // kc, blank, 0)


def _select(proj):
    b, s, _ = proj.shape
    tq = min(128, s)
    kc = min(512, s)
    n_sel = min(TOPK_MAX, s // 4)
    return pl.pallas_call(
        functools.partial(_select_kernel, tq=tq, kc=kc, n_sel=n_sel),
        out_shape=jax.ShapeDtypeStruct((b, s, s), BF16),
        grid=(b, s // tq),
        in_specs=[pl.BlockSpec((None, tq, 1024), lambda bi, i: (bi, i, OFF_QIDX // 1024)),
                  pl.BlockSpec((None, s, 128), lambda bi, i: (bi, 0, OFF_KIDX // 128)),
                  pl.BlockSpec((None, tq, 128), lambda bi, i: (bi, i, OFF_WIDX // 128))],
        out_specs=pl.BlockSpec((None, tq, s), lambda bi, i: (bi, i, 0)),
        scratch_shapes=[pltpu.VMEM((tq, s), jnp.int32)],
        compiler_params=_cparams("arbitrary", "arbitrary"),
        name="idx_select",
    )(proj, proj, proj)


def _out_kernel(a_ref, b_ref, c_ref, d_ref, w_ref, x_ref, gate_ref, g_ref, o_ref):
    y = jnp.dot(a_ref[...], w_ref[0:512, :], preferred_element_type=F32)
    y += jnp.dot(b_ref[...], w_ref[512:1024, :], preferred_element_type=F32)
    y += jnp.dot(c_ref[...], w_ref[1024:1536, :], preferred_element_type=F32)
    y += jnp.dot(d_ref[...], w_ref[1536:2048, :], preferred_element_type=F32)
    o_ref[...] = x_ref[...] + gate_ref[...] * _rms(y, g_ref[...])


def _out_proj(outs, w_out, x, mod3, g_post):
    b, s, d = x.shape
    tm = min(512, s)
    mix = lambda bi, i: (bi, i, 0)
    return pl.pallas_call(
        _out_kernel,
        out_shape=jax.ShapeDtypeStruct((b, s, d), F32),
        grid=(b, s // tm),
        in_specs=[pl.BlockSpec((None, tm, BRANCH_WIDTH), mix)] * 4
        + [pl.BlockSpec(w_out.shape, lambda bi, i: (0, 0)),
           pl.BlockSpec((None, tm, d), mix),
           pl.BlockSpec((None, 1, d), lambda bi, i: (bi, 0, 2)),
           pl.BlockSpec((1, d), lambda bi, i: (0, 0))],
        out_specs=pl.BlockSpec((None, tm, d), mix),
        compiler_params=_cparams("arbitrary", "arbitrary"),
        name="out_proj",
    )(*outs, w_out, x, mod3, g_post.reshape(1, d))


def _rope_tables(s):
    half = MLA_ROPE // 2
    inv = ROPE_THETA ** (-jnp.arange(half, dtype=F32) / half)
    ang = jnp.arange(s, dtype=F32)[:, None] * inv[None, :]
    z = jnp.zeros((s, 128 - MLA_ROPE), F32)
    cos, sin = jnp.cos(ang), jnp.sin(ang)
    return jnp.concatenate([cos, cos, z], axis=-1), jnp.concatenate([sin, sin, z], axis=-1)


def _rot_cols(w):
    half = w.shape[-1] // 2
    return jnp.concatenate([-w[..., half:], w[..., :half]], axis=-1)


def _layout_w_in(w):
    d = w.shape[0]
    names = (("a_cq", 384), ("a_ckv", 256), ("a_krope", 64), ("b_q", 512), ("b_k", 512), ("b_v", 512),
             ("c_q", 512), ("c_k", 512), ("c_v", 512), ("c_qidx", 1024), ("c_kidx", 64), ("c_widx", 16),
             ("d_q", 512), ("d_k", 512), ("d_v", 512), ("gate", 2048))
    p, start = {}, 0
    for name, width in names:
        p[name] = w[:, start:start + width]
        start += width
    z = lambda n: jnp.zeros((d, n), w.dtype)
    cols = [p["c_qidx"] * IDX_DIM ** -0.5,
            p["a_cq"], p["a_ckv"], p["a_krope"], z(64), _rot_cols(p["a_krope"]), z(64), p["c_kidx"], p["c_kidx"],
            p["b_q"] * (LOG2E * HEAD_DIM ** -0.5), p["b_k"], p["b_v"],
            p["c_q"] * (LOG2E * HEAD_DIM ** -0.5), p["c_k"], p["c_v"],
            p["d_q"] * (LOG2E * (HEAD_DIM // 2) ** -0.5), p["d_k"], p["d_v"],
            p["gate"],
            p["c_widx"] * IDX_HEADS ** -0.5, z(112)]
    out = jnp.concatenate(cols, axis=1).astype(BF16)
    assert out.shape[1] == PROJ_WIDTH
    return out


def _layout_w_uq(w):
    r = w.shape[0]
    w = w.reshape(r, N_HEADS, MLA_NOPE + MLA_ROPE) * (LOG2E * (MLA_NOPE + MLA_ROPE) ** -0.5)
    z = jnp.zeros((r, N_HEADS, 128 - MLA_ROPE), w.dtype)
    rope = w[..., MLA_NOPE:]
    return jnp.concatenate([w[..., :MLA_NOPE], rope, z, _rot_cols(rope), z], axis=-1).reshape(r, -1).astype(BF16)


def kernel(x, c, w_ada, b_ada, g_pre, g_post, w_in, g_q_a, w_uq_a, g_kv_a, w_ukv_a,
           lam_q1, lam_k1, lam_q2, lam_k2, g_sub_d, w_out, rel_bias):
    b, s, d = x.shape
    depth = w_ada.shape[0]
    t = min(ATT_TILE, s)
    half = t // 2
    nq = s // t
    near_bias = min(nq, -(-(_first_far_diagonal(half) + 1) // 2))
    near_band = min(nq, -(-(DILATED_PATTERNS[-1][0] // half + 1) // 2))

    cos_t, sin_t = _rope_tables(s)
    bank_b = _bank(rel_bias[:, 0:N_HEADS], 2 * near_band + 1, half, -1, True)
    bank_cd = _bank(rel_bias[:, N_HEADS:3 * N_HEADS], 2 * near_bias + 1, half, -1, False)
    bank_c, bank_d = bank_cd[:N_HEADS], bank_cd[N_HEADS:]
    far_c = rel_bias[REL_BUCKETS - 1, N_HEADS:2 * N_HEADS]
    far_d = rel_bias[REL_BUCKETS - 1, 2 * N_HEADS:3 * N_HEADS]
    bank_spec = lambda n: pl.BlockSpec((N_HEADS, n, half, half), lambda bi, i: (0, 0, 0, 0),
                                       pipeline_mode=pl.Buffered(1))

    mod = _ada_mod(c, w_ada, b_ada)
    for li in range(depth):
        mod3 = mod[li].reshape(b, 1, 3 * d)
        h = _prenorm(x, g_pre[li], mod3)
        proj = _in_proj(h.reshape(b * s, d), _layout_w_in(w_in[li])).reshape(b, s, PROJ_WIDTH)

        q_a, k_a, v_a = _mla_prep(proj, cos_t, sin_t, g_q_a[li], g_kv_a[li],
                                  _layout_w_uq(w_uq_a[li]), w_ukv_a[li].astype(BF16))
        gate0 = OFF_GATE // BRANCH_WIDTH
        out_a = _attention("a", q_a, k_a, v_a, proj, 0, gate0)
        out_b = _attention("band", proj, proj, proj, proj, OFF_B // BRANCH_WIDTH, gate0 + 1,
                           extra_in=(bank_b,), extra_specs=(bank_spec(2 * near_band + 1),), near=near_band)
        sel = _select(proj)
        out_c = _attention("sel", proj, proj, proj, proj, OFF_C // BRANCH_WIDTH, gate0 + 2,
                           extra_in=(bank_c, sel),
                           extra_specs=(bank_spec(2 * near_bias + 1),
                                        pl.BlockSpec((None, t, s), lambda bi, i: (bi, i, 0))),
                           smem_in=(far_c,), near=near_bias)
        lam_init = 0.8 - 0.6 * math.exp(-0.3 * li)
        lam_vecs = jnp.stack([lam_q1[li], lam_k1[li], lam_q2[li], lam_k2[li]])
        out_d = _attention("diff", proj, proj, proj, proj, OFF_D // BRANCH_WIDTH, gate0 + 3,
                           extra_in=(bank_d, lam_vecs, g_sub_d[li].reshape(1, HEAD_DIM)),
                           extra_specs=(bank_spec(2 * near_bias + 1),
                                        pl.BlockSpec(lam_vecs.shape, lambda bi, i: (0, 0)),
                                        pl.BlockSpec((1, HEAD_DIM), lambda bi, i: (0, 0))),
                           smem_in=(far_d,), near=near_bias, lam_init=lam_init)
        x = _out_proj((out_a, out_b, out_c, out_d), w_out[li].astype(BF16), x, mod3, g_post[li])
    return x
```

```python
import functools
import math

import numpy as np
import jax
import jax.numpy as jnp
from jax import lax
from jax.experimental import pallas as pl
from jax.experimental.pallas import tpu as pltpu

F32 = jnp.float32
BF16 = jnp.bfloat16

HEAD_DIM = 128
N_HEADS = 4
BRANCH_WIDTH = N_HEADS * HEAD_DIM
MIX_WIDTH = 4 * BRANCH_WIDTH
MLA_Q_RANK = 384
MLA_KV_RANK = 256
MLA_NOPE = 128
MLA_ROPE = 64
ROPE_THETA = 10000.0
DILATED_PATTERNS = ((128, 1), (512, 4), (2048, 16))
IDX_HEADS = 16
IDX_DIM = 64
TOPK_MAX = 256
REL_BUCKETS = 32
REL_MAX_DIST = 2048
NORM_EPS = 1e-6
NEG = -1e30
LOG2E = math.log2(math.e)

OFF_QIDX = 0
OFF_A = 1024
OFF_KIDX = OFF_A + 896
OFF_B = 2048
OFF_C = 3584
OFF_D = 5120
OFF_GATE = 6656
OFF_WIDX = 8704
PROJ_WIDTH = 8832

ATT_TILE = 512
VMEM_LIMIT = 56 * 1024 * 1024


def _cparams(*sem):
    return pltpu.CompilerParams(dimension_semantics=sem, vmem_limit_bytes=VMEM_LIMIT)


def _ada_kernel(c_ref, w_ref, b_ref, o_ref):
    c = c_ref[...]
    a = c * jax.nn.sigmoid(c)
    o_ref[...] = jnp.dot(a, w_ref[...], preferred_element_type=F32,
                         precision=lax.Precision.HIGHEST) + b_ref[...]


def _ada_mod(c, w_ada, b_ada):
    depth, d, n = w_ada.shape
    b = c.shape[0]
    tn = 768
    return pl.pallas_call(
        _ada_kernel,
        out_shape=jax.ShapeDtypeStruct((depth, b, n), F32),
        grid=(depth, n // tn),
        in_specs=[pl.BlockSpec((b, d), lambda l, j: (0, 0)),
                  pl.BlockSpec((None, d, tn), lambda l, j: (l, 0, j)),
                  pl.BlockSpec((None, 1, tn), lambda l, j: (l, 0, j))],
        out_specs=pl.BlockSpec((None, b, tn), lambda l, j: (l, 0, j)),
        compiler_params=_cparams("arbitrary", "arbitrary"),
        name="ada_mod",
    )(c, w_ada, b_ada.reshape(depth, 1, n))


def _prenorm_kernel(x_ref, g_ref, shift_ref, scale_ref, o_ref):
    x = x_ref[...]
    y = x * lax.rsqrt(jnp.mean(x * x, axis=-1, keepdims=True) + NORM_EPS) * g_ref[...]
    o_ref[...] = (y * (1.0 + scale_ref[...]) + shift_ref[...]).astype(BF16)


def _prenorm(x, g_pre, mod3):
    b, s, d = x.shape
    tm = min(512, s)
    return pl.pallas_call(
        _prenorm_kernel,
        out_shape=jax.ShapeDtypeStruct((b, s, d), BF16),
        grid=(b, s // tm),
        in_specs=[pl.BlockSpec((None, tm, d), lambda bi, i: (bi, i, 0)),
                  pl.BlockSpec((1, d), lambda bi, i: (0, 0)),
                  pl.BlockSpec((None, 1, d), lambda bi, i: (bi, 0, 0)),
                  pl.BlockSpec((None, 1, d), lambda bi, i: (bi, 0, 1))],
        out_specs=pl.BlockSpec((None, tm, d), lambda bi, i: (bi, i, 0)),
        compiler_params=_cparams("arbitrary", "arbitrary"),
        name="prenorm",
    )(x, g_pre.reshape(1, d), mod3, mod3)


def _matmul_kernel(a_ref, w_ref, o_ref):
    o_ref[...] = jnp.dot(a_ref[...], w_ref[...], preferred_element_type=F32).astype(o_ref.dtype)


def _in_proj(h2d, w):
    m, k = h2d.shape
    n = w.shape[1]
    tm = min(512, m)
    tn = n // 3
    return pl.pallas_call(
        _matmul_kernel,
        out_shape=jax.ShapeDtypeStruct((m, n), BF16),
        grid=(n // tn, m // tm),
        in_specs=[pl.BlockSpec((tm, k), lambda j, i: (i, 0)),
                  pl.BlockSpec((k, tn), lambda j, i: (0, j))],
        out_specs=pl.BlockSpec((tm, tn), lambda j, i: (i, j)),
        compiler_params=_cparams("arbitrary", "arbitrary"),
        name="in_proj",
    )(h2d, w)


def _rms(x, g):
    return x * lax.rsqrt(jnp.mean(x * x, axis=-1, keepdims=True) + NORM_EPS) * g


def _mla_prep_kernel(p_ref, cos_ref, sin_ref, gq_ref, gkv_ref, wq_ref, wkv_ref, q_ref, k_ref, v_ref):
    cos = cos_ref[...]
    sin = sin_ref[...]
    cq = _rms(p_ref[:, 0:MLA_Q_RANK].astype(F32), gq_ref[...]).astype(BF16)
    ckv = _rms(p_ref[:, MLA_Q_RANK:MLA_Q_RANK + MLA_KV_RANK].astype(F32), gkv_ref[...]).astype(BF16)
    q = jnp.dot(cq, wq_ref[...], preferred_element_type=F32)
    kv = jnp.dot(ckv, wkv_ref[...], preferred_element_type=F32)
    k_rope = (p_ref[:, 640:768].astype(F32) * cos + p_ref[:, 768:896].astype(F32) * sin).astype(BF16)
    for h in range(N_HEADS):
        qh = q[:, h * 384:(h + 1) * 384]
        q_ref[:, h * 256:h * 256 + 128] = qh[:, 0:128].astype(BF16)
        q_ref[:, h * 256 + 128:(h + 1) * 256] = (qh[:, 128:256] * cos + qh[:, 256:384] * sin).astype(BF16)
        k_ref[:, h * 256:h * 256 + 128] = kv[:, h * 256:h * 256 + 128].astype(BF16)
        k_ref[:, h * 256 + 128:(h + 1) * 256] = k_rope
        v_ref[:, h * 128:(h + 1) * 128] = kv[:, h * 256 + 128:(h + 1) * 256].astype(BF16)


def _mla_prep(proj, cos_t, sin_t, g_q, g_kv, wq, wkv):
    b, s, _ = proj.shape
    tm = min(512, s)
    const = lambda bi, i: (0, 0)
    return pl.pallas_call(
        _mla_prep_kernel,
        out_shape=(jax.ShapeDtypeStruct((b, s, N_HEADS * 256), BF16),
                   jax.ShapeDtypeStruct((b, s, N_HEADS * 256), BF16),
                   jax.ShapeDtypeStruct((b, s, BRANCH_WIDTH), BF16)),
        grid=(b, s // tm),
        in_specs=[pl.BlockSpec((None, tm, 1024), lambda bi, i: (bi, i, OFF_A // 1024)),
                  pl.BlockSpec((tm, 128), lambda bi, i: (i, 0)),
                  pl.BlockSpec((tm, 128), lambda bi, i: (i, 0)),
                  pl.BlockSpec((1, MLA_Q_RANK), const),
                  pl.BlockSpec((1, MLA_KV_RANK), const),
                  pl.BlockSpec(wq.shape, const),
                  pl.BlockSpec(wkv.shape, const)],
        out_specs=(pl.BlockSpec((None, tm, N_HEADS * 256), lambda bi, i: (bi, i, 0)),
                   pl.BlockSpec((None, tm, N_HEADS * 256), lambda bi, i: (bi, i, 0)),
                   pl.BlockSpec((None, tm, BRANCH_WIDTH), lambda bi, i: (bi, i, 0))),
        compiler_params=_cparams("arbitrary", "arbitrary"),
        name="mla_prep",
    )(proj, cos_t, sin_t, g_q.reshape(1, -1), g_kv.reshape(1, -1), wq, wkv)


def _bucket_np(n):
    max_exact = REL_BUCKETS // 2
    nf = np.maximum(n, max_exact).astype(np.float32)
    large = max_exact + (np.log(nf / np.float32(max_exact)) / np.float32(math.log(REL_MAX_DIST / max_exact))
                         * np.float32(REL_BUCKETS - max_exact)).astype(np.int32)
    return np.where(n < max_exact, n, np.minimum(large, REL_BUCKETS - 1)).astype(np.int32)


def _bucket_starts():
    buckets = _bucket_np(np.arange(2 * REL_MAX_DIST))
    return [int(np.argmax(buckets >= b)) for b in range(REL_BUCKETS)]


def _bank_kernel(tab_ref, o_ref, *, t, d_min, band):
    d = pl.program_id(0) + d_min
    dist = (t * d + lax.broadcasted_iota(jnp.int32, (t, t), 0) - lax.broadcasted_iota(jnp.int32, (t, t), 1))
    starts = _bucket_starts()
    ge = [dist >= starts[b] for b in range(1, REL_BUCKETS)]
    if band:
        mult = jnp.zeros((t, t), jnp.int32)
        for window, dil in DILATED_PATTERNS:
            mult += jnp.where((dist >= 0) & (dist <= window) & ((dist & (dil - 1)) == 0), 1, 0)
        logm = jnp.where(mult == 3, math.log(3.0), jnp.where(mult == 2, math.log(2.0), 0.0))
        keep = mult > 0
    else:
        logm = 0.0
        keep = dist >= 0
    for h in range(o_ref.shape[0]):
        val = jnp.full((t, t), tab_ref[0, h], F32)
        for b in range(1, REL_BUCKETS):
            val = jnp.where(ge[b - 1], tab_ref[b, h], val)
        o_ref[h] = jnp.where(keep, (val + logm) * LOG2E, NEG)


def _bank(tab, n_tables, t, d_min, band):
    nh = tab.shape[1]
    return pl.pallas_call(
        functools.partial(_bank_kernel, t=t, d_min=d_min, band=band),
        out_shape=jax.ShapeDtypeStruct((nh, n_tables, t, t), F32),
        grid=(n_tables,),
        in_specs=[pl.BlockSpec(memory_space=pltpu.SMEM)],
        out_specs=pl.BlockSpec((nh, None, t, t), lambda j: (0, j, 0, 0)),
        compiler_params=_cparams("arbitrary"),
        name="bank_band" if band else "bank_bias",
    )(tab)


def _first_far_diagonal(t):
    last = _bucket_starts()[REL_BUCKETS - 1]
    return -(-(last + t - 1) // t)


def _qk(q, k):
    return lax.dot_general(q, k, (((1,), (1,)), ((), ())), preferred_element_type=F32)


def _flash_init(m_ref, acc_ref):
    m_ref[...] = jnp.full(m_ref.shape, NEG, F32)
    acc_ref[...] = jnp.zeros(acc_ref.shape, F32)


def _flash_update(slot, s, v, m_ref, acc_ref):
    m_prev = m_ref[slot]
    m_new = jnp.maximum(m_prev, jnp.max(s, axis=-1, keepdims=True))
    alpha = jnp.exp2(m_prev - m_new)
    p = jnp.concatenate([jnp.exp2(s[:, j * 128:(j + 1) * 128] - m_new) for j in range(s.shape[1] // 128)],
                        axis=1).astype(BF16)
    v_ones = jnp.concatenate([v, jnp.ones_like(v)], axis=1)
    acc_ref[slot] = (jnp.concatenate([alpha, alpha], axis=1) * acc_ref[slot]
                     + jnp.dot(p, v_ones, preferred_element_type=F32))
    m_ref[slot] = m_new


def _flash_result(slot, acc_ref):
    acc = acc_ref[slot]
    return acc[:, :HEAD_DIM] / acc[:, HEAD_DIM:]


def _silu(g):
    return g * jax.nn.sigmoid(g)


def _tile(ref, kj, t, c0, width):
    return ref[pl.ds(pl.multiple_of(kj * t, t), t), c0:c0 + width]


def _bank_tile(bank_ref, h, i, kj):
    base = 2 * (i - kj) + 1
    top = jnp.concatenate([bank_ref[h, base], bank_ref[h, base - 1]], axis=1)
    bot = jnp.concatenate([bank_ref[h, base + 1], bank_ref[h, base]], axis=1)
    return jnp.concatenate([top, bot], axis=0)


def _emit(o_ref, g_ref, h, o):
    gate = g_ref[:, h * 128:(h + 1) * 128].astype(F32)
    o_ref[:, h * 128:(h + 1) * 128] = (o * _silu(gate)).astype(BF16)


def _walk(lo, hi, step, *args):
    def body(kj, carry):
        step(kj, *args)
        return carry
    lax.fori_loop(lo, hi, body, 0)


def _attn_a_kernel(q_ref, k_ref, v_ref, g_ref, o_ref, m_ref, acc_ref, *, t):
    i = pl.program_id(1)
    _flash_init(m_ref, acc_ref)

    def step(kj, masked):
        s_all = [_qk(q_ref[:, h * 256:(h + 1) * 256], _tile(k_ref, kj, t, h * 256, 256))
                 for h in range(N_HEADS)]
        for h in range(N_HEADS):
            s = s_all[h]
            if masked:
                causal = (lax.broadcasted_iota(jnp.int32, (t, t), 1)
                          <= lax.broadcasted_iota(jnp.int32, (t, t), 0))
                s = jnp.where(causal, s, NEG)
            _flash_update(h, s, _tile(v_ref, kj, t, h * 128, 128), m_ref, acc_ref)

    _walk(0, i, step, False)
    step(i, True)
    for h in range(N_HEADS):
        _emit(o_ref, g_ref, h, _flash_result(h, acc_ref))


def _attn_band_kernel(q_ref, k_ref, v_ref, g_ref, bank_ref, o_ref, m_ref, acc_ref, *, t, near):
    i = pl.program_id(1)
    _flash_init(m_ref, acc_ref)

    def step(kj):
        s_all = [_qk(q_ref[:, h * 128:(h + 1) * 128], _tile(k_ref, kj, t, h * 128, 128))
                 for h in range(N_HEADS)]
        for h in range(N_HEADS):
            s = s_all[h] + _bank_tile(bank_ref, h, i, kj)
            _flash_update(h, s, _tile(v_ref, kj, t, h * 128, 128), m_ref, acc_ref)

    _walk(jnp.maximum(i - (near - 1), 0), i + 1, step)
    for h in range(N_HEADS):
        _emit(o_ref, g_ref, h, _flash_result(h, acc_ref))


def _attn_sel_kernel(far_ref, q_ref, k_ref, v_ref, g_ref, bank_ref, sel_ref, o_ref, m_ref, acc_ref,
                     *, t, near):
    i = pl.program_id(1)
    n_far = jnp.maximum(i - (near - 1), 0)
    _flash_init(m_ref, acc_ref)

    def step(kj, is_near):
        s_all = [_qk(q_ref[:, h * 128:(h + 1) * 128], _tile(k_ref, kj, t, h * 128, 128))
                 for h in range(N_HEADS)]
        sel = sel_ref[:, pl.ds(pl.multiple_of(kj * t, t), t)].astype(F32)
        for h in range(N_HEADS):
            bias = _bank_tile(bank_ref, h, i, kj) if is_near else far_ref[h] * LOG2E
            _flash_update(h, s_all[h] + (sel + bias), _tile(v_ref, kj, t, h * 128, 128), m_ref, acc_ref)

    _walk(0, n_far, step, False)
    _walk(n_far, i + 1, step, True)
    for h in range(N_HEADS):
        _emit(o_ref, g_ref, h, _flash_result(h, acc_ref))


def _attn_diff_kernel(far_ref, q_ref, k_ref, v_ref, g_ref, bank_ref, lam_ref, gsub_ref, o_ref,
                      q1_ref, q2_ref, m_ref, acc_ref, *, t, near, lam_init):
    i = pl.program_id(1)
    n_far = jnp.maximum(i - (near - 1), 0)
    _flash_init(m_ref, acc_ref)
    lane = lax.broadcasted_iota(jnp.int32, q_ref.shape, 1) & (HEAD_DIM - 1)
    q = q_ref[...]
    q1_ref[...] = jnp.where(lane < HEAD_DIM // 2, q, jnp.zeros_like(q))
    q2_ref[...] = jnp.where(lane >= HEAD_DIM // 2, q, jnp.zeros_like(q))

    def step(kj, is_near):
        s_all = []
        for h in range(N_HEADS):
            k = _tile(k_ref, kj, t, h * 128, 128)
            s_all.append(_qk(q1_ref[:, h * 128:(h + 1) * 128], k))
            s_all.append(_qk(q2_ref[:, h * 128:(h + 1) * 128], k))
        for h in range(N_HEADS):
            v = _tile(v_ref, kj, t, h * 128, 128)
            bias = _bank_tile(bank_ref, h, i, kj) if is_near else far_ref[h] * LOG2E
            _flash_update(2 * h, s_all[2 * h] + bias, v, m_ref, acc_ref)
            _flash_update(2 * h + 1, s_all[2 * h + 1] + bias, v, m_ref, acc_ref)

    _walk(0, n_far, step, False)
    _walk(n_far, i + 1, step, True)
    lam_v = lam_ref[...]
    lam = (jnp.exp(jnp.sum(lam_v[0:1] * lam_v[1:2], axis=-1, keepdims=True))
           - jnp.exp(jnp.sum(lam_v[2:3] * lam_v[3:4], axis=-1, keepdims=True)) + lam_init)
    for h in range(N_HEADS):
        o = _flash_result(2 * h, acc_ref) - lam * _flash_result(2 * h + 1, acc_ref)
        _emit(o_ref, g_ref, h, _rms(o, gsub_ref[...]) * (1.0 - lam_init))


def _attention(kind, q_arr, k_arr, v_arr, proj, q_blk, gate_blk, extra_in=(), extra_specs=(),
               smem_in=(), **kw):
    b, s, _ = proj.shape
    t = min(ATT_TILE, s)
    dk = 256 if kind == "a" else 128
    qw = N_HEADS * dk
    k_blk = 0 if kind == "a" else q_blk + 1
    v_blk = 0 if kind == "a" else q_blk + 2
    body = {"a": _attn_a_kernel, "band": _attn_band_kernel, "sel": _attn_sel_kernel,
            "diff": _attn_diff_kernel}[kind]
    slots = 2 * N_HEADS if kind == "diff" else N_HEADS
    scratch = [pltpu.VMEM((slots, t, HEAD_DIM), F32), pltpu.VMEM((slots, t, 2 * HEAD_DIM), F32)]
    if kind == "diff":
        scratch = [pltpu.VMEM((t, qw), BF16), pltpu.VMEM((t, qw), BF16)] + scratch
    in_specs = [pl.BlockSpec(memory_space=pltpu.SMEM) for _ in smem_in]
    in_specs += [pl.BlockSpec((None, t, qw), lambda bi, i: (bi, i, q_blk)),
                 pl.BlockSpec((None, s, qw), lambda bi, i: (bi, 0, k_blk)),
                 pl.BlockSpec((None, s, BRANCH_WIDTH), lambda bi, i: (bi, 0, v_blk)),
                 pl.BlockSpec((None, t, BRANCH_WIDTH), lambda bi, i: (bi, i, gate_blk))]
    in_specs += list(extra_specs)
    return pl.pallas_call(
        functools.partial(body, t=t, **kw),
        out_shape=jax.ShapeDtypeStruct((b, s, BRANCH_WIDTH), BF16),
        grid=(b, s // t),
        in_specs=in_specs,
        out_specs=pl.BlockSpec((None, t, BRANCH_WIDTH), lambda bi, i: (bi, i, 0)),
        scratch_shapes=scratch,
        compiler_params=_cparams("arbitrary", "arbitrary"),
        name="attn_" + kind,
    )(*smem_in, q_arr, k_arr, v_arr, proj, *extra_in)


def _select_kernel(qi_ref, ki_ref, wi_ref, o_ref, key_ref, *, tq, kc, n_sel):
    i = pl.program_id(1)
    s_len = o_ref.shape[1]
    n_ch = (i * tq + tq + kc - 1) // kc
    lane = lax.broadcasted_iota(jnp.int32, (tq, 128), 1)
    w_t = wi_ref[...].astype(F32).T
    q_heads = []
    for j in range(IDX_HEADS // 2):
        q2 = qi_ref[:, j * 128:(j + 1) * 128]
        q_heads.append(jnp.where(lane < IDX_DIM, q2, jnp.zeros_like(q2)))
        q_heads.append(jnp.where(lane >= IDX_DIM, q2, jnp.zeros_like(q2)))
    qpos = i * tq + lax.broadcasted_iota(jnp.int32, (kc, tq), 1)
    kiota = lax.broadcasted_iota(jnp.int32, (kc, tq), 0)

    def chunk(c):
        return pl.ds(pl.multiple_of(c * kc, kc), kc)

    def score_chunk(c, _):
        k = ki_ref[chunk(c), :]
        acc = jnp.zeros((kc, tq), F32)
        for hh in range(IDX_HEADS):
            acc = acc + jnp.maximum(_qk(k, q_heads[hh]), 0.0) * w_t[hh:hh + 1, :]
        acc = jnp.where(c * kc + kiota <= qpos, acc + 0.0, NEG)
        bits = pltpu.bitcast(acc, jnp.int32)
        key_ref[chunk(c), :] = bits ^ ((bits >> 31) & jnp.int32(0x7FFFFFFF))
        return 0

    lax.fori_loop(0, n_ch, score_chunk, 0)

    def count_ge(cand):
        def body(c, part):
            hit = jnp.where(key_ref[chunk(c), :] >= cand, 1, 0)
            return part + jnp.sum(hit.reshape(kc // 8, 8, tq), axis=0)
        part = lax.fori_loop(0, n_ch, body, jnp.zeros((8, tq), jnp.int32))
        return jnp.sum(part, axis=0, keepdims=True)

    int_min = jnp.int32(-2 ** 31)
    thr = jnp.where(count_ge(jnp.zeros((1, tq), jnp.int32)) >= n_sel, jnp.int32(0), int_min)

    def bit_step(b, thr):
        cand = thr + (jnp.int32(1) << (30 - b))
        return jnp.where(count_ge(cand) >= n_sel, cand, thr)

    thr = lax.fori_loop(0, 31, bit_step, thr)

    def emit(c, _):
        keep = jnp.where(key_ref[chunk(c), :] >= thr, 0.0, NEG)
        o_ref[:, chunk(c)] = keep.T.astype(BF16)
        return 0

    lax.fori_loop(0, n_ch, emit, 0)

    def blank(c, _):
        o_ref[:, chunk(c)] = jnp.full((tq, kc), NEG, BF16)
        return 0

    lax.fori_loop(n_ch, s_len // kc, blank, 0)


def _select(proj):
    b, s, _ = proj.shape
    tq = min(256, s)
    kc = min(512, s)
    n_sel = min(TOPK_MAX, s // 4)
    return pl.pallas_call(
        functools.partial(_select_kernel, tq=tq, kc=kc, n_sel=n_sel),
        out_shape=jax.ShapeDtypeStruct((b, s, s), BF16),
        grid=(b, s // tq),
        in_specs=[pl.BlockSpec((None, tq, 1024), lambda bi, i: (bi, i, OFF_QIDX // 1024)),
                  pl.BlockSpec((None, s, 128), lambda bi, i: (bi, 0, OFF_KIDX // 128)),
                  pl.BlockSpec((None, tq, 128), lambda bi, i: (bi, i, OFF_WIDX // 128))],
        out_specs=pl.BlockSpec((None, tq, s), lambda bi, i: (bi, i, 0)),
        scratch_shapes=[pltpu.VMEM((s, tq), jnp.int32)],
        compiler_params=_cparams("arbitrary", "arbitrary"),
        name="idx_select",
    )(proj, proj, proj)


def _out_kernel(a_ref, b_ref, c_ref, d_ref, w_ref, x_ref, gate_ref, g_ref, o_ref):
    y = jnp.dot(a_ref[...], w_ref[0:512, :], preferred_element_type=F32)
    y += jnp.dot(b_ref[...], w_ref[512:1024, :], preferred_element_type=F32)
    y += jnp.dot(c_ref[...], w_ref[1024:1536, :], preferred_element_type=F32)
    y += jnp.dot(d_ref[...], w_ref[1536:2048, :], preferred_element_type=F32)
    o_ref[...] = x_ref[...] + gate_ref[...] * _rms(y, g_ref[...])


def _out_proj(outs, w_out, x, mod3, g_post):
    b, s, d = x.shape
    tm = min(512, s)
    mix = lambda bi, i: (bi, i, 0)
    return pl.pallas_call(
        _out_kernel,
        out_shape=jax.ShapeDtypeStruct((b, s, d), F32),
        grid=(b, s // tm),
        in_specs=[pl.BlockSpec((None, tm, BRANCH_WIDTH), mix)] * 4
        + [pl.BlockSpec(w_out.shape, lambda bi, i: (0, 0)),
           pl.BlockSpec((None, tm, d), mix),
           pl.BlockSpec((None, 1, d), lambda bi, i: (bi, 0, 2)),
           pl.BlockSpec((1, d), lambda bi, i: (0, 0))],
        out_specs=pl.BlockSpec((None, tm, d), mix),
        compiler_params=_cparams("arbitrary", "arbitrary"),
        name="out_proj",
    )(*outs, w_out, x, mod3, g_post.reshape(1, d))


def _rope_tables(s):
    half = MLA_ROPE // 2
    inv = ROPE_THETA ** (-jnp.arange(half, dtype=F32) / half)
    ang = jnp.arange(s, dtype=F32)[:, None] * inv[None, :]
    z = jnp.zeros((s, 128 - MLA_ROPE), F32)
    cos, sin = jnp.cos(ang), jnp.sin(ang)
    return jnp.concatenate([cos, cos, z], axis=-1), jnp.concatenate([sin, sin, z], axis=-1)


def _rot_cols(w):
    half = w.shape[-1] // 2
    return jnp.concatenate([-w[..., half:], w[..., :half]], axis=-1)


def _layout_w_in(w):
    d = w.shape[0]
    names = (("a_cq", 384), ("a_ckv", 256), ("a_krope", 64), ("b_q", 512), ("b_k", 512), ("b_v", 512),
             ("c_q", 512), ("c_k", 512), ("c_v", 512), ("c_qidx", 1024), ("c_kidx", 64), ("c_widx", 16),
             ("d_q", 512), ("d_k", 512), ("d_v", 512), ("gate", 2048))
    p, start = {}, 0
    for name, width in names:
        p[name] = w[:, start:start + width]
        start += width
    z = lambda n: jnp.zeros((d, n), w.dtype)
    cols = [p["c_qidx"] * IDX_DIM ** -0.5,
            p["a_cq"], p["a_ckv"], p["a_krope"], z(64), _rot_cols(p["a_krope"]), z(64), p["c_kidx"], p["c_kidx"],
            p["b_q"] * (LOG2E * HEAD_DIM ** -0.5), p["b_k"], p["b_v"],
            p["c_q"] * (LOG2E * HEAD_DIM ** -0.5), p["c_k"], p["c_v"],
            p["d_q"] * (LOG2E * (HEAD_DIM // 2) ** -0.5), p["d_k"], p["d_v"],
            p["gate"],
            p["c_widx"] * IDX_HEADS ** -0.5, z(112)]
    out = jnp.concatenate(cols, axis=1).astype(BF16)
    assert out.shape[1] == PROJ_WIDTH
    return out


def _layout_w_uq(w):
    r = w.shape[0]
    w = w.reshape(r, N_HEADS, MLA_NOPE + MLA_ROPE) * (LOG2E * (MLA_NOPE + MLA_ROPE) ** -0.5)
    z = jnp.zeros((r, N_HEADS, 128 - MLA_ROPE), w.dtype)
    rope = w[..., MLA_NOPE:]
    return jnp.concatenate([w[..., :MLA_NOPE], rope, z, _rot_cols(rope), z], axis=-1).reshape(r, -1).astype(BF16)


def kernel(x, c, w_ada, b_ada, g_pre, g_post, w_in, g_q_a, w_uq_a, g_kv_a, w_ukv_a,
           lam_q1, lam_k1, lam_q2, lam_k2, g_sub_d, w_out, rel_bias):
    b, s, d = x.shape
    depth = w_ada.shape[0]
    t = min(ATT_TILE, s)
    half = t // 2
    nq = s // t
    near_bias = min(nq, -(-(_first_far_diagonal(half) + 1) // 2))
    near_band = min(nq, -(-(DILATED_PATTERNS[-1][0] // half + 1) // 2))

    cos_t, sin_t = _rope_tables(s)
    bank_b = _bank(rel_bias[:, 0:N_HEADS], 2 * near_band + 1, half, -1, True)
    bank_cd = _bank(rel_bias[:, N_HEADS:3 * N_HEADS], 2 * near_bias + 1, half, -1, False)
    bank_c, bank_d = bank_cd[:N_HEADS], bank_cd[N_HEADS:]
    far_c = rel_bias[REL_BUCKETS - 1, N_HEADS:2 * N_HEADS]
    far_d = rel_bias[REL_BUCKETS - 1, 2 * N_HEADS:3 * N_HEADS]
    bank_spec = lambda n: pl.BlockSpec((N_HEADS, n, half, half), lambda bi, i: (0, 0, 0, 0),
                                       pipeline_mode=pl.Buffered(1))

    mod = _ada_mod(c, w_ada, b_ada)
    for li in range(depth):
        mod3 = mod[li].reshape(b, 1, 3 * d)
        h = _prenorm(x, g_pre[li], mod3)
        proj = _in_proj(h.reshape(b * s, d), _layout_w_in(w_in[li])).reshape(b, s, PROJ_WIDTH)

        q_a, k_a, v_a = _mla_prep(proj, cos_t, sin_t, g_q_a[li], g_kv_a[li],
                                  _layout_w_uq(w_uq_a[li]), w_ukv_a[li].astype(BF16))
        gate0 = OFF_GATE // BRANCH_WIDTH
        out_a = _attention("a", q_a, k_a, v_a, proj, 0, gate0)
        out_b = _attention("band", proj, proj, proj, proj, OFF_B // BRANCH_WIDTH, gate0 + 1,
                           extra_in=(bank_b,), extra_specs=(bank_spec(2 * near_band + 1),), near=near_band)
        sel = _select(proj)
        out_c = _attention("sel", proj, proj, proj, proj, OFF_C // BRANCH_WIDTH, gate0 + 2,
                           extra_in=(bank_c, sel),
                           extra_specs=(bank_spec(2 * near_bias + 1),
                                        pl.BlockSpec((None, t, s), lambda bi, i: (bi, i, 0))),
                           smem_in=(far_c,), near=near_bias)
        lam_init = 0.8 - 0.6 * math.exp(-0.3 * li)
        lam_vecs = jnp.stack([lam_q1[li], lam_k1[li], lam_q2[li], lam_k2[li]])
        out_d = _attention("diff", proj, proj, proj, proj, OFF_D // BRANCH_WIDTH, gate0 + 3,
                           extra_in=(bank_d, lam_vecs, g_sub_d[li].reshape(1, HEAD_DIM)),
                           extra_specs=(bank_spec(2 * near_bias + 1),
                                        pl.BlockSpec(lam_vecs.shape, lambda bi, i: (0, 0)),
                                        pl.BlockSpec((1, HEAD_DIM), lambda bi, i: (0, 0))),
                           smem_in=(far_d,), near=near_bias, lam_init=lam_init)
        x = _out_proj((out_a, out_b, out_c, out_d), w_out[li].astype(BF16), x, mod3, g_post[li])
    return x
```

```python
import functools
import math

import numpy as np
import jax
import jax.numpy as jnp
from jax import lax
from jax.experimental import pallas as pl
from jax.experimental.pallas import tpu as pltpu

F32 = jnp.float32
BF16 = jnp.bfloat16

HEAD_DIM = 128
N_HEADS = 4
BRANCH_WIDTH = N_HEADS * HEAD_DIM
MIX_WIDTH = 4 * BRANCH_WIDTH
MLA_Q_RANK = 384
MLA_KV_RANK = 256
MLA_NOPE = 128
MLA_ROPE = 64
ROPE_THETA = 10000.0
DILATED_PATTERNS = ((128, 1), (512, 4), (2048, 16))
IDX_HEADS = 16
IDX_DIM = 64
TOPK_MAX = 256
REL_BUCKETS = 32
REL_MAX_DIST = 2048
NORM_EPS = 1e-6
NEG = -1e30
LOG2E = math.log2(math.e)

OFF_QIDX = 0
OFF_A = 1024
OFF_KIDX = OFF_A + 896
OFF_B = 2048
OFF_C = 3584
OFF_D = 5120
OFF_GATE = 6656
OFF_WIDX = 8704
PROJ_WIDTH = 8832

ATT_TILE = 512
VMEM_LIMIT = 56 * 1024 * 1024


def _cparams(*sem):
    return pltpu.CompilerParams(dimension_semantics=sem, vmem_limit_bytes=VMEM_LIMIT)


def _ada_kernel(c_ref, w_ref, b_ref, o_ref):
    c = c_ref[...]
    a = c * jax.nn.sigmoid(c)
    o_ref[...] = jnp.dot(a, w_ref[...], preferred_element_type=F32,
                         precision=lax.Precision.HIGHEST) + b_ref[...]


def _ada_mod(c, w_ada, b_ada):
    depth, d, n = w_ada.shape
    b = c.shape[0]
    tn = 768
    return pl.pallas_call(
        _ada_kernel,
        out_shape=jax.ShapeDtypeStruct((depth, b, n), F32),
        grid=(depth, n // tn),
        in_specs=[pl.BlockSpec((b, d), lambda l, j: (0, 0)),
                  pl.BlockSpec((None, d, tn), lambda l, j: (l, 0, j)),
                  pl.BlockSpec((None, 1, tn), lambda l, j: (l, 0, j))],
        out_specs=pl.BlockSpec((None, b, tn), lambda l, j: (l, 0, j)),
        compiler_params=_cparams("arbitrary", "arbitrary"),
        name="ada_mod",
    )(c, w_ada, b_ada.reshape(depth, 1, n))


def _prenorm_kernel(x_ref, g_ref, shift_ref, scale_ref, o_ref):
    x = x_ref[...]
    y = x * lax.rsqrt(jnp.mean(x * x, axis=-1, keepdims=True) + NORM_EPS) * g_ref[...]
    o_ref[...] = (y * (1.0 + scale_ref[...]) + shift_ref[...]).astype(BF16)


def _prenorm(x, g_pre, mod3):
    b, s, d = x.shape
    tm = min(512, s)
    return pl.pallas_call(
        _prenorm_kernel,
        out_shape=jax.ShapeDtypeStruct((b, s, d), BF16),
        grid=(b, s // tm),
        in_specs=[pl.BlockSpec((None, tm, d), lambda bi, i: (bi, i, 0)),
                  pl.BlockSpec((1, d), lambda bi, i: (0, 0)),
                  pl.BlockSpec((None, 1, d), lambda bi, i: (bi, 0, 0)),
                  pl.BlockSpec((None, 1, d), lambda bi, i: (bi, 0, 1))],
        out_specs=pl.BlockSpec((None, tm, d), lambda bi, i: (bi, i, 0)),
        compiler_params=_cparams("arbitrary", "arbitrary"),
        name="prenorm",
    )(x, g_pre.reshape(1, d), mod3, mod3)


def _matmul_kernel(a_ref, w_ref, o_ref):
    o_ref[...] = jnp.dot(a_ref[...], w_ref[...], preferred_element_type=F32).astype(o_ref.dtype)


def _in_proj(h2d, w):
    m, k = h2d.shape
    n = w.shape[1]
    tm = min(512, m)
    tn = n // 3
    return pl.pallas_call(
        _matmul_kernel,
        out_shape=jax.ShapeDtypeStruct((m, n), BF16),
        grid=(n // tn, m // tm),
        in_specs=[pl.BlockSpec((tm, k), lambda j, i: (i, 0)),
                  pl.BlockSpec((k, tn), lambda j, i: (0, j))],
        out_specs=pl.BlockSpec((tm, tn), lambda j, i: (i, j)),
        compiler_params=_cparams("arbitrary", "arbitrary"),
        name="in_proj",
    )(h2d, w)


def _rms(x, g):
    return x * lax.rsqrt(jnp.mean(x * x, axis=-1, keepdims=True) + NORM_EPS) * g


def _mla_prep_kernel(p_ref, cos_ref, sin_ref, gq_ref, gkv_ref, wq_ref, wkv_ref, q_ref, k_ref, v_ref):
    cos = cos_ref[...]
    sin = sin_ref[...]
    cq = _rms(p_ref[:, 0:MLA_Q_RANK].astype(F32), gq_ref[...]).astype(BF16)
    ckv = _rms(p_ref[:, MLA_Q_RANK:MLA_Q_RANK + MLA_KV_RANK].astype(F32), gkv_ref[...]).astype(BF16)
    q = jnp.dot(cq, wq_ref[...], preferred_element_type=F32)
    kv = jnp.dot(ckv, wkv_ref[...], preferred_element_type=F32)
    k_rope = (p_ref[:, 640:768].astype(F32) * cos + p_ref[:, 768:896].astype(F32) * sin).astype(BF16)
    for h in range(N_HEADS):
        qh = q[:, h * 384:(h + 1) * 384]
        q_ref[:, h * 256:h * 256 + 128] = qh[:, 0:128].astype(BF16)
        q_ref[:, h * 256 + 128:(h + 1) * 256] = (qh[:, 128:256] * cos + qh[:, 256:384] * sin).astype(BF16)
        k_ref[:, h * 256:h * 256 + 128] = kv[:, h * 256:h * 256 + 128].astype(BF16)
        k_ref[:, h * 256 + 128:(h + 1) * 256] = k_rope
        v_ref[:, h * 128:(h + 1) * 128] = kv[:, h * 256 + 128:(h + 1) * 256].astype(BF16)


def _mla_prep(proj, cos_t, sin_t, g_q, g_kv, wq, wkv):
    b, s, _ = proj.shape
    tm = min(512, s)
    const = lambda bi, i: (0, 0)
    return pl.pallas_call(
        _mla_prep_kernel,
        out_shape=(jax.ShapeDtypeStruct((b, s, N_HEADS * 256), BF16),
                   jax.ShapeDtypeStruct((b, s, N_HEADS * 256), BF16),
                   jax.ShapeDtypeStruct((b, s, BRANCH_WIDTH), BF16)),
        grid=(b, s // tm),
        in_specs=[pl.BlockSpec((None, tm, 1024), lambda bi, i: (bi, i, OFF_A // 1024)),
                  pl.BlockSpec((tm, 128), lambda bi, i: (i, 0)),
                  pl.BlockSpec((tm, 128), lambda bi, i: (i, 0)),
                  pl.BlockSpec((1, MLA_Q_RANK), const),
                  pl.BlockSpec((1, MLA_KV_RANK), const),
                  pl.BlockSpec(wq.shape, const),
                  pl.BlockSpec(wkv.shape, const)],
        out_specs=(pl.BlockSpec((None, tm, N_HEADS * 256), lambda bi, i: (bi, i, 0)),
                   pl.BlockSpec((None, tm, N_HEADS * 256), lambda bi, i: (bi, i, 0)),
                   pl.BlockSpec((None, tm, BRANCH_WIDTH), lambda bi, i: (bi, i, 0))),
        compiler_params=_cparams("arbitrary", "arbitrary"),
        name="mla_prep",
    )(proj, cos_t, sin_t, g_q.reshape(1, -1), g_kv.reshape(1, -1), wq, wkv)


def _bucket_np(n):
    max_exact = REL_BUCKETS // 2
    nf = np.maximum(n, max_exact).astype(np.float32)
    large = max_exact + (np.log(nf / np.float32(max_exact)) / np.float32(math.log(REL_MAX_DIST / max_exact))
                         * np.float32(REL_BUCKETS - max_exact)).astype(np.int32)
    return np.where(n < max_exact, n, np.minimum(large, REL_BUCKETS - 1)).astype(np.int32)


def _bucket_starts():
    buckets = _bucket_np(np.arange(2 * REL_MAX_DIST))
    return [int(np.argmax(buckets >= b)) for b in range(REL_BUCKETS)]


def _bank_kernel(tab_ref, o_ref, *, t, d_min, band):
    d = pl.program_id(0) + d_min
    dist = (t * d + lax.broadcasted_iota(jnp.int32, (t, t), 0) - lax.broadcasted_iota(jnp.int32, (t, t), 1))
    starts = _bucket_starts()
    ge = [dist >= starts[b] for b in range(1, REL_BUCKETS)]
    if band:
        mult = jnp.zeros((t, t), jnp.int32)
        for window, dil in DILATED_PATTERNS:
            mult += jnp.where((dist >= 0) & (dist <= window) & ((dist & (dil - 1)) == 0), 1, 0)
        logm = jnp.where(mult == 3, math.log(3.0), jnp.where(mult == 2, math.log(2.0), 0.0))
        keep = mult > 0
    else:
        logm = 0.0
        keep = dist >= 0
    for h in range(o_ref.shape[0]):
        val = jnp.full((t, t), tab_ref[0, h], F32)
        for b in range(1, REL_BUCKETS):
            val = jnp.where(ge[b - 1], tab_ref[b, h], val)
        o_ref[h] = jnp.where(keep, (val + logm) * LOG2E, NEG)


def _bank(tab, n_tables, t, d_min, band):
    nh = tab.shape[1]
    return pl.pallas_call(
        functools.partial(_bank_kernel, t=t, d_min=d_min, band=band),
        out_shape=jax.ShapeDtypeStruct((nh, n_tables, t, t), F32),
        grid=(n_tables,),
        in_specs=[pl.BlockSpec(memory_space=pltpu.SMEM)],
        out_specs=pl.BlockSpec((nh, None, t, t), lambda j: (0, j, 0, 0)),
        compiler_params=_cparams("arbitrary"),
        name="bank_band" if band else "bank_bias",
    )(tab)


def _first_far_diagonal(t):
    last = _bucket_starts()[REL_BUCKETS - 1]
    return -(-(last + t - 1) // t)


def _qk(q, k):
    return lax.dot_general(q, k, (((1,), (1,)), ((), ())), preferred_element_type=F32)


def _flash_init(m_ref, acc_ref):
    m_ref[...] = jnp.full(m_ref.shape, NEG, F32)
    acc_ref[...] = jnp.zeros(acc_ref.shape, F32)


def _flash_update(slot, s, v, m_ref, acc_ref):
    m_prev = m_ref[slot]
    m_new = jnp.maximum(m_prev, jnp.max(s, axis=-1, keepdims=True))
    alpha = jnp.exp2(m_prev - m_new)
    p = jnp.concatenate([jnp.exp2(s[:, j * 128:(j + 1) * 128] - m_new) for j in range(s.shape[1] // 128)],
                        axis=1).astype(BF16)
    v_ones = jnp.concatenate([v, jnp.ones_like(v)], axis=1)
    acc_ref[slot] = (jnp.concatenate([alpha, alpha], axis=1) * acc_ref[slot]
                     + jnp.dot(p, v_ones, preferred_element_type=F32))
    m_ref[slot] = m_new


def _flash_result(slot, acc_ref):
    acc = acc_ref[slot]
    return acc[:, :HEAD_DIM] / acc[:, HEAD_DIM:]


def _silu(g):
    return g * jax.nn.sigmoid(g)


def _tile(ref, kj, t, c0, width):
    return ref[pl.ds(pl.multiple_of(kj * t, t), t), c0:c0 + width]


def _bank_tile(bank_ref, h, i, kj):
    base = 2 * (i - kj) + 1
    top = jnp.concatenate([bank_ref[h, base], bank_ref[h, base - 1]], axis=1)
    bot = jnp.concatenate([bank_ref[h, base + 1], bank_ref[h, base]], axis=1)
    return jnp.concatenate([top, bot], axis=0)


def _emit(o_ref, g_ref, h, o):
    gate = g_ref[:, h * 128:(h + 1) * 128].astype(F32)
    o_ref[:, h * 128:(h + 1) * 128] = (o * _silu(gate)).astype(BF16)


def _walk(lo, hi, step, *args):
    def body(kj, carry):
        step(kj, *args)
        return carry
    lax.fori_loop(lo, hi, body, 0)


def _attn_a_kernel(q_ref, k_ref, v_ref, g_ref, o_ref, m_ref, acc_ref, *, t):
    i = pl.program_id(1)
    _flash_init(m_ref, acc_ref)

    def step(kj, masked):
        s_all = [_qk(q_ref[:, h * 256:(h + 1) * 256], _tile(k_ref, kj, t, h * 256, 256))
                 for h in range(N_HEADS)]
        for h in range(N_HEADS):
            s = s_all[h]
            if masked:
                causal = (lax.broadcasted_iota(jnp.int32, (t, t), 1)
                          <= lax.broadcasted_iota(jnp.int32, (t, t), 0))
                s = jnp.where(causal, s, NEG)
            _flash_update(h, s, _tile(v_ref, kj, t, h * 128, 128), m_ref, acc_ref)

    _walk(0, i, step, False)
    step(i, True)
    for h in range(N_HEADS):
        _emit(o_ref, g_ref, h, _flash_result(h, acc_ref))


def _attn_band_kernel(q_ref, k_ref, v_ref, g_ref, bank_ref, o_ref, m_ref, acc_ref, *, t, near):
    i = pl.program_id(1)
    _flash_init(m_ref, acc_ref)

    def step(kj):
        s_all = [_qk(q_ref[:, h * 128:(h + 1) * 128], _tile(k_ref, kj, t, h * 128, 128))
                 for h in range(N_HEADS)]
        for h in range(N_HEADS):
            s = s_all[h] + _bank_tile(bank_ref, h, i, kj)
            _flash_update(h, s, _tile(v_ref, kj, t, h * 128, 128), m_ref, acc_ref)

    _walk(jnp.maximum(i - (near - 1), 0), i + 1, step)
    for h in range(N_HEADS):
        _emit(o_ref, g_ref, h, _flash_result(h, acc_ref))


def _attn_sel_kernel(far_ref, q_ref, k_ref, v_ref, g_ref, bank_ref, sel_ref, o_ref, m_ref, acc_ref,
                     *, t, near):
    i = pl.program_id(1)
    n_far = jnp.maximum(i - (near - 1), 0)
    _flash_init(m_ref, acc_ref)

    def step(kj, is_near):
        s_all = [_qk(q_ref[:, h * 128:(h + 1) * 128], _tile(k_ref, kj, t, h * 128, 128))
                 for h in range(N_HEADS)]
        sel = sel_ref[:, pl.ds(pl.multiple_of(kj * t, t), t)].astype(F32)
        for h in range(N_HEADS):
            bias = _bank_tile(bank_ref, h, i, kj) if is_near else far_ref[h] * LOG2E
            _flash_update(h, s_all[h] + (sel + bias), _tile(v_ref, kj, t, h * 128, 128), m_ref, acc_ref)

    _walk(0, n_far, step, False)
    _walk(n_far, i + 1, step, True)
    for h in range(N_HEADS):
        _emit(o_ref, g_ref, h, _flash_result(h, acc_ref))


def _attn_diff_kernel(far_ref, q_ref, k_ref, v_ref, g_ref, bank_ref, lam_ref, gsub_ref, o_ref,
                      q1_ref, q2_ref, m_ref, acc_ref, *, t, near, lam_init):
    i = pl.program_id(1)
    n_far = jnp.maximum(i - (near - 1), 0)
    _flash_init(m_ref, acc_ref)
    lane = lax.broadcasted_iota(jnp.int32, q_ref.shape, 1) & (HEAD_DIM - 1)
    q = q_ref[...]
    q1_ref[...] = jnp.where(lane < HEAD_DIM // 2, q, jnp.zeros_like(q))
    q2_ref[...] = jnp.where(lane >= HEAD_DIM // 2, q, jnp.zeros_like(q))

    def step(kj, is_near):
        s_all = []
        for h in range(N_HEADS):
            k = _tile(k_ref, kj, t, h * 128, 128)
            s_all.append(_qk(q1_ref[:, h * 128:(h + 1) * 128], k))
            s_all.append(_qk(q2_ref[:, h * 128:(h + 1) * 128], k))
        for h in range(N_HEADS):
            v = _tile(v_ref, kj, t, h * 128, 128)
            bias = _bank_tile(bank_ref, h, i, kj) if is_near else far_ref[h] * LOG2E
            _flash_update(2 * h, s_all[2 * h] + bias, v, m_ref, acc_ref)
            _flash_update(2 * h + 1, s_all[2 * h + 1] + bias, v, m_ref, acc_ref)

    _walk(0, n_far, step, False)
    _walk(n_far, i + 1, step, True)
    lam_v = lam_ref[...]
    lam = (jnp.exp(jnp.sum(lam_v[0:1] * lam_v[1:2], axis=-1, keepdims=True))
           - jnp.exp(jnp.sum(lam_v[2:3] * lam_v[3:4], axis=-1, keepdims=True)) + lam_init)
    for h in range(N_HEADS):
        o = _flash_result(2 * h, acc_ref) - lam * _flash_result(2 * h + 1, acc_ref)
        _emit(o_ref, g_ref, h, _rms(o, gsub_ref[...]) * (1.0 - lam_init))


def _attention(kind, q_arr, k_arr, v_arr, proj, q_blk, gate_blk, extra_in=(), extra_specs=(),
               smem_in=(), **kw):
    b, s, _ = proj.shape
    t = min(ATT_TILE, s)
    dk = 256 if kind == "a" else 128
    qw = N_HEADS * dk
    k_blk = 0 if kind == "a" else q_blk + 1
    v_blk = 0 if kind == "a" else q_blk + 2
    body = {"a": _attn_a_kernel, "band": _attn_band_kernel, "sel": _attn_sel_kernel,
            "diff": _attn_diff_kernel}[kind]
    slots = 2 * N_HEADS if kind == "diff" else N_HEADS
    scratch = [pltpu.VMEM((slots, t, HEAD_DIM), F32), pltpu.VMEM((slots, t, 2 * HEAD_DIM), F32)]
    if kind == "diff":
        scratch = [pltpu.VMEM((t, qw), BF16), pltpu.VMEM((t, qw), BF16)] + scratch
    in_specs = [pl.BlockSpec(memory_space=pltpu.SMEM) for _ in smem_in]
    in_specs += [pl.BlockSpec((None, t, qw), lambda bi, i: (bi, i, q_blk)),
                 pl.BlockSpec((None, s, qw), lambda bi, i: (bi, 0, k_blk)),
                 pl.BlockSpec((None, s, BRANCH_WIDTH), lambda bi, i: (bi, 0, v_blk)),
                 pl.BlockSpec((None, t, BRANCH_WIDTH), lambda bi, i: (bi, i, gate_blk))]
    in_specs += list(extra_specs)
    return pl.pallas_call(
        functools.partial(body, t=t, **kw),
        out_shape=jax.ShapeDtypeStruct((b, s, BRANCH_WIDTH), BF16),
        grid=(b, s // t),
        in_specs=in_specs,
        out_specs=pl.BlockSpec((None, t, BRANCH_WIDTH), lambda bi, i: (bi, i, 0)),
        scratch_shapes=scratch,
        compiler_params=_cparams("arbitrary", "arbitrary"),
        name="attn_" + kind,
    )(*smem_in, q_arr, k_arr, v_arr, proj, *extra_in)


def _bit_transpose32(words):
    a = list(words)
    j, m = 16, 0x0000FFFF
    while j:
        for k in range(32):
            if not k & j:
                t = (a[k] ^ lax.shift_right_logical(a[k + j], jnp.int32(j))) & jnp.int32(m)
                a[k] = a[k] ^ t
                a[k + j] = a[k + j] ^ (t << j)
        j >>= 1
        m = (m ^ (m << j)) & 0xFFFFFFFF if j else m
    return a


def _select_kernel(qi_ref, ki_ref, wi_ref, o_ref, key_ref, plane_ref, alive_ref, *, tq, kc, n_sel):
    i = pl.program_id(1)
    s_len = o_ref.shape[1]
    n_ch = (i * tq + tq + kc - 1) // kc
    wpc = kc // 32
    int_min = jnp.int32(-2 ** 31)
    lane = lax.broadcasted_iota(jnp.int32, (tq, 128), 1)
    w_t = wi_ref[...].astype(F32).T
    q_heads = []
    for j in range(IDX_HEADS // 2):
        q2 = qi_ref[:, j * 128:(j + 1) * 128]
        q_heads.append(jnp.where(lane < IDX_DIM, q2, jnp.zeros_like(q2)))
        q_heads.append(jnp.where(lane >= IDX_DIM, q2, jnp.zeros_like(q2)))
    qpos = i * tq + lax.broadcasted_iota(jnp.int32, (kc, tq), 1)
    kiota = lax.broadcasted_iota(jnp.int32, (kc, tq), 0)

    def chunk(c):
        return pl.ds(pl.multiple_of(c * kc, kc), kc)

    def score_chunk(c, _):
        k = ki_ref[chunk(c), :]
        acc = jnp.zeros((kc, tq), F32)
        for hh in range(IDX_HEADS):
            acc = acc + jnp.maximum(_qk(k, q_heads[hh]), 0.0) * w_t[hh:hh + 1, :]
        acc = jnp.where(c * kc + kiota <= qpos, acc + 0.0, NEG)
        bits = pltpu.bitcast(acc, jnp.int32)
        keys = bits ^ ((bits >> 31) & jnp.int32(0x7FFFFFFF))
        key_ref[chunk(c), :] = keys
        ukeys = keys ^ int_min
        for blk in range(kc // 256):
            planes = _bit_transpose32([ukeys[blk * 256 + 8 * j:blk * 256 + 8 * j + 8, :] for j in range(32)])
            row = pl.multiple_of(c * wpc + blk * 8, 8)
            for b in range(32):
                plane_ref[b, pl.ds(row, 8), :] = planes[b]
        return 0

    lax.fori_loop(0, n_ch, score_chunk, 0)

    def clear_chunk(c, _):
        plane_ref[:, pl.ds(pl.multiple_of(c * wpc, wpc), wpc), :] = jnp.zeros((32, wpc, tq), jnp.int32)
        return 0

    lax.fori_loop(n_ch, s_len // kc, clear_chunk, 0)
    n_rows = s_len // 32
    word_row = lax.broadcasted_iota(jnp.int32, (n_rows, tq), 0)
    alive_ref[...] = jnp.where(word_row < n_ch * wpc, jnp.int32(-1), jnp.int32(0))

    def radix_step(bi, carry):
        thr, above = carry
        alive = alive_ref[...]
        plane = plane_ref[bi]
        ones = lax.population_count(alive & plane)
        ones = jnp.sum(jnp.sum(ones.reshape(n_rows // 8, 8, tq), axis=0), axis=0, keepdims=True)
        take = above + ones >= n_sel
        thr = jnp.where(take, thr | (jnp.int32(1) << (31 - bi)), thr)
        above = jnp.where(take, above, above + ones)
        alive_ref[...] = alive & (plane ^ jnp.where(take, jnp.int32(0), jnp.int32(-1)))
        return thr, above

    zeros = jnp.zeros((1, tq), jnp.int32)
    thr, _ = lax.fori_loop(0, 32, radix_step, (zeros, zeros))
    thr = thr ^ int_min

    def emit(c, _):
        keep = jnp.where(key_ref[chunk(c), :] >= thr, 0.0, NEG)
        o_ref[:, chunk(c)] = keep.T.astype(BF16)
        return 0

    lax.fori_loop(0, n_ch, emit, 0)

    def blank(c, _):
        o_ref[:, chunk(c)] = jnp.full((tq, kc), NEG, BF16)
        return 0

    lax.fori_loop(n_ch, s_len // kc, blank, 0)


def _select(proj):
    b, s, _ = proj.shape
    tq = min(256, s)
    kc = min(512, s)
    n_sel = min(TOPK_MAX, s // 4)
    return pl.pallas_call(
        functools.partial(_select_kernel, tq=tq, kc=kc, n_sel=n_sel),
        out_shape=jax.ShapeDtypeStruct((b, s, s), BF16),
        grid=(b, s // tq),
        in_specs=[pl.BlockSpec((None, tq, 1024), lambda bi, i: (bi, i, OFF_QIDX // 1024)),
                  pl.BlockSpec((None, s, 128), lambda bi, i: (bi, 0, OFF_KIDX // 128)),
                  pl.BlockSpec((None, tq, 128), lambda bi, i: (bi, i, OFF_WIDX // 128))],
        out_specs=pl.BlockSpec((None, tq, s), lambda bi, i: (bi, i, 0)),
        scratch_shapes=[pltpu.VMEM((s, tq), jnp.int32), pltpu.VMEM((32, s // 32, tq), jnp.int32),
                        pltpu.VMEM((s // 32, tq), jnp.int32)],
        compiler_params=_cparams("arbitrary", "arbitrary"),
        name="idx_select",
    )(proj, proj, proj)


def _out_kernel(a_ref, b_ref, c_ref, d_ref, w_ref, x_ref, gate_ref, g_ref, o_ref):
    y = jnp.dot(a_ref[...], w_ref[0:512, :], preferred_element_type=F32)
    y += jnp.dot(b_ref[...], w_ref[512:1024, :], preferred_element_type=F32)
    y += jnp.dot(c_ref[...], w_ref[1024:1536, :], preferred_element_type=F32)
    y += jnp.dot(d_ref[...], w_ref[1536:2048, :], preferred_element_type=F32)
    o_ref[...] = x_ref[...] + gate_ref[...] * _rms(y, g_ref[...])


def _out_proj(outs, w_out, x, mod3, g_post):
    b, s, d = x.shape
    tm = min(512, s)
    mix = lambda bi, i: (bi, i, 0)
    return pl.pallas_call(
        _out_kernel,
        out_shape=jax.ShapeDtypeStruct((b, s, d), F32),
        grid=(b, s // tm),
        in_specs=[pl.BlockSpec((None, tm, BRANCH_WIDTH), mix)] * 4
        + [pl.BlockSpec(w_out.shape, lambda bi, i: (0, 0)),
           pl.BlockSpec((None, tm, d), mix),
           pl.BlockSpec((None, 1, d), lambda bi, i: (bi, 0, 2)),
           pl.BlockSpec((1, d), lambda bi, i: (0, 0))],
        out_specs=pl.BlockSpec((None, tm, d), mix),
        compiler_params=_cparams("arbitrary", "arbitrary"),
        name="out_proj",
    )(*outs, w_out, x, mod3, g_post.reshape(1, d))


def _rope_tables(s):
    half = MLA_ROPE // 2
    inv = ROPE_THETA ** (-jnp.arange(half, dtype=F32) / half)
    ang = jnp.arange(s, dtype=F32)[:, None] * inv[None, :]
    z = jnp.zeros((s, 128 - MLA_ROPE), F32)
    cos, sin = jnp.cos(ang), jnp.sin(ang)
    return jnp.concatenate([cos, cos, z], axis=-1), jnp.concatenate([sin, sin, z], axis=-1)


def _rot_cols(w):
    half = w.shape[-1] // 2
    return jnp.concatenate([-w[..., half:], w[..., :half]], axis=-1)


def _layout_w_in(w):
    d = w.shape[0]
    names = (("a_cq", 384), ("a_ckv", 256), ("a_krope", 64), ("b_q", 512), ("b_k", 512), ("b_v", 512),
             ("c_q", 512), ("c_k", 512), ("c_v", 512), ("c_qidx", 1024), ("c_kidx", 64), ("c_widx", 16),
             ("d_q", 512), ("d_k", 512), ("d_v", 512), ("gate", 2048))
    p, start = {}, 0
    for name, width in names:
        p[name] = w[:, start:start + width]
        start += width
    z = lambda n: jnp.zeros((d, n), w.dtype)
    cols = [p["c_qidx"] * IDX_DIM ** -0.5,
            p["a_cq"], p["a_ckv"], p["a_krope"], z(64), _rot_cols(p["a_krope"]), z(64), p["c_kidx"], p["c_kidx"],
            p["b_q"] * (LOG2E * HEAD_DIM ** -0.5), p["b_k"], p["b_v"],
            p["c_q"] * (LOG2E * HEAD_DIM ** -0.5), p["c_k"], p["c_v"],
            p["d_q"] * (LOG2E * (HEAD_DIM // 2) ** -0.5), p["d_k"], p["d_v"],
            p["gate"],
            p["c_widx"] * IDX_HEADS ** -0.5, z(112)]
    out = jnp.concatenate(cols, axis=1).astype(BF16)
    assert out.shape[1] == PROJ_WIDTH
    return out


def _layout_w_uq(w):
    r = w.shape[0]
    w = w.reshape(r, N_HEADS, MLA_NOPE + MLA_ROPE) * (LOG2E * (MLA_NOPE + MLA_ROPE) ** -0.5)
    z = jnp.zeros((r, N_HEADS, 128 - MLA_ROPE), w.dtype)
    rope = w[..., MLA_NOPE:]
    return jnp.concatenate([w[..., :MLA_NOPE], rope, z, _rot_cols(rope), z], axis=-1).reshape(r, -1).astype(BF16)


def kernel(x, c, w_ada, b_ada, g_pre, g_post, w_in, g_q_a, w_uq_a, g_kv_a, w_ukv_a,
           lam_q1, lam_k1, lam_q2, lam_k2, g_sub_d, w_out, rel_bias):
    b, s, d = x.shape
    depth = w_ada.shape[0]
    t = min(ATT_TILE, s)
    half = t // 2
    nq = s // t
    near_bias = min(nq, -(-(_first_far_diagonal(half) + 1) // 2))
    near_band = min(nq, -(-(DILATED_PATTERNS[-1][0] // half + 1) // 2))

    cos_t, sin_t = _rope_tables(s)
    bank_b = _bank(rel_bias[:, 0:N_HEADS], 2 * near_band + 1, half, -1, True)
    bank_cd = _bank(rel_bias[:, N_HEADS:3 * N_HEADS], 2 * near_bias + 1, half, -1, False)
    bank_c, bank_d = bank_cd[:N_HEADS], bank_cd[N_HEADS:]
    far_c = rel_bias[REL_BUCKETS - 1, N_HEADS:2 * N_HEADS]
    far_d = rel_bias[REL_BUCKETS - 1, 2 * N_HEADS:3 * N_HEADS]
    bank_spec = lambda n: pl.BlockSpec((N_HEADS, n, half, half), lambda bi, i: (0, 0, 0, 0),
                                       pipeline_mode=pl.Buffered(1))

    mod = _ada_mod(c, w_ada, b_ada)
    for li in range(depth):
        mod3 = mod[li].reshape(b, 1, 3 * d)
        h = _prenorm(x, g_pre[li], mod3)
        proj = _in_proj(h.reshape(b * s, d), _layout_w_in(w_in[li])).reshape(b, s, PROJ_WIDTH)

        q_a, k_a, v_a = _mla_prep(proj, cos_t, sin_t, g_q_a[li], g_kv_a[li],
                                  _layout_w_uq(w_uq_a[li]), w_ukv_a[li].astype(BF16))
        gate0 = OFF_GATE // BRANCH_WIDTH
        out_a = _attention("a", q_a, k_a, v_a, proj, 0, gate0)
        out_b = _attention("band", proj, proj, proj, proj, OFF_B // BRANCH_WIDTH, gate0 + 1,
                           extra_in=(bank_b,), extra_specs=(bank_spec(2 * near_band + 1),), near=near_band)
        sel = _select(proj)
        out_c = _attention("sel", proj, proj, proj, proj, OFF_C // BRANCH_WIDTH, gate0 + 2,
                           extra_in=(bank_c, sel),
                           extra_specs=(bank_spec(2 * near_bias + 1),
                                        pl.BlockSpec((None, t, s), lambda bi, i: (bi, i, 0))),
                           smem_in=(far_c,), near=near_bias)
        lam_init = 0.8 - 0.6 * math.exp(-0.3 * li)
        lam_vecs = jnp.stack([lam_q1[li], lam_k1[li], lam_q2[li], lam_k2[li]])
        out_d = _attention("diff", proj, proj, proj, proj, OFF_D // BRANCH_WIDTH, gate0 + 3,
                           extra_in=(bank_d, lam_vecs, g_sub_d[li].reshape(1, HEAD_DIM)),
                           extra_specs=(bank_spec(2 * near_bias + 1),
                                        pl.BlockSpec(lam_vecs.shape, lambda bi, i: (0, 0)),
                                        pl.BlockSpec((1, HEAD_DIM), lambda bi, i: (0, 0))),
                           smem_in=(far_d,), near=near_bias, lam_init=lam_init)
        x = _out_proj((out_a, out_b, out_c, out_d), w_out[li].astype(BF16), x, mod3, g_post[li])
    return x
```

```python
import functools
import math

import numpy as np
import jax
import jax.numpy as jnp
from jax import lax
from jax.experimental import pallas as pl
from jax.experimental.pallas import tpu as pltpu

F32 = jnp.float32
BF16 = jnp.bfloat16

HEAD_DIM = 128
N_HEADS = 4
BRANCH_WIDTH = N_HEADS * HEAD_DIM
MIX_WIDTH = 4 * BRANCH_WIDTH
MLA_Q_RANK = 384
MLA_KV_RANK = 256
MLA_NOPE = 128
MLA_ROPE = 64
ROPE_THETA = 10000.0
DILATED_PATTERNS = ((128, 1), (512, 4), (2048, 16))
IDX_HEADS = 16
IDX_DIM = 64
TOPK_MAX = 256
REL_BUCKETS = 32
REL_MAX_DIST = 2048
NORM_EPS = 1e-6
NEG = -1e30
LOG2E = math.log2(math.e)

OFF_QIDX = 0
OFF_A = 1024
OFF_KIDX = OFF_A + 896
OFF_B = 2048
OFF_C = 3584
OFF_D = 5120
OFF_GATE = 6656
OFF_WIDX = 8704
PROJ_WIDTH = 8832

ATT_TILE = 512
VMEM_LIMIT = 56 * 1024 * 1024


def _cparams(*sem):
    return pltpu.CompilerParams(dimension_semantics=sem, vmem_limit_bytes=VMEM_LIMIT)


def _ada_kernel(c_ref, w_ref, b_ref, o_ref):
    c = c_ref[...]
    a = c * jax.nn.sigmoid(c)
    o_ref[...] = jnp.dot(a, w_ref[...], preferred_element_type=F32,
                         precision=lax.Precision.HIGHEST) + b_ref[...]


def _ada_mod(c, w_ada, b_ada):
    depth, d, n = w_ada.shape
    b = c.shape[0]
    tn = 768
    return pl.pallas_call(
        _ada_kernel,
        out_shape=jax.ShapeDtypeStruct((depth, b, n), F32),
        grid=(depth, n // tn),
        in_specs=[pl.BlockSpec((b, d), lambda l, j: (0, 0)),
                  pl.BlockSpec((None, d, tn), lambda l, j: (l, 0, j)),
                  pl.BlockSpec((None, 1, tn), lambda l, j: (l, 0, j))],
        out_specs=pl.BlockSpec((None, b, tn), lambda l, j: (l, 0, j)),
        compiler_params=_cparams("arbitrary", "arbitrary"),
        name="ada_mod",
    )(c, w_ada, b_ada.reshape(depth, 1, n))


def _prenorm_kernel(x_ref, g_ref, shift_ref, scale_ref, o_ref):
    x = x_ref[...]
    y = x * lax.rsqrt(jnp.mean(x * x, axis=-1, keepdims=True) + NORM_EPS) * g_ref[...]
    o_ref[...] = (y * (1.0 + scale_ref[...]) + shift_ref[...]).astype(BF16)


def _prenorm(x, g_pre, mod3):
    b, s, d = x.shape
    tm = min(512, s)
    return pl.pallas_call(
        _prenorm_kernel,
        out_shape=jax.ShapeDtypeStruct((b, s, d), BF16),
        grid=(b, s // tm),
        in_specs=[pl.BlockSpec((None, tm, d), lambda bi, i: (bi, i, 0)),
                  pl.BlockSpec((1, d), lambda bi, i: (0, 0)),
                  pl.BlockSpec((None, 1, d), lambda bi, i: (bi, 0, 0)),
                  pl.BlockSpec((None, 1, d), lambda bi, i: (bi, 0, 1))],
        out_specs=pl.BlockSpec((None, tm, d), lambda bi, i: (bi, i, 0)),
        compiler_params=_cparams("arbitrary", "arbitrary"),
        name="prenorm",
    )(x, g_pre.reshape(1, d), mod3, mod3)


def _matmul_kernel(a_ref, w_ref, o_ref):
    o_ref[...] = jnp.dot(a_ref[...], w_ref[...], preferred_element_type=F32).astype(o_ref.dtype)


def _in_proj(h2d, w):
    m, k = h2d.shape
    n = w.shape[1]
    tm = min(512, m)
    tn = n // 3
    return pl.pallas_call(
        _matmul_kernel,
        out_shape=jax.ShapeDtypeStruct((m, n), BF16),
        grid=(n // tn, m // tm),
        in_specs=[pl.BlockSpec((tm, k), lambda j, i: (i, 0)),
                  pl.BlockSpec((k, tn), lambda j, i: (0, j))],
        out_specs=pl.BlockSpec((tm, tn), lambda j, i: (i, j)),
        compiler_params=_cparams("arbitrary", "arbitrary"),
        name="in_proj",
    )(h2d, w)


def _rms(x, g):
    return x * lax.rsqrt(jnp.mean(x * x, axis=-1, keepdims=True) + NORM_EPS) * g


def _mla_prep_kernel(p_ref, cos_ref, sin_ref, gq_ref, gkv_ref, wq_ref, wkv_ref, q_ref, k_ref, v_ref):
    cos = cos_ref[...]
    sin = sin_ref[...]
    cq = _rms(p_ref[:, 0:MLA_Q_RANK].astype(F32), gq_ref[...]).astype(BF16)
    ckv = _rms(p_ref[:, MLA_Q_RANK:MLA_Q_RANK + MLA_KV_RANK].astype(F32), gkv_ref[...]).astype(BF16)
    q = jnp.dot(cq, wq_ref[...], preferred_element_type=F32)
    kv = jnp.dot(ckv, wkv_ref[...], preferred_element_type=F32)
    k_rope = (p_ref[:, 640:768].astype(F32) * cos + p_ref[:, 768:896].astype(F32) * sin).astype(BF16)
    for h in range(N_HEADS):
        qh = q[:, h * 384:(h + 1) * 384]
        q_ref[:, h * 256:h * 256 + 128] = qh[:, 0:128].astype(BF16)
        q_ref[:, h * 256 + 128:(h + 1) * 256] = (qh[:, 128:256] * cos + qh[:, 256:384] * sin).astype(BF16)
        k_ref[:, h * 256:h * 256 + 128] = kv[:, h * 256:h * 256 + 128].astype(BF16)
        k_ref[:, h * 256 + 128:(h + 1) * 256] = k_rope
        v_ref[:, h * 128:(h + 1) * 128] = kv[:, h * 256 + 128:(h + 1) * 256].astype(BF16)


def _mla_prep(proj, cos_t, sin_t, g_q, g_kv, wq, wkv):
    b, s, _ = proj.shape
    tm = min(512, s)
    const = lambda bi, i: (0, 0)
    return pl.pallas_call(
        _mla_prep_kernel,
        out_shape=(jax.ShapeDtypeStruct((b, s, N_HEADS * 256), BF16),
                   jax.ShapeDtypeStruct((b, s, N_HEADS * 256), BF16),
                   jax.ShapeDtypeStruct((b, s, BRANCH_WIDTH), BF16)),
        grid=(b, s // tm),
        in_specs=[pl.BlockSpec((None, tm, 1024), lambda bi, i: (bi, i, OFF_A // 1024)),
                  pl.BlockSpec((tm, 128), lambda bi, i: (i, 0)),
                  pl.BlockSpec((tm, 128), lambda bi, i: (i, 0)),
                  pl.BlockSpec((1, MLA_Q_RANK), const),
                  pl.BlockSpec((1, MLA_KV_RANK), const),
                  pl.BlockSpec(wq.shape, const),
                  pl.BlockSpec(wkv.shape, const)],
        out_specs=(pl.BlockSpec((None, tm, N_HEADS * 256), lambda bi, i: (bi, i, 0)),
                   pl.BlockSpec((None, tm, N_HEADS * 256), lambda bi, i: (bi, i, 0)),
                   pl.BlockSpec((None, tm, BRANCH_WIDTH), lambda bi, i: (bi, i, 0))),
        compiler_params=_cparams("arbitrary", "arbitrary"),
        name="mla_prep",
    )(proj, cos_t, sin_t, g_q.reshape(1, -1), g_kv.reshape(1, -1), wq, wkv)


def _bucket_np(n):
    max_exact = REL_BUCKETS // 2
    nf = np.maximum(n, max_exact).astype(np.float32)
    large = max_exact + (np.log(nf / np.float32(max_exact)) / np.float32(math.log(REL_MAX_DIST / max_exact))
                         * np.float32(REL_BUCKETS - max_exact)).astype(np.int32)
    return np.where(n < max_exact, n, np.minimum(large, REL_BUCKETS - 1)).astype(np.int32)


def _bucket_starts():
    buckets = _bucket_np(np.arange(2 * REL_MAX_DIST))
    return [int(np.argmax(buckets >= b)) for b in range(REL_BUCKETS)]


def _bank_kernel(tab_ref, o_ref, *, t, d_min, band):
    d = pl.program_id(0) + d_min
    dist = (t * d + lax.broadcasted_iota(jnp.int32, (t, t), 0) - lax.broadcasted_iota(jnp.int32, (t, t), 1))
    starts = _bucket_starts()
    ge = [dist >= starts[b] for b in range(1, REL_BUCKETS)]
    if band:
        mult = jnp.zeros((t, t), jnp.int32)
        for window, dil in DILATED_PATTERNS:
            mult += jnp.where((dist >= 0) & (dist <= window) & ((dist & (dil - 1)) == 0), 1, 0)
        logm = jnp.where(mult == 3, math.log(3.0), jnp.where(mult == 2, math.log(2.0), 0.0))
        keep = mult > 0
    else:
        logm = 0.0
        keep = dist >= 0
    for h in range(o_ref.shape[0]):
        val = jnp.full((t, t), tab_ref[0, h], F32)
        for b in range(1, REL_BUCKETS):
            val = jnp.where(ge[b - 1], tab_ref[b, h], val)
        o_ref[h] = jnp.where(keep, (val + logm) * LOG2E, NEG)


def _bank(tab, n_tables, t, d_min, band):
    nh = tab.shape[1]
    return pl.pallas_call(
        functools.partial(_bank_kernel, t=t, d_min=d_min, band=band),
        out_shape=jax.ShapeDtypeStruct((nh, n_tables, t, t), F32),
        grid=(n_tables,),
        in_specs=[pl.BlockSpec(memory_space=pltpu.SMEM)],
        out_specs=pl.BlockSpec((nh, None, t, t), lambda j: (0, j, 0, 0)),
        compiler_params=_cparams("arbitrary"),
        name="bank_band" if band else "bank_bias",
    )(tab)


def _first_far_diagonal(t):
    last = _bucket_starts()[REL_BUCKETS - 1]
    return -(-(last + t - 1) // t)


def _qk(q, k):
    return lax.dot_general(q, k, (((1,), (1,)), ((), ())), preferred_element_type=F32)


def _flash_init(m_ref, acc_ref):
    m_ref[...] = jnp.full(m_ref.shape, NEG, F32)
    acc_ref[...] = jnp.zeros(acc_ref.shape, F32)


def _flash_update(slot, s, v, m_ref, acc_ref):
    m_prev = m_ref[slot]
    m_new = jnp.maximum(m_prev, jnp.max(s, axis=-1, keepdims=True))
    alpha = jnp.exp2(m_prev - m_new)
    p = jnp.concatenate([jnp.exp2(s[:, j * 128:(j + 1) * 128] - m_new) for j in range(s.shape[1] // 128)],
                        axis=1).astype(BF16)
    v_ones = jnp.concatenate([v, jnp.ones_like(v)], axis=1)
    acc_ref[slot] = (jnp.concatenate([alpha, alpha], axis=1) * acc_ref[slot]
                     + jnp.dot(p, v_ones, preferred_element_type=F32))
    m_ref[slot] = m_new


def _flash_result(slot, acc_ref):
    acc = acc_ref[slot]
    return acc[:, :HEAD_DIM] / acc[:, HEAD_DIM:]


def _silu(g):
    return g * jax.nn.sigmoid(g)


def _tile(ref, kj, t, c0, width):
    return ref[pl.ds(pl.multiple_of(kj * t, t), t), c0:c0 + width]


def _bank_tile(bank_ref, h, i, kj):
    base = 2 * (i - kj) + 1
    top = jnp.concatenate([bank_ref[h, base], bank_ref[h, base - 1]], axis=1)
    bot = jnp.concatenate([bank_ref[h, base + 1], bank_ref[h, base]], axis=1)
    return jnp.concatenate([top, bot], axis=0)


def _emit(o_ref, g_ref, h, o):
    gate = g_ref[:, h * 128:(h + 1) * 128].astype(F32)
    o_ref[:, h * 128:(h + 1) * 128] = (o * _silu(gate)).astype(BF16)


def _walk(lo, hi, step, *args):
    def body(kj, carry):
        step(kj, *args)
        return carry
    lax.fori_loop(lo, hi, body, 0)


def _attn_a_kernel(q_ref, k_ref, v_ref, g_ref, o_ref, m_ref, acc_ref, *, t):
    i = pl.program_id(1)
    _flash_init(m_ref, acc_ref)

    def step(kj, masked):
        s_all = [_qk(q_ref[:, h * 256:(h + 1) * 256], _tile(k_ref, kj, t, h * 256, 256))
                 for h in range(N_HEADS)]
        for h in range(N_HEADS):
            s = s_all[h]
            if masked:
                causal = (lax.broadcasted_iota(jnp.int32, (t, t), 1)
                          <= lax.broadcasted_iota(jnp.int32, (t, t), 0))
                s = jnp.where(causal, s, NEG)
            _flash_update(h, s, _tile(v_ref, kj, t, h * 128, 128), m_ref, acc_ref)

    _walk(0, i, step, False)
    step(i, True)
    for h in range(N_HEADS):
        _emit(o_ref, g_ref, h, _flash_result(h, acc_ref))


def _attn_band_kernel(q_ref, k_ref, v_ref, g_ref, bank_ref, o_ref, m_ref, acc_ref, *, t, near):
    i = pl.program_id(1)
    _flash_init(m_ref, acc_ref)

    def step(kj):
        s_all = [_qk(q_ref[:, h * 128:(h + 1) * 128], _tile(k_ref, kj, t, h * 128, 128))
                 for h in range(N_HEADS)]
        for h in range(N_HEADS):
            s = s_all[h] + _bank_tile(bank_ref, h, i, kj)
            _flash_update(h, s, _tile(v_ref, kj, t, h * 128, 128), m_ref, acc_ref)

    _walk(jnp.maximum(i - (near - 1), 0), i + 1, step)
    for h in range(N_HEADS):
        _emit(o_ref, g_ref, h, _flash_result(h, acc_ref))


def _attn_sel_kernel(far_ref, q_ref, k_ref, v_ref, g_ref, bank_ref, sel_ref, o_ref, m_ref, acc_ref,
                     *, t, near):
    i = pl.program_id(1)
    n_far = jnp.maximum(i - (near - 1), 0)
    _flash_init(m_ref, acc_ref)

    def step(kj, is_near):
        s_all = [_qk(q_ref[:, h * 128:(h + 1) * 128], _tile(k_ref, kj, t, h * 128, 128))
                 for h in range(N_HEADS)]
        sel = sel_ref[:, pl.ds(pl.multiple_of(kj * t, t), t)].astype(F32)
        for h in range(N_HEADS):
            bias = _bank_tile(bank_ref, h, i, kj) if is_near else far_ref[h] * LOG2E
            _flash_update(h, s_all[h] + (sel + bias), _tile(v_ref, kj, t, h * 128, 128), m_ref, acc_ref)

    _walk(0, n_far, step, False)
    _walk(n_far, i + 1, step, True)
    for h in range(N_HEADS):
        _emit(o_ref, g_ref, h, _flash_result(h, acc_ref))


def _attn_diff_kernel(far_ref, q_ref, k_ref, v_ref, g_ref, bank_ref, lam_ref, gsub_ref, o_ref,
                      q1_ref, q2_ref, m_ref, acc_ref, *, t, near, lam_init):
    i = pl.program_id(1)
    n_far = jnp.maximum(i - (near - 1), 0)
    _flash_init(m_ref, acc_ref)
    lane = lax.broadcasted_iota(jnp.int32, q_ref.shape, 1) & (HEAD_DIM - 1)
    q = q_ref[...]
    q1_ref[...] = jnp.where(lane < HEAD_DIM // 2, q, jnp.zeros_like(q))
    q2_ref[...] = jnp.where(lane >= HEAD_DIM // 2, q, jnp.zeros_like(q))

    def step(kj, is_near):
        s_all = []
        for h in range(N_HEADS):
            k = _tile(k_ref, kj, t, h * 128, 128)
            s_all.append(_qk(q1_ref[:, h * 128:(h + 1) * 128], k))
            s_all.append(_qk(q2_ref[:, h * 128:(h + 1) * 128], k))
        for h in range(N_HEADS):
            v = _tile(v_ref, kj, t, h * 128, 128)
            bias = _bank_tile(bank_ref, h, i, kj) if is_near else far_ref[h] * LOG2E
            _flash_update(2 * h, s_all[2 * h] + bias, v, m_ref, acc_ref)
            _flash_update(2 * h + 1, s_all[2 * h + 1] + bias, v, m_ref, acc_ref)

    _walk(0, n_far, step, False)
    _walk(n_far, i + 1, step, True)
    lam_v = lam_ref[...]
    lam = (jnp.exp(jnp.sum(lam_v[0:1] * lam_v[1:2], axis=-1, keepdims=True))
           - jnp.exp(jnp.sum(lam_v[2:3] * lam_v[3:4], axis=-1, keepdims=True)) + lam_init)
    for h in range(N_HEADS):
        o = _flash_result(2 * h, acc_ref) - lam * _flash_result(2 * h + 1, acc_ref)
        _emit(o_ref, g_ref, h, _rms(o, gsub_ref[...]) * (1.0 - lam_init))


def _attention(kind, q_arr, k_arr, v_arr, proj, q_blk, gate_blk, extra_in=(), extra_specs=(),
               smem_in=(), **kw):
    b, s, _ = proj.shape
    t = min(ATT_TILE, s)
    dk = 256 if kind == "a" else 128
    qw = N_HEADS * dk
    k_blk = 0 if kind == "a" else q_blk + 1
    v_blk = 0 if kind == "a" else q_blk + 2
    body = {"a": _attn_a_kernel, "band": _attn_band_kernel, "sel": _attn_sel_kernel,
            "diff": _attn_diff_kernel}[kind]
    slots = 2 * N_HEADS if kind == "diff" else N_HEADS
    scratch = [pltpu.VMEM((slots, t, HEAD_DIM), F32), pltpu.VMEM((slots, t, 2 * HEAD_DIM), F32)]
    if kind == "diff":
        scratch = [pltpu.VMEM((t, qw), BF16), pltpu.VMEM((t, qw), BF16)] + scratch
    in_specs = [pl.BlockSpec(memory_space=pltpu.SMEM) for _ in smem_in]
    in_specs += [pl.BlockSpec((None, t, qw), lambda bi, i: (bi, i, q_blk)),
                 pl.BlockSpec((None, s, qw), lambda bi, i: (bi, 0, k_blk)),
                 pl.BlockSpec((None, s, BRANCH_WIDTH), lambda bi, i: (bi, 0, v_blk)),
                 pl.BlockSpec((None, t, BRANCH_WIDTH), lambda bi, i: (bi, i, gate_blk))]
    in_specs += list(extra_specs)
    return pl.pallas_call(
        functools.partial(body, t=t, **kw),
        out_shape=jax.ShapeDtypeStruct((b, s, BRANCH_WIDTH), BF16),
        grid=(b, s // t),
        in_specs=in_specs,
        out_specs=pl.BlockSpec((None, t, BRANCH_WIDTH), lambda bi, i: (bi, i, 0)),
        scratch_shapes=scratch,
        compiler_params=_cparams("arbitrary", "arbitrary"),
        name="attn_" + kind,
    )(*smem_in, q_arr, k_arr, v_arr, proj, *extra_in)


def _bit_transpose32(words):
    a = list(words)
    j, m = 16, 0x0000FFFF
    while j:
        for k in range(32):
            if not k & j:
                t = (a[k] ^ lax.shift_right_logical(a[k + j], jnp.int32(j))) & jnp.int32(m)
                a[k] = a[k] ^ t
                a[k + j] = a[k + j] ^ (t << j)
        j >>= 1
        m = (m ^ (m << j)) & 0xFFFFFFFF if j else m
    return a


def _select_kernel(qi_ref, ki_ref, wi_ref, o_ref, key_ref, plane_ref, alive_ref, *, tq, kc, n_sel):
    i = pl.program_id(1)
    s_len = o_ref.shape[1]
    n_ch = (i * tq + tq + kc - 1) // kc
    wpc = kc // 32
    int_min = jnp.int32(-2 ** 31)
    lane = lax.broadcasted_iota(jnp.int32, (tq, 128), 1)
    w_t = wi_ref[...].astype(F32).T
    q_heads = []
    for j in range(IDX_HEADS // 2):
        q2 = qi_ref[:, j * 128:(j + 1) * 128]
        q_heads.append(jnp.where(lane < IDX_DIM, q2, jnp.zeros_like(q2)))
        q_heads.append(jnp.where(lane >= IDX_DIM, q2, jnp.zeros_like(q2)))
    qpos = i * tq + lax.broadcasted_iota(jnp.int32, (kc, tq), 1)
    kiota = lax.broadcasted_iota(jnp.int32, (kc, tq), 0)

    def chunk(c):
        return pl.ds(pl.multiple_of(c * kc, kc), kc)

    def score_chunk(c, _):
        k = ki_ref[chunk(c), :]
        acc = jnp.zeros((kc, tq), F32)
        for hh in range(IDX_HEADS):
            acc = acc + jnp.maximum(_qk(k, q_heads[hh]), 0.0) * w_t[hh:hh + 1, :]
        acc = jnp.where(c * kc + kiota <= qpos, acc + 0.0, NEG)
        bits = pltpu.bitcast(acc, jnp.int32)
        keys = bits ^ ((bits >> 31) & jnp.int32(0x7FFFFFFF))
        key_ref[chunk(c), :] = keys
        ukeys = keys ^ int_min
        for blk in range(kc // 256):
            planes = _bit_transpose32([ukeys[blk * 256 + 8 * j:blk * 256 + 8 * j + 8, :] for j in range(32)])
            row = pl.multiple_of(c * wpc + blk * 8, 8)
            for b in range(32):
                plane_ref[b, pl.ds(row, 8), :] = planes[b]
        return 0

    lax.fori_loop(0, n_ch, score_chunk, 0)

    def clear_chunk(c, _):
        plane_ref[:, pl.ds(pl.multiple_of(c * wpc, wpc), wpc), :] = jnp.zeros((32, wpc, tq), jnp.int32)
        return 0

    lax.fori_loop(n_ch, s_len // kc, clear_chunk, 0)
    n_rows = s_len // 32
    word_row = lax.broadcasted_iota(jnp.int32, (n_rows, tq), 0)
    alive_ref[...] = jnp.where(word_row < n_ch * wpc, jnp.int32(-1), jnp.int32(0))

    def radix_step(bi, carry):
        thr, above = carry
        alive = alive_ref[...]
        plane = plane_ref[bi]
        ones = lax.population_count(alive & plane)
        ones = jnp.sum(jnp.sum(ones.reshape(n_rows // 8, 8, tq), axis=0), axis=0, keepdims=True)
        take = above + ones >= n_sel
        thr = jnp.where(take, thr | (jnp.int32(1) << (31 - bi)), thr)
        above = jnp.where(take, above, above + ones)
        alive_ref[...] = alive & (plane ^ jnp.where(take, jnp.int32(0), jnp.int32(-1)))
        return thr, above

    zeros = jnp.zeros((1, tq), jnp.int32)
    thr, _ = lax.fori_loop(0, 32, radix_step, (zeros, zeros))
    thr = thr ^ int_min

    def emit(c, _):
        keep = jnp.where(key_ref[chunk(c), :] >= thr, 0.0, NEG)
        o_ref[:, chunk(c)] = keep.T.astype(BF16)
        return 0

    lax.fori_loop(0, n_ch, emit, 0)

    def blank(c, _):
        o_ref[:, chunk(c)] = jnp.full((tq, kc), NEG, BF16)
        return 0

    lax.fori_loop(n_ch, s_len // kc, blank, 0)


def _select(proj):
    b, s, _ = proj.shape
    tq = min(256, s)
    kc = min(512, s)
    n_sel = min(TOPK_MAX, s // 4)
    return pl.pallas_call(
        functools.partial(_select_kernel, tq=tq, kc=kc, n_sel=n_sel),
        out_shape=jax.ShapeDtypeStruct((b, s, s), BF16),
        grid=(b, s // tq),
        in_specs=[pl.BlockSpec((None, tq, 1024), lambda bi, i: (bi, i, OFF_QIDX // 1024)),
                  pl.BlockSpec((None, s, 128), lambda bi, i: (bi, 0, OFF_KIDX // 128)),
                  pl.BlockSpec((None, tq, 128), lambda bi, i: (bi, i, OFF_WIDX // 128))],
        out_specs=pl.BlockSpec((None, tq, s), lambda bi, i: (bi, i, 0)),
        scratch_shapes=[pltpu.VMEM((s, tq), jnp.int32), pltpu.VMEM((32, s // 32, tq), jnp.int32),
                        pltpu.VMEM((s // 32, tq), jnp.int32)],
        compiler_params=_cparams("arbitrary", "arbitrary"),
        name="idx_select",
    )(proj, proj, proj)


def _out_kernel(a_ref, b_ref, c_ref, d_ref, w_ref, x_ref, gate_ref, g_ref, o_ref, wb_ref):
    @pl.when((pl.program_id(0) == 0) & (pl.program_id(1) == 0))
    def _():
        wb_ref[...] = w_ref[...].astype(BF16)

    y = jnp.dot(a_ref[...], wb_ref[0:512, :], preferred_element_type=F32)
    y += jnp.dot(b_ref[...], wb_ref[512:1024, :], preferred_element_type=F32)
    y += jnp.dot(c_ref[...], wb_ref[1024:1536, :], preferred_element_type=F32)
    y += jnp.dot(d_ref[...], wb_ref[1536:2048, :], preferred_element_type=F32)
    o_ref[...] = x_ref[...] + gate_ref[...] * _rms(y, g_ref[...])


def _out_proj(outs, w_out, li, x, mod3, g_post):
    b, s, d = x.shape
    tm = min(512, s)
    mix = lambda bi, i: (bi, i, 0)
    return pl.pallas_call(
        _out_kernel,
        out_shape=jax.ShapeDtypeStruct((b, s, d), F32),
        grid=(b, s // tm),
        in_specs=[pl.BlockSpec((None, tm, BRANCH_WIDTH), mix)] * 4
        + [pl.BlockSpec((None,) + w_out.shape[1:], lambda bi, i: (li, 0, 0), pipeline_mode=pl.Buffered(1)),
           pl.BlockSpec((None, tm, d), mix),
           pl.BlockSpec((None, 1, d), lambda bi, i: (bi, 0, 2)),
           pl.BlockSpec((1, d), lambda bi, i: (0, 0))],
        out_specs=pl.BlockSpec((None, tm, d), mix),
        scratch_shapes=[pltpu.VMEM(w_out.shape[1:], BF16)],
        compiler_params=_cparams("arbitrary", "arbitrary"),
        name="out_proj",
    )(*outs, w_out, x, mod3, g_post.reshape(1, d))


def _rope_tables(s):
    half = MLA_ROPE // 2
    inv = ROPE_THETA ** (-jnp.arange(half, dtype=F32) / half)
    ang = jnp.arange(s, dtype=F32)[:, None] * inv[None, :]
    z = jnp.zeros((s, 128 - MLA_ROPE), F32)
    cos, sin = jnp.cos(ang), jnp.sin(ang)
    return jnp.concatenate([cos, cos, z], axis=-1), jnp.concatenate([sin, sin, z], axis=-1)


def _rot_cols(w):
    half = w.shape[-1] // 2
    return jnp.concatenate([-w[..., half:], w[..., :half]], axis=-1)


IN_SPLITS = (("a_cq", 384), ("a_ckv", 256), ("a_krope", 64), ("b_q", 512), ("b_k", 512), ("b_v", 512),
             ("c_q", 512), ("c_k", 512), ("c_v", 512), ("c_qidx", 1024), ("c_kidx", 64), ("c_widx", 16),
             ("d_q", 512), ("d_k", 512), ("d_v", 512), ("gate", 2048))
IN_WIDTH = sum(width for _, width in IN_SPLITS)


def _layout_w_in_kernel(w_ref, o_ref):
    src, start = {}, 0
    for name, width in IN_SPLITS:
        src[name] = start
        start += width
    rows = w_ref.shape[0]

    def col(name, width, offset=0):
        a = src[name] + offset
        return w_ref[:, a:a + width]

    def put(dst, val):
        o_ref[:, dst:dst + val.shape[1]] = val.astype(BF16)

    z64 = jnp.zeros((rows, 64), F32)
    half = MLA_ROPE // 2
    put(OFF_QIDX, col("c_qidx", 1024) * IDX_DIM ** -0.5)
    put(OFF_A, col("a_cq", MLA_Q_RANK + MLA_KV_RANK))
    put(OFF_A + 640, jnp.concatenate(
        [col("a_krope", MLA_ROPE), z64, -col("a_krope", half, half), col("a_krope", half), z64,
         col("c_kidx", IDX_DIM), col("c_kidx", IDX_DIM)], axis=1))
    for off, name, dim in ((OFF_B, "b", HEAD_DIM), (OFF_C, "c", HEAD_DIM), (OFF_D, "d", HEAD_DIM // 2)):
        put(off, col(name + "_q", BRANCH_WIDTH) * (LOG2E * dim ** -0.5))
        put(off + BRANCH_WIDTH, col(name + "_k", 2 * BRANCH_WIDTH))
    put(OFF_GATE, col("gate", MIX_WIDTH))
    put(OFF_WIDX, jnp.concatenate([col("c_widx", IDX_HEADS) * IDX_HEADS ** -0.5,
                                   jnp.zeros((rows, 128 - IDX_HEADS), F32)], axis=1))


def _layout_w_in(w_in, li):
    d = w_in.shape[1]
    tr = 256
    return pl.pallas_call(
        _layout_w_in_kernel,
        out_shape=jax.ShapeDtypeStruct((d, PROJ_WIDTH), BF16),
        grid=(d // tr,),
        in_specs=[pl.BlockSpec((None, tr, IN_WIDTH), lambda i: (li, i, 0))],
        out_specs=pl.BlockSpec((tr, PROJ_WIDTH), lambda i: (i, 0)),
        compiler_params=_cparams("arbitrary"),
        name="layout_w_in",
    )(w_in)


def _layout_w_uq(w):
    r = w.shape[0]
    w = w.reshape(r, N_HEADS, MLA_NOPE + MLA_ROPE) * (LOG2E * (MLA_NOPE + MLA_ROPE) ** -0.5)
    z = jnp.zeros((r, N_HEADS, 128 - MLA_ROPE), w.dtype)
    rope = w[..., MLA_NOPE:]
    return jnp.concatenate([w[..., :MLA_NOPE], rope, z, _rot_cols(rope), z], axis=-1).reshape(r, -1).astype(BF16)


def kernel(x, c, w_ada, b_ada, g_pre, g_post, w_in, g_q_a, w_uq_a, g_kv_a, w_ukv_a,
           lam_q1, lam_k1, lam_q2, lam_k2, g_sub_d, w_out, rel_bias):
    b, s, d = x.shape
    depth = w_ada.shape[0]
    t = min(ATT_TILE, s)
    half = t // 2
    nq = s // t
    near_bias = min(nq, -(-(_first_far_diagonal(half) + 1) // 2))
    near_band = min(nq, -(-(DILATED_PATTERNS[-1][0] // half + 1) // 2))

    cos_t, sin_t = _rope_tables(s)
    bank_b = _bank(rel_bias[:, 0:N_HEADS], 2 * near_band + 1, half, -1, True)
    bank_cd = _bank(rel_bias[:, N_HEADS:3 * N_HEADS], 2 * near_bias + 1, half, -1, False)
    far_c = rel_bias[REL_BUCKETS - 1, N_HEADS:2 * N_HEADS]
    far_d = rel_bias[REL_BUCKETS - 1, 2 * N_HEADS:3 * N_HEADS]
    bank_spec = lambda n, group=0: pl.BlockSpec((N_HEADS, n, half, half), lambda bi, i: (group, 0, 0, 0),
                                                pipeline_mode=pl.Buffered(1))

    mod = _ada_mod(c, w_ada, b_ada)
    for li in range(depth):
        mod3 = mod[li].reshape(b, 1, 3 * d)
        h = _prenorm(x, g_pre[li], mod3)
        proj = _in_proj(h.reshape(b * s, d), _layout_w_in(w_in, li)).reshape(b, s, PROJ_WIDTH)

        q_a, k_a, v_a = _mla_prep(proj, cos_t, sin_t, g_q_a[li], g_kv_a[li],
                                  _layout_w_uq(w_uq_a[li]), w_ukv_a[li].astype(BF16))
        gate0 = OFF_GATE // BRANCH_WIDTH
        out_a = _attention("a", q_a, k_a, v_a, proj, 0, gate0)
        out_b = _attention("band", proj, proj, proj, proj, OFF_B // BRANCH_WIDTH, gate0 + 1,
                           extra_in=(bank_b,), extra_specs=(bank_spec(2 * near_band + 1),), near=near_band)
        sel = _select(proj)
        out_c = _attention("sel", proj, proj, proj, proj, OFF_C // BRANCH_WIDTH, gate0 + 2,
                           extra_in=(bank_cd, sel),
                           extra_specs=(bank_spec(2 * near_bias + 1, 0),
                                        pl.BlockSpec((None, t, s), lambda bi, i: (bi, i, 0))),
                           smem_in=(far_c,), near=near_bias)
        lam_init = 0.8 - 0.6 * math.exp(-0.3 * li)
        lam_vecs = jnp.stack([lam_q1[li], lam_k1[li], lam_q2[li], lam_k2[li]])
        out_d = _attention("diff", proj, proj, proj, proj, OFF_D // BRANCH_WIDTH, gate0 + 3,
                           extra_in=(bank_cd, lam_vecs, g_sub_d[li].reshape(1, HEAD_DIM)),
                           extra_specs=(bank_spec(2 * near_bias + 1, 1),
                                        pl.BlockSpec(lam_vecs.shape, lambda bi, i: (0, 0)),
                                        pl.BlockSpec((1, HEAD_DIM), lambda bi, i: (0, 0))),
                           smem_in=(far_d,), near=near_bias, lam_init=lam_init)
        x = _out_proj((out_a, out_b, out_c, out_d), w_out, li, x, mod3, g_post[li])
    return x
```

```python
import functools
import math

import numpy as np
import jax
import jax.numpy as jnp
from jax import lax
from jax.experimental import pallas as pl
from jax.experimental.pallas import tpu as pltpu

F32 = jnp.float32
BF16 = jnp.bfloat16

HEAD_DIM = 128
N_HEADS = 4
BRANCH_WIDTH = N_HEADS * HEAD_DIM
MIX_WIDTH = 4 * BRANCH_WIDTH
MLA_Q_RANK = 384
MLA_KV_RANK = 256
MLA_NOPE = 128
MLA_ROPE = 64
ROPE_THETA = 10000.0
DILATED_PATTERNS = ((128, 1), (512, 4), (2048, 16))
IDX_HEADS = 16
IDX_DIM = 64
TOPK_MAX = 256
REL_BUCKETS = 32
REL_MAX_DIST = 2048
NORM_EPS = 1e-6
NEG = -1e30
LOG2E = math.log2(math.e)

OFF_QIDX = 0
OFF_A = 1024
OFF_KIDX = OFF_A + 896
OFF_B = 2048
OFF_C = 3584
OFF_D = 5120
OFF_GATE = 6656
OFF_WIDX = 8704
PROJ_WIDTH = 8832

ATT_TILE = 512
VMEM_LIMIT = 56 * 1024 * 1024


def _cparams(*sem):
    return pltpu.CompilerParams(dimension_semantics=sem, vmem_limit_bytes=VMEM_LIMIT)


def _ada_kernel(c_ref, w_ref, b_ref, o_ref):
    c = c_ref[...]
    a = c * jax.nn.sigmoid(c)
    o_ref[...] = jnp.dot(a, w_ref[...], preferred_element_type=F32,
                         precision=lax.Precision.HIGHEST) + b_ref[...]


def _ada_mod(c, w_ada, b_ada):
    depth, d, n = w_ada.shape
    b = c.shape[0]
    tn = 768
    return pl.pallas_call(
        _ada_kernel,
        out_shape=jax.ShapeDtypeStruct((depth, b, n), F32),
        grid=(depth, n // tn),
        in_specs=[pl.BlockSpec((b, d), lambda l, j: (0, 0)),
                  pl.BlockSpec((None, d, tn), lambda l, j: (l, 0, j)),
                  pl.BlockSpec((None, 1, tn), lambda l, j: (l, 0, j))],
        out_specs=pl.BlockSpec((None, b, tn), lambda l, j: (l, 0, j)),
        compiler_params=_cparams("arbitrary", "arbitrary"),
        name="ada_mod",
    )(c, w_ada, b_ada.reshape(depth, 1, n))


def _prenorm_kernel(x_ref, g_ref, shift_ref, scale_ref, o_ref):
    x = x_ref[...]
    y = x * lax.rsqrt(jnp.mean(x * x, axis=-1, keepdims=True) + NORM_EPS) * g_ref[...]
    o_ref[...] = (y * (1.0 + scale_ref[...]) + shift_ref[...]).astype(BF16)


def _prenorm(x, g_pre, mod3):
    b, s, d = x.shape
    tm = min(512, s)
    return pl.pallas_call(
        _prenorm_kernel,
        out_shape=jax.ShapeDtypeStruct((b, s, d), BF16),
        grid=(b, s // tm),
        in_specs=[pl.BlockSpec((None, tm, d), lambda bi, i: (bi, i, 0)),
                  pl.BlockSpec((1, d), lambda bi, i: (0, 0)),
                  pl.BlockSpec((None, 1, d), lambda bi, i: (bi, 0, 0)),
                  pl.BlockSpec((None, 1, d), lambda bi, i: (bi, 0, 1))],
        out_specs=pl.BlockSpec((None, tm, d), lambda bi, i: (bi, i, 0)),
        compiler_params=_cparams("arbitrary", "arbitrary"),
        name="prenorm",
    )(x, g_pre.reshape(1, d), mod3, mod3)


def _matmul_kernel(a_ref, w_ref, o_ref):
    o_ref[...] = jnp.dot(a_ref[...], w_ref[...], preferred_element_type=F32).astype(o_ref.dtype)


def _in_proj(h2d, w):
    m, k = h2d.shape
    n = w.shape[1]
    tm = min(512, m)
    tn = n // 3
    return pl.pallas_call(
        _matmul_kernel,
        out_shape=jax.ShapeDtypeStruct((m, n), BF16),
        grid=(n // tn, m // tm),
        in_specs=[pl.BlockSpec((tm, k), lambda j, i: (i, 0)),
                  pl.BlockSpec((k, tn), lambda j, i: (0, j))],
        out_specs=pl.BlockSpec((tm, tn), lambda j, i: (i, j)),
        compiler_params=_cparams("arbitrary", "arbitrary"),
        name="in_proj",
    )(h2d, w)


def _rms(x, g):
    return x * lax.rsqrt(jnp.mean(x * x, axis=-1, keepdims=True) + NORM_EPS) * g


def _mla_prep_kernel(p_ref, cos_ref, sin_ref, gq_ref, gkv_ref, wq_ref, wkv_ref, q_ref, k_ref, v_ref):
    cos = cos_ref[...]
    sin = sin_ref[...]
    cq = _rms(p_ref[:, 0:MLA_Q_RANK].astype(F32), gq_ref[...]).astype(BF16)
    ckv = _rms(p_ref[:, MLA_Q_RANK:MLA_Q_RANK + MLA_KV_RANK].astype(F32), gkv_ref[...]).astype(BF16)
    q = jnp.dot(cq, wq_ref[...], preferred_element_type=F32)
    kv = jnp.dot(ckv, wkv_ref[...], preferred_element_type=F32)
    k_rope = (p_ref[:, 640:768].astype(F32) * cos + p_ref[:, 768:896].astype(F32) * sin).astype(BF16)
    for h in range(N_HEADS):
        qh = q[:, h * 384:(h + 1) * 384]
        q_ref[:, h * 256:h * 256 + 128] = qh[:, 0:128].astype(BF16)
        q_ref[:, h * 256 + 128:(h + 1) * 256] = (qh[:, 128:256] * cos + qh[:, 256:384] * sin).astype(BF16)
        k_ref[:, h * 256:h * 256 + 128] = kv[:, h * 256:h * 256 + 128].astype(BF16)
        k_ref[:, h * 256 + 128:(h + 1) * 256] = k_rope
        v_ref[:, h * 128:(h + 1) * 128] = kv[:, h * 256 + 128:(h + 1) * 256].astype(BF16)


def _mla_prep(proj, cos_t, sin_t, g_q, g_kv, wq, wkv):
    b, s, _ = proj.shape
    tm = min(512, s)
    const = lambda bi, i: (0, 0)
    return pl.pallas_call(
        _mla_prep_kernel,
        out_shape=(jax.ShapeDtypeStruct((b, s, N_HEADS * 256), BF16),
                   jax.ShapeDtypeStruct((b, s, N_HEADS * 256), BF16),
                   jax.ShapeDtypeStruct((b, s, BRANCH_WIDTH), BF16)),
        grid=(b, s // tm),
        in_specs=[pl.BlockSpec((None, tm, 1024), lambda bi, i: (bi, i, OFF_A // 1024)),
                  pl.BlockSpec((tm, 128), lambda bi, i: (i, 0)),
                  pl.BlockSpec((tm, 128), lambda bi, i: (i, 0)),
                  pl.BlockSpec((1, MLA_Q_RANK), const),
                  pl.BlockSpec((1, MLA_KV_RANK), const),
                  pl.BlockSpec(wq.shape, const),
                  pl.BlockSpec(wkv.shape, const)],
        out_specs=(pl.BlockSpec((None, tm, N_HEADS * 256), lambda bi, i: (bi, i, 0)),
                   pl.BlockSpec((None, tm, N_HEADS * 256), lambda bi, i: (bi, i, 0)),
                   pl.BlockSpec((None, tm, BRANCH_WIDTH), lambda bi, i: (bi, i, 0))),
        compiler_params=_cparams("arbitrary", "arbitrary"),
        name="mla_prep",
    )(proj, cos_t, sin_t, g_q.reshape(1, -1), g_kv.reshape(1, -1), wq, wkv)


def _bucket_np(n):
    max_exact = REL_BUCKETS // 2
    nf = np.maximum(n, max_exact).astype(np.float32)
    large = max_exact + (np.log(nf / np.float32(max_exact)) / np.float32(math.log(REL_MAX_DIST / max_exact))
                         * np.float32(REL_BUCKETS - max_exact)).astype(np.int32)
    return np.where(n < max_exact, n, np.minimum(large, REL_BUCKETS - 1)).astype(np.int32)


def _bucket_starts():
    buckets = _bucket_np(np.arange(2 * REL_MAX_DIST))
    return [int(np.argmax(buckets >= b)) for b in range(REL_BUCKETS)]


def _bank_kernel(tab_ref, o_ref, *, t, d_min, band):
    d = pl.program_id(0) + d_min
    dist = (t * d + lax.broadcasted_iota(jnp.int32, (t, t), 0) - lax.broadcasted_iota(jnp.int32, (t, t), 1))
    starts = _bucket_starts()
    ge = [dist >= starts[b] for b in range(1, REL_BUCKETS)]
    if band:
        mult = jnp.zeros((t, t), jnp.int32)
        for window, dil in DILATED_PATTERNS:
            mult += jnp.where((dist >= 0) & (dist <= window) & ((dist & (dil - 1)) == 0), 1, 0)
        logm = jnp.where(mult == 3, math.log(3.0), jnp.where(mult == 2, math.log(2.0), 0.0))
        keep = mult > 0
    else:
        logm = 0.0
        keep = dist >= 0
    for h in range(o_ref.shape[0]):
        val = jnp.full((t, t), tab_ref[0, h], F32)
        for b in range(1, REL_BUCKETS):
            val = jnp.where(ge[b - 1], tab_ref[b, h], val)
        o_ref[h] = jnp.where(keep, (val + logm) * LOG2E, NEG)


def _bank(tab, n_tables, t, d_min, band):
    nh = tab.shape[1]
    return pl.pallas_call(
        functools.partial(_bank_kernel, t=t, d_min=d_min, band=band),
        out_shape=jax.ShapeDtypeStruct((nh, n_tables, t, t), F32),
        grid=(n_tables,),
        in_specs=[pl.BlockSpec(memory_space=pltpu.SMEM)],
        out_specs=pl.BlockSpec((nh, None, t, t), lambda j: (0, j, 0, 0)),
        compiler_params=_cparams("arbitrary"),
        name="bank_band" if band else "bank_bias",
    )(tab)


def _first_far_diagonal(t):
    last = _bucket_starts()[REL_BUCKETS - 1]
    return -(-(last + t - 1) // t)


def _qk(q, k):
    return lax.dot_general(q, k, (((1,), (1,)), ((), ())), preferred_element_type=F32)


def _flash_init(m_ref, acc_ref):
    m_ref[...] = jnp.full(m_ref.shape, NEG, F32)
    acc_ref[...] = jnp.zeros(acc_ref.shape, F32)


def _flash_update(slot, s, v, m_ref, acc_ref, s_ref, next_scores):
    m_prev = m_ref[slot]
    m_new = jnp.maximum(m_prev, jnp.max(s, axis=-1, keepdims=True))
    alpha = jnp.exp2(m_prev - m_new)
    p = jnp.concatenate([jnp.exp2(s[:, j * 128:(j + 1) * 128] - m_new) for j in range(s.shape[1] // 128)],
                        axis=1).astype(BF16)
    if next_scores is not None:
        s_ref[slot] = next_scores()
    v_ones = jnp.concatenate([v, jnp.ones_like(v)], axis=1)
    acc_ref[slot] = (jnp.concatenate([alpha, alpha], axis=1) * acc_ref[slot]
                     + jnp.dot(p, v_ones, preferred_element_type=F32))
    m_ref[slot] = m_new


def _flash_result(slot, acc_ref):
    acc = acc_ref[slot]
    return acc[:, :HEAD_DIM] / acc[:, HEAD_DIM:]


def _silu(g):
    return g * jax.nn.sigmoid(g)


def _tile(ref, kj, t, c0, width):
    start = kj * t if isinstance(kj, int) else pl.multiple_of(kj * t, t)
    return ref[pl.ds(start, t), c0:c0 + width]


def _bank_tile(bank_ref, h, i, kj):
    base = 2 * (i - kj) + 1
    top = jnp.concatenate([bank_ref[h, base], bank_ref[h, base - 1]], axis=1)
    bot = jnp.concatenate([bank_ref[h, base + 1], bank_ref[h, base]], axis=1)
    return jnp.concatenate([top, bot], axis=0)


def _emit(o_ref, g_ref, h, o):
    gate = g_ref[:, h * 128:(h + 1) * 128].astype(F32)
    o_ref[:, h * 128:(h + 1) * 128] = (o * _silu(gate)).astype(BF16)


def _flash_walk(i, lo, n_far, slots, score, logits, value, s_ref, m_ref, acc_ref, prep=None):
    _flash_init(m_ref, acc_ref)
    for slot in range(slots):
        s_ref[slot] = score(slot, lo)

    def step(kj, phase):
        ctx = prep(kj) if prep is not None else None
        for slot in range(slots):
            s = logits(slot, s_ref[slot], kj, phase, ctx)
            nxt = None if phase == "last" else functools.partial(score, slot, kj + 1)
            _flash_update(slot, s, value(slot, kj), m_ref, acc_ref, s_ref, nxt)

    def walk(a, b, phase):
        def body(kj, carry):
            step(kj, phase)
            return carry
        lax.fori_loop(a, b, body, 0)

    if n_far is not None:
        walk(lo, n_far, "far")
        lo = n_far
    walk(lo, i, "near")
    step(i, "last")


def _attn_a_kernel(q_ref, k_ref, v_ref, g_ref, o_ref, s_ref, m_ref, acc_ref, *, t):
    i = pl.program_id(1)

    def score(h, kj):
        return _qk(q_ref[:, h * 256:(h + 1) * 256], _tile(k_ref, kj, t, h * 256, 256))

    def logits(h, s, kj, phase, ctx):
        if phase != "last":
            return s
        causal = lax.broadcasted_iota(jnp.int32, (t, t), 1) <= lax.broadcasted_iota(jnp.int32, (t, t), 0)
        return jnp.where(causal, s, NEG)

    def value(h, kj):
        return _tile(v_ref, kj, t, h * 128, 128)

    _flash_walk(i, 0, None, N_HEADS, score, logits, value, s_ref, m_ref, acc_ref)
    for h in range(N_HEADS):
        _emit(o_ref, g_ref, h, _flash_result(h, acc_ref))


def _attn_band_kernel(q_ref, k_ref, v_ref, g_ref, bank_ref, o_ref, s_ref, m_ref, acc_ref, *, t, near):
    i = pl.program_id(1)

    def score(h, kj):
        return _qk(q_ref[:, h * 128:(h + 1) * 128], _tile(k_ref, kj, t, h * 128, 128))

    def logits(h, s, kj, phase, ctx):
        return s + _bank_tile(bank_ref, h, i, kj)

    def value(h, kj):
        return _tile(v_ref, kj, t, h * 128, 128)

    _flash_walk(i, jnp.maximum(i - (near - 1), 0), None, N_HEADS, score, logits, value, s_ref, m_ref, acc_ref)
    for h in range(N_HEADS):
        _emit(o_ref, g_ref, h, _flash_result(h, acc_ref))


def _attn_sel_kernel(far_ref, q_ref, k_ref, v_ref, g_ref, bank_ref, sel_ref, o_ref, s_ref, m_ref, acc_ref,
                     *, t, near):
    i = pl.program_id(1)

    def score(h, kj):
        return _qk(q_ref[:, h * 128:(h + 1) * 128], _tile(k_ref, kj, t, h * 128, 128))

    def prep(kj):
        return sel_ref[:, pl.ds(pl.multiple_of(kj * t, t), t)].astype(F32)

    def logits(h, s, kj, phase, sel):
        bias = far_ref[h] * LOG2E if phase == "far" else _bank_tile(bank_ref, h, i, kj)
        return s + (sel + bias)

    def value(h, kj):
        return _tile(v_ref, kj, t, h * 128, 128)

    _flash_walk(i, 0, jnp.maximum(i - (near - 1), 0), N_HEADS, score, logits, value, s_ref, m_ref, acc_ref, prep)
    for h in range(N_HEADS):
        _emit(o_ref, g_ref, h, _flash_result(h, acc_ref))


def _attn_diff_kernel(far_ref, q_ref, k_ref, v_ref, g_ref, bank_ref, lam_ref, gsub_ref, o_ref,
                      q12_ref, s_ref, m_ref, acc_ref, *, t, near, lam_init):
    i = pl.program_id(1)
    lane = lax.broadcasted_iota(jnp.int32, q_ref.shape, 1) & (HEAD_DIM - 1)
    q = q_ref[...]
    q12_ref[0] = jnp.where(lane < HEAD_DIM // 2, q, jnp.zeros_like(q))
    q12_ref[1] = jnp.where(lane >= HEAD_DIM // 2, q, jnp.zeros_like(q))

    def score(slot, kj):
        h = slot // 2
        return _qk(q12_ref[slot % 2, :, h * 128:(h + 1) * 128], _tile(k_ref, kj, t, h * 128, 128))

    def logits(slot, s, kj, phase, ctx):
        h = slot // 2
        return s + (far_ref[h] * LOG2E if phase == "far" else _bank_tile(bank_ref, h, i, kj))

    def value(slot, kj):
        return _tile(v_ref, kj, t, (slot // 2) * 128, 128)

    _flash_walk(i, 0, jnp.maximum(i - (near - 1), 0), 2 * N_HEADS, score, logits, value, s_ref, m_ref, acc_ref)
    lam_v = lam_ref[...]
    lam = (jnp.exp(jnp.sum(lam_v[0:1] * lam_v[1:2], axis=-1, keepdims=True))
           - jnp.exp(jnp.sum(lam_v[2:3] * lam_v[3:4], axis=-1, keepdims=True)) + lam_init)
    for h in range(N_HEADS):
        o = _flash_result(2 * h, acc_ref) - lam * _flash_result(2 * h + 1, acc_ref)
        _emit(o_ref, g_ref, h, _rms(o, gsub_ref[...]) * (1.0 - lam_init))


def _attention(kind, q_arr, k_arr, v_arr, proj, q_blk, gate_blk, extra_in=(), extra_specs=(),
               smem_in=(), **kw):
    b, s, _ = proj.shape
    t = min(ATT_TILE, s)
    dk = 256 if kind == "a" else 128
    qw = N_HEADS * dk
    k_blk = 0 if kind == "a" else q_blk + 1
    v_blk = 0 if kind == "a" else q_blk + 2
    body = {"a": _attn_a_kernel, "band": _attn_band_kernel, "sel": _attn_sel_kernel,
            "diff": _attn_diff_kernel}[kind]
    slots = 2 * N_HEADS if kind == "diff" else N_HEADS
    scratch = [pltpu.VMEM((slots, t, t), F32), pltpu.VMEM((slots, t, HEAD_DIM), F32),
               pltpu.VMEM((slots, t, 2 * HEAD_DIM), F32)]
    if kind == "diff":
        scratch = [pltpu.VMEM((2, t, qw), BF16)] + scratch
    resident = dict(pipeline_mode=pl.Buffered(1))
    in_specs = [pl.BlockSpec(memory_space=pltpu.SMEM) for _ in smem_in]
    in_specs += [pl.BlockSpec((None, t, qw), lambda bi, i: (bi, i, q_blk)),
                 pl.BlockSpec((None, s, qw), lambda bi, i: (bi, 0, k_blk), **resident),
                 pl.BlockSpec((None, s, BRANCH_WIDTH), lambda bi, i: (bi, 0, v_blk), **resident),
                 pl.BlockSpec((None, t, BRANCH_WIDTH), lambda bi, i: (bi, i, gate_blk))]
    in_specs += list(extra_specs)
    return pl.pallas_call(
        functools.partial(body, t=t, **kw),
        out_shape=jax.ShapeDtypeStruct((b, s, BRANCH_WIDTH), BF16),
        grid=(b, s // t),
        in_specs=in_specs,
        out_specs=pl.BlockSpec((None, t, BRANCH_WIDTH), lambda bi, i: (bi, i, 0)),
        scratch_shapes=scratch,
        compiler_params=_cparams("arbitrary", "arbitrary"),
        name="attn_" + kind,
    )(*smem_in, q_arr, k_arr, v_arr, proj, *extra_in)


def _bit_transpose32(words):
    a = list(words)
    j, m = 16, 0x0000FFFF
    while j:
        for k in range(32):
            if not k & j:
                t = (a[k] ^ lax.shift_right_logical(a[k + j], jnp.int32(j))) & jnp.int32(m)
                a[k] = a[k] ^ t
                a[k + j] = a[k + j] ^ (t << j)
        j >>= 1
        m = (m ^ (m << j)) & 0xFFFFFFFF if j else m
    return a


def _select_kernel(qi_ref, ki_ref, wi_ref, o_ref, key_ref, plane_ref, alive_ref, *, tq, kc, n_sel):
    i = pl.program_id(1)
    s_len = o_ref.shape[1]
    n_ch = (i * tq + tq + kc - 1) // kc
    wpc = kc // 32
    int_min = jnp.int32(-2 ** 31)
    lane = lax.broadcasted_iota(jnp.int32, (tq, 128), 1)
    w_t = wi_ref[...].astype(F32).T
    q_heads = []
    for j in range(IDX_HEADS // 2):
        q2 = qi_ref[:, j * 128:(j + 1) * 128]
        q_heads.append(jnp.where(lane < IDX_DIM, q2, jnp.zeros_like(q2)))
        q_heads.append(jnp.where(lane >= IDX_DIM, q2, jnp.zeros_like(q2)))
    qpos = i * tq + lax.broadcasted_iota(jnp.int32, (kc, tq), 1)
    kiota = lax.broadcasted_iota(jnp.int32, (kc, tq), 0)

    def chunk(c):
        return pl.ds(pl.multiple_of(c * kc, kc), kc)

    def score_chunk(c, _):
        k = ki_ref[chunk(c), :]
        acc = jnp.zeros((kc, tq), F32)
        for hh in range(IDX_HEADS):
            acc = acc + jnp.maximum(_qk(k, q_heads[hh]), 0.0) * w_t[hh:hh + 1, :]
        acc = jnp.where(c * kc + kiota <= qpos, acc + 0.0, NEG)
        bits = pltpu.bitcast(acc, jnp.int32)
        keys = bits ^ ((bits >> 31) & jnp.int32(0x7FFFFFFF))
        key_ref[chunk(c), :] = keys
        ukeys = keys ^ int_min
        for blk in range(kc // 256):
            planes = _bit_transpose32([ukeys[blk * 256 + 8 * j:blk * 256 + 8 * j + 8, :] for j in range(32)])
            row = pl.multiple_of(c * wpc + blk * 8, 8)
            for b in range(32):
                plane_ref[b, pl.ds(row, 8), :] = planes[b]
        return 0

    lax.fori_loop(0, n_ch, score_chunk, 0)

    def clear_chunk(c, _):
        plane_ref[:, pl.ds(pl.multiple_of(c * wpc, wpc), wpc), :] = jnp.zeros((32, wpc, tq), jnp.int32)
        return 0

    lax.fori_loop(n_ch, s_len // kc, clear_chunk, 0)
    n_rows = s_len // 32
    word_row = lax.broadcasted_iota(jnp.int32, (n_rows, tq), 0)
    alive_ref[...] = jnp.where(word_row < n_ch * wpc, jnp.int32(-1), jnp.int32(0))

    def radix_step(bi, carry):
        thr, above = carry
        alive = alive_ref[...]
        plane = plane_ref[bi]
        ones = lax.population_count(alive & plane)
        ones = jnp.sum(jnp.sum(ones.reshape(n_rows // 8, 8, tq), axis=0), axis=0, keepdims=True)
        take = above + ones >= n_sel
        thr = jnp.where(take, thr | (jnp.int32(1) << (31 - bi)), thr)
        above = jnp.where(take, above, above + ones)
        alive_ref[...] = alive & (plane ^ jnp.where(take, jnp.int32(0), jnp.int32(-1)))
        return thr, above

    zeros = jnp.zeros((1, tq), jnp.int32)
    thr, _ = lax.fori_loop(0, 32, radix_step, (zeros, zeros))
    thr = thr ^ int_min

    def emit(c, _):
        keep = jnp.where(key_ref[chunk(c), :] >= thr, 0.0, NEG)
        o_ref[:, chunk(c)] = keep.T.astype(BF16)
        return 0

    lax.fori_loop(0, n_ch, emit, 0)

    def blank(c, _):
        o_ref[:, chunk(c)] = jnp.full((tq, kc), NEG, BF16)
        return 0

    lax.fori_loop(n_ch, s_len // kc, blank, 0)


def _select(proj):
    b, s, _ = proj.shape
    tq = min(256, s)
    kc = min(512, s)
    n_sel = min(TOPK_MAX, s // 4)
    return pl.pallas_call(
        functools.partial(_select_kernel, tq=tq, kc=kc, n_sel=n_sel),
        out_shape=jax.ShapeDtypeStruct((b, s, s), BF16),
        grid=(b, s // tq),
        in_specs=[pl.BlockSpec((None, tq, 1024), lambda bi, i: (bi, i, OFF_QIDX // 1024)),
                  pl.BlockSpec((None, s, 128), lambda bi, i: (bi, 0, OFF_KIDX // 128)),
                  pl.BlockSpec((None, tq, 128), lambda bi, i: (bi, i, OFF_WIDX // 128))],
        out_specs=pl.BlockSpec((None, tq, s), lambda bi, i: (bi, i, 0)),
        scratch_shapes=[pltpu.VMEM((s, tq), jnp.int32), pltpu.VMEM((32, s // 32, tq), jnp.int32),
                        pltpu.VMEM((s // 32, tq), jnp.int32)],
        compiler_params=_cparams("arbitrary", "arbitrary"),
        name="idx_select",
    )(proj, proj, proj)


def _out_kernel(a_ref, b_ref, c_ref, d_ref, w_ref, x_ref, gate_ref, g_ref, o_ref, wb_ref):
    @pl.when((pl.program_id(0) == 0) & (pl.program_id(1) == 0))
    def _():
        wb_ref[...] = w_ref[...].astype(BF16)

    y = jnp.dot(a_ref[...], wb_ref[0:512, :], preferred_element_type=F32)
    y += jnp.dot(b_ref[...], wb_ref[512:1024, :], preferred_element_type=F32)
    y += jnp.dot(c_ref[...], wb_ref[1024:1536, :], preferred_element_type=F32)
    y += jnp.dot(d_ref[...], wb_ref[1536:2048, :], preferred_element_type=F32)
    o_ref[...] = x_ref[...] + gate_ref[...] * _rms(y, g_ref[...])


def _out_proj(outs, w_out, li, x, mod3, g_post):
    b, s, d = x.shape
    tm = min(512, s)
    mix = lambda bi, i: (bi, i, 0)
    return pl.pallas_call(
        _out_kernel,
        out_shape=jax.ShapeDtypeStruct((b, s, d), F32),
        grid=(b, s // tm),
        in_specs=[pl.BlockSpec((None, tm, BRANCH_WIDTH), mix)] * 4
        + [pl.BlockSpec((None,) + w_out.shape[1:], lambda bi, i: (li, 0, 0), pipeline_mode=pl.Buffered(1)),
           pl.BlockSpec((None, tm, d), mix),
           pl.BlockSpec((None, 1, d), lambda bi, i: (bi, 0, 2)),
           pl.BlockSpec((1, d), lambda bi, i: (0, 0))],
        out_specs=pl.BlockSpec((None, tm, d), mix),
        scratch_shapes=[pltpu.VMEM(w_out.shape[1:], BF16)],
        compiler_params=_cparams("arbitrary", "arbitrary"),
        name="out_proj",
    )(*outs, w_out, x, mod3, g_post.reshape(1, d))


def _rope_tables(s):
    half = MLA_ROPE // 2
    inv = ROPE_THETA ** (-jnp.arange(half, dtype=F32) / half)
    ang = jnp.arange(s, dtype=F32)[:, None] * inv[None, :]
    z = jnp.zeros((s, 128 - MLA_ROPE), F32)
    cos, sin = jnp.cos(ang), jnp.sin(ang)
    return jnp.concatenate([cos, cos, z], axis=-1), jnp.concatenate([sin, sin, z], axis=-1)


def _rot_cols(w):
    half = w.shape[-1] // 2
    return jnp.concatenate([-w[..., half:], w[..., :half]], axis=-1)


IN_SPLITS = (("a_cq", 384), ("a_ckv", 256), ("a_krope", 64), ("b_q", 512), ("b_k", 512), ("b_v", 512),
             ("c_q", 512), ("c_k", 512), ("c_v", 512), ("c_qidx", 1024), ("c_kidx", 64), ("c_widx", 16),
             ("d_q", 512), ("d_k", 512), ("d_v", 512), ("gate", 2048))
IN_WIDTH = sum(width for _, width in IN_SPLITS)


def _layout_w_in_kernel(w_ref, o_ref):
    src, start = {}, 0
    for name, width in IN_SPLITS:
        src[name] = start
        start += width
    tk = w_ref.shape[1]

    def rows(name, width, offset=0):
        a = src[name] + offset
        return w_ref[a:a + width, :]

    def put(dst, val, scale=None):
        for r in range(0, val.shape[0], 512):
            piece = val[r:r + 512]
            if scale is not None:
                piece = piece * scale
            o_ref[:, dst + r:dst + r + piece.shape[0]] = piece.T.astype(BF16)

    z64 = jnp.zeros((64, tk), F32)
    half = MLA_ROPE // 2
    put(OFF_QIDX, rows("c_qidx", 1024), IDX_DIM ** -0.5)
    put(OFF_A, rows("a_cq", MLA_Q_RANK + MLA_KV_RANK))
    put(OFF_A + 640, jnp.concatenate(
        [rows("a_krope", MLA_ROPE), z64, -rows("a_krope", half, half), rows("a_krope", half), z64,
         rows("c_kidx", IDX_DIM), rows("c_kidx", IDX_DIM)], axis=0))
    for off, name, dim in ((OFF_B, "b", HEAD_DIM), (OFF_C, "c", HEAD_DIM), (OFF_D, "d", HEAD_DIM // 2)):
        put(off, rows(name + "_q", BRANCH_WIDTH), LOG2E * dim ** -0.5)
        put(off + BRANCH_WIDTH, rows(name + "_k", 2 * BRANCH_WIDTH))
    put(OFF_GATE, rows("gate", MIX_WIDTH))
    put(OFF_WIDX, jnp.concatenate([rows("c_widx", IDX_HEADS) * IDX_HEADS ** -0.5,
                                   jnp.zeros((128 - IDX_HEADS, tk), F32)], axis=0))


def _layout_w_in(w_in_t, li):
    d = w_in_t.shape[2]
    tk = 256
    return pl.pallas_call(
        _layout_w_in_kernel,
        out_shape=jax.ShapeDtypeStruct((d, PROJ_WIDTH), BF16),
        grid=(d // tk,),
        in_specs=[pl.BlockSpec((None, IN_WIDTH, tk), lambda i: (li, 0, i))],
        out_specs=pl.BlockSpec((tk, PROJ_WIDTH), lambda i: (i, 0)),
        compiler_params=_cparams("arbitrary"),
        name="layout_w_in",
    )(w_in_t)


def _layout_w_uq(w):
    r = w.shape[0]
    w = w.reshape(r, N_HEADS, MLA_NOPE + MLA_ROPE) * (LOG2E * (MLA_NOPE + MLA_ROPE) ** -0.5)
    z = jnp.zeros((r, N_HEADS, 128 - MLA_ROPE), w.dtype)
    rope = w[..., MLA_NOPE:]
    return jnp.concatenate([w[..., :MLA_NOPE], rope, z, _rot_cols(rope), z], axis=-1).reshape(r, -1).astype(BF16)


def kernel(x, c, w_ada, b_ada, g_pre, g_post, w_in, g_q_a, w_uq_a, g_kv_a, w_ukv_a,
           lam_q1, lam_k1, lam_q2, lam_k2, g_sub_d, w_out, rel_bias):
    b, s, d = x.shape
    depth = w_ada.shape[0]
    t = min(ATT_TILE, s)
    half = t // 2
    nq = s // t
    near_bias = min(nq, -(-(_first_far_diagonal(half) + 1) // 2))
    near_band = min(nq, -(-(DILATED_PATTERNS[-1][0] // half + 1) // 2))

    cos_t, sin_t = _rope_tables(s)
    bank_b = _bank(rel_bias[:, 0:N_HEADS], 2 * near_band + 1, half, -1, True)
    bank_cd = _bank(rel_bias[:, N_HEADS:3 * N_HEADS], 2 * near_bias + 1, half, -1, False)
    far_c = rel_bias[REL_BUCKETS - 1, N_HEADS:2 * N_HEADS]
    far_d = rel_bias[REL_BUCKETS - 1, 2 * N_HEADS:3 * N_HEADS]
    bank_spec = lambda n, group=0: pl.BlockSpec((N_HEADS, n, half, half), lambda bi, i: (group, 0, 0, 0),
                                                pipeline_mode=pl.Buffered(1))

    w_in_t = jnp.swapaxes(w_in, 1, 2)
    mod = _ada_mod(c, w_ada, b_ada)
    for li in range(depth):
        mod3 = mod[li].reshape(b, 1, 3 * d)
        h = _prenorm(x, g_pre[li], mod3)
        proj = _in_proj(h.reshape(b * s, d), _layout_w_in(w_in_t, li)).reshape(b, s, PROJ_WIDTH)

        q_a, k_a, v_a = _mla_prep(proj, cos_t, sin_t, g_q_a[li], g_kv_a[li],
                                  _layout_w_uq(w_uq_a[li]), w_ukv_a[li].astype(BF16))
        gate0 = OFF_GATE // BRANCH_WIDTH
        out_a = _attention("a", q_a, k_a, v_a, proj, 0, gate0)
        out_b = _attention("band", proj, proj, proj, proj, OFF_B // BRANCH_WIDTH, gate0 + 1,
                           extra_in=(bank_b,), extra_specs=(bank_spec(2 * near_band + 1),), near=near_band)
        sel = _select(proj)
        out_c = _attention("sel", proj, proj, proj, proj, OFF_C // BRANCH_WIDTH, gate0 + 2,
                           extra_in=(bank_cd, sel),
                           extra_specs=(bank_spec(2 * near_bias + 1, 0),
                                        pl.BlockSpec((None, t, s), lambda bi, i: (bi, i, 0))),
                           smem_in=(far_c,), near=near_bias)
        lam_init = 0.8 - 0.6 * math.exp(-0.3 * li)
        lam_vecs = jnp.stack([lam_q1[li], lam_k1[li], lam_q2[li], lam_k2[li]])
        out_d = _attention("diff", proj, proj, proj, proj, OFF_D // BRANCH_WIDTH, gate0 + 3,
                           extra_in=(bank_cd, lam_vecs, g_sub_d[li].reshape(1, HEAD_DIM)),
                           extra_specs=(bank_spec(2 * near_bias + 1, 1),
                                        pl.BlockSpec(lam_vecs.shape, lambda bi, i: (0, 0)),
                                        pl.BlockSpec((1, HEAD_DIM), lambda bi, i: (0, 0))),
                           smem_in=(far_d,), near=near_bias, lam_init=lam_init)
        x = _out_proj((out_a, out_b, out_c, out_d), w_out, li, x, mod3, g_post[li])
    return x
```

```python
import functools
import math

import numpy as np
import jax
import jax.numpy as jnp
from jax import lax
from jax.experimental import pallas as pl
from jax.experimental.pallas import tpu as pltpu

F32 = jnp.float32
BF16 = jnp.bfloat16

HEAD_DIM = 128
N_HEADS = 4
BRANCH_WIDTH = N_HEADS * HEAD_DIM
MIX_WIDTH = 4 * BRANCH_WIDTH
MLA_Q_RANK = 384
MLA_KV_RANK = 256
MLA_NOPE = 128
MLA_ROPE = 64
ROPE_THETA = 10000.0
DILATED_PATTERNS = ((128, 1), (512, 4), (2048, 16))
IDX_HEADS = 16
IDX_DIM = 64
TOPK_MAX = 256
REL_BUCKETS = 32
REL_MAX_DIST = 2048
NORM_EPS = 1e-6
NEG = -1e30
LOG2E = math.log2(math.e)

OFF_QIDX = 0
OFF_A = 1024
OFF_KIDX = OFF_A + 896
OFF_B = 2048
OFF_C = 3584
OFF_D = 5120
OFF_GATE = 6656
OFF_WIDX = 8704
PROJ_WIDTH = 8832

ATT_TILE = 512
VMEM_LIMIT = 56 * 1024 * 1024


def _cparams(*sem):
    return pltpu.CompilerParams(dimension_semantics=sem, vmem_limit_bytes=VMEM_LIMIT)


def _ada_kernel(c_ref, w_ref, b_ref, o_ref):
    c = c_ref[...]
    a = c * jax.nn.sigmoid(c)
    o_ref[...] = jnp.dot(a, w_ref[...], preferred_element_type=F32,
                         precision=lax.Precision.HIGHEST) + b_ref[...]


def _ada_mod(c, w_ada, b_ada):
    depth, d, n = w_ada.shape
    b = c.shape[0]
    tn = 768
    return pl.pallas_call(
        _ada_kernel,
        out_shape=jax.ShapeDtypeStruct((depth, b, n), F32),
        grid=(depth, n // tn),
        in_specs=[pl.BlockSpec((b, d), lambda l, j: (0, 0)),
                  pl.BlockSpec((None, d, tn), lambda l, j: (l, 0, j)),
                  pl.BlockSpec((None, 1, tn), lambda l, j: (l, 0, j))],
        out_specs=pl.BlockSpec((None, b, tn), lambda l, j: (l, 0, j)),
        compiler_params=_cparams("arbitrary", "arbitrary"),
        name="ada_mod",
    )(c, w_ada, b_ada.reshape(depth, 1, n))


def _in_proj_kernel(x_ref, g_ref, shift_ref, scale_ref, w_ref, o_ref):
    x = x_ref[...]
    y = x * lax.rsqrt(jnp.mean(x * x, axis=-1, keepdims=True) + NORM_EPS) * g_ref[...]
    h = (y * (1.0 + scale_ref[...]) + shift_ref[...]).astype(BF16)
    o_ref[...] = jnp.dot(h, w_ref[...], preferred_element_type=F32).astype(o_ref.dtype)


def _in_proj(x, g_pre, mod3, w):
    b, s, d = x.shape
    n = w.shape[1]
    tm = min(512, s)
    nt = s // tm
    tn = n // 3
    row = lambda j, i: (i // nt, i % nt, 0)
    return pl.pallas_call(
        _in_proj_kernel,
        out_shape=jax.ShapeDtypeStruct((b, s, n), BF16),
        grid=(n // tn, b * nt),
        in_specs=[pl.BlockSpec((None, tm, d), row),
                  pl.BlockSpec((1, d), lambda j, i: (0, 0)),
                  pl.BlockSpec((None, 1, d), lambda j, i: (i // nt, 0, 0)),
                  pl.BlockSpec((None, 1, d), lambda j, i: (i // nt, 0, 1)),
                  pl.BlockSpec((d, tn), lambda j, i: (0, j), pipeline_mode=pl.Buffered(1))],
        out_specs=pl.BlockSpec((None, tm, tn), lambda j, i: (i // nt, i % nt, j)),
        compiler_params=_cparams("arbitrary", "arbitrary"),
        name="in_proj",
    )(x, g_pre.reshape(1, d), mod3, mod3, w)


def _rms(x, g):
    return x * lax.rsqrt(jnp.mean(x * x, axis=-1, keepdims=True) + NORM_EPS) * g


def _mla_prep_kernel(p_ref, cos_ref, sin_ref, gq_ref, gkv_ref, wq_ref, wkv_ref, q_ref, k_ref, v_ref):
    cos = cos_ref[...]
    sin = sin_ref[...]
    cq = _rms(p_ref[:, 0:MLA_Q_RANK].astype(F32), gq_ref[...]).astype(BF16)
    ckv = _rms(p_ref[:, MLA_Q_RANK:MLA_Q_RANK + MLA_KV_RANK].astype(F32), gkv_ref[...]).astype(BF16)
    q = jnp.dot(cq, wq_ref[...], preferred_element_type=F32)
    kv = jnp.dot(ckv, wkv_ref[...], preferred_element_type=F32)
    k_rope = (p_ref[:, 640:768].astype(F32) * cos + p_ref[:, 768:896].astype(F32) * sin).astype(BF16)
    for h in range(N_HEADS):
        qh = q[:, h * 384:(h + 1) * 384]
        q_ref[:, h * 256:h * 256 + 128] = qh[:, 0:128].astype(BF16)
        q_ref[:, h * 256 + 128:(h + 1) * 256] = (qh[:, 128:256] * cos + qh[:, 256:384] * sin).astype(BF16)
        k_ref[:, h * 256:h * 256 + 128] = kv[:, h * 256:h * 256 + 128].astype(BF16)
        k_ref[:, h * 256 + 128:(h + 1) * 256] = k_rope
        v_ref[:, h * 128:(h + 1) * 128] = kv[:, h * 256 + 128:(h + 1) * 256].astype(BF16)


def _mla_prep(proj, cos_t, sin_t, g_q, g_kv, wq, wkv):
    b, s, _ = proj.shape
    tm = min(512, s)
    const = lambda bi, i: (0, 0)
    return pl.pallas_call(
        _mla_prep_kernel,
        out_shape=(jax.ShapeDtypeStruct((b, s, N_HEADS * 256), BF16),
                   jax.ShapeDtypeStruct((b, s, N_HEADS * 256), BF16),
                   jax.ShapeDtypeStruct((b, s, BRANCH_WIDTH), BF16)),
        grid=(b, s // tm),
        in_specs=[pl.BlockSpec((None, tm, 1024), lambda bi, i: (bi, i, OFF_A // 1024)),
                  pl.BlockSpec((tm, 128), lambda bi, i: (i, 0)),
                  pl.BlockSpec((tm, 128), lambda bi, i: (i, 0)),
                  pl.BlockSpec((1, MLA_Q_RANK), const),
                  pl.BlockSpec((1, MLA_KV_RANK), const),
                  pl.BlockSpec(wq.shape, const),
                  pl.BlockSpec(wkv.shape, const)],
        out_specs=(pl.BlockSpec((None, tm, N_HEADS * 256), lambda bi, i: (bi, i, 0)),
                   pl.BlockSpec((None, tm, N_HEADS * 256), lambda bi, i: (bi, i, 0)),
                   pl.BlockSpec((None, tm, BRANCH_WIDTH), lambda bi, i: (bi, i, 0))),
        compiler_params=_cparams("arbitrary", "arbitrary"),
        name="mla_prep",
    )(proj, cos_t, sin_t, g_q.reshape(1, -1), g_kv.reshape(1, -1), wq, wkv)


def _bucket_np(n):
    max_exact = REL_BUCKETS // 2
    nf = np.maximum(n, max_exact).astype(np.float32)
    large = max_exact + (np.log(nf / np.float32(max_exact)) / np.float32(math.log(REL_MAX_DIST / max_exact))
                         * np.float32(REL_BUCKETS - max_exact)).astype(np.int32)
    return np.where(n < max_exact, n, np.minimum(large, REL_BUCKETS - 1)).astype(np.int32)


def _bucket_starts():
    buckets = _bucket_np(np.arange(2 * REL_MAX_DIST))
    return [int(np.argmax(buckets >= b)) for b in range(REL_BUCKETS)]


def _bank_kernel(tab_ref, o_ref, *, t, d_min, band):
    d = pl.program_id(0) + d_min
    dist = (t * d + lax.broadcasted_iota(jnp.int32, (t, t), 0) - lax.broadcasted_iota(jnp.int32, (t, t), 1))
    starts = _bucket_starts()
    ge = [dist >= starts[b] for b in range(1, REL_BUCKETS)]
    if band:
        mult = jnp.zeros((t, t), jnp.int32)
        for window, dil in DILATED_PATTERNS:
            mult += jnp.where((dist >= 0) & (dist <= window) & ((dist & (dil - 1)) == 0), 1, 0)
        logm = jnp.where(mult == 3, math.log(3.0), jnp.where(mult == 2, math.log(2.0), 0.0))
        keep = mult > 0
    else:
        logm = 0.0
        keep = dist >= 0
    for h in range(o_ref.shape[0]):
        val = jnp.full((t, t), tab_ref[0, h], F32)
        for b in range(1, REL_BUCKETS):
            val = jnp.where(ge[b - 1], tab_ref[b, h], val)
        o_ref[h] = jnp.where(keep, (val + logm) * LOG2E, NEG)


def _bank(tab, n_tables, t, d_min, band):
    nh = tab.shape[1]
    return pl.pallas_call(
        functools.partial(_bank_kernel, t=t, d_min=d_min, band=band),
        out_shape=jax.ShapeDtypeStruct((nh, n_tables, t, t), F32),
        grid=(n_tables,),
        in_specs=[pl.BlockSpec(memory_space=pltpu.SMEM)],
        out_specs=pl.BlockSpec((nh, None, t, t), lambda j: (0, j, 0, 0)),
        compiler_params=_cparams("arbitrary"),
        name="bank_band" if band else "bank_bias",
    )(tab)


def _first_far_diagonal(t):
    last = _bucket_starts()[REL_BUCKETS - 1]
    return -(-(last + t - 1) // t)


def _qk(q, k):
    return lax.dot_general(q, k, (((1,), (1,)), ((), ())), preferred_element_type=F32)


def _flash_init(m_ref, acc_ref):
    m_ref[...] = jnp.full(m_ref.shape, NEG, F32)
    acc_ref[...] = jnp.zeros(acc_ref.shape, F32)


def _flash_update(slot, s, v, m_ref, acc_ref, s_ref, next_scores):
    m_prev = m_ref[slot]
    m_new = jnp.maximum(m_prev, jnp.max(s, axis=-1, keepdims=True))
    alpha = jnp.exp2(m_prev - m_new)
    p = jnp.concatenate([jnp.exp2(s[:, j * 128:(j + 1) * 128] - m_new) for j in range(s.shape[1] // 128)],
                        axis=1).astype(BF16)
    if next_scores is not None:
        s_ref[slot] = next_scores()
    v_ones = jnp.concatenate([v, jnp.ones_like(v)], axis=1)
    acc_ref[slot] = (jnp.concatenate([alpha, alpha], axis=1) * acc_ref[slot]
                     + jnp.dot(p, v_ones, preferred_element_type=F32))
    m_ref[slot] = m_new


def _flash_result(slot, acc_ref):
    acc = acc_ref[slot]
    return acc[:, :HEAD_DIM] / acc[:, HEAD_DIM:]


def _silu(g):
    return g * jax.nn.sigmoid(g)


def _tile(ref, kj, t, c0, width):
    start = kj * t if isinstance(kj, int) else pl.multiple_of(kj * t, t)
    return ref[pl.ds(start, t), c0:c0 + width]


def _bank_tile(bank_ref, h, i, kj):
    base = 2 * (i - kj) + 1
    top = jnp.concatenate([bank_ref[h, base], bank_ref[h, base - 1]], axis=1)
    bot = jnp.concatenate([bank_ref[h, base + 1], bank_ref[h, base]], axis=1)
    return jnp.concatenate([top, bot], axis=0)


def _emit(o_ref, g_ref, h, o):
    gate = g_ref[:, h * 128:(h + 1) * 128].astype(F32)
    o_ref[:, h * 128:(h + 1) * 128] = (o * _silu(gate)).astype(BF16)


def _flash_walk(i, first_key, n_far, slots, score, logits, value, s_ref, m_ref, acc_ref, prep=None):
    n_tiles = pl.num_programs(1)
    lo = first_key(i)
    i_next = jnp.minimum(i + 1, n_tiles - 1)
    _flash_init(m_ref, acc_ref)

    @pl.when(i == 0)
    def _():
        for slot in range(slots):
            s_ref[slot] = score(slot, i, lo)

    def step(kj, phase):
        ctx = prep(kj) if prep is not None else None
        for slot in range(slots):
            s = logits(slot, s_ref[slot], kj, phase, ctx)
            if phase == "last":
                nxt = functools.partial(score, slot, i_next, first_key(i_next))
            else:
                nxt = functools.partial(score, slot, i, kj + 1)
            _flash_update(slot, s, value(slot, kj), m_ref, acc_ref, s_ref, nxt)

    def walk(a, b, phase):
        def body(kj, carry):
            step(kj, phase)
            return carry
        lax.fori_loop(a, b, body, 0)

    if n_far is not None:
        walk(lo, n_far, "far")
        lo = n_far
    walk(lo, i, "near")
    step(i, "last")


def _attn_a_kernel(q_ref, k_ref, v_ref, g_ref, o_ref, s_ref, m_ref, acc_ref, *, t):
    i = pl.program_id(1)

    def score(h, qi, kj):
        return _qk(_tile(q_ref, qi, t, h * 256, 256), _tile(k_ref, kj, t, h * 256, 256))

    def logits(h, s, kj, phase, ctx):
        if phase != "last":
            return s
        causal = lax.broadcasted_iota(jnp.int32, (t, t), 1) <= lax.broadcasted_iota(jnp.int32, (t, t), 0)
        return jnp.where(causal, s, NEG)

    def value(h, kj):
        return _tile(v_ref, kj, t, h * 128, 128)

    _flash_walk(i, lambda qi: 0, None, N_HEADS, score, logits, value, s_ref, m_ref, acc_ref)
    for h in range(N_HEADS):
        _emit(o_ref, g_ref, h, _flash_result(h, acc_ref))


def _attn_band_kernel(q_ref, k_ref, v_ref, g_ref, bank_ref, o_ref, s_ref, m_ref, acc_ref, *, t, near):
    i = pl.program_id(1)

    def score(h, qi, kj):
        return _qk(_tile(q_ref, qi, t, h * 128, 128), _tile(k_ref, kj, t, h * 128, 128))

    def logits(h, s, kj, phase, ctx):
        return s + _bank_tile(bank_ref, h, i, kj)

    def value(h, kj):
        return _tile(v_ref, kj, t, h * 128, 128)

    _flash_walk(i, lambda qi: jnp.maximum(qi - (near - 1), 0), None, N_HEADS, score, logits, value,
                s_ref, m_ref, acc_ref)
    for h in range(N_HEADS):
        _emit(o_ref, g_ref, h, _flash_result(h, acc_ref))


def _attn_sel_kernel(far_ref, q_ref, k_ref, v_ref, g_ref, bank_ref, sel_ref, o_ref, s_ref, m_ref, acc_ref,
                     *, t, near):
    i = pl.program_id(1)

    def score(h, qi, kj):
        return _qk(_tile(q_ref, qi, t, h * 128, 128), _tile(k_ref, kj, t, h * 128, 128))

    def prep(kj):
        return sel_ref[:, pl.ds(pl.multiple_of(kj * t, t), t)].astype(F32)

    def logits(h, s, kj, phase, sel):
        bias = far_ref[h] * LOG2E if phase == "far" else _bank_tile(bank_ref, h, i, kj)
        return s + (sel + bias)

    def value(h, kj):
        return _tile(v_ref, kj, t, h * 128, 128)

    _flash_walk(i, lambda qi: 0, jnp.maximum(i - (near - 1), 0), N_HEADS, score, logits, value,
                s_ref, m_ref, acc_ref, prep)
    for h in range(N_HEADS):
        _emit(o_ref, g_ref, h, _flash_result(h, acc_ref))


def _attn_diff_kernel(far_ref, q_ref, k_ref, v_ref, g_ref, bank_ref, lam_ref, gsub_ref, o_ref,
                      s_ref, m_ref, acc_ref, *, t, near, lam_init):
    i = pl.program_id(1)
    first_half = lax.broadcasted_iota(jnp.int32, (t, HEAD_DIM), 1) < HEAD_DIM // 2

    def score(slot, qi, kj):
        h = slot // 2
        q = _tile(q_ref, qi, t, h * 128, 128)
        keep = first_half if slot % 2 == 0 else jnp.logical_not(first_half)
        q = jnp.where(keep, q, jnp.zeros_like(q))
        return _qk(q, _tile(k_ref, kj, t, h * 128, 128))

    def logits(slot, s, kj, phase, ctx):
        h = slot // 2
        return s + (far_ref[h] * LOG2E if phase == "far" else _bank_tile(bank_ref, h, i, kj))

    def value(slot, kj):
        return _tile(v_ref, kj, t, (slot // 2) * 128, 128)

    _flash_walk(i, lambda qi: 0, jnp.maximum(i - (near - 1), 0), 2 * N_HEADS, score, logits, value,
                s_ref, m_ref, acc_ref)
    lam_v = lam_ref[...]
    lam = (jnp.exp(jnp.sum(lam_v[0:1] * lam_v[1:2], axis=-1, keepdims=True))
           - jnp.exp(jnp.sum(lam_v[2:3] * lam_v[3:4], axis=-1, keepdims=True)) + lam_init)
    for h in range(N_HEADS):
        o = _flash_result(2 * h, acc_ref) - lam * _flash_result(2 * h + 1, acc_ref)
        _emit(o_ref, g_ref, h, _rms(o, gsub_ref[...]) * (1.0 - lam_init))


def _attention(kind, q_arr, k_arr, v_arr, proj, q_blk, gate_blk, extra_in=(), extra_specs=(),
               smem_in=(), **kw):
    b, s, _ = proj.shape
    t = min(ATT_TILE, s)
    dk = 256 if kind == "a" else 128
    qw = N_HEADS * dk
    k_blk = 0 if kind == "a" else q_blk + 1
    v_blk = 0 if kind == "a" else q_blk + 2
    body = {"a": _attn_a_kernel, "band": _attn_band_kernel, "sel": _attn_sel_kernel,
            "diff": _attn_diff_kernel}[kind]
    slots = 2 * N_HEADS if kind == "diff" else N_HEADS
    scratch = [pltpu.VMEM((slots, t, t), F32), pltpu.VMEM((slots, t, HEAD_DIM), F32),
               pltpu.VMEM((slots, t, 2 * HEAD_DIM), F32)]
    resident = dict(pipeline_mode=pl.Buffered(1))
    in_specs = [pl.BlockSpec(memory_space=pltpu.SMEM) for _ in smem_in]
    in_specs += [pl.BlockSpec((None, s, qw), lambda bi, i: (bi, 0, q_blk), **resident),
                 pl.BlockSpec((None, s, qw), lambda bi, i: (bi, 0, k_blk), **resident),
                 pl.BlockSpec((None, s, BRANCH_WIDTH), lambda bi, i: (bi, 0, v_blk), **resident),
                 pl.BlockSpec((None, t, BRANCH_WIDTH), lambda bi, i: (bi, i, gate_blk))]
    in_specs += list(extra_specs)
    return pl.pallas_call(
        functools.partial(body, t=t, **kw),
        out_shape=jax.ShapeDtypeStruct((b, s, BRANCH_WIDTH), BF16),
        grid=(b, s // t),
        in_specs=in_specs,
        out_specs=pl.BlockSpec((None, t, BRANCH_WIDTH), lambda bi, i: (bi, i, 0)),
        scratch_shapes=scratch,
        compiler_params=_cparams("arbitrary", "arbitrary"),
        name="attn_" + kind,
    )(*smem_in, q_arr, k_arr, v_arr, proj, *extra_in)


def _bit_transpose32(words):
    a = list(words)
    j, m = 16, 0x0000FFFF
    while j:
        for k in range(32):
            if not k & j:
                t = (a[k] ^ lax.shift_right_logical(a[k + j], jnp.int32(j))) & jnp.int32(m)
                a[k] = a[k] ^ t
                a[k + j] = a[k + j] ^ (t << j)
        j >>= 1
        m = (m ^ (m << j)) & 0xFFFFFFFF if j else m
    return a


def _select_kernel(qi_ref, ki_ref, wi_ref, o_ref, key_ref, plane_ref, alive_ref, *, tq, kc, n_sel):
    i = pl.program_id(1)
    s_len = o_ref.shape[1]
    n_ch = (i * tq + tq + kc - 1) // kc
    wpc = kc // 32
    int_min = jnp.int32(-2 ** 31)
    lane = lax.broadcasted_iota(jnp.int32, (tq, 128), 1)
    w_t = wi_ref[...].astype(F32).T
    q_heads = []
    for j in range(IDX_HEADS // 2):
        q2 = qi_ref[:, j * 128:(j + 1) * 128]
        q_heads.append(jnp.where(lane < IDX_DIM, q2, jnp.zeros_like(q2)))
        q_heads.append(jnp.where(lane >= IDX_DIM, q2, jnp.zeros_like(q2)))
    qpos = i * tq + lax.broadcasted_iota(jnp.int32, (kc, tq), 1)
    kiota = lax.broadcasted_iota(jnp.int32, (kc, tq), 0)

    def chunk(c):
        return pl.ds(pl.multiple_of(c * kc, kc), kc)

    def score_chunk(c, _):
        k = ki_ref[chunk(c), :]
        acc = jnp.zeros((kc, tq), F32)
        for hh in range(IDX_HEADS):
            acc = acc + jnp.maximum(_qk(k, q_heads[hh]), 0.0) * w_t[hh:hh + 1, :]
        acc = jnp.where(c * kc + kiota <= qpos, acc + 0.0, NEG)
        bits = pltpu.bitcast(acc, jnp.int32)
        keys = bits ^ ((bits >> 31) & jnp.int32(0x7FFFFFFF))
        key_ref[chunk(c), :] = keys
        ukeys = keys ^ int_min
        for blk in range(kc // 256):
            planes = _bit_transpose32([ukeys[blk * 256 + 8 * j:blk * 256 + 8 * j + 8, :] for j in range(32)])
            row = pl.multiple_of(c * wpc + blk * 8, 8)
            for b in range(32):
                plane_ref[b, pl.ds(row, 8), :] = planes[b]
        return 0

    lax.fori_loop(0, n_ch, score_chunk, 0)

    def clear_chunk(c, _):
        plane_ref[:, pl.ds(pl.multiple_of(c * wpc, wpc), wpc), :] = jnp.zeros((32, wpc, tq), jnp.int32)
        return 0

    lax.fori_loop(n_ch, s_len // kc, clear_chunk, 0)
    n_rows = s_len // 32
    word_row = lax.broadcasted_iota(jnp.int32, (n_rows, tq), 0)
    alive_ref[...] = jnp.where(word_row < n_ch * wpc, jnp.int32(-1), jnp.int32(0))

    def radix_step(bi, carry):
        thr, above = carry
        alive = alive_ref[...]
        plane = plane_ref[bi]
        ones = lax.population_count(alive & plane)
        ones = jnp.sum(jnp.sum(ones.reshape(n_rows // 8, 8, tq), axis=0), axis=0, keepdims=True)
        take = above + ones >= n_sel
        thr = jnp.where(take, thr | (jnp.int32(1) << (31 - bi)), thr)
        above = jnp.where(take, above, above + ones)
        alive_ref[...] = alive & (plane ^ jnp.where(take, jnp.int32(0), jnp.int32(-1)))
        return thr, above

    zeros = jnp.zeros((1, tq), jnp.int32)
    thr, _ = lax.fori_loop(0, 32, radix_step, (zeros, zeros))
    thr = thr ^ int_min

    def emit(c, _):
        keep = jnp.where(key_ref[chunk(c), :] >= thr, 0.0, NEG)
        o_ref[:, chunk(c)] = keep.T.astype(BF16)
        return 0

    lax.fori_loop(0, n_ch, emit, 0)

    def blank(c, _):
        o_ref[:, chunk(c)] = jnp.full((tq, kc), NEG, BF16)
        return 0

    lax.fori_loop(n_ch, s_len // kc, blank, 0)


def _select(proj):
    b, s, _ = proj.shape
    tq = min(256, s)
    kc = min(512, s)
    n_sel = min(TOPK_MAX, s // 4)
    return pl.pallas_call(
        functools.partial(_select_kernel, tq=tq, kc=kc, n_sel=n_sel),
        out_shape=jax.ShapeDtypeStruct((b, s, s), BF16),
        grid=(b, s // tq),
        in_specs=[pl.BlockSpec((None, tq, 1024), lambda bi, i: (bi, i, OFF_QIDX // 1024)),
                  pl.BlockSpec((None, s, 128), lambda bi, i: (bi, 0, OFF_KIDX // 128)),
                  pl.BlockSpec((None, tq, 128), lambda bi, i: (bi, i, OFF_WIDX // 128))],
        out_specs=pl.BlockSpec((None, tq, s), lambda bi, i: (bi, i, 0)),
        scratch_shapes=[pltpu.VMEM((s, tq), jnp.int32), pltpu.VMEM((32, s // 32, tq), jnp.int32),
                        pltpu.VMEM((s // 32, tq), jnp.int32)],
        compiler_params=_cparams("arbitrary", "arbitrary"),
        name="idx_select",
    )(proj, proj, proj)


def _out_kernel(a_ref, b_ref, c_ref, d_ref, w_ref, x_ref, gate_ref, g_ref, o_ref, wb_ref):
    @pl.when((pl.program_id(0) == 0) & (pl.program_id(1) == 0))
    def _():
        wb_ref[...] = w_ref[...].astype(BF16)

    y = jnp.dot(a_ref[...], wb_ref[0:512, :], preferred_element_type=F32)
    y += jnp.dot(b_ref[...], wb_ref[512:1024, :], preferred_element_type=F32)
    y += jnp.dot(c_ref[...], wb_ref[1024:1536, :], preferred_element_type=F32)
    y += jnp.dot(d_ref[...], wb_ref[1536:2048, :], preferred_element_type=F32)
    o_ref[...] = x_ref[...] + gate_ref[...] * _rms(y, g_ref[...])


def _out_proj(outs, w_out, li, x, mod3, g_post):
    b, s, d = x.shape
    tm = min(512, s)
    mix = lambda bi, i: (bi, i, 0)
    return pl.pallas_call(
        _out_kernel,
        out_shape=jax.ShapeDtypeStruct((b, s, d), F32),
        grid=(b, s // tm),
        in_specs=[pl.BlockSpec((None, tm, BRANCH_WIDTH), mix)] * 4
        + [pl.BlockSpec((None,) + w_out.shape[1:], lambda bi, i: (li, 0, 0), pipeline_mode=pl.Buffered(1)),
           pl.BlockSpec((None, tm, d), mix),
           pl.BlockSpec((None, 1, d), lambda bi, i: (bi, 0, 2)),
           pl.BlockSpec((1, d), lambda bi, i: (0, 0))],
        out_specs=pl.BlockSpec((None, tm, d), mix),
        scratch_shapes=[pltpu.VMEM(w_out.shape[1:], BF16)],
        compiler_params=_cparams("arbitrary", "arbitrary"),
        name="out_proj",
    )(*outs, w_out, x, mod3, g_post.reshape(1, d))


def _rope_tables(s):
    half = MLA_ROPE // 2
    inv = ROPE_THETA ** (-jnp.arange(half, dtype=F32) / half)
    ang = jnp.arange(s, dtype=F32)[:, None] * inv[None, :]
    z = jnp.zeros((s, 128 - MLA_ROPE), F32)
    cos, sin = jnp.cos(ang), jnp.sin(ang)
    return jnp.concatenate([cos, cos, z], axis=-1), jnp.concatenate([sin, sin, z], axis=-1)


def _rot_cols(w):
    half = w.shape[-1] // 2
    return jnp.concatenate([-w[..., half:], w[..., :half]], axis=-1)


IN_SPLITS = (("a_cq", 384), ("a_ckv", 256), ("a_krope", 64), ("b_q", 512), ("b_k", 512), ("b_v", 512),
             ("c_q", 512), ("c_k", 512), ("c_v", 512), ("c_qidx", 1024), ("c_kidx", 64), ("c_widx", 16),
             ("d_q", 512), ("d_k", 512), ("d_v", 512), ("gate", 2048))
IN_WIDTH = sum(width for _, width in IN_SPLITS)


def _layout_w_in_kernel(w_ref, o_ref):
    src, start = {}, 0
    for name, width in IN_SPLITS:
        src[name] = start
        start += width
    tk = w_ref.shape[1]

    def rows(name, width, offset=0):
        a = src[name] + offset
        return w_ref[a:a + width, :]

    def put(dst, val, scale=None):
        for r in range(0, val.shape[0], 512):
            piece = val[r:r + 512]
            if scale is not None:
                piece = piece * scale
            o_ref[:, dst + r:dst + r + piece.shape[0]] = piece.T.astype(BF16)

    z64 = jnp.zeros((64, tk), F32)
    half = MLA_ROPE // 2
    put(OFF_QIDX, rows("c_qidx", 1024), IDX_DIM ** -0.5)
    put(OFF_A, rows("a_cq", MLA_Q_RANK + MLA_KV_RANK))
    put(OFF_A + 640, jnp.concatenate(
        [rows("a_krope", MLA_ROPE), z64, -rows("a_krope", half, half), rows("a_krope", half), z64,
         rows("c_kidx", IDX_DIM), rows("c_kidx", IDX_DIM)], axis=0))
    for off, name, dim in ((OFF_B, "b", HEAD_DIM), (OFF_C, "c", HEAD_DIM), (OFF_D, "d", HEAD_DIM // 2)):
        put(off, rows(name + "_q", BRANCH_WIDTH), LOG2E * dim ** -0.5)
        put(off + BRANCH_WIDTH, rows(name + "_k", 2 * BRANCH_WIDTH))
    put(OFF_GATE, rows("gate", MIX_WIDTH))
    put(OFF_WIDX, jnp.concatenate([rows("c_widx", IDX_HEADS) * IDX_HEADS ** -0.5,
                                   jnp.zeros((128 - IDX_HEADS, tk), F32)], axis=0))


def _layout_w_in(w_in_t, li):
    d = w_in_t.shape[2]
    tk = 256
    return pl.pallas_call(
        _layout_w_in_kernel,
        out_shape=jax.ShapeDtypeStruct((d, PROJ_WIDTH), BF16),
        grid=(d // tk,),
        in_specs=[pl.BlockSpec((None, IN_WIDTH, tk), lambda i: (li, 0, i))],
        out_specs=pl.BlockSpec((tk, PROJ_WIDTH), lambda i: (i, 0)),
        compiler_params=_cparams("arbitrary"),
        name="layout_w_in",
    )(w_in_t)


def _layout_w_uq(w):
    r = w.shape[0]
    w = w.reshape(r, N_HEADS, MLA_NOPE + MLA_ROPE) * (LOG2E * (MLA_NOPE + MLA_ROPE) ** -0.5)
    z = jnp.zeros((r, N_HEADS, 128 - MLA_ROPE), w.dtype)
    rope = w[..., MLA_NOPE:]
    return jnp.concatenate([w[..., :MLA_NOPE], rope, z, _rot_cols(rope), z], axis=-1).reshape(r, -1).astype(BF16)


def kernel(x, c, w_ada, b_ada, g_pre, g_post, w_in, g_q_a, w_uq_a, g_kv_a, w_ukv_a,
           lam_q1, lam_k1, lam_q2, lam_k2, g_sub_d, w_out, rel_bias):
    b, s, d = x.shape
    depth = w_ada.shape[0]
    t = min(ATT_TILE, s)
    half = t // 2
    nq = s // t
    near_bias = min(nq, -(-(_first_far_diagonal(half) + 1) // 2))
    near_band = min(nq, -(-(DILATED_PATTERNS[-1][0] // half + 1) // 2))

    cos_t, sin_t = _rope_tables(s)
    bank_b = _bank(rel_bias[:, 0:N_HEADS], 2 * near_band + 1, half, -1, True)
    bank_cd = _bank(rel_bias[:, N_HEADS:3 * N_HEADS], 2 * near_bias + 1, half, -1, False)
    far_c = rel_bias[REL_BUCKETS - 1, N_HEADS:2 * N_HEADS]
    far_d = rel_bias[REL_BUCKETS - 1, 2 * N_HEADS:3 * N_HEADS]
    bank_spec = lambda n, group=0: pl.BlockSpec((N_HEADS, n, half, half), lambda bi, i: (group, 0, 0, 0),
                                                pipeline_mode=pl.Buffered(1))

    w_in_t = jnp.swapaxes(w_in, 1, 2)
    mod = _ada_mod(c, w_ada, b_ada)
    for li in range(depth):
        mod3 = mod[li].reshape(b, 1, 3 * d)
        proj = _in_proj(x, g_pre[li], mod3, _layout_w_in(w_in_t, li))

        q_a, k_a, v_a = _mla_prep(proj, cos_t, sin_t, g_q_a[li], g_kv_a[li],
                                  _layout_w_uq(w_uq_a[li]), w_ukv_a[li].astype(BF16))
        gate0 = OFF_GATE // BRANCH_WIDTH
        out_a = _attention("a", q_a, k_a, v_a, proj, 0, gate0)
        out_b = _attention("band", proj, proj, proj, proj, OFF_B // BRANCH_WIDTH, gate0 + 1,
                           extra_in=(bank_b,), extra_specs=(bank_spec(2 * near_band + 1),), near=near_band)
        sel = _select(proj)
        out_c = _attention("sel", proj, proj, proj, proj, OFF_C // BRANCH_WIDTH, gate0 + 2,
                           extra_in=(bank_cd, sel),
                           extra_specs=(bank_spec(2 * near_bias + 1, 0),
                                        pl.BlockSpec((None, t, s), lambda bi, i: (bi, i, 0))),
                           smem_in=(far_c,), near=near_bias)
        lam_init = 0.8 - 0.6 * math.exp(-0.3 * li)
        lam_vecs = jnp.stack([lam_q1[li], lam_k1[li], lam_q2[li], lam_k2[li]])
        out_d = _attention("diff", proj, proj, proj, proj, OFF_D // BRANCH_WIDTH, gate0 + 3,
                           extra_in=(bank_cd, lam_vecs, g_sub_d[li].reshape(1, HEAD_DIM)),
                           extra_specs=(bank_spec(2 * near_bias + 1, 1),
                                        pl.BlockSpec(lam_vecs.shape, lambda bi, i: (0, 0)),
                                        pl.BlockSpec((1, HEAD_DIM), lambda bi, i: (0, 0))),
                           smem_in=(far_d,), near=near_bias, lam_init=lam_init)
        x = _out_proj((out_a, out_b, out_c, out_d), w_out, li, x, mod3, g_post[li])
    return x
```

```python
import functools
import math

import numpy as np
import jax
import jax.numpy as jnp
from jax import lax
from jax.experimental import pallas as pl
from jax.experimental.pallas import tpu as pltpu

F32 = jnp.float32
BF16 = jnp.bfloat16

HEAD_DIM = 128
N_HEADS = 4
BRANCH_WIDTH = N_HEADS * HEAD_DIM
MIX_WIDTH = 4 * BRANCH_WIDTH
MLA_Q_RANK = 384
MLA_KV_RANK = 256
MLA_NOPE = 128
MLA_ROPE = 64
ROPE_THETA = 10000.0
DILATED_PATTERNS = ((128, 1), (512, 4), (2048, 16))
IDX_HEADS = 16
IDX_DIM = 64
TOPK_MAX = 256
REL_BUCKETS = 32
REL_MAX_DIST = 2048
NORM_EPS = 1e-6
NEG = -1e30
LOG2E = math.log2(math.e)

OFF_QIDX = 0
OFF_A = 1024
OFF_KIDX = OFF_A + 896
OFF_B = 2048
OFF_C = 3584
OFF_D = 5120
OFF_GATE = 6656
OFF_WIDX = 8704
PROJ_WIDTH = 8832

ATT_TILE = 512
VMEM_LIMIT = 56 * 1024 * 1024


def _cparams(*sem):
    return pltpu.CompilerParams(dimension_semantics=sem, vmem_limit_bytes=VMEM_LIMIT)


def _ada_kernel(c_ref, w_ref, b_ref, o_ref):
    c = c_ref[...]
    a = c * jax.nn.sigmoid(c)
    o_ref[...] = jnp.dot(a, w_ref[...], preferred_element_type=F32,
                         precision=lax.Precision.HIGHEST) + b_ref[...]


def _ada_mod(c, w_ada, b_ada):
    depth, d, n = w_ada.shape
    b = c.shape[0]
    tn = 768
    return pl.pallas_call(
        _ada_kernel,
        out_shape=jax.ShapeDtypeStruct((depth, b, n), F32),
        grid=(depth, n // tn),
        in_specs=[pl.BlockSpec((b, d), lambda l, j: (0, 0)),
                  pl.BlockSpec((None, d, tn), lambda l, j: (l, 0, j)),
                  pl.BlockSpec((None, 1, tn), lambda l, j: (l, 0, j))],
        out_specs=pl.BlockSpec((None, b, tn), lambda l, j: (l, 0, j)),
        compiler_params=_cparams("arbitrary", "arbitrary"),
        name="ada_mod",
    )(c, w_ada, b_ada.reshape(depth, 1, n))


def _in_proj_kernel(x_ref, g_ref, shift_ref, scale_ref, w_ref, o_ref):
    x = x_ref[...]
    y = x * lax.rsqrt(jnp.mean(x * x, axis=-1, keepdims=True) + NORM_EPS) * g_ref[...]
    h = (y * (1.0 + scale_ref[...]) + shift_ref[...]).astype(BF16)
    o_ref[...] = jnp.dot(h, w_ref[...], preferred_element_type=F32).astype(o_ref.dtype)


def _in_proj(x, g_pre, mod3, w):
    b, s, d = x.shape
    n = w.shape[1]
    tm = min(512, s)
    nt = s // tm
    tn = n // 3
    row = lambda j, i: (i // nt, i % nt, 0)
    return pl.pallas_call(
        _in_proj_kernel,
        out_shape=jax.ShapeDtypeStruct((b, s, n), BF16),
        grid=(n // tn, b * nt),
        in_specs=[pl.BlockSpec((None, tm, d), row),
                  pl.BlockSpec((1, d), lambda j, i: (0, 0)),
                  pl.BlockSpec((None, 1, d), lambda j, i: (i // nt, 0, 0)),
                  pl.BlockSpec((None, 1, d), lambda j, i: (i // nt, 0, 1)),
                  pl.BlockSpec((d, tn), lambda j, i: (0, j), pipeline_mode=pl.Buffered(1))],
        out_specs=pl.BlockSpec((None, tm, tn), lambda j, i: (i // nt, i % nt, j)),
        compiler_params=_cparams("arbitrary", "arbitrary"),
        name="in_proj",
    )(x, g_pre.reshape(1, d), mod3, mod3, w)


def _rms(x, g):
    return x * lax.rsqrt(jnp.mean(x * x, axis=-1, keepdims=True) + NORM_EPS) * g


def _mla_prep_kernel(p_ref, cos_ref, sin_ref, gq_ref, gkv_ref, wq_ref, wkv_ref, q_ref, k_ref, v_ref):
    cos = cos_ref[...]
    sin = sin_ref[...]
    cq = _rms(p_ref[:, 0:MLA_Q_RANK].astype(F32), gq_ref[...]).astype(BF16)
    ckv = _rms(p_ref[:, MLA_Q_RANK:MLA_Q_RANK + MLA_KV_RANK].astype(F32), gkv_ref[...]).astype(BF16)
    q = jnp.dot(cq, wq_ref[...], preferred_element_type=F32)
    kv = jnp.dot(ckv, wkv_ref[...], preferred_element_type=F32)
    k_rope = (p_ref[:, 640:768].astype(F32) * cos + p_ref[:, 768:896].astype(F32) * sin).astype(BF16)
    for h in range(N_HEADS):
        qh = q[:, h * 384:(h + 1) * 384]
        q_ref[:, h * 256:h * 256 + 128] = qh[:, 0:128].astype(BF16)
        q_ref[:, h * 256 + 128:(h + 1) * 256] = (qh[:, 128:256] * cos + qh[:, 256:384] * sin).astype(BF16)
        k_ref[:, h * 256:h * 256 + 128] = kv[:, h * 256:h * 256 + 128].astype(BF16)
        k_ref[:, h * 256 + 128:(h + 1) * 256] = k_rope
        v_ref[:, h * 128:(h + 1) * 128] = kv[:, h * 256 + 128:(h + 1) * 256].astype(BF16)


def _mla_prep(proj, cos_t, sin_t, g_q, g_kv, wq, wkv):
    b, s, _ = proj.shape
    tm = min(512, s)
    const = lambda bi, i: (0, 0)
    return pl.pallas_call(
        _mla_prep_kernel,
        out_shape=(jax.ShapeDtypeStruct((b, s, N_HEADS * 256), BF16),
                   jax.ShapeDtypeStruct((b, s, N_HEADS * 256), BF16),
                   jax.ShapeDtypeStruct((b, s, BRANCH_WIDTH), BF16)),
        grid=(b, s // tm),
        in_specs=[pl.BlockSpec((None, tm, 1024), lambda bi, i: (bi, i, OFF_A // 1024)),
                  pl.BlockSpec((tm, 128), lambda bi, i: (i, 0)),
                  pl.BlockSpec((tm, 128), lambda bi, i: (i, 0)),
                  pl.BlockSpec((1, MLA_Q_RANK), const),
                  pl.BlockSpec((1, MLA_KV_RANK), const),
                  pl.BlockSpec(wq.shape, const),
                  pl.BlockSpec(wkv.shape, const)],
        out_specs=(pl.BlockSpec((None, tm, N_HEADS * 256), lambda bi, i: (bi, i, 0)),
                   pl.BlockSpec((None, tm, N_HEADS * 256), lambda bi, i: (bi, i, 0)),
                   pl.BlockSpec((None, tm, BRANCH_WIDTH), lambda bi, i: (bi, i, 0))),
        compiler_params=_cparams("arbitrary", "arbitrary"),
        name="mla_prep",
    )(proj, cos_t, sin_t, g_q.reshape(1, -1), g_kv.reshape(1, -1), wq, wkv)


def _bucket_np(n):
    max_exact = REL_BUCKETS // 2
    nf = np.maximum(n, max_exact).astype(np.float32)
    large = max_exact + (np.log(nf / np.float32(max_exact)) / np.float32(math.log(REL_MAX_DIST / max_exact))
                         * np.float32(REL_BUCKETS - max_exact)).astype(np.int32)
    return np.where(n < max_exact, n, np.minimum(large, REL_BUCKETS - 1)).astype(np.int32)


def _bucket_starts():
    buckets = _bucket_np(np.arange(2 * REL_MAX_DIST))
    return [int(np.argmax(buckets >= b)) for b in range(REL_BUCKETS)]


def _bank_kernel(tab_ref, o_ref, *, t, d_min, band):
    d = pl.program_id(0) + d_min
    dist = (t * d + lax.broadcasted_iota(jnp.int32, (t, t), 0) - lax.broadcasted_iota(jnp.int32, (t, t), 1))
    starts = _bucket_starts()
    ge = [dist >= starts[b] for b in range(1, REL_BUCKETS)]
    if band:
        mult = jnp.zeros((t, t), jnp.int32)
        for window, dil in DILATED_PATTERNS:
            mult += jnp.where((dist >= 0) & (dist <= window) & ((dist & (dil - 1)) == 0), 1, 0)
        logm = jnp.where(mult == 3, math.log(3.0), jnp.where(mult == 2, math.log(2.0), 0.0))
        keep = mult > 0
    else:
        logm = 0.0
        keep = dist >= 0
    for h in range(o_ref.shape[0]):
        val = jnp.full((t, t), tab_ref[0, h], F32)
        for b in range(1, REL_BUCKETS):
            val = jnp.where(ge[b - 1], tab_ref[b, h], val)
        o_ref[h] = jnp.where(keep, (val + logm) * LOG2E, NEG)


def _bank(tab, n_tables, t, d_min, band):
    nh = tab.shape[1]
    return pl.pallas_call(
        functools.partial(_bank_kernel, t=t, d_min=d_min, band=band),
        out_shape=jax.ShapeDtypeStruct((nh, n_tables, t, t), F32),
        grid=(n_tables,),
        in_specs=[pl.BlockSpec(memory_space=pltpu.SMEM)],
        out_specs=pl.BlockSpec((nh, None, t, t), lambda j: (0, j, 0, 0)),
        compiler_params=_cparams("arbitrary"),
        name="bank_band" if band else "bank_bias",
    )(tab)


def _first_far_diagonal(t):
    last = _bucket_starts()[REL_BUCKETS - 1]
    return -(-(last + t - 1) // t)


def _qk(q, k):
    return lax.dot_general(q, k, (((1,), (1,)), ((), ())), preferred_element_type=F32)


def _flash_init(m_ref, acc_ref):
    m_ref[...] = jnp.full(m_ref.shape, NEG, F32)
    acc_ref[...] = jnp.zeros(acc_ref.shape, F32)


def _flash_update(slot, s, v, m_ref, acc_ref, s_ref, next_scores):
    m_prev = m_ref[slot]
    m_new = jnp.maximum(m_prev, jnp.max(s, axis=-1, keepdims=True))
    alpha = jnp.exp2(m_prev - m_new)
    p = jnp.concatenate([jnp.exp2(s[:, j * 128:(j + 1) * 128] - m_new) for j in range(s.shape[1] // 128)],
                        axis=1).astype(BF16)
    if next_scores is not None:
        s_ref[slot] = next_scores()
    v_ones = jnp.concatenate([v, jnp.ones_like(v)], axis=1)
    acc_ref[slot] = (jnp.concatenate([alpha, alpha], axis=1) * acc_ref[slot]
                     + jnp.dot(p, v_ones, preferred_element_type=F32))
    m_ref[slot] = m_new


def _flash_result(slot, acc_ref):
    acc = acc_ref[slot]
    return acc[:, :HEAD_DIM] / acc[:, HEAD_DIM:]


def _silu(g):
    return g * jax.nn.sigmoid(g)


def _tile(ref, kj, t, c0, width):
    start = kj * t if isinstance(kj, int) else pl.multiple_of(kj * t, t)
    return ref[pl.ds(start, t), c0:c0 + width]


def _bank_tile(bank_ref, h, i, kj):
    base = 2 * (i - kj) + 1
    top = jnp.concatenate([bank_ref[h, base], bank_ref[h, base - 1]], axis=1)
    bot = jnp.concatenate([bank_ref[h, base + 1], bank_ref[h, base]], axis=1)
    return jnp.concatenate([top, bot], axis=0)


def _emit(o_ref, g_ref, h, o):
    gate = g_ref[:, h * 128:(h + 1) * 128].astype(F32)
    o_ref[:, h * 128:(h + 1) * 128] = (o * _silu(gate)).astype(BF16)


def _flash_walk(i, first_key, n_far, slots, score, logits, value, s_ref, m_ref, acc_ref, prep=None):
    lo = first_key(i)
    i_next = jnp.minimum(i + 1, pl.num_programs(1) - 1)
    _flash_init(m_ref, acc_ref)

    @pl.when(i == 0)
    def _():
        for slot in range(slots):
            s_ref[slot] = score(slot, False, lo)

    def step(kj, phase):
        ctx = prep(kj) if prep is not None else None
        for slot in range(slots):
            s = logits(slot, s_ref[slot], kj, phase, ctx)
            if phase == "last":
                nxt = functools.partial(score, slot, True, first_key(i_next))
            else:
                nxt = functools.partial(score, slot, False, kj + 1)
            _flash_update(slot, s, value(slot, kj), m_ref, acc_ref, s_ref, nxt)

    def walk(a, b, phase):
        def body(kj, carry):
            step(kj, phase)
            return carry
        lax.fori_loop(a, b, body, 0)

    if n_far is not None:
        walk(lo, n_far, "far")
        lo = n_far
    walk(lo, i, "near")
    step(i, "last")


def _attn_a_kernel(q_ref, qn_ref, k_ref, v_ref, g_ref, o_ref, s_ref, m_ref, acc_ref, *, t):
    i = pl.program_id(1)

    def score(h, next_q, kj):
        q = (qn_ref if next_q else q_ref)[:, h * 256:(h + 1) * 256]
        return _qk(q, _tile(k_ref, kj, t, h * 256, 256))

    def logits(h, s, kj, phase, ctx):
        if phase != "last":
            return s
        causal = lax.broadcasted_iota(jnp.int32, (t, t), 1) <= lax.broadcasted_iota(jnp.int32, (t, t), 0)
        return jnp.where(causal, s, NEG)

    def value(h, kj):
        return _tile(v_ref, kj, t, h * 128, 128)

    _flash_walk(i, lambda qi: 0, None, N_HEADS, score, logits, value, s_ref, m_ref, acc_ref)
    for h in range(N_HEADS):
        _emit(o_ref, g_ref, h, _flash_result(h, acc_ref))


def _attn_band_kernel(q_ref, qn_ref, k_ref, v_ref, g_ref, bank_ref, o_ref, s_ref, m_ref, acc_ref, *, t, near):
    i = pl.program_id(1)

    def score(h, next_q, kj):
        q = (qn_ref if next_q else q_ref)[:, h * 128:(h + 1) * 128]
        return _qk(q, _tile(k_ref, kj, t, h * 128, 128))

    def logits(h, s, kj, phase, ctx):
        return s + _bank_tile(bank_ref, h, i, kj)

    def value(h, kj):
        return _tile(v_ref, kj, t, h * 128, 128)

    _flash_walk(i, lambda qi: jnp.maximum(qi - (near - 1), 0), None, N_HEADS, score, logits, value,
                s_ref, m_ref, acc_ref)
    for h in range(N_HEADS):
        _emit(o_ref, g_ref, h, _flash_result(h, acc_ref))


def _attn_sel_kernel(far_ref, q_ref, qn_ref, k_ref, v_ref, g_ref, bank_ref, sel_ref, o_ref,
                     s_ref, m_ref, acc_ref, *, t, near):
    i = pl.program_id(1)

    def score(h, next_q, kj):
        q = (qn_ref if next_q else q_ref)[:, h * 128:(h + 1) * 128]
        return _qk(q, _tile(k_ref, kj, t, h * 128, 128))

    def prep(kj):
        return sel_ref[:, pl.ds(pl.multiple_of(kj * t, t), t)].astype(F32)

    def logits(h, s, kj, phase, sel):
        bias = far_ref[h] * LOG2E if phase == "far" else _bank_tile(bank_ref, h, i, kj)
        return s + (sel + bias)

    def value(h, kj):
        return _tile(v_ref, kj, t, h * 128, 128)

    _flash_walk(i, lambda qi: 0, jnp.maximum(i - (near - 1), 0), N_HEADS, score, logits, value,
                s_ref, m_ref, acc_ref, prep)
    for h in range(N_HEADS):
        _emit(o_ref, g_ref, h, _flash_result(h, acc_ref))


def _attn_diff_kernel(far_ref, q_ref, qn_ref, k_ref, v_ref, g_ref, bank_ref, lam_ref, gsub_ref, o_ref,
                      s_ref, m_ref, acc_ref, *, t, near, lam_init):
    i = pl.program_id(1)
    first_half = lax.broadcasted_iota(jnp.int32, (t, HEAD_DIM), 1) < HEAD_DIM // 2

    def score(slot, next_q, kj):
        h = slot // 2
        q = (qn_ref if next_q else q_ref)[:, h * 128:(h + 1) * 128]
        keep = first_half if slot % 2 == 0 else jnp.logical_not(first_half)
        q = jnp.where(keep, q, jnp.zeros_like(q))
        return _qk(q, _tile(k_ref, kj, t, h * 128, 128))

    def logits(slot, s, kj, phase, ctx):
        h = slot // 2
        return s + (far_ref[h] * LOG2E if phase == "far" else _bank_tile(bank_ref, h, i, kj))

    def value(slot, kj):
        return _tile(v_ref, kj, t, (slot // 2) * 128, 128)

    _flash_walk(i, lambda qi: 0, jnp.maximum(i - (near - 1), 0), 2 * N_HEADS, score, logits, value,
                s_ref, m_ref, acc_ref)
    lam_v = lam_ref[...]
    lam = (jnp.exp(jnp.sum(lam_v[0:1] * lam_v[1:2], axis=-1, keepdims=True))
           - jnp.exp(jnp.sum(lam_v[2:3] * lam_v[3:4], axis=-1, keepdims=True)) + lam_init)
    for h in range(N_HEADS):
        o = _flash_result(2 * h, acc_ref) - lam * _flash_result(2 * h + 1, acc_ref)
        _emit(o_ref, g_ref, h, _rms(o, gsub_ref[...]) * (1.0 - lam_init))


def _attention(kind, q_arr, k_arr, v_arr, proj, q_blk, gate_blk, extra_in=(), extra_specs=(),
               smem_in=(), **kw):
    b, s, _ = proj.shape
    t = min(ATT_TILE, s)
    dk = 256 if kind == "a" else 128
    qw = N_HEADS * dk
    k_blk = 0 if kind == "a" else q_blk + 1
    v_blk = 0 if kind == "a" else q_blk + 2
    body = {"a": _attn_a_kernel, "band": _attn_band_kernel, "sel": _attn_sel_kernel,
            "diff": _attn_diff_kernel}[kind]
    slots = 2 * N_HEADS if kind == "diff" else N_HEADS
    scratch = [pltpu.VMEM((slots, t, t), F32), pltpu.VMEM((slots, t, HEAD_DIM), F32),
               pltpu.VMEM((slots, t, 2 * HEAD_DIM), F32)]
    last = s // t - 1
    in_specs = [pl.BlockSpec(memory_space=pltpu.SMEM) for _ in smem_in]
    in_specs += [pl.BlockSpec((None, t, qw), lambda bi, i: (bi, i, q_blk)),
                 pl.BlockSpec((None, t, qw), lambda bi, i: (bi, jnp.minimum(i + 1, last), q_blk)),
                 pl.BlockSpec((None, s, qw), lambda bi, i: (bi, 0, k_blk)),
                 pl.BlockSpec((None, s, BRANCH_WIDTH), lambda bi, i: (bi, 0, v_blk)),
                 pl.BlockSpec((None, t, BRANCH_WIDTH), lambda bi, i: (bi, i, gate_blk))]
    in_specs += list(extra_specs)
    return pl.pallas_call(
        functools.partial(body, t=t, **kw),
        out_shape=jax.ShapeDtypeStruct((b, s, BRANCH_WIDTH), BF16),
        grid=(b, s // t),
        in_specs=in_specs,
        out_specs=pl.BlockSpec((None, t, BRANCH_WIDTH), lambda bi, i: (bi, i, 0)),
        scratch_shapes=scratch,
        compiler_params=_cparams("arbitrary", "arbitrary"),
        name="attn_" + kind,
    )(*smem_in, q_arr, q_arr, k_arr, v_arr, proj, *extra_in)


def _bit_transpose32(words):
    a = list(words)
    j, m = 16, 0x0000FFFF
    while j:
        for k in range(32):
            if not k & j:
                t = (a[k] ^ lax.shift_right_logical(a[k + j], jnp.int32(j))) & jnp.int32(m)
                a[k] = a[k] ^ t
                a[k + j] = a[k + j] ^ (t << j)
        j >>= 1
        m = (m ^ (m << j)) & 0xFFFFFFFF if j else m
    return a


def _select_kernel(qi_ref, ki_ref, wi_ref, o_ref, key_ref, plane_ref, alive_ref, *, tq, kc, n_sel):
    i = pl.program_id(1)
    s_len = o_ref.shape[1]
    n_ch = (i * tq + tq + kc - 1) // kc
    wpc = kc // 32
    int_min = jnp.int32(-2 ** 31)
    lane = lax.broadcasted_iota(jnp.int32, (tq, 128), 1)
    w_t = wi_ref[...].astype(F32).T
    q_heads = []
    for j in range(IDX_HEADS // 2):
        q2 = qi_ref[:, j * 128:(j + 1) * 128]
        q_heads.append(jnp.where(lane < IDX_DIM, q2, jnp.zeros_like(q2)))
        q_heads.append(jnp.where(lane >= IDX_DIM, q2, jnp.zeros_like(q2)))
    qpos = i * tq + lax.broadcasted_iota(jnp.int32, (kc, tq), 1)
    kiota = lax.broadcasted_iota(jnp.int32, (kc, tq), 0)

    def chunk(c):
        return pl.ds(pl.multiple_of(c * kc, kc), kc)

    def score_chunk(c, _):
        k = ki_ref[chunk(c), :]
        acc = jnp.zeros((kc, tq), F32)
        for hh in range(IDX_HEADS):
            acc = acc + jnp.maximum(_qk(k, q_heads[hh]), 0.0) * w_t[hh:hh + 1, :]
        acc = jnp.where(c * kc + kiota <= qpos, acc + 0.0, NEG)
        bits = pltpu.bitcast(acc, jnp.int32)
        keys = bits ^ ((bits >> 31) & jnp.int32(0x7FFFFFFF))
        key_ref[chunk(c), :] = keys
        ukeys = keys ^ int_min
        for blk in range(kc // 256):
            planes = _bit_transpose32([ukeys[blk * 256 + 8 * j:blk * 256 + 8 * j + 8, :] for j in range(32)])
            row = pl.multiple_of(c * wpc + blk * 8, 8)
            for b in range(32):
                plane_ref[b, pl.ds(row, 8), :] = planes[b]
        return 0

    lax.fori_loop(0, n_ch, score_chunk, 0)

    def clear_chunk(c, _):
        plane_ref[:, pl.ds(pl.multiple_of(c * wpc, wpc), wpc), :] = jnp.zeros((32, wpc, tq), jnp.int32)
        return 0

    lax.fori_loop(n_ch, s_len // kc, clear_chunk, 0)
    n_rows = s_len // 32
    word_row = lax.broadcasted_iota(jnp.int32, (n_rows, tq), 0)
    alive_ref[...] = jnp.where(word_row < n_ch * wpc, jnp.int32(-1), jnp.int32(0))

    def radix_step(bi, carry):
        thr, above = carry
        alive = alive_ref[...]
        plane = plane_ref[bi]
        ones = lax.population_count(alive & plane)
        ones = jnp.sum(jnp.sum(ones.reshape(n_rows // 8, 8, tq), axis=0), axis=0, keepdims=True)
        take = above + ones >= n_sel
        thr = jnp.where(take, thr | (jnp.int32(1) << (31 - bi)), thr)
        above = jnp.where(take, above, above + ones)
        alive_ref[...] = alive & (plane ^ jnp.where(take, jnp.int32(0), jnp.int32(-1)))
        return thr, above

    zeros = jnp.zeros((1, tq), jnp.int32)
    thr_u, above = lax.fori_loop(0, 32, radix_step, (zeros, zeros))
    thr = thr_u ^ int_min
    n_equal = lax.population_count(alive_ref[...])
    n_equal = jnp.sum(jnp.sum(n_equal.reshape(n_rows // 8, 8, tq), axis=0), axis=0, keepdims=True)
    need = n_sel - above
    masked_key = int(np.float32(NEG).view(np.int32)) ^ 0x7FFFFFFF
    tie = (n_equal > need) & (thr != masked_key)
    any_tie = jnp.max(jnp.where(tie, 1, 0)) > 0

    @pl.when(jnp.logical_not(any_tie))
    def _():
        def emit(c, _):
            keep = jnp.where(key_ref[chunk(c), :] >= thr, 0.0, NEG)
            o_ref[:, chunk(c)] = keep.T.astype(BF16)
            return 0

        lax.fori_loop(0, n_ch, emit, 0)

    @pl.when(any_tie)
    def _():
        def equal_below(bound):
            def body(c, part):
                hit = jnp.where((key_ref[chunk(c), :] == thr) & (c * kc + kiota < bound), 1, 0)
                return part + jnp.sum(hit.reshape(kc // 8, 8, tq), axis=0)
            part = lax.fori_loop(0, n_ch, body, jnp.zeros((8, tq), jnp.int32))
            return jnp.sum(part, axis=0, keepdims=True)

        n_bits = s_len.bit_length()

        def bound_step(bi, cut):
            cand = cut + (jnp.int32(1) << (n_bits - 1 - bi))
            ok = (cand <= s_len) & (equal_below(cand) <= need)
            return jnp.where(ok, cand, cut)

        cut = lax.fori_loop(0, n_bits, bound_step, zeros)

        def emit(c, _):
            keys = key_ref[chunk(c), :]
            kept = (keys > thr) | ((keys == thr) & (c * kc + kiota < cut))
            o_ref[:, chunk(c)] = jnp.where(kept, 0.0, NEG).T.astype(BF16)
            return 0

        lax.fori_loop(0, n_ch, emit, 0)

    def blank(c, _):
        o_ref[:, chunk(c)] = jnp.full((tq, kc), NEG, BF16)
        return 0

    lax.fori_loop(n_ch, s_len // kc, blank, 0)


def _select(proj):
    b, s, _ = proj.shape
    tq = min(256, s)
    kc = min(512, s)
    n_sel = min(TOPK_MAX, s // 4)
    return pl.pallas_call(
        functools.partial(_select_kernel, tq=tq, kc=kc, n_sel=n_sel),
        out_shape=jax.ShapeDtypeStruct((b, s, s), BF16),
        grid=(b, s // tq),
        in_specs=[pl.BlockSpec((None, tq, 1024), lambda bi, i: (bi, i, OFF_QIDX // 1024)),
                  pl.BlockSpec((None, s, 128), lambda bi, i: (bi, 0, OFF_KIDX // 128)),
                  pl.BlockSpec((None, tq, 128), lambda bi, i: (bi, i, OFF_WIDX // 128))],
        out_specs=pl.BlockSpec((None, tq, s), lambda bi, i: (bi, i, 0)),
        scratch_shapes=[pltpu.VMEM((s, tq), jnp.int32), pltpu.VMEM((32, s // 32, tq), jnp.int32),
                        pltpu.VMEM((s // 32, tq), jnp.int32)],
        compiler_params=_cparams("arbitrary", "arbitrary"),
        name="idx_select",
    )(proj, proj, proj)


def _out_kernel(a_ref, b_ref, c_ref, d_ref, w_ref, x_ref, gate_ref, g_ref, o_ref, wb_ref):
    @pl.when((pl.program_id(0) == 0) & (pl.program_id(1) == 0))
    def _():
        wb_ref[...] = w_ref[...].astype(BF16)

    y = jnp.dot(a_ref[...], wb_ref[0:512, :], preferred_element_type=F32)
    y += jnp.dot(b_ref[...], wb_ref[512:1024, :], preferred_element_type=F32)
    y += jnp.dot(c_ref[...], wb_ref[1024:1536, :], preferred_element_type=F32)
    y += jnp.dot(d_ref[...], wb_ref[1536:2048, :], preferred_element_type=F32)
    o_ref[...] = x_ref[...] + gate_ref[...] * _rms(y, g_ref[...])


def _out_proj(outs, w_out, li, x, mod3, g_post):
    b, s, d = x.shape
    tm = min(512, s)
    mix = lambda bi, i: (bi, i, 0)
    return pl.pallas_call(
        _out_kernel,
        out_shape=jax.ShapeDtypeStruct((b, s, d), F32),
        grid=(b, s // tm),
        in_specs=[pl.BlockSpec((None, tm, BRANCH_WIDTH), mix)] * 4
        + [pl.BlockSpec((None,) + w_out.shape[1:], lambda bi, i: (li, 0, 0), pipeline_mode=pl.Buffered(1)),
           pl.BlockSpec((None, tm, d), mix),
           pl.BlockSpec((None, 1, d), lambda bi, i: (bi, 0, 2)),
           pl.BlockSpec((1, d), lambda bi, i: (0, 0))],
        out_specs=pl.BlockSpec((None, tm, d), mix),
        scratch_shapes=[pltpu.VMEM(w_out.shape[1:], BF16)],
        compiler_params=_cparams("arbitrary", "arbitrary"),
        name="out_proj",
    )(*outs, w_out, x, mod3, g_post.reshape(1, d))


def _rope_tables(s):
    half = MLA_ROPE // 2
    inv = ROPE_THETA ** (-jnp.arange(half, dtype=F32) / half)
    ang = jnp.arange(s, dtype=F32)[:, None] * inv[None, :]
    z = jnp.zeros((s, 128 - MLA_ROPE), F32)
    cos, sin = jnp.cos(ang), jnp.sin(ang)
    return jnp.concatenate([cos, cos, z], axis=-1), jnp.concatenate([sin, sin, z], axis=-1)


def _rot_cols(w):
    half = w.shape[-1] // 2
    return jnp.concatenate([-w[..., half:], w[..., :half]], axis=-1)


IN_SPLITS = (("a_cq", 384), ("a_ckv", 256), ("a_krope", 64), ("b_q", 512), ("b_k", 512), ("b_v", 512),
             ("c_q", 512), ("c_k", 512), ("c_v", 512), ("c_qidx", 1024), ("c_kidx", 64), ("c_widx", 16),
             ("d_q", 512), ("d_k", 512), ("d_v", 512), ("gate", 2048))
IN_WIDTH = sum(width for _, width in IN_SPLITS)


def _layout_w_in_kernel(w_ref, o_ref):
    src, start = {}, 0
    for name, width in IN_SPLITS:
        src[name] = start
        start += width
    tk = w_ref.shape[1]

    def rows(name, width, offset=0):
        a = src[name] + offset
        return w_ref[a:a + width, :]

    def put(dst, val, scale=None):
        for r in range(0, val.shape[0], 512):
            piece = val[r:r + 512]
            if scale is not None:
                piece = piece * scale
            o_ref[:, dst + r:dst + r + piece.shape[0]] = piece.T.astype(BF16)

    z64 = jnp.zeros((64, tk), F32)
    half = MLA_ROPE // 2
    put(OFF_QIDX, rows("c_qidx", 1024), IDX_DIM ** -0.5)
    put(OFF_A, rows("a_cq", MLA_Q_RANK + MLA_KV_RANK))
    put(OFF_A + 640, jnp.concatenate(
        [rows("a_krope", MLA_ROPE), z64, -rows("a_krope", half, half), rows("a_krope", half), z64,
         rows("c_kidx", IDX_DIM), rows("c_kidx", IDX_DIM)], axis=0))
    for off, name, dim in ((OFF_B, "b", HEAD_DIM), (OFF_C, "c", HEAD_DIM), (OFF_D, "d", HEAD_DIM // 2)):
        put(off, rows(name + "_q", BRANCH_WIDTH), LOG2E * dim ** -0.5)
        put(off + BRANCH_WIDTH, rows(name + "_k", 2 * BRANCH_WIDTH))
    put(OFF_GATE, rows("gate", MIX_WIDTH))
    put(OFF_WIDX, jnp.concatenate([rows("c_widx", IDX_HEADS) * IDX_HEADS ** -0.5,
                                   jnp.zeros((128 - IDX_HEADS, tk), F32)], axis=0))


def _layout_w_in(w_in_t, li):
    d = w_in_t.shape[2]
    tk = 256
    return pl.pallas_call(
        _layout_w_in_kernel,
        out_shape=jax.ShapeDtypeStruct((d, PROJ_WIDTH), BF16),
        grid=(d // tk,),
        in_specs=[pl.BlockSpec((None, IN_WIDTH, tk), lambda i: (li, 0, i))],
        out_specs=pl.BlockSpec((tk, PROJ_WIDTH), lambda i: (i, 0)),
        compiler_params=_cparams("arbitrary"),
        name="layout_w_in",
    )(w_in_t)


def _layout_w_uq(w):
    r = w.shape[0]
    w = w.reshape(r, N_HEADS, MLA_NOPE + MLA_ROPE) * (LOG2E * (MLA_NOPE + MLA_ROPE) ** -0.5)
    z = jnp.zeros((r, N_HEADS, 128 - MLA_ROPE), w.dtype)
    rope = w[..., MLA_NOPE:]
    return jnp.concatenate([w[..., :MLA_NOPE], rope, z, _rot_cols(rope), z], axis=-1).reshape(r, -1).astype(BF16)


def kernel(x, c, w_ada, b_ada, g_pre, g_post, w_in, g_q_a, w_uq_a, g_kv_a, w_ukv_a,
           lam_q1, lam_k1, lam_q2, lam_k2, g_sub_d, w_out, rel_bias):
    b, s, d = x.shape
    depth = w_ada.shape[0]
    t = min(ATT_TILE, s)
    half = t // 2
    nq = s // t
    near_bias = min(nq, -(-(_first_far_diagonal(half) + 1) // 2))
    near_band = min(nq, -(-(DILATED_PATTERNS[-1][0] // half + 1) // 2))

    cos_t, sin_t = _rope_tables(s)
    bank_b = _bank(rel_bias[:, 0:N_HEADS], 2 * near_band + 1, half, -1, True)
    bank_cd = _bank(rel_bias[:, N_HEADS:3 * N_HEADS], 2 * near_bias + 1, half, -1, False)
    far_c = rel_bias[REL_BUCKETS - 1, N_HEADS:2 * N_HEADS]
    far_d = rel_bias[REL_BUCKETS - 1, 2 * N_HEADS:3 * N_HEADS]
    bank_spec = lambda n, group=0: pl.BlockSpec((N_HEADS, n, half, half), lambda bi, i: (group, 0, 0, 0),
                                                pipeline_mode=pl.Buffered(1))

    w_in_t = jnp.swapaxes(w_in, 1, 2)
    mod = _ada_mod(c, w_ada, b_ada)
    for li in range(depth):
        mod3 = mod[li].reshape(b, 1, 3 * d)
        proj = _in_proj(x, g_pre[li], mod3, _layout_w_in(w_in_t, li))

        q_a, k_a, v_a = _mla_prep(proj, cos_t, sin_t, g_q_a[li], g_kv_a[li],
                                  _layout_w_uq(w_uq_a[li]), w_ukv_a[li].astype(BF16))
        gate0 = OFF_GATE // BRANCH_WIDTH
        out_a = _attention("a", q_a, k_a, v_a, proj, 0, gate0)
        out_b = _attention("band", proj, proj, proj, proj, OFF_B // BRANCH_WIDTH, gate0 + 1,
                           extra_in=(bank_b,), extra_specs=(bank_spec(2 * near_band + 1),), near=near_band)
        sel = _select(proj)
        out_c = _attention("sel", proj, proj, proj, proj, OFF_C // BRANCH_WIDTH, gate0 + 2,
                           extra_in=(bank_cd, sel),
                           extra_specs=(bank_spec(2 * near_bias + 1, 0),
                                        pl.BlockSpec((None, t, s), lambda bi, i: (bi, i, 0))),
                           smem_in=(far_c,), near=near_bias)
        lam_init = 0.8 - 0.6 * math.exp(-0.3 * li)
        lam_vecs = jnp.stack([lam_q1[li], lam_k1[li], lam_q2[li], lam_k2[li]])
        out_d = _attention("diff", proj, proj, proj, proj, OFF_D // BRANCH_WIDTH, gate0 + 3,
                           extra_in=(bank_cd, lam_vecs, g_sub_d[li].reshape(1, HEAD_DIM)),
                           extra_specs=(bank_spec(2 * near_bias + 1, 1),
                                        pl.BlockSpec(lam_vecs.shape, lambda bi, i: (0, 0)),
                                        pl.BlockSpec((1, HEAD_DIM), lambda bi, i: (0, 0))),
                           smem_in=(far_d,), near=near_bias, lam_init=lam_init)
        x = _out_proj((out_a, out_b, out_c, out_d), w_out, li, x, mod3, g_post[li])
    return x
```

```python
import functools
import math

import numpy as np
import jax
import jax.numpy as jnp
from jax import lax
from jax.experimental import pallas as pl
from jax.experimental.pallas import tpu as pltpu

F32 = jnp.float32
BF16 = jnp.bfloat16

HEAD_DIM = 128
N_HEADS = 4
BRANCH_WIDTH = N_HEADS * HEAD_DIM
MIX_WIDTH = 4 * BRANCH_WIDTH
MLA_Q_RANK = 384
MLA_KV_RANK = 256
MLA_NOPE = 128
MLA_ROPE = 64
ROPE_THETA = 10000.0
DILATED_PATTERNS = ((128, 1), (512, 4), (2048, 16))
IDX_HEADS = 16
IDX_DIM = 64
TOPK_MAX = 256
REL_BUCKETS = 32
REL_MAX_DIST = 2048
NORM_EPS = 1e-6
NEG = -1e30
LOG2E = math.log2(math.e)

OFF_QIDX = 0
OFF_A = 1024
OFF_KIDX = OFF_A + 896
OFF_B = 2048
OFF_C = 3584
OFF_D = 5120
OFF_GATE = 6656
OFF_WIDX = 8704
PROJ_WIDTH = 8832

ATT_TILE = 512
VMEM_LIMIT = 56 * 1024 * 1024


def _cparams(*sem):
    return pltpu.CompilerParams(dimension_semantics=sem, vmem_limit_bytes=VMEM_LIMIT)


def _ada_kernel(c_ref, w_ref, b_ref, o_ref):
    c = c_ref[...]
    a = c * jax.nn.sigmoid(c)
    o_ref[...] = jnp.dot(a, w_ref[...], preferred_element_type=F32,
                         precision=lax.Precision.HIGHEST) + b_ref[...]


def _ada_mod(c, w_ada, b_ada):
    depth, d, n = w_ada.shape
    b = c.shape[0]
    tn = 768
    return pl.pallas_call(
        _ada_kernel,
        out_shape=jax.ShapeDtypeStruct((depth, b, n), F32),
        grid=(depth, n // tn),
        in_specs=[pl.BlockSpec((b, d), lambda l, j: (0, 0)),
                  pl.BlockSpec((None, d, tn), lambda l, j: (l, 0, j)),
                  pl.BlockSpec((None, 1, tn), lambda l, j: (l, 0, j))],
        out_specs=pl.BlockSpec((None, b, tn), lambda l, j: (l, 0, j)),
        compiler_params=_cparams("arbitrary", "arbitrary"),
        name="ada_mod",
    )(c, w_ada, b_ada.reshape(depth, 1, n))


def _in_proj_kernel(x_ref, g_ref, shift_ref, scale_ref, w_ref, o_ref):
    x = x_ref[...]
    y = x * lax.rsqrt(jnp.mean(x * x, axis=-1, keepdims=True) + NORM_EPS) * g_ref[...]
    h = (y * (1.0 + scale_ref[...]) + shift_ref[...]).astype(BF16)
    o_ref[...] = jnp.dot(h, w_ref[...], preferred_element_type=F32).astype(o_ref.dtype)


def _in_proj(x, g_pre, mod3, w):
    b, s, d = x.shape
    n = w.shape[1]
    tm = min(512, s)
    nt = s // tm
    tn = n // 3
    row = lambda j, i: (i // nt, i % nt, 0)
    return pl.pallas_call(
        _in_proj_kernel,
        out_shape=jax.ShapeDtypeStruct((b, s, n), BF16),
        grid=(n // tn, b * nt),
        in_specs=[pl.BlockSpec((None, tm, d), row),
                  pl.BlockSpec((1, d), lambda j, i: (0, 0)),
                  pl.BlockSpec((None, 1, d), lambda j, i: (i // nt, 0, 0)),
                  pl.BlockSpec((None, 1, d), lambda j, i: (i // nt, 0, 1)),
                  pl.BlockSpec((d, tn), lambda j, i: (0, j), pipeline_mode=pl.Buffered(1))],
        out_specs=pl.BlockSpec((None, tm, tn), lambda j, i: (i // nt, i % nt, j)),
        compiler_params=_cparams("arbitrary", "arbitrary"),
        name="in_proj",
    )(x, g_pre.reshape(1, d), mod3, mod3, w)


def _rms(x, g):
    return x * lax.rsqrt(jnp.mean(x * x, axis=-1, keepdims=True) + NORM_EPS) * g


def _mla_prep_kernel(p_ref, cos_ref, sin_ref, gq_ref, gkv_ref, wq_ref, wkv_ref, q_ref, k_ref, v_ref):
    cos = cos_ref[...]
    sin = sin_ref[...]
    cq = _rms(p_ref[:, 0:MLA_Q_RANK].astype(F32), gq_ref[...]).astype(BF16)
    ckv = _rms(p_ref[:, MLA_Q_RANK:MLA_Q_RANK + MLA_KV_RANK].astype(F32), gkv_ref[...]).astype(BF16)
    q = jnp.dot(cq, wq_ref[...], preferred_element_type=F32)
    kv = jnp.dot(ckv, wkv_ref[...], preferred_element_type=F32)
    k_rope = (p_ref[:, 640:768].astype(F32) * cos + p_ref[:, 768:896].astype(F32) * sin).astype(BF16)
    for h in range(N_HEADS):
        qh = q[:, h * 384:(h + 1) * 384]
        q_ref[:, h * 256:h * 256 + 128] = qh[:, 0:128].astype(BF16)
        q_ref[:, h * 256 + 128:(h + 1) * 256] = (qh[:, 128:256] * cos + qh[:, 256:384] * sin).astype(BF16)
        k_ref[:, h * 256:h * 256 + 128] = kv[:, h * 256:h * 256 + 128].astype(BF16)
        k_ref[:, h * 256 + 128:(h + 1) * 256] = k_rope
        v_ref[:, h * 128:(h + 1) * 128] = kv[:, h * 256 + 128:(h + 1) * 256].astype(BF16)


def _mla_prep(proj, cos_t, sin_t, g_q, g_kv, wq, wkv):
    b, s, _ = proj.shape
    tm = min(512, s)
    const = lambda bi, i: (0, 0)
    return pl.pallas_call(
        _mla_prep_kernel,
        out_shape=(jax.ShapeDtypeStruct((b, s, N_HEADS * 256), BF16),
                   jax.ShapeDtypeStruct((b, s, N_HEADS * 256), BF16),
                   jax.ShapeDtypeStruct((b, s, BRANCH_WIDTH), BF16)),
        grid=(b, s // tm),
        in_specs=[pl.BlockSpec((None, tm, 1024), lambda bi, i: (bi, i, OFF_A // 1024)),
                  pl.BlockSpec((tm, 128), lambda bi, i: (i, 0)),
                  pl.BlockSpec((tm, 128), lambda bi, i: (i, 0)),
                  pl.BlockSpec((1, MLA_Q_RANK), const),
                  pl.BlockSpec((1, MLA_KV_RANK), const),
                  pl.BlockSpec(wq.shape, const),
                  pl.BlockSpec(wkv.shape, const)],
        out_specs=(pl.BlockSpec((None, tm, N_HEADS * 256), lambda bi, i: (bi, i, 0)),
                   pl.BlockSpec((None, tm, N_HEADS * 256), lambda bi, i: (bi, i, 0)),
                   pl.BlockSpec((None, tm, BRANCH_WIDTH), lambda bi, i: (bi, i, 0))),
        compiler_params=_cparams("arbitrary", "arbitrary"),
        name="mla_prep",
    )(proj, cos_t, sin_t, g_q.reshape(1, -1), g_kv.reshape(1, -1), wq, wkv)


def _bucket_np(n):
    max_exact = REL_BUCKETS // 2
    nf = np.maximum(n, max_exact).astype(np.float32)
    large = max_exact + (np.log(nf / np.float32(max_exact)) / np.float32(math.log(REL_MAX_DIST / max_exact))
                         * np.float32(REL_BUCKETS - max_exact)).astype(np.int32)
    return np.where(n < max_exact, n, np.minimum(large, REL_BUCKETS - 1)).astype(np.int32)


def _bucket_starts():
    buckets = _bucket_np(np.arange(2 * REL_MAX_DIST))
    return [int(np.argmax(buckets >= b)) for b in range(REL_BUCKETS)]


def _bank_kernel(tab_ref, o_ref, *, t, d_min, band):
    d = pl.program_id(0) + d_min
    dist = (t * d + lax.broadcasted_iota(jnp.int32, (t, t), 0) - lax.broadcasted_iota(jnp.int32, (t, t), 1))
    starts = _bucket_starts()
    ge = [dist >= starts[b] for b in range(1, REL_BUCKETS)]
    if band:
        mult = jnp.zeros((t, t), jnp.int32)
        for window, dil in DILATED_PATTERNS:
            mult += jnp.where((dist >= 0) & (dist <= window) & ((dist & (dil - 1)) == 0), 1, 0)
        logm = jnp.where(mult == 3, math.log(3.0), jnp.where(mult == 2, math.log(2.0), 0.0))
        keep = mult > 0
    else:
        logm = 0.0
        keep = dist >= 0
    for h in range(o_ref.shape[0]):
        val = jnp.full((t, t), tab_ref[0, h], F32)
        for b in range(1, REL_BUCKETS):
            val = jnp.where(ge[b - 1], tab_ref[b, h], val)
        o_ref[h] = jnp.where(keep, (val + logm) * LOG2E, NEG)


def _bank(tab, n_tables, t, d_min, band):
    nh = tab.shape[1]
    return pl.pallas_call(
        functools.partial(_bank_kernel, t=t, d_min=d_min, band=band),
        out_shape=jax.ShapeDtypeStruct((nh, n_tables, t, t), F32),
        grid=(n_tables,),
        in_specs=[pl.BlockSpec(memory_space=pltpu.SMEM)],
        out_specs=pl.BlockSpec((nh, None, t, t), lambda j: (0, j, 0, 0)),
        compiler_params=_cparams("arbitrary"),
        name="bank_band" if band else "bank_bias",
    )(tab)


def _first_far_diagonal(t):
    last = _bucket_starts()[REL_BUCKETS - 1]
    return -(-(last + t - 1) // t)


def _qk(q, k):
    return lax.dot_general(q, k, (((1,), (1,)), ((), ())), preferred_element_type=F32)


def _flash_init(m_ref, acc_ref):
    m_ref[...] = jnp.full(m_ref.shape, NEG, F32)
    acc_ref[...] = jnp.zeros(acc_ref.shape, F32)


def _flash_update(slot, s, v, m_ref, acc_ref, s_ref, next_scores):
    m_prev = m_ref[slot]
    m_new = jnp.maximum(m_prev, jnp.max(s, axis=-1, keepdims=True))
    alpha = jnp.exp2(m_prev - m_new)
    p = jnp.concatenate([jnp.exp2(s[:, j * 128:(j + 1) * 128] - m_new) for j in range(s.shape[1] // 128)],
                        axis=1).astype(BF16)
    if next_scores is not None:
        s_ref[slot] = next_scores()
    v_ones = jnp.concatenate([v, jnp.ones_like(v)], axis=1)
    acc_ref[slot] = (jnp.concatenate([alpha, alpha], axis=1) * acc_ref[slot]
                     + jnp.dot(p, v_ones, preferred_element_type=F32))
    m_ref[slot] = m_new


def _flash_result(slot, acc_ref):
    acc = acc_ref[slot]
    return acc[:, :HEAD_DIM] / acc[:, HEAD_DIM:]


def _silu(g):
    return g * jax.nn.sigmoid(g)


def _tile(ref, kj, t, c0, width):
    start = kj * t if isinstance(kj, int) else pl.multiple_of(kj * t, t)
    return ref[pl.ds(start, t), c0:c0 + width]


def _bank_tile(bank_ref, h, i, kj):
    base = 2 * (i - kj) + 1
    top = jnp.concatenate([bank_ref[h, base], bank_ref[h, base - 1]], axis=1)
    bot = jnp.concatenate([bank_ref[h, base + 1], bank_ref[h, base]], axis=1)
    return jnp.concatenate([top, bot], axis=0)


def _emit(o_ref, g_ref, h, o):
    gate = g_ref[:, h * 128:(h + 1) * 128].astype(F32)
    o_ref[:, h * 128:(h + 1) * 128] = (o * _silu(gate)).astype(BF16)


def _flash_walk(i, first_key, n_far, slots, score, logits, value, s_ref, m_ref, acc_ref, prep=None):
    lo = first_key(i)
    i_next = jnp.minimum(i + 1, pl.num_programs(1) - 1)
    _flash_init(m_ref, acc_ref)

    @pl.when(i == 0)
    def _():
        for slot in range(slots):
            s_ref[slot] = score(slot, False, lo)

    def step(kj, phase):
        ctx = prep(kj) if prep is not None else None
        for slot in range(slots):
            s = logits(slot, s_ref[slot], kj, phase, ctx)
            if phase == "last":
                nxt = functools.partial(score, slot, True, first_key(i_next))
            else:
                nxt = functools.partial(score, slot, False, kj + 1)
            _flash_update(slot, s, value(slot, kj), m_ref, acc_ref, s_ref, nxt)

    def walk(a, b, phase):
        def body(kj, carry):
            step(kj, phase)
            return carry
        lax.fori_loop(a, b, body, 0)

    if n_far is not None:
        walk(lo, n_far, "far")
        lo = n_far
    walk(lo, i, "near")
    step(i, "last")


def _attn_a_kernel(q_ref, qn_ref, k_ref, v_ref, g_ref, o_ref, s_ref, m_ref, acc_ref, *, t):
    i = pl.program_id(1)

    def score(h, next_q, kj):
        q = (qn_ref if next_q else q_ref)[:, h * 256:(h + 1) * 256]
        return _qk(q, _tile(k_ref, kj, t, h * 256, 256))

    def logits(h, s, kj, phase, ctx):
        if phase != "last":
            return s
        causal = lax.broadcasted_iota(jnp.int32, (t, t), 1) <= lax.broadcasted_iota(jnp.int32, (t, t), 0)
        return jnp.where(causal, s, NEG)

    def value(h, kj):
        return _tile(v_ref, kj, t, h * 128, 128)

    _flash_walk(i, lambda qi: 0, None, N_HEADS, score, logits, value, s_ref, m_ref, acc_ref)
    for h in range(N_HEADS):
        _emit(o_ref, g_ref, h, _flash_result(h, acc_ref))


def _attn_band_kernel(q_ref, qn_ref, k_ref, v_ref, g_ref, bank_ref, o_ref, s_ref, m_ref, acc_ref, *, t, near):
    i = pl.program_id(1)

    def score(h, next_q, kj):
        q = (qn_ref if next_q else q_ref)[:, h * 128:(h + 1) * 128]
        return _qk(q, _tile(k_ref, kj, t, h * 128, 128))

    def logits(h, s, kj, phase, ctx):
        return s + _bank_tile(bank_ref, h, i, kj)

    def value(h, kj):
        return _tile(v_ref, kj, t, h * 128, 128)

    _flash_walk(i, lambda qi: jnp.maximum(qi - (near - 1), 0), None, N_HEADS, score, logits, value,
                s_ref, m_ref, acc_ref)
    for h in range(N_HEADS):
        _emit(o_ref, g_ref, h, _flash_result(h, acc_ref))


def _attn_sel_kernel(far_ref, q_ref, qn_ref, k_ref, v_ref, g_ref, bank_ref, sel_ref, o_ref,
                     s_ref, m_ref, acc_ref, *, t, near):
    i = pl.program_id(1)

    def score(h, next_q, kj):
        q = (qn_ref if next_q else q_ref)[:, h * 128:(h + 1) * 128]
        return _qk(q, _tile(k_ref, kj, t, h * 128, 128))

    def prep(kj):
        return sel_ref[:, pl.ds(pl.multiple_of(kj * t, t), t)].astype(F32)

    def logits(h, s, kj, phase, sel):
        bias = far_ref[h] * LOG2E if phase == "far" else _bank_tile(bank_ref, h, i, kj)
        return s + (sel + bias)

    def value(h, kj):
        return _tile(v_ref, kj, t, h * 128, 128)

    _flash_walk(i, lambda qi: 0, jnp.maximum(i - (near - 1), 0), N_HEADS, score, logits, value,
                s_ref, m_ref, acc_ref, prep)
    for h in range(N_HEADS):
        _emit(o_ref, g_ref, h, _flash_result(h, acc_ref))


def _attn_diff_kernel(far_ref, q_ref, qn_ref, k_ref, v_ref, g_ref, bank_ref, lam_ref, gsub_ref, o_ref,
                      s_ref, m_ref, acc_ref, *, t, near, lam_init):
    i = pl.program_id(1)
    first_half = lax.broadcasted_iota(jnp.int32, (t, HEAD_DIM), 1) < HEAD_DIM // 2

    def score(slot, next_q, kj):
        h = slot // 2
        q = (qn_ref if next_q else q_ref)[:, h * 128:(h + 1) * 128]
        keep = first_half if slot % 2 == 0 else jnp.logical_not(first_half)
        q = jnp.where(keep, q, jnp.zeros_like(q))
        return _qk(q, _tile(k_ref, kj, t, h * 128, 128))

    def logits(slot, s, kj, phase, ctx):
        h = slot // 2
        return s + (far_ref[h] * LOG2E if phase == "far" else _bank_tile(bank_ref, h, i, kj))

    def value(slot, kj):
        return _tile(v_ref, kj, t, (slot // 2) * 128, 128)

    _flash_walk(i, lambda qi: 0, jnp.maximum(i - (near - 1), 0), 2 * N_HEADS, score, logits, value,
                s_ref, m_ref, acc_ref)
    lam_v = lam_ref[...]
    lam = (jnp.exp(jnp.sum(lam_v[0:1] * lam_v[1:2], axis=-1, keepdims=True))
           - jnp.exp(jnp.sum(lam_v[2:3] * lam_v[3:4], axis=-1, keepdims=True)) + lam_init)
    for h in range(N_HEADS):
        o = _flash_result(2 * h, acc_ref) - lam * _flash_result(2 * h + 1, acc_ref)
        _emit(o_ref, g_ref, h, _rms(o, gsub_ref[...]) * (1.0 - lam_init))


def _attention(kind, q_arr, k_arr, v_arr, proj, q_blk, gate_blk, extra_in=(), extra_specs=(),
               smem_in=(), **kw):
    b, s, _ = proj.shape
    t = min(ATT_TILE, s)
    dk = 256 if kind == "a" else 128
    qw = N_HEADS * dk
    k_blk = 0 if kind == "a" else q_blk + 1
    v_blk = 0 if kind == "a" else q_blk + 2
    body = {"a": _attn_a_kernel, "band": _attn_band_kernel, "sel": _attn_sel_kernel,
            "diff": _attn_diff_kernel}[kind]
    slots = 2 * N_HEADS if kind == "diff" else N_HEADS
    scratch = [pltpu.VMEM((slots, t, t), F32), pltpu.VMEM((slots, t, HEAD_DIM), F32),
               pltpu.VMEM((slots, t, 2 * HEAD_DIM), F32)]
    last = s // t - 1
    in_specs = [pl.BlockSpec(memory_space=pltpu.SMEM) for _ in smem_in]
    in_specs += [pl.BlockSpec((None, t, qw), lambda bi, i: (bi, i, q_blk)),
                 pl.BlockSpec((None, t, qw), lambda bi, i: (bi, jnp.minimum(i + 1, last), q_blk)),
                 pl.BlockSpec((None, s, qw), lambda bi, i: (bi, 0, k_blk)),
                 pl.BlockSpec((None, s, BRANCH_WIDTH), lambda bi, i: (bi, 0, v_blk)),
                 pl.BlockSpec((None, t, BRANCH_WIDTH), lambda bi, i: (bi, i, gate_blk))]
    in_specs += list(extra_specs)
    return pl.pallas_call(
        functools.partial(body, t=t, **kw),
        out_shape=jax.ShapeDtypeStruct((b, s, BRANCH_WIDTH), BF16),
        grid=(b, s // t),
        in_specs=in_specs,
        out_specs=pl.BlockSpec((None, t, BRANCH_WIDTH), lambda bi, i: (bi, i, 0)),
        scratch_shapes=scratch,
        compiler_params=_cparams("arbitrary", "arbitrary"),
        name="attn_" + kind,
    )(*smem_in, q_arr, q_arr, k_arr, v_arr, proj, *extra_in)


def _bit_transpose32(words):
    a = list(words)
    j, m = 16, 0x0000FFFF
    while j:
        for k in range(32):
            if not k & j:
                t = (a[k] ^ lax.shift_right_logical(a[k + j], jnp.int32(j))) & jnp.int32(m)
                a[k] = a[k] ^ t
                a[k + j] = a[k + j] ^ (t << j)
        j >>= 1
        m = (m ^ (m << j)) & 0xFFFFFFFF if j else m
    return a


def _select_kernel(qi_ref, ki_ref, wi_ref, o_ref, key_ref, plane_ref, alive_ref, *, tq, kc, n_sel):
    i = pl.program_id(1)
    s_len = o_ref.shape[1]
    n_ch = (i * tq + tq + kc - 1) // kc
    wpc = kc // 32
    int_min = jnp.int32(-2 ** 31)
    lane = lax.broadcasted_iota(jnp.int32, (tq, 128), 1)
    w_t = wi_ref[...].astype(F32).T
    q_heads = []
    for j in range(IDX_HEADS // 2):
        q2 = qi_ref[:, j * 128:(j + 1) * 128]
        q_heads.append(jnp.where(lane < IDX_DIM, q2, jnp.zeros_like(q2)))
        q_heads.append(jnp.where(lane >= IDX_DIM, q2, jnp.zeros_like(q2)))
    qpos = i * tq + lax.broadcasted_iota(jnp.int32, (kc, tq), 1)
    kiota = lax.broadcasted_iota(jnp.int32, (kc, tq), 0)

    def chunk(c):
        return pl.ds(pl.multiple_of(c * kc, kc), kc)

    def score_chunk(c, _):
        k = ki_ref[chunk(c), :]
        acc = jnp.zeros((kc, tq), F32)
        for hh in range(IDX_HEADS):
            acc = acc + jnp.maximum(_qk(k, q_heads[hh]), 0.0) * w_t[hh:hh + 1, :]
        acc = jnp.where(c * kc + kiota <= qpos, acc + 0.0, NEG)
        bits = pltpu.bitcast(acc, jnp.int32)
        keys = bits ^ ((bits >> 31) & jnp.int32(0x7FFFFFFF))
        key_ref[chunk(c), :] = keys
        ukeys = keys ^ int_min
        for blk in range(kc // 256):
            planes = _bit_transpose32([ukeys[blk * 256 + 8 * j:blk * 256 + 8 * j + 8, :] for j in range(32)])
            row = pl.multiple_of(c * wpc + blk * 8, 8)
            for b in range(32):
                plane_ref[b, pl.ds(row, 8), :] = planes[b]
        return 0

    lax.fori_loop(0, n_ch, score_chunk, 0)

    def clear_chunk(c, _):
        plane_ref[:, pl.ds(pl.multiple_of(c * wpc, wpc), wpc), :] = jnp.zeros((32, wpc, tq), jnp.int32)
        return 0

    lax.fori_loop(n_ch, s_len // kc, clear_chunk, 0)
    n_rows = s_len // 32
    word_row = lax.broadcasted_iota(jnp.int32, (n_rows, tq), 0)
    alive_ref[...] = jnp.where(word_row < n_ch * wpc, jnp.int32(-1), jnp.int32(0))

    def radix_step(bi, carry):
        thr, above = carry
        alive = alive_ref[...]
        plane = plane_ref[bi]
        ones = lax.population_count(alive & plane)
        ones = jnp.sum(jnp.sum(ones.reshape(n_rows // 8, 8, tq), axis=0), axis=0, keepdims=True)
        take = above + ones >= n_sel
        thr = jnp.where(take, thr | (jnp.int32(1) << (31 - bi)), thr)
        above = jnp.where(take, above, above + ones)
        alive_ref[...] = alive & (plane ^ jnp.where(take, jnp.int32(0), jnp.int32(-1)))
        return thr, above

    zeros = jnp.zeros((1, tq), jnp.int32)
    thr_u, above = lax.fori_loop(0, 32, radix_step, (zeros, zeros))
    thr = thr_u ^ int_min
    n_equal = lax.population_count(alive_ref[...])
    n_equal = jnp.sum(jnp.sum(n_equal.reshape(n_rows // 8, 8, tq), axis=0), axis=0, keepdims=True)
    need = n_sel - above
    masked_key = int(np.float32(NEG).view(np.int32)) ^ 0x7FFFFFFF
    tie = (n_equal > need) & (thr != masked_key)
    any_tie = jnp.max(jnp.where(tie, 1, 0)) > 0

    @pl.when(jnp.logical_not(any_tie))
    def _():
        def emit(c, _):
            keep = jnp.where(key_ref[chunk(c), :] >= thr, 0.0, NEG)
            o_ref[:, chunk(c)] = keep.T.astype(BF16)
            return 0

        lax.fori_loop(0, n_ch, emit, 0)

    @pl.when(any_tie)
    def _():
        def equal_below(bound):
            def body(c, part):
                hit = jnp.where((key_ref[chunk(c), :] == thr) & (c * kc + kiota < bound), 1, 0)
                return part + jnp.sum(hit.reshape(kc // 8, 8, tq), axis=0)
            part = lax.fori_loop(0, n_ch, body, jnp.zeros((8, tq), jnp.int32))
            return jnp.sum(part, axis=0, keepdims=True)

        n_bits = s_len.bit_length()

        def bound_step(bi, cut):
            cand = cut + (jnp.int32(1) << (n_bits - 1 - bi))
            ok = (cand <= s_len) & (equal_below(cand) <= need)
            return jnp.where(ok, cand, cut)

        cut = lax.fori_loop(0, n_bits, bound_step, zeros)

        def emit(c, _):
            keys = key_ref[chunk(c), :]
            kept = (keys > thr) | ((keys == thr) & (c * kc + kiota < cut))
            o_ref[:, chunk(c)] = jnp.where(kept, 0.0, NEG).T.astype(BF16)
            return 0

        lax.fori_loop(0, n_ch, emit, 0)

    def blank(c, _):
        o_ref[:, chunk(c)] = jnp.full((tq, kc), NEG, BF16)
        return 0

    lax.fori_loop(n_ch, s_len // kc, blank, 0)


def _select(proj):
    b, s, _ = proj.shape
    tq = min(512, s)
    kc = min(512, s)
    n_sel = min(TOPK_MAX, s // 4)
    return pl.pallas_call(
        functools.partial(_select_kernel, tq=tq, kc=kc, n_sel=n_sel),
        out_shape=jax.ShapeDtypeStruct((b, s, s), BF16),
        grid=(b, s // tq),
        in_specs=[pl.BlockSpec((None, tq, 1024), lambda bi, i: (bi, i, OFF_QIDX // 1024)),
                  pl.BlockSpec((None, s, 128), lambda bi, i: (bi, 0, OFF_KIDX // 128)),
                  pl.BlockSpec((None, tq, 128), lambda bi, i: (bi, i, OFF_WIDX // 128))],
        out_specs=pl.BlockSpec((None, tq, s), lambda bi, i: (bi, i, 0)),
        scratch_shapes=[pltpu.VMEM((s, tq), jnp.int32), pltpu.VMEM((32, s // 32, tq), jnp.int32),
                        pltpu.VMEM((s // 32, tq), jnp.int32)],
        compiler_params=_cparams("arbitrary", "arbitrary"),
        name="idx_select",
    )(proj, proj, proj)


def _out_kernel(a_ref, b_ref, c_ref, d_ref, w_ref, x_ref, gate_ref, g_ref, o_ref, wb_ref):
    @pl.when((pl.program_id(0) == 0) & (pl.program_id(1) == 0))
    def _():
        wb_ref[...] = w_ref[...].astype(BF16)

    y = jnp.dot(a_ref[...], wb_ref[0:512, :], preferred_element_type=F32)
    y += jnp.dot(b_ref[...], wb_ref[512:1024, :], preferred_element_type=F32)
    y += jnp.dot(c_ref[...], wb_ref[1024:1536, :], preferred_element_type=F32)
    y += jnp.dot(d_ref[...], wb_ref[1536:2048, :], preferred_element_type=F32)
    o_ref[...] = x_ref[...] + gate_ref[...] * _rms(y, g_ref[...])


def _out_proj(outs, w_out, li, x, mod3, g_post):
    b, s, d = x.shape
    tm = min(512, s)
    mix = lambda bi, i: (bi, i, 0)
    return pl.pallas_call(
        _out_kernel,
        out_shape=jax.ShapeDtypeStruct((b, s, d), F32),
        grid=(b, s // tm),
        in_specs=[pl.BlockSpec((None, tm, BRANCH_WIDTH), mix)] * 4
        + [pl.BlockSpec((None,) + w_out.shape[1:], lambda bi, i: (li, 0, 0), pipeline_mode=pl.Buffered(1)),
           pl.BlockSpec((None, tm, d), mix),
           pl.BlockSpec((None, 1, d), lambda bi, i: (bi, 0, 2)),
           pl.BlockSpec((1, d), lambda bi, i: (0, 0))],
        out_specs=pl.BlockSpec((None, tm, d), mix),
        scratch_shapes=[pltpu.VMEM(w_out.shape[1:], BF16)],
        compiler_params=_cparams("arbitrary", "arbitrary"),
        name="out_proj",
    )(*outs, w_out, x, mod3, g_post.reshape(1, d))


def _rope_tables(s):
    half = MLA_ROPE // 2
    inv = ROPE_THETA ** (-jnp.arange(half, dtype=F32) / half)
    ang = jnp.arange(s, dtype=F32)[:, None] * inv[None, :]
    z = jnp.zeros((s, 128 - MLA_ROPE), F32)
    cos, sin = jnp.cos(ang), jnp.sin(ang)
    return jnp.concatenate([cos, cos, z], axis=-1), jnp.concatenate([sin, sin, z], axis=-1)


def _rot_cols(w):
    half = w.shape[-1] // 2
    return jnp.concatenate([-w[..., half:], w[..., :half]], axis=-1)


IN_SPLITS = (("a_cq", 384), ("a_ckv", 256), ("a_krope", 64), ("b_q", 512), ("b_k", 512), ("b_v", 512),
             ("c_q", 512), ("c_k", 512), ("c_v", 512), ("c_qidx", 1024), ("c_kidx", 64), ("c_widx", 16),
             ("d_q", 512), ("d_k", 512), ("d_v", 512), ("gate", 2048))
IN_WIDTH = sum(width for _, width in IN_SPLITS)


def _layout_w_in_kernel(w_ref, o_ref):
    src, start = {}, 0
    for name, width in IN_SPLITS:
        src[name] = start
        start += width
    tk = w_ref.shape[1]

    def rows(name, width, offset=0):
        a = src[name] + offset
        return w_ref[a:a + width, :]

    def put(dst, val, scale=None):
        for r in range(0, val.shape[0], 512):
            piece = val[r:r + 512]
            if scale is not None:
                piece = piece * scale
            o_ref[:, dst + r:dst + r + piece.shape[0]] = piece.T.astype(BF16)

    z64 = jnp.zeros((64, tk), F32)
    half = MLA_ROPE // 2
    put(OFF_QIDX, rows("c_qidx", 1024), IDX_DIM ** -0.5)
    put(OFF_A, rows("a_cq", MLA_Q_RANK + MLA_KV_RANK))
    put(OFF_A + 640, jnp.concatenate(
        [rows("a_krope", MLA_ROPE), z64, -rows("a_krope", half, half), rows("a_krope", half), z64,
         rows("c_kidx", IDX_DIM), rows("c_kidx", IDX_DIM)], axis=0))
    for off, name, dim in ((OFF_B, "b", HEAD_DIM), (OFF_C, "c", HEAD_DIM), (OFF_D, "d", HEAD_DIM // 2)):
        put(off, rows(name + "_q", BRANCH_WIDTH), LOG2E * dim ** -0.5)
        put(off + BRANCH_WIDTH, rows(name + "_k", 2 * BRANCH_WIDTH))
    put(OFF_GATE, rows("gate", MIX_WIDTH))
    put(OFF_WIDX, jnp.concatenate([rows("c_widx", IDX_HEADS) * IDX_HEADS ** -0.5,
                                   jnp.zeros((128 - IDX_HEADS, tk), F32)], axis=0))


def _layout_w_in(w_in_t, li):
    d = w_in_t.shape[2]
    tk = 256
    return pl.pallas_call(
        _layout_w_in_kernel,
        out_shape=jax.ShapeDtypeStruct((d, PROJ_WIDTH), BF16),
        grid=(d // tk,),
        in_specs=[pl.BlockSpec((None, IN_WIDTH, tk), lambda i: (li, 0, i))],
        out_specs=pl.BlockSpec((tk, PROJ_WIDTH), lambda i: (i, 0)),
        compiler_params=_cparams("arbitrary"),
        name="layout_w_in",
    )(w_in_t)


def _layout_w_uq(w):
    r = w.shape[0]
    w = w.reshape(r, N_HEADS, MLA_NOPE + MLA_ROPE) * (LOG2E * (MLA_NOPE + MLA_ROPE) ** -0.5)
    z = jnp.zeros((r, N_HEADS, 128 - MLA_ROPE), w.dtype)
    rope = w[..., MLA_NOPE:]
    return jnp.concatenate([w[..., :MLA_NOPE], rope, z, _rot_cols(rope), z], axis=-1).reshape(r, -1).astype(BF16)


def kernel(x, c, w_ada, b_ada, g_pre, g_post, w_in, g_q_a, w_uq_a, g_kv_a, w_ukv_a,
           lam_q1, lam_k1, lam_q2, lam_k2, g_sub_d, w_out, rel_bias):
    b, s, d = x.shape
    depth = w_ada.shape[0]
    t = min(ATT_TILE, s)
    half = t // 2
    nq = s // t
    near_bias = min(nq, -(-(_first_far_diagonal(half) + 1) // 2))
    near_band = min(nq, -(-(DILATED_PATTERNS[-1][0] // half + 1) // 2))

    cos_t, sin_t = _rope_tables(s)
    bank_b = _bank(rel_bias[:, 0:N_HEADS], 2 * near_band + 1, half, -1, True)
    bank_cd = _bank(rel_bias[:, N_HEADS:3 * N_HEADS], 2 * near_bias + 1, half, -1, False)
    far_c = rel_bias[REL_BUCKETS - 1, N_HEADS:2 * N_HEADS]
    far_d = rel_bias[REL_BUCKETS - 1, 2 * N_HEADS:3 * N_HEADS]
    bank_spec = lambda n, group=0: pl.BlockSpec((N_HEADS, n, half, half), lambda bi, i: (group, 0, 0, 0),
                                                pipeline_mode=pl.Buffered(1))

    w_in_t = jnp.swapaxes(w_in, 1, 2)
    mod = _ada_mod(c, w_ada, b_ada)
    for li in range(depth):
        mod3 = mod[li].reshape(b, 1, 3 * d)
        proj = _in_proj(x, g_pre[li], mod3, _layout_w_in(w_in_t, li))

        q_a, k_a, v_a = _mla_prep(proj, cos_t, sin_t, g_q_a[li], g_kv_a[li],
                                  _layout_w_uq(w_uq_a[li]), w_ukv_a[li].astype(BF16))
        gate0 = OFF_GATE // BRANCH_WIDTH
        out_a = _attention("a", q_a, k_a, v_a, proj, 0, gate0)
        out_b = _attention("band", proj, proj, proj, proj, OFF_B // BRANCH_WIDTH, gate0 + 1,
                           extra_in=(bank_b,), extra_specs=(bank_spec(2 * near_band + 1),), near=near_band)
        sel = _select(proj)
        out_c = _attention("sel", proj, proj, proj, proj, OFF_C // BRANCH_WIDTH, gate0 + 2,
                           extra_in=(bank_cd, sel),
                           extra_specs=(bank_spec(2 * near_bias + 1, 0),
                                        pl.BlockSpec((None, t, s), lambda bi, i: (bi, i, 0))),
                           smem_in=(far_c,), near=near_bias)
        lam_init = 0.8 - 0.6 * math.exp(-0.3 * li)
        lam_vecs = jnp.stack([lam_q1[li], lam_k1[li], lam_q2[li], lam_k2[li]])
        out_d = _attention("diff", proj, proj, proj, proj, OFF_D // BRANCH_WIDTH, gate0 + 3,
                           extra_in=(bank_cd, lam_vecs, g_sub_d[li].reshape(1, HEAD_DIM)),
                           extra_specs=(bank_spec(2 * near_bias + 1, 1),
                                        pl.BlockSpec(lam_vecs.shape, lambda bi, i: (0, 0)),
                                        pl.BlockSpec((1, HEAD_DIM), lambda bi, i: (0, 0))),
                           smem_in=(far_d,), near=near_bias, lam_init=lam_init)
        x = _out_proj((out_a, out_b, out_c, out_d), w_out, li, x, mod3, g_post[li])
    return x
```

```python
import functools
import math

import numpy as np
import jax
import jax.numpy as jnp
from jax import lax
from jax.experimental import pallas as pl
from jax.experimental.pallas import tpu as pltpu

F32 = jnp.float32
BF16 = jnp.bfloat16

HEAD_DIM = 128
N_HEADS = 4
BRANCH_WIDTH = N_HEADS * HEAD_DIM
MIX_WIDTH = 4 * BRANCH_WIDTH
MLA_Q_RANK = 384
MLA_KV_RANK = 256
MLA_NOPE = 128
MLA_ROPE = 64
ROPE_THETA = 10000.0
DILATED_PATTERNS = ((128, 1), (512, 4), (2048, 16))
IDX_HEADS = 16
IDX_DIM = 64
TOPK_MAX = 256
REL_BUCKETS = 32
REL_MAX_DIST = 2048
NORM_EPS = 1e-6
NEG = -1e30
LOG2E = math.log2(math.e)

OFF_QIDX = 0
OFF_A = 1024
OFF_KIDX = OFF_A + 896
OFF_B = 2048
OFF_C = 3584
OFF_D = 5120
OFF_GATE = 6656
OFF_WIDX = 8704
PROJ_WIDTH = 8832

ATT_TILE = 512
VMEM_LIMIT = 56 * 1024 * 1024


def _cparams(*sem):
    return pltpu.CompilerParams(dimension_semantics=sem, vmem_limit_bytes=VMEM_LIMIT)


def _ada_kernel(c_ref, w_ref, b_ref, o_ref):
    c = c_ref[...]
    a = c * jax.nn.sigmoid(c)
    o_ref[...] = jnp.dot(a, w_ref[...], preferred_element_type=F32,
                         precision=lax.Precision.HIGHEST) + b_ref[...]


def _ada_mod(c, w_ada, b_ada):
    depth, d, n = w_ada.shape
    b = c.shape[0]
    tn = 768
    return pl.pallas_call(
        _ada_kernel,
        out_shape=jax.ShapeDtypeStruct((depth, b, n), F32),
        grid=(depth, n // tn),
        in_specs=[pl.BlockSpec((b, d), lambda l, j: (0, 0)),
                  pl.BlockSpec((None, d, tn), lambda l, j: (l, 0, j)),
                  pl.BlockSpec((None, 1, tn), lambda l, j: (l, 0, j))],
        out_specs=pl.BlockSpec((None, b, tn), lambda l, j: (l, 0, j)),
        compiler_params=_cparams("arbitrary", "arbitrary"),
        name="ada_mod",
    )(c, w_ada, b_ada.reshape(depth, 1, n))


def _in_proj_kernel(x_ref, g_ref, shift_ref, scale_ref, w_ref, o_ref):
    x = x_ref[...]
    y = x * lax.rsqrt(jnp.mean(x * x, axis=-1, keepdims=True) + NORM_EPS) * g_ref[...]
    h = (y * (1.0 + scale_ref[...]) + shift_ref[...]).astype(BF16)
    o_ref[...] = jnp.dot(h, w_ref[...], preferred_element_type=F32).astype(o_ref.dtype)


def _in_proj(x, g_pre, mod3, w):
    b, s, d = x.shape
    n = w.shape[1]
    tm = min(512, s)
    nt = s // tm
    tn = n // 3
    row = lambda j, i: (i // nt, i % nt, 0)
    return pl.pallas_call(
        _in_proj_kernel,
        out_shape=jax.ShapeDtypeStruct((b, s, n), BF16),
        grid=(n // tn, b * nt),
        in_specs=[pl.BlockSpec((None, tm, d), row),
                  pl.BlockSpec((1, d), lambda j, i: (0, 0)),
                  pl.BlockSpec((None, 1, d), lambda j, i: (i // nt, 0, 0)),
                  pl.BlockSpec((None, 1, d), lambda j, i: (i // nt, 0, 1)),
                  pl.BlockSpec((d, tn), lambda j, i: (0, j), pipeline_mode=pl.Buffered(1))],
        out_specs=pl.BlockSpec((None, tm, tn), lambda j, i: (i // nt, i % nt, j)),
        compiler_params=_cparams("arbitrary", "arbitrary"),
        name="in_proj",
    )(x, g_pre.reshape(1, d), mod3, mod3, w)


def _rms(x, g):
    return x * lax.rsqrt(jnp.mean(x * x, axis=-1, keepdims=True) + NORM_EPS) * g


def _mla_prep_kernel(p_ref, cos_ref, sin_ref, gq_ref, gkv_ref, wq_ref, wkv_ref, q_ref, k_ref, v_ref):
    cos = cos_ref[...]
    sin = sin_ref[...]
    cq = _rms(p_ref[:, 0:MLA_Q_RANK].astype(F32), gq_ref[...]).astype(BF16)
    ckv = _rms(p_ref[:, MLA_Q_RANK:MLA_Q_RANK + MLA_KV_RANK].astype(F32), gkv_ref[...]).astype(BF16)
    q = jnp.dot(cq, wq_ref[...], preferred_element_type=F32)
    kv = jnp.dot(ckv, wkv_ref[...], preferred_element_type=F32)
    k_rope = (p_ref[:, 640:768].astype(F32) * cos + p_ref[:, 768:896].astype(F32) * sin).astype(BF16)
    for h in range(N_HEADS):
        qh = q[:, h * 384:(h + 1) * 384]
        q_ref[:, h * 256:h * 256 + 128] = qh[:, 0:128].astype(BF16)
        q_ref[:, h * 256 + 128:(h + 1) * 256] = (qh[:, 128:256] * cos + qh[:, 256:384] * sin).astype(BF16)
        k_ref[:, h * 256:h * 256 + 128] = kv[:, h * 256:h * 256 + 128].astype(BF16)
        k_ref[:, h * 256 + 128:(h + 1) * 256] = k_rope
        v_ref[:, h * 128:(h + 1) * 128] = kv[:, h * 256 + 128:(h + 1) * 256].astype(BF16)


def _mla_prep(proj, cos_t, sin_t, g_q, g_kv, wq, wkv):
    b, s, _ = proj.shape
    tm = min(512, s)
    const = lambda bi, i: (0, 0)
    return pl.pallas_call(
        _mla_prep_kernel,
        out_shape=(jax.ShapeDtypeStruct((b, s, N_HEADS * 256), BF16),
                   jax.ShapeDtypeStruct((b, s, N_HEADS * 256), BF16),
                   jax.ShapeDtypeStruct((b, s, BRANCH_WIDTH), BF16)),
        grid=(b, s // tm),
        in_specs=[pl.BlockSpec((None, tm, 1024), lambda bi, i: (bi, i, OFF_A // 1024)),
                  pl.BlockSpec((tm, 128), lambda bi, i: (i, 0)),
                  pl.BlockSpec((tm, 128), lambda bi, i: (i, 0)),
                  pl.BlockSpec((1, MLA_Q_RANK), const),
                  pl.BlockSpec((1, MLA_KV_RANK), const),
                  pl.BlockSpec(wq.shape, const),
                  pl.BlockSpec(wkv.shape, const)],
        out_specs=(pl.BlockSpec((None, tm, N_HEADS * 256), lambda bi, i: (bi, i, 0)),
                   pl.BlockSpec((None, tm, N_HEADS * 256), lambda bi, i: (bi, i, 0)),
                   pl.BlockSpec((None, tm, BRANCH_WIDTH), lambda bi, i: (bi, i, 0))),
        compiler_params=_cparams("arbitrary", "arbitrary"),
        name="mla_prep",
    )(proj, cos_t, sin_t, g_q.reshape(1, -1), g_kv.reshape(1, -1), wq, wkv)


def _bucket_np(n):
    max_exact = REL_BUCKETS // 2
    nf = np.maximum(n, max_exact).astype(np.float32)
    large = max_exact + (np.log(nf / np.float32(max_exact)) / np.float32(math.log(REL_MAX_DIST / max_exact))
                         * np.float32(REL_BUCKETS - max_exact)).astype(np.int32)
    return np.where(n < max_exact, n, np.minimum(large, REL_BUCKETS - 1)).astype(np.int32)


def _bucket_starts():
    buckets = _bucket_np(np.arange(2 * REL_MAX_DIST))
    return [int(np.argmax(buckets >= b)) for b in range(REL_BUCKETS)]


def _bank_kernel(tab_ref, o_ref, *, t, d_min, band):
    d = pl.program_id(0) + d_min
    dist = (t * d + lax.broadcasted_iota(jnp.int32, (t, t), 1) - lax.broadcasted_iota(jnp.int32, (t, t), 0))
    starts = _bucket_starts()
    ge = [dist >= starts[b] for b in range(1, REL_BUCKETS)]
    if band:
        mult = jnp.zeros((t, t), jnp.int32)
        for window, dil in DILATED_PATTERNS:
            mult += jnp.where((dist >= 0) & (dist <= window) & ((dist & (dil - 1)) == 0), 1, 0)
        logm = jnp.where(mult == 3, math.log(3.0), jnp.where(mult == 2, math.log(2.0), 0.0))
        keep = mult > 0
    else:
        logm = 0.0
        keep = dist >= 0
    for h in range(o_ref.shape[0]):
        val = jnp.full((t, t), tab_ref[0, h], F32)
        for b in range(1, REL_BUCKETS):
            val = jnp.where(ge[b - 1], tab_ref[b, h], val)
        if not band:
            val = val - tab_ref[REL_BUCKETS - 1, h]
        o_ref[h] = jnp.where(keep, (val + logm) * LOG2E, NEG)


def _bank(tab, n_tables, t, d_min, band):
    nh = tab.shape[1]
    return pl.pallas_call(
        functools.partial(_bank_kernel, t=t, d_min=d_min, band=band),
        out_shape=jax.ShapeDtypeStruct((nh, n_tables, t, t), F32),
        grid=(n_tables,),
        in_specs=[pl.BlockSpec(memory_space=pltpu.SMEM)],
        out_specs=pl.BlockSpec((nh, None, t, t), lambda j: (0, j, 0, 0)),
        compiler_params=_cparams("arbitrary"),
        name="bank_band" if band else "bank_bias",
    )(tab)


def _first_far_diagonal(t):
    last = _bucket_starts()[REL_BUCKETS - 1]
    return -(-(last + t - 1) // t)


def _qk(q, k):
    return lax.dot_general(q, k, (((1,), (1,)), ((), ())), preferred_element_type=F32)


def _flash_init(m_ref, acc_ref):
    m_ref[...] = jnp.full(m_ref.shape, NEG, F32)
    acc_ref[...] = jnp.zeros(acc_ref.shape, F32)


def _flash_update(slot, logits_of, vt, m_ref, acc_ref, s_ref, next_scores):
    half = s_ref.shape[1] // 2
    s = jnp.concatenate([logits_of(b, s_ref[slot, b * half:(b + 1) * half, :]) for b in range(2)], axis=0)
    m_prev = m_ref[slot]
    m_new = jnp.maximum(m_prev, jnp.max(s, axis=0, keepdims=True))
    alpha = jnp.exp2(m_prev - m_new)
    p = jnp.exp2(s - m_new[0:1, :]).astype(BF16)
    if next_scores is not None:
        s_ref[slot] = next_scores()
    acc_ref[slot] = alpha[0:1, :] * acc_ref[slot] + jnp.dot(vt, p, preferred_element_type=F32)
    m_ref[slot] = m_new


def _flash_result(slot, acc_ref):
    acc = acc_ref[slot]
    return (acc[:HEAD_DIM] / acc[HEAD_DIM:HEAD_DIM + 1]).T


def _silu(g):
    return g * jax.nn.sigmoid(g)


def _tile(ref, kj, t, c0, width):
    start = kj * t if isinstance(kj, int) else pl.multiple_of(kj * t, t)
    return ref[pl.ds(start, t), c0:c0 + width]


def _bank_half(bank_ref, h, i, kj, b):
    base = 2 * (i - kj) + 1 - b
    return jnp.concatenate([bank_ref[h, base], bank_ref[h, base + 1]], axis=1)


def _emit(o_ref, g_ref, h, o):
    gate = g_ref[:, h * 128:(h + 1) * 128].astype(F32)
    o_ref[:, h * 128:(h + 1) * 128] = (o * _silu(gate)).astype(BF16)


def _flash_walk(i, first_key, n_far, slots, score, logits, v_ref, vt_ref, s_ref, m_ref, acc_ref):
    t = s_ref.shape[1]
    heads = vt_ref.shape[0]
    lo = first_key(i)
    i_next = jnp.minimum(i + 1, pl.num_programs(1) - 1)
    _flash_init(m_ref, acc_ref)

    @pl.when(i == 0)
    def _():
        for h in range(heads):
            for c in range(v_ref.shape[0] // t):
                v = v_ref[c * t:(c + 1) * t, h * HEAD_DIM:(h + 1) * HEAD_DIM]
                vt_ref[h, 0:HEAD_DIM, c * t:(c + 1) * t] = v.astype(F32).T.astype(BF16)
            vt_ref[h, HEAD_DIM:, :] = jnp.ones((vt_ref.shape[1] - HEAD_DIM, vt_ref.shape[2]), BF16)
        for slot in range(slots):
            s_ref[slot] = score(slot, False, lo)

    def step(kj, phase):
        for slot in range(slots):
            if phase == "last":
                nxt = functools.partial(score, slot, True, first_key(i_next))
            else:
                nxt = functools.partial(score, slot, False, kj + 1)
            start = kj * t if isinstance(kj, int) else pl.multiple_of(kj * t, t)
            vt = vt_ref[slot * heads // slots, :, pl.ds(start, t)]
            _flash_update(slot, functools.partial(logits, slot, kj, phase), vt, m_ref, acc_ref, s_ref, nxt)

    def walk(a, b, phase):
        def body(kj, carry):
            step(kj, phase)
            return carry
        lax.fori_loop(a, b, body, 0)

    if n_far is not None:
        walk(lo, n_far, "far")
        lo = n_far
    walk(lo, i, "near")
    step(i, "last")


def _attn_a_kernel(q_ref, qn_ref, k_ref, v_ref, g_ref, o_ref, vt_ref, s_ref, m_ref, acc_ref, *, t):
    i = pl.program_id(1)

    def score(h, next_q, kj):
        q = (qn_ref if next_q else q_ref)[:, h * 256:(h + 1) * 256]
        return _qk(_tile(k_ref, kj, t, h * 256, 256), q)

    def logits(h, kj, phase, b, s):
        if phase != "last":
            return s
        key = lax.broadcasted_iota(jnp.int32, s.shape, 0) + b * s.shape[0]
        return jnp.where(key <= lax.broadcasted_iota(jnp.int32, s.shape, 1), s, NEG)

    _flash_walk(i, lambda qi: 0, None, N_HEADS, score, logits, v_ref, vt_ref, s_ref, m_ref, acc_ref)
    for h in range(N_HEADS):
        _emit(o_ref, g_ref, h, _flash_result(h, acc_ref))


def _attn_band_kernel(q_ref, qn_ref, k_ref, v_ref, g_ref, bank_ref, o_ref, vt_ref, s_ref, m_ref, acc_ref,
                      *, t, near):
    i = pl.program_id(1)

    def score(h, next_q, kj):
        q = (qn_ref if next_q else q_ref)[:, h * 128:(h + 1) * 128]
        return _qk(_tile(k_ref, kj, t, h * 128, 128), q)

    def logits(h, kj, phase, b, s):
        return s + _bank_half(bank_ref, h, i, kj, b)

    _flash_walk(i, lambda qi: jnp.maximum(qi - (near - 1), 0), None, N_HEADS, score, logits,
                v_ref, vt_ref, s_ref, m_ref, acc_ref)
    for h in range(N_HEADS):
        _emit(o_ref, g_ref, h, _flash_result(h, acc_ref))


def _attn_sel_kernel(q_ref, qn_ref, k_ref, v_ref, g_ref, bank_ref, sel_ref, o_ref,
                     vt_ref, s_ref, m_ref, acc_ref, *, t, near):
    i = pl.program_id(1)

    def score(h, next_q, kj):
        q = (qn_ref if next_q else q_ref)[:, h * 128:(h + 1) * 128]
        return _qk(_tile(k_ref, kj, t, h * 128, 128), q)

    def logits(h, kj, phase, b, s):
        rows = s.shape[0]
        sel = sel_ref[pl.ds(pl.multiple_of(kj * t + b * rows, rows), rows), :].astype(F32)
        return s + sel if phase == "far" else s + (sel + _bank_half(bank_ref, h, i, kj, b))

    _flash_walk(i, lambda qi: 0, jnp.maximum(i - (near - 1), 0), N_HEADS, score, logits,
                v_ref, vt_ref, s_ref, m_ref, acc_ref)
    for h in range(N_HEADS):
        _emit(o_ref, g_ref, h, _flash_result(h, acc_ref))


def _attn_diff_kernel(q_ref, qn_ref, k_ref, v_ref, g_ref, bank_ref, lam_ref, gsub_ref, o_ref,
                      vt_ref, s_ref, m_ref, acc_ref, *, t, near, lam_init):
    i = pl.program_id(1)
    first_half = lax.broadcasted_iota(jnp.int32, (t, HEAD_DIM), 1) < HEAD_DIM // 2

    def score(slot, next_q, kj):
        h = slot // 2
        q = (qn_ref if next_q else q_ref)[:, h * 128:(h + 1) * 128]
        keep = first_half if slot % 2 == 0 else jnp.logical_not(first_half)
        q = jnp.where(keep, q, jnp.zeros_like(q))
        return _qk(_tile(k_ref, kj, t, h * 128, 128), q)

    def logits(slot, kj, phase, b, s):
        h = slot // 2
        return s if phase == "far" else s + _bank_half(bank_ref, h, i, kj, b)

    _flash_walk(i, lambda qi: 0, jnp.maximum(i - (near - 1), 0), 2 * N_HEADS, score, logits,
                v_ref, vt_ref, s_ref, m_ref, acc_ref)
    lam_v = lam_ref[...]
    lam = (jnp.exp(jnp.sum(lam_v[0:1] * lam_v[1:2], axis=-1, keepdims=True))
           - jnp.exp(jnp.sum(lam_v[2:3] * lam_v[3:4], axis=-1, keepdims=True)) + lam_init)
    for h in range(N_HEADS):
        o = _flash_result(2 * h, acc_ref) - lam * _flash_result(2 * h + 1, acc_ref)
        _emit(o_ref, g_ref, h, _rms(o, gsub_ref[...]) * (1.0 - lam_init))


def _attention(kind, q_arr, k_arr, v_arr, proj, q_blk, gate_blk, extra_in=(), extra_specs=(), **kw):
    b, s, _ = proj.shape
    t = min(ATT_TILE, s)
    dk = 256 if kind == "a" else 128
    qw = N_HEADS * dk
    k_blk = 0 if kind == "a" else q_blk + 1
    v_blk = 0 if kind == "a" else q_blk + 2
    body = {"a": _attn_a_kernel, "band": _attn_band_kernel, "sel": _attn_sel_kernel,
            "diff": _attn_diff_kernel}[kind]
    slots = 2 * N_HEADS if kind == "diff" else N_HEADS
    acc_rows = HEAD_DIM + 16
    scratch = [pltpu.VMEM((N_HEADS, acc_rows, s), BF16), pltpu.VMEM((slots, t, t), F32),
               pltpu.VMEM((slots, 8, t), F32), pltpu.VMEM((slots, acc_rows, t), F32)]
    last = s // t - 1
    in_specs = [pl.BlockSpec((None, t, qw), lambda bi, i: (bi, i, q_blk)),
                pl.BlockSpec((None, t, qw), lambda bi, i: (bi, jnp.minimum(i + 1, last), q_blk)),
                pl.BlockSpec((None, s, qw), lambda bi, i: (bi, 0, k_blk)),
                pl.BlockSpec((None, s, BRANCH_WIDTH), lambda bi, i: (bi, 0, v_blk)),
                pl.BlockSpec((None, t, BRANCH_WIDTH), lambda bi, i: (bi, i, gate_blk))]
    in_specs += list(extra_specs)
    return pl.pallas_call(
        functools.partial(body, t=t, **kw),
        out_shape=jax.ShapeDtypeStruct((b, s, BRANCH_WIDTH), BF16),
        grid=(b, s // t),
        in_specs=in_specs,
        out_specs=pl.BlockSpec((None, t, BRANCH_WIDTH), lambda bi, i: (bi, i, 0)),
        scratch_shapes=scratch,
        compiler_params=_cparams("arbitrary", "arbitrary"),
        name="attn_" + kind,
    )(q_arr, q_arr, k_arr, v_arr, proj, *extra_in)


def _bit_transpose32(words):
    a = list(words)
    j, m = 16, 0x0000FFFF
    while j:
        for k in range(32):
            if not k & j:
                t = (a[k] ^ lax.shift_right_logical(a[k + j], jnp.int32(j))) & jnp.int32(m)
                a[k] = a[k] ^ t
                a[k + j] = a[k + j] ^ (t << j)
        j >>= 1
        m = (m ^ (m << j)) & 0xFFFFFFFF if j else m
    return a


def _select_kernel(qi_ref, ki_ref, wi_ref, o_ref, key_ref, plane_ref, alive_ref, *, tq, kc, n_sel):
    i = pl.program_id(1)
    s_len = o_ref.shape[0]
    n_ch = (i * tq + tq + kc - 1) // kc
    wpc = kc // 32
    int_min = jnp.int32(-2 ** 31)
    lane = lax.broadcasted_iota(jnp.int32, (tq, 128), 1)
    w_t = wi_ref[...].astype(F32).T
    q_heads = []
    for j in range(IDX_HEADS // 2):
        q2 = qi_ref[:, j * 128:(j + 1) * 128]
        q_heads.append(jnp.where(lane < IDX_DIM, q2, jnp.zeros_like(q2)))
        q_heads.append(jnp.where(lane >= IDX_DIM, q2, jnp.zeros_like(q2)))
    qpos = i * tq + lax.broadcasted_iota(jnp.int32, (kc, tq), 1)
    kiota = lax.broadcasted_iota(jnp.int32, (kc, tq), 0)

    def chunk(c):
        return pl.ds(pl.multiple_of(c * kc, kc), kc)

    def score_chunk(c, _):
        k = ki_ref[chunk(c), :]
        acc = jnp.zeros((kc, tq), F32)
        for hh in range(IDX_HEADS):
            acc = acc + jnp.maximum(_qk(k, q_heads[hh]), 0.0) * w_t[hh:hh + 1, :]
        acc = jnp.where(c * kc + kiota <= qpos, acc + 0.0, NEG)
        bits = pltpu.bitcast(acc, jnp.int32)
        keys = bits ^ ((bits >> 31) & jnp.int32(0x7FFFFFFF))
        key_ref[chunk(c), :] = keys
        ukeys = keys ^ int_min
        for blk in range(kc // 256):
            planes = _bit_transpose32([ukeys[blk * 256 + 8 * j:blk * 256 + 8 * j + 8, :] for j in range(32)])
            row = pl.multiple_of(c * wpc + blk * 8, 8)
            for b in range(32):
                plane_ref[b, pl.ds(row, 8), :] = planes[b]
        return 0

    lax.fori_loop(0, n_ch, score_chunk, 0)

    def clear_chunk(c, _):
        plane_ref[:, pl.ds(pl.multiple_of(c * wpc, wpc), wpc), :] = jnp.zeros((32, wpc, tq), jnp.int32)
        return 0

    lax.fori_loop(n_ch, s_len // kc, clear_chunk, 0)
    n_rows = s_len // 32
    word_row = lax.broadcasted_iota(jnp.int32, (n_rows, tq), 0)
    alive_ref[...] = jnp.where(word_row < n_ch * wpc, jnp.int32(-1), jnp.int32(0))

    def radix_step(bi, carry):
        thr, above = carry
        alive = alive_ref[...]
        plane = plane_ref[bi]
        ones = lax.population_count(alive & plane)
        ones = jnp.sum(jnp.sum(ones.reshape(n_rows // 8, 8, tq), axis=0), axis=0, keepdims=True)
        take = above + ones >= n_sel
        thr = jnp.where(take, thr | (jnp.int32(1) << (31 - bi)), thr)
        above = jnp.where(take, above, above + ones)
        alive_ref[...] = alive & (plane ^ jnp.where(take, jnp.int32(0), jnp.int32(-1)))
        return thr, above

    zeros = jnp.zeros((1, tq), jnp.int32)
    thr_u, above = lax.fori_loop(0, 32, radix_step, (zeros, zeros))
    thr = thr_u ^ int_min
    n_equal = lax.population_count(alive_ref[...])
    n_equal = jnp.sum(jnp.sum(n_equal.reshape(n_rows // 8, 8, tq), axis=0), axis=0, keepdims=True)
    need = n_sel - above
    masked_key = int(np.float32(NEG).view(np.int32)) ^ 0x7FFFFFFF
    tie = (n_equal > need) & (thr != masked_key)
    any_tie = jnp.max(jnp.where(tie, 1, 0)) > 0

    @pl.when(jnp.logical_not(any_tie))
    def _():
        def emit(c, _):
            keep = jnp.where(key_ref[chunk(c), :] >= thr, 0.0, NEG)
            o_ref[chunk(c), :] = keep.astype(BF16)
            return 0

        lax.fori_loop(0, n_ch, emit, 0)

    @pl.when(any_tie)
    def _():
        def equal_below(bound):
            def body(c, part):
                hit = jnp.where((key_ref[chunk(c), :] == thr) & (c * kc + kiota < bound), 1, 0)
                return part + jnp.sum(hit.reshape(kc // 8, 8, tq), axis=0)
            part = lax.fori_loop(0, n_ch, body, jnp.zeros((8, tq), jnp.int32))
            return jnp.sum(part, axis=0, keepdims=True)

        n_bits = s_len.bit_length()

        def bound_step(bi, cut):
            cand = cut + (jnp.int32(1) << (n_bits - 1 - bi))
            ok = (cand <= s_len) & (equal_below(cand) <= need)
            return jnp.where(ok, cand, cut)

        cut = lax.fori_loop(0, n_bits, bound_step, zeros)

        def emit(c, _):
            keys = key_ref[chunk(c), :]
            kept = (keys > thr) | ((keys == thr) & (c * kc + kiota < cut))
            o_ref[chunk(c), :] = jnp.where(kept, 0.0, NEG).astype(BF16)
            return 0

        lax.fori_loop(0, n_ch, emit, 0)

    def blank(c, _):
        o_ref[chunk(c), :] = jnp.full((kc, tq), NEG, BF16)
        return 0

    lax.fori_loop(n_ch, s_len // kc, blank, 0)


def _select(proj):
    b, s, _ = proj.shape
    tq = min(512, s)
    kc = min(512, s)
    n_sel = min(TOPK_MAX, s // 4)
    return pl.pallas_call(
        functools.partial(_select_kernel, tq=tq, kc=kc, n_sel=n_sel),
        out_shape=jax.ShapeDtypeStruct((b, s, s), BF16),
        grid=(b, s // tq),
        in_specs=[pl.BlockSpec((None, tq, 1024), lambda bi, i: (bi, i, OFF_QIDX // 1024)),
                  pl.BlockSpec((None, s, 128), lambda bi, i: (bi, 0, OFF_KIDX // 128)),
                  pl.BlockSpec((None, tq, 128), lambda bi, i: (bi, i, OFF_WIDX // 128))],
        out_specs=pl.BlockSpec((None, s, tq), lambda bi, i: (bi, 0, i)),
        scratch_shapes=[pltpu.VMEM((s, tq), jnp.int32), pltpu.VMEM((32, s // 32, tq), jnp.int32),
                        pltpu.VMEM((s // 32, tq), jnp.int32)],
        compiler_params=_cparams("arbitrary", "arbitrary"),
        name="idx_select",
    )(proj, proj, proj)


def _out_kernel(a_ref, b_ref, c_ref, d_ref, w_ref, x_ref, gate_ref, g_ref, o_ref, wb_ref):
    @pl.when((pl.program_id(0) == 0) & (pl.program_id(1) == 0))
    def _():
        wb_ref[...] = w_ref[...].astype(BF16)

    y = jnp.dot(a_ref[...], wb_ref[0:512, :], preferred_element_type=F32)
    y += jnp.dot(b_ref[...], wb_ref[512:1024, :], preferred_element_type=F32)
    y += jnp.dot(c_ref[...], wb_ref[1024:1536, :], preferred_element_type=F32)
    y += jnp.dot(d_ref[...], wb_ref[1536:2048, :], preferred_element_type=F32)
    o_ref[...] = x_ref[...] + gate_ref[...] * _rms(y, g_ref[...])


def _out_proj(outs, w_out, li, x, mod3, g_post):
    b, s, d = x.shape
    tm = min(512, s)
    mix = lambda bi, i: (bi, i, 0)
    return pl.pallas_call(
        _out_kernel,
        out_shape=jax.ShapeDtypeStruct((b, s, d), F32),
        grid=(b, s // tm),
        in_specs=[pl.BlockSpec((None, tm, BRANCH_WIDTH), mix)] * 4
        + [pl.BlockSpec((None,) + w_out.shape[1:], lambda bi, i: (li, 0, 0), pipeline_mode=pl.Buffered(1)),
           pl.BlockSpec((None, tm, d), mix),
           pl.BlockSpec((None, 1, d), lambda bi, i: (bi, 0, 2)),
           pl.BlockSpec((1, d), lambda bi, i: (0, 0))],
        out_specs=pl.BlockSpec((None, tm, d), mix),
        scratch_shapes=[pltpu.VMEM(w_out.shape[1:], BF16)],
        compiler_params=_cparams("arbitrary", "arbitrary"),
        name="out_proj",
    )(*outs, w_out, x, mod3, g_post.reshape(1, d))


def _rope_tables(s):
    half = MLA_ROPE // 2
    inv = ROPE_THETA ** (-jnp.arange(half, dtype=F32) / half)
    ang = jnp.arange(s, dtype=F32)[:, None] * inv[None, :]
    z = jnp.zeros((s, 128 - MLA_ROPE), F32)
    cos, sin = jnp.cos(ang), jnp.sin(ang)
    return jnp.concatenate([cos, cos, z], axis=-1), jnp.concatenate([sin, sin, z], axis=-1)


def _rot_cols(w):
    half = w.shape[-1] // 2
    return jnp.concatenate([-w[..., half:], w[..., :half]], axis=-1)


IN_SPLITS = (("a_cq", 384), ("a_ckv", 256), ("a_krope", 64), ("b_q", 512), ("b_k", 512), ("b_v", 512),
             ("c_q", 512), ("c_k", 512), ("c_v", 512), ("c_qidx", 1024), ("c_kidx", 64), ("c_widx", 16),
             ("d_q", 512), ("d_k", 512), ("d_v", 512), ("gate", 2048))
IN_WIDTH = sum(width for _, width in IN_SPLITS)


def _layout_w_in_kernel(w_ref, o_ref):
    src, start = {}, 0
    for name, width in IN_SPLITS:
        src[name] = start
        start += width
    tk = w_ref.shape[1]

    def rows(name, width, offset=0):
        a = src[name] + offset
        return w_ref[a:a + width, :]

    def put(dst, val, scale=None):
        for r in range(0, val.shape[0], 512):
            piece = val[r:r + 512]
            if scale is not None:
                piece = piece * scale
            o_ref[:, dst + r:dst + r + piece.shape[0]] = piece.T.astype(BF16)

    z64 = jnp.zeros((64, tk), F32)
    half = MLA_ROPE // 2
    put(OFF_QIDX, rows("c_qidx", 1024), IDX_DIM ** -0.5)
    put(OFF_A, rows("a_cq", MLA_Q_RANK + MLA_KV_RANK))
    put(OFF_A + 640, jnp.concatenate(
        [rows("a_krope", MLA_ROPE), z64, -rows("a_krope", half, half), rows("a_krope", half), z64,
         rows("c_kidx", IDX_DIM), rows("c_kidx", IDX_DIM)], axis=0))
    for off, name, dim in ((OFF_B, "b", HEAD_DIM), (OFF_C, "c", HEAD_DIM), (OFF_D, "d", HEAD_DIM // 2)):
        put(off, rows(name + "_q", BRANCH_WIDTH), LOG2E * dim ** -0.5)
        put(off + BRANCH_WIDTH, rows(name + "_k", 2 * BRANCH_WIDTH))
    put(OFF_GATE, rows("gate", MIX_WIDTH))
    put(OFF_WIDX, jnp.concatenate([rows("c_widx", IDX_HEADS) * IDX_HEADS ** -0.5,
                                   jnp.zeros((128 - IDX_HEADS, tk), F32)], axis=0))


def _layout_w_in(w_in_t, li):
    d = w_in_t.shape[2]
    tk = 256
    return pl.pallas_call(
        _layout_w_in_kernel,
        out_shape=jax.ShapeDtypeStruct((d, PROJ_WIDTH), BF16),
        grid=(d // tk,),
        in_specs=[pl.BlockSpec((None, IN_WIDTH, tk), lambda i: (li, 0, i))],
        out_specs=pl.BlockSpec((tk, PROJ_WIDTH), lambda i: (i, 0)),
        compiler_params=_cparams("arbitrary"),
        name="layout_w_in",
    )(w_in_t)


def _layout_w_uq(w):
    r = w.shape[0]
    w = w.reshape(r, N_HEADS, MLA_NOPE + MLA_ROPE) * (LOG2E * (MLA_NOPE + MLA_ROPE) ** -0.5)
    z = jnp.zeros((r, N_HEADS, 128 - MLA_ROPE), w.dtype)
    rope = w[..., MLA_NOPE:]
    return jnp.concatenate([w[..., :MLA_NOPE], rope, z, _rot_cols(rope), z], axis=-1).reshape(r, -1).astype(BF16)


def kernel(x, c, w_ada, b_ada, g_pre, g_post, w_in, g_q_a, w_uq_a, g_kv_a, w_ukv_a,
           lam_q1, lam_k1, lam_q2, lam_k2, g_sub_d, w_out, rel_bias):
    b, s, d = x.shape
    depth = w_ada.shape[0]
    t = min(ATT_TILE, s)
    half = t // 2
    nq = s // t
    near_bias = min(nq, -(-(_first_far_diagonal(half) + 1) // 2))
    near_band = min(nq, -(-(DILATED_PATTERNS[-1][0] // half + 1) // 2))

    cos_t, sin_t = _rope_tables(s)
    bank_b = _bank(rel_bias[:, 0:N_HEADS], 2 * near_band + 1, half, -1, True)
    bank_cd = _bank(rel_bias[:, N_HEADS:3 * N_HEADS], 2 * near_bias + 1, half, -1, False)
    bank_spec = lambda n, group=0: pl.BlockSpec((N_HEADS, n, half, half), lambda bi, i: (group, 0, 0, 0),
                                                pipeline_mode=pl.Buffered(1))

    w_in_t = jnp.swapaxes(w_in, 1, 2)
    mod = _ada_mod(c, w_ada, b_ada)
    for li in range(depth):
        mod3 = mod[li].reshape(b, 1, 3 * d)
        proj = _in_proj(x, g_pre[li], mod3, _layout_w_in(w_in_t, li))

        q_a, k_a, v_a = _mla_prep(proj, cos_t, sin_t, g_q_a[li], g_kv_a[li],
                                  _layout_w_uq(w_uq_a[li]), w_ukv_a[li].astype(BF16))
        gate0 = OFF_GATE // BRANCH_WIDTH
        out_a = _attention("a", q_a, k_a, v_a, proj, 0, gate0)
        out_b = _attention("band", proj, proj, proj, proj, OFF_B // BRANCH_WIDTH, gate0 + 1,
                           extra_in=(bank_b,), extra_specs=(bank_spec(2 * near_band + 1),), near=near_band)
        sel = _select(proj)
        out_c = _attention("sel", proj, proj, proj, proj, OFF_C // BRANCH_WIDTH, gate0 + 2,
                           extra_in=(bank_cd, sel),
                           extra_specs=(bank_spec(2 * near_bias + 1, 0),
                                        pl.BlockSpec((None, s, t), lambda bi, i: (bi, 0, i))),
                           near=near_bias)
        lam_init = 0.8 - 0.6 * math.exp(-0.3 * li)
        lam_vecs = jnp.stack([lam_q1[li], lam_k1[li], lam_q2[li], lam_k2[li]])
        out_d = _attention("diff", proj, proj, proj, proj, OFF_D // BRANCH_WIDTH, gate0 + 3,
                           extra_in=(bank_cd, lam_vecs, g_sub_d[li].reshape(1, HEAD_DIM)),
                           extra_specs=(bank_spec(2 * near_bias + 1, 1),
                                        pl.BlockSpec(lam_vecs.shape, lambda bi, i: (0, 0)),
                                        pl.BlockSpec((1, HEAD_DIM), lambda bi, i: (0, 0))),
                           near=near_bias, lam_init=lam_init)
        x = _out_proj((out_a, out_b, out_c, out_d), w_out, li, x, mod3, g_post[li])
    return x
```

```python
import functools
import math

import numpy as np
import jax
import jax.numpy as jnp
from jax import lax
from jax.experimental import pallas as pl
from jax.experimental.pallas import tpu as pltpu

F32 = jnp.float32
BF16 = jnp.bfloat16

HEAD_DIM = 128
N_HEADS = 4
BRANCH_WIDTH = N_HEADS * HEAD_DIM
MIX_WIDTH = 4 * BRANCH_WIDTH
MLA_Q_RANK = 384
MLA_KV_RANK = 256
MLA_NOPE = 128
MLA_ROPE = 64
ROPE_THETA = 10000.0
DILATED_PATTERNS = ((128, 1), (512, 4), (2048, 16))
IDX_HEADS = 16
IDX_DIM = 64
TOPK_MAX = 256
REL_BUCKETS = 32
REL_MAX_DIST = 2048
NORM_EPS = 1e-6
NEG = -1e30
LOG2E = math.log2(math.e)

OFF_QIDX = 0
OFF_A = 1024
OFF_KIDX = OFF_A + 896
OFF_B = 2048
OFF_C = 3584
OFF_D = 5120
OFF_GATE = 6656
OFF_WIDX = 8704
PROJ_WIDTH = 8832

ATT_TILE = 512
VMEM_LIMIT = 56 * 1024 * 1024


def _cparams(*sem):
    return pltpu.CompilerParams(dimension_semantics=sem, vmem_limit_bytes=VMEM_LIMIT)


def _ada_kernel(c_ref, w_ref, b_ref, o_ref):
    c = c_ref[...]
    a = c * jax.nn.sigmoid(c)
    o_ref[...] = jnp.dot(a, w_ref[...], preferred_element_type=F32,
                         precision=lax.Precision.HIGHEST) + b_ref[...]


def _ada_mod(c, w_ada, b_ada):
    depth, d, n = w_ada.shape
    b = c.shape[0]
    tn = 768
    return pl.pallas_call(
        _ada_kernel,
        out_shape=jax.ShapeDtypeStruct((depth, b, n), F32),
        grid=(depth, n // tn),
        in_specs=[pl.BlockSpec((b, d), lambda l, j: (0, 0)),
                  pl.BlockSpec((None, d, tn), lambda l, j: (l, 0, j)),
                  pl.BlockSpec((None, 1, tn), lambda l, j: (l, 0, j))],
        out_specs=pl.BlockSpec((None, b, tn), lambda l, j: (l, 0, j)),
        compiler_params=_cparams("arbitrary", "arbitrary"),
        name="ada_mod",
    )(c, w_ada, b_ada.reshape(depth, 1, n))


def _in_proj_kernel(x_ref, g_ref, shift_ref, scale_ref, w_ref, o_ref):
    x = x_ref[...]
    y = x * lax.rsqrt(jnp.mean(x * x, axis=-1, keepdims=True) + NORM_EPS) * g_ref[...]
    h = (y * (1.0 + scale_ref[...]) + shift_ref[...]).astype(BF16)
    o_ref[...] = jnp.dot(h, w_ref[...], preferred_element_type=F32).astype(o_ref.dtype)


def _in_proj(x, g_pre, mod3, w):
    b, s, d = x.shape
    n = w.shape[1]
    tm = min(512, s)
    nt = s // tm
    tn = n // 3
    row = lambda j, i: (i // nt, i % nt, 0)
    return pl.pallas_call(
        _in_proj_kernel,
        out_shape=jax.ShapeDtypeStruct((b, s, n), BF16),
        grid=(n // tn, b * nt),
        in_specs=[pl.BlockSpec((None, tm, d), row),
                  pl.BlockSpec((1, d), lambda j, i: (0, 0)),
                  pl.BlockSpec((None, 1, d), lambda j, i: (i // nt, 0, 0)),
                  pl.BlockSpec((None, 1, d), lambda j, i: (i // nt, 0, 1)),
                  pl.BlockSpec((d, tn), lambda j, i: (0, j), pipeline_mode=pl.Buffered(1))],
        out_specs=pl.BlockSpec((None, tm, tn), lambda j, i: (i // nt, i % nt, j)),
        compiler_params=_cparams("arbitrary", "arbitrary"),
        name="in_proj",
    )(x, g_pre.reshape(1, d), mod3, mod3, w)


def _rms(x, g):
    return x * lax.rsqrt(jnp.mean(x * x, axis=-1, keepdims=True) + NORM_EPS) * g


def _mla_prep_kernel(p_ref, cos_ref, sin_ref, gq_ref, gkv_ref, wq_ref, wkv_ref, q_ref, k_ref, v_ref):
    cos = cos_ref[...]
    sin = sin_ref[...]
    cq = _rms(p_ref[:, 0:MLA_Q_RANK].astype(F32), gq_ref[...]).astype(BF16)
    ckv = _rms(p_ref[:, MLA_Q_RANK:MLA_Q_RANK + MLA_KV_RANK].astype(F32), gkv_ref[...]).astype(BF16)
    q = jnp.dot(cq, wq_ref[...], preferred_element_type=F32)
    kv = jnp.dot(ckv, wkv_ref[...], preferred_element_type=F32)
    k_rope = (p_ref[:, 640:768].astype(F32) * cos + p_ref[:, 768:896].astype(F32) * sin).astype(BF16)
    for h in range(N_HEADS):
        qh = q[:, h * 384:(h + 1) * 384]
        q_ref[:, h * 256:h * 256 + 128] = qh[:, 0:128].astype(BF16)
        q_ref[:, h * 256 + 128:(h + 1) * 256] = (qh[:, 128:256] * cos + qh[:, 256:384] * sin).astype(BF16)
        k_ref[:, h * 256:h * 256 + 128] = kv[:, h * 256:h * 256 + 128].astype(BF16)
        k_ref[:, h * 256 + 128:(h + 1) * 256] = k_rope
        v_ref[:, h * 128:(h + 1) * 128] = kv[:, h * 256 + 128:(h + 1) * 256].astype(BF16)


def _mla_prep(proj, cos_t, sin_t, g_q, g_kv, wq, wkv):
    b, s, _ = proj.shape
    tm = min(512, s)
    const = lambda bi, i: (0, 0)
    return pl.pallas_call(
        _mla_prep_kernel,
        out_shape=(jax.ShapeDtypeStruct((b, s, N_HEADS * 256), BF16),
                   jax.ShapeDtypeStruct((b, s, N_HEADS * 256), BF16),
                   jax.ShapeDtypeStruct((b, s, BRANCH_WIDTH), BF16)),
        grid=(b, s // tm),
        in_specs=[pl.BlockSpec((None, tm, 1024), lambda bi, i: (bi, i, OFF_A // 1024)),
                  pl.BlockSpec((tm, 128), lambda bi, i: (i, 0)),
                  pl.BlockSpec((tm, 128), lambda bi, i: (i, 0)),
                  pl.BlockSpec((1, MLA_Q_RANK), const),
                  pl.BlockSpec((1, MLA_KV_RANK), const),
                  pl.BlockSpec(wq.shape, const),
                  pl.BlockSpec(wkv.shape, const)],
        out_specs=(pl.BlockSpec((None, tm, N_HEADS * 256), lambda bi, i: (bi, i, 0)),
                   pl.BlockSpec((None, tm, N_HEADS * 256), lambda bi, i: (bi, i, 0)),
                   pl.BlockSpec((None, tm, BRANCH_WIDTH), lambda bi, i: (bi, i, 0))),
        compiler_params=_cparams("arbitrary", "arbitrary"),
        name="mla_prep",
    )(proj, cos_t, sin_t, g_q.reshape(1, -1), g_kv.reshape(1, -1), wq, wkv)


def _bucket_np(n):
    max_exact = REL_BUCKETS // 2
    nf = np.maximum(n, max_exact).astype(np.float32)
    large = max_exact + (np.log(nf / np.float32(max_exact)) / np.float32(math.log(REL_MAX_DIST / max_exact))
                         * np.float32(REL_BUCKETS - max_exact)).astype(np.int32)
    return np.where(n < max_exact, n, np.minimum(large, REL_BUCKETS - 1)).astype(np.int32)


def _bucket_starts():
    buckets = _bucket_np(np.arange(2 * REL_MAX_DIST))
    return [int(np.argmax(buckets >= b)) for b in range(REL_BUCKETS)]


def _bank_kernel(tab_ref, o_ref, *, t, d_min, band):
    h = pl.program_id(0)
    starts = _bucket_starts()
    row = lax.broadcasted_iota(jnp.int32, (t, t), 0)
    col = lax.broadcasted_iota(jnp.int32, (t, t), 1)
    for j in range(o_ref.shape[0]):
        d = d_min + j
        if d < 0:
            o_ref[j] = jnp.full((t, t), NEG, F32)
            continue
        dist = t * d + row - col
        b_lo, b_hi = (int(x) for x in _bucket_np(np.array([max(t * d - t + 1, 0), t * d + t - 1])))
        val = jnp.full((t, t), tab_ref[b_lo, h], F32)
        for b in range(b_lo + 1, b_hi + 1):
            val = jnp.where(dist >= starts[b], tab_ref[b, h], val)
        if band:
            mult = jnp.zeros((t, t), jnp.int32)
            for window, dil in DILATED_PATTERNS:
                mult += jnp.where((dist >= 0) & (dist <= window) & ((dist & (dil - 1)) == 0), 1, 0)
            val = val + jnp.where(mult == 3, math.log(3.0), jnp.where(mult == 2, math.log(2.0), 0.0))
            keep = mult > 0
        else:
            val = val - tab_ref[REL_BUCKETS - 1, h]
            keep = dist >= 0
        o_ref[j] = jnp.where(keep, val * LOG2E, NEG)


def _bank(tab, n_tables, t, d_min, band):
    nh = tab.shape[1]
    return pl.pallas_call(
        functools.partial(_bank_kernel, t=t, d_min=d_min, band=band),
        out_shape=jax.ShapeDtypeStruct((nh, n_tables, t, t), F32),
        grid=(nh,),
        in_specs=[pl.BlockSpec(memory_space=pltpu.SMEM)],
        out_specs=pl.BlockSpec((None, n_tables, t, t), lambda h: (h, 0, 0, 0)),
        compiler_params=_cparams("arbitrary"),
        name="bank_band" if band else "bank_bias",
    )(tab)


def _first_far_diagonal(t):
    last = _bucket_starts()[REL_BUCKETS - 1]
    return -(-(last + t - 1) // t)


def _qk(q, k):
    return lax.dot_general(q, k, (((1,), (1,)), ((), ())), preferred_element_type=F32)


def _flash_init(m_ref, acc_ref):
    m_ref[...] = jnp.full(m_ref.shape, NEG, F32)
    acc_ref[...] = jnp.zeros(acc_ref.shape, F32)


def _flash_update(slot, s, v, m_ref, acc_ref, s_ref, next_scores):
    m_prev = m_ref[slot]
    m_new = jnp.maximum(m_prev, jnp.max(s, axis=-1, keepdims=True))
    alpha = jnp.exp2(m_prev - m_new)
    p = jnp.concatenate([jnp.exp2(s[:, j * 128:(j + 1) * 128] - m_new) for j in range(s.shape[1] // 128)],
                        axis=1).astype(BF16)
    if next_scores is not None:
        s_ref[slot] = next_scores()
    v_ones = jnp.concatenate([v, jnp.ones_like(v)], axis=1)
    acc_ref[slot] = (jnp.concatenate([alpha, alpha], axis=1) * acc_ref[slot]
                     + jnp.dot(p, v_ones, preferred_element_type=F32))
    m_ref[slot] = m_new


def _flash_result(slot, acc_ref):
    acc = acc_ref[slot]
    return acc[:, :HEAD_DIM] / acc[:, HEAD_DIM:]


def _silu(g):
    return g * jax.nn.sigmoid(g)


def _tile(ref, kj, t, c0, width):
    start = kj * t if isinstance(kj, int) else pl.multiple_of(kj * t, t)
    return ref[pl.ds(start, t), c0:c0 + width]


def _bank_tile(bank_ref, h, i, kj):
    base = 2 * (i - kj) + 1
    top = jnp.concatenate([bank_ref[h, base], bank_ref[h, base - 1]], axis=1)
    bot = jnp.concatenate([bank_ref[h, base + 1], bank_ref[h, base]], axis=1)
    return jnp.concatenate([top, bot], axis=0)


def _emit(o_ref, g_ref, h, o):
    gate = g_ref[:, h * 128:(h + 1) * 128].astype(F32)
    o_ref[:, h * 128:(h + 1) * 128] = (o * _silu(gate)).astype(BF16)


def _flash_walk(i, first_key, n_far, slots, score, logits, value, s_ref, m_ref, acc_ref, prep=None):
    lo = first_key(i)
    i_next = jnp.minimum(i + 1, pl.num_programs(1) - 1)
    _flash_init(m_ref, acc_ref)

    @pl.when(i == 0)
    def _():
        for slot in range(slots):
            s_ref[slot] = score(slot, False, lo)

    def step(kj, phase):
        ctx = prep(kj) if prep is not None else None
        for slot in range(slots):
            s = logits(slot, s_ref[slot], kj, phase, ctx)
            if phase == "last":
                nxt = functools.partial(score, slot, True, first_key(i_next))
            else:
                nxt = functools.partial(score, slot, False, kj + 1)
            _flash_update(slot, s, value(slot, kj), m_ref, acc_ref, s_ref, nxt)

    def walk(a, b, phase):
        def body(kj, carry):
            step(kj, phase)
            return carry
        lax.fori_loop(a, b, body, 0)

    if n_far is not None:
        walk(lo, n_far, "far")
        lo = n_far
    walk(lo, i, "near")
    step(i, "last")


def _attn_a_kernel(q_ref, qn_ref, k_ref, v_ref, g_ref, o_ref, s_ref, m_ref, acc_ref, *, t):
    i = pl.program_id(1)

    def score(h, next_q, kj):
        q = (qn_ref if next_q else q_ref)[:, h * 256:(h + 1) * 256]
        return _qk(q, _tile(k_ref, kj, t, h * 256, 256))

    def logits(h, s, kj, phase, ctx):
        if phase != "last":
            return s
        causal = lax.broadcasted_iota(jnp.int32, (t, t), 1) <= lax.broadcasted_iota(jnp.int32, (t, t), 0)
        return jnp.where(causal, s, NEG)

    def value(h, kj):
        return _tile(v_ref, kj, t, h * 128, 128)

    _flash_walk(i, lambda qi: 0, None, N_HEADS, score, logits, value, s_ref, m_ref, acc_ref)
    for h in range(N_HEADS):
        _emit(o_ref, g_ref, h, _flash_result(h, acc_ref))


def _attn_band_kernel(q_ref, qn_ref, k_ref, v_ref, g_ref, bank_ref, o_ref, s_ref, m_ref, acc_ref, *, t, near):
    i = pl.program_id(1)

    def score(h, next_q, kj):
        q = (qn_ref if next_q else q_ref)[:, h * 128:(h + 1) * 128]
        return _qk(q, _tile(k_ref, kj, t, h * 128, 128))

    def logits(h, s, kj, phase, ctx):
        return s + _bank_tile(bank_ref, h, i, kj)

    def value(h, kj):
        return _tile(v_ref, kj, t, h * 128, 128)

    _flash_walk(i, lambda qi: jnp.maximum(qi - (near - 1), 0), None, N_HEADS, score, logits, value,
                s_ref, m_ref, acc_ref)
    for h in range(N_HEADS):
        _emit(o_ref, g_ref, h, _flash_result(h, acc_ref))


def _attn_sel_kernel(q_ref, qn_ref, k_ref, v_ref, g_ref, bank_ref, sel_ref, o_ref,
                     s_ref, m_ref, acc_ref, *, t, near):
    i = pl.program_id(1)

    def score(h, next_q, kj):
        q = (qn_ref if next_q else q_ref)[:, h * 128:(h + 1) * 128]
        return _qk(q, _tile(k_ref, kj, t, h * 128, 128))

    def prep(kj):
        return sel_ref[:, pl.ds(pl.multiple_of(kj * t, t), t)].astype(F32)

    def logits(h, s, kj, phase, sel):
        return s + sel if phase == "far" else s + (sel + _bank_tile(bank_ref, h, i, kj))

    def value(h, kj):
        return _tile(v_ref, kj, t, h * 128, 128)

    _flash_walk(i, lambda qi: 0, jnp.maximum(i - (near - 1), 0), N_HEADS, score, logits, value,
                s_ref, m_ref, acc_ref, prep)
    for h in range(N_HEADS):
        _emit(o_ref, g_ref, h, _flash_result(h, acc_ref))


def _attn_diff_kernel(q_ref, qn_ref, k_ref, v_ref, g_ref, bank_ref, lam_ref, gsub_ref, o_ref,
                      s_ref, m_ref, acc_ref, *, t, near, lam_init):
    i = pl.program_id(1)
    first_half = lax.broadcasted_iota(jnp.int32, (t, HEAD_DIM), 1) < HEAD_DIM // 2

    def score(slot, next_q, kj):
        h = slot // 2
        q = (qn_ref if next_q else q_ref)[:, h * 128:(h + 1) * 128]
        keep = first_half if slot % 2 == 0 else jnp.logical_not(first_half)
        q = jnp.where(keep, q, jnp.zeros_like(q))
        return _qk(q, _tile(k_ref, kj, t, h * 128, 128))

    def logits(slot, s, kj, phase, ctx):
        return s if phase == "far" else s + _bank_tile(bank_ref, slot // 2, i, kj)

    def value(slot, kj):
        return _tile(v_ref, kj, t, (slot // 2) * 128, 128)

    _flash_walk(i, lambda qi: 0, jnp.maximum(i - (near - 1), 0), 2 * N_HEADS, score, logits, value,
                s_ref, m_ref, acc_ref)
    lam_v = lam_ref[...]
    lam = (jnp.exp(jnp.sum(lam_v[0:1] * lam_v[1:2], axis=-1, keepdims=True))
           - jnp.exp(jnp.sum(lam_v[2:3] * lam_v[3:4], axis=-1, keepdims=True)) + lam_init)
    for h in range(N_HEADS):
        o = _flash_result(2 * h, acc_ref) - lam * _flash_result(2 * h + 1, acc_ref)
        _emit(o_ref, g_ref, h, _rms(o, gsub_ref[...]) * (1.0 - lam_init))


def _attention(kind, q_arr, k_arr, v_arr, proj, q_blk, gate_blk, extra_in=(), extra_specs=(), **kw):
    b, s, _ = proj.shape
    t = min(ATT_TILE, s)
    dk = 256 if kind == "a" else 128
    qw = N_HEADS * dk
    k_blk = 0 if kind == "a" else q_blk + 1
    v_blk = 0 if kind == "a" else q_blk + 2
    body = {"a": _attn_a_kernel, "band": _attn_band_kernel, "sel": _attn_sel_kernel,
            "diff": _attn_diff_kernel}[kind]
    slots = 2 * N_HEADS if kind == "diff" else N_HEADS
    scratch = [pltpu.VMEM((slots, t, t), F32), pltpu.VMEM((slots, t, HEAD_DIM), F32),
               pltpu.VMEM((slots, t, 2 * HEAD_DIM), F32)]
    last = s // t - 1
    in_specs = [pl.BlockSpec((None, t, qw), lambda bi, i: (bi, i, q_blk)),
                pl.BlockSpec((None, t, qw), lambda bi, i: (bi, jnp.minimum(i + 1, last), q_blk)),
                pl.BlockSpec((None, s, qw), lambda bi, i: (bi, 0, k_blk)),
                pl.BlockSpec((None, s, BRANCH_WIDTH), lambda bi, i: (bi, 0, v_blk)),
                pl.BlockSpec((None, t, BRANCH_WIDTH), lambda bi, i: (bi, i, gate_blk))]
    in_specs += list(extra_specs)
    return pl.pallas_call(
        functools.partial(body, t=t, **kw),
        out_shape=jax.ShapeDtypeStruct((b, s, BRANCH_WIDTH), BF16),
        grid=(b, s // t),
        in_specs=in_specs,
        out_specs=pl.BlockSpec((None, t, BRANCH_WIDTH), lambda bi, i: (bi, i, 0)),
        scratch_shapes=scratch,
        compiler_params=_cparams("arbitrary", "arbitrary"),
        name="attn_" + kind,
    )(q_arr, q_arr, k_arr, v_arr, proj, *extra_in)


def _bit_transpose32(words):
    a = list(words)
    j, m = 16, 0x0000FFFF
    while j:
        for k in range(32):
            if not k & j:
                t = (a[k] ^ lax.shift_right_logical(a[k + j], jnp.int32(j))) & jnp.int32(m)
                a[k] = a[k] ^ t
                a[k + j] = a[k + j] ^ (t << j)
        j >>= 1
        m = (m ^ (m << j)) & 0xFFFFFFFF if j else m
    return a


def _select_kernel(qi_ref, ki_ref, wi_ref, o_ref, key_ref, plane_ref, alive_ref, *, tq, kc, n_sel):
    i = pl.program_id(1)
    s_len = o_ref.shape[1]
    n_ch = (i * tq + tq + kc - 1) // kc
    wpc = kc // 32
    int_min = jnp.int32(-2 ** 31)
    lane = lax.broadcasted_iota(jnp.int32, (tq, 128), 1)
    w_t = wi_ref[...].astype(F32).T
    q_heads = []
    for j in range(IDX_HEADS // 2):
        q2 = qi_ref[:, j * 128:(j + 1) * 128]
        q_heads.append(jnp.where(lane < IDX_DIM, q2, jnp.zeros_like(q2)))
        q_heads.append(jnp.where(lane >= IDX_DIM, q2, jnp.zeros_like(q2)))
    qpos = i * tq + lax.broadcasted_iota(jnp.int32, (kc, tq), 1)
    kiota = lax.broadcasted_iota(jnp.int32, (kc, tq), 0)

    def chunk(c):
        return pl.ds(pl.multiple_of(c * kc, kc), kc)

    def score_chunk(c, _):
        k = ki_ref[chunk(c), :]
        acc = jnp.zeros((kc, tq), F32)
        for hh in range(IDX_HEADS):
            acc = acc + jnp.maximum(_qk(k, q_heads[hh]), 0.0) * w_t[hh:hh + 1, :]
        acc = jnp.where(c * kc + kiota <= qpos, acc + 0.0, NEG)
        bits = pltpu.bitcast(acc, jnp.int32)
        keys = bits ^ ((bits >> 31) & jnp.int32(0x7FFFFFFF))
        key_ref[chunk(c), :] = keys
        ukeys = keys ^ int_min
        for blk in range(kc // 256):
            planes = _bit_transpose32([ukeys[blk * 256 + 8 * j:blk * 256 + 8 * j + 8, :] for j in range(32)])
            row = pl.multiple_of(c * wpc + blk * 8, 8)
            for b in range(32):
                plane_ref[b, pl.ds(row, 8), :] = planes[b]
        return 0

    lax.fori_loop(0, n_ch, score_chunk, 0)

    def clear_chunk(c, _):
        plane_ref[:, pl.ds(pl.multiple_of(c * wpc, wpc), wpc), :] = jnp.zeros((32, wpc, tq), jnp.int32)
        return 0

    lax.fori_loop(n_ch, s_len // kc, clear_chunk, 0)
    n_rows = s_len // 32
    word_row = lax.broadcasted_iota(jnp.int32, (n_rows, tq), 0)
    alive_ref[...] = jnp.where(word_row < n_ch * wpc, jnp.int32(-1), jnp.int32(0))

    def radix_step(bi, carry):
        thr, above = carry
        alive = alive_ref[...]
        plane = plane_ref[bi]
        ones = lax.population_count(alive & plane)
        ones = jnp.sum(jnp.sum(ones.reshape(n_rows // 8, 8, tq), axis=0), axis=0, keepdims=True)
        take = above + ones >= n_sel
        thr = jnp.where(take, thr | (jnp.int32(1) << (31 - bi)), thr)
        above = jnp.where(take, above, above + ones)
        alive_ref[...] = alive & (plane ^ jnp.where(take, jnp.int32(0), jnp.int32(-1)))
        return thr, above

    zeros = jnp.zeros((1, tq), jnp.int32)
    thr_u, above = lax.fori_loop(0, 32, radix_step, (zeros, zeros))
    thr = thr_u ^ int_min
    n_equal = lax.population_count(alive_ref[...])
    n_equal = jnp.sum(jnp.sum(n_equal.reshape(n_rows // 8, 8, tq), axis=0), axis=0, keepdims=True)
    need = n_sel - above
    masked_key = int(np.float32(NEG).view(np.int32)) ^ 0x7FFFFFFF
    tie = (n_equal > need) & (thr != masked_key)
    any_tie = jnp.max(jnp.where(tie, 1, 0)) > 0

    @pl.when(jnp.logical_not(any_tie))
    def _():
        def emit(c, _):
            keep = jnp.where(key_ref[chunk(c), :] >= thr, 0.0, NEG)
            o_ref[:, chunk(c)] = keep.T.astype(BF16)
            return 0

        lax.fori_loop(0, n_ch, emit, 0)

    @pl.when(any_tie)
    def _():
        def equal_below(bound):
            def body(c, part):
                hit = jnp.where((key_ref[chunk(c), :] == thr) & (c * kc + kiota < bound), 1, 0)
                return part + jnp.sum(hit.reshape(kc // 8, 8, tq), axis=0)
            part = lax.fori_loop(0, n_ch, body, jnp.zeros((8, tq), jnp.int32))
            return jnp.sum(part, axis=0, keepdims=True)

        n_bits = s_len.bit_length()

        def bound_step(bi, cut):
            cand = cut + (jnp.int32(1) << (n_bits - 1 - bi))
            ok = (cand <= s_len) & (equal_below(cand) <= need)
            return jnp.where(ok, cand, cut)

        cut = lax.fori_loop(0, n_bits, bound_step, zeros)

        def emit(c, _):
            keys = key_ref[chunk(c), :]
            kept = (keys > thr) | ((keys == thr) & (c * kc + kiota < cut))
            o_ref[:, chunk(c)] = jnp.where(kept, 0.0, NEG).T.astype(BF16)
            return 0

        lax.fori_loop(0, n_ch, emit, 0)

    def blank(c, _):
        o_ref[:, chunk(c)] = jnp.full((tq, kc), NEG, BF16)
        return 0

    lax.fori_loop(n_ch, s_len // kc, blank, 0)


def _select(proj):
    b, s, _ = proj.shape
    tq = min(512, s)
    kc = min(512, s)
    n_sel = min(TOPK_MAX, s // 4)
    return pl.pallas_call(
        functools.partial(_select_kernel, tq=tq, kc=kc, n_sel=n_sel),
        out_shape=jax.ShapeDtypeStruct((b, s, s), BF16),
        grid=(b, s // tq),
        in_specs=[pl.BlockSpec((None, tq, 1024), lambda bi, i: (bi, i, OFF_QIDX // 1024)),
                  pl.BlockSpec((None, s, 128), lambda bi, i: (bi, 0, OFF_KIDX // 128)),
                  pl.BlockSpec((None, tq, 128), lambda bi, i: (bi, i, OFF_WIDX // 128))],
        out_specs=pl.BlockSpec((None, tq, s), lambda bi, i: (bi, i, 0)),
        scratch_shapes=[pltpu.VMEM((s, tq), jnp.int32), pltpu.VMEM((32, s // 32, tq), jnp.int32),
                        pltpu.VMEM((s // 32, tq), jnp.int32)],
        compiler_params=_cparams("arbitrary", "arbitrary"),
        name="idx_select",
    )(proj, proj, proj)


def _out_kernel(a_ref, b_ref, c_ref, d_ref, w_ref, x_ref, gate_ref, g_ref, o_ref, wb_ref):
    @pl.when((pl.program_id(0) == 0) & (pl.program_id(1) == 0))
    def _():
        wb_ref[...] = w_ref[...].astype(BF16)

    y = jnp.dot(a_ref[...], wb_ref[0:512, :], preferred_element_type=F32)
    y += jnp.dot(b_ref[...], wb_ref[512:1024, :], preferred_element_type=F32)
    y += jnp.dot(c_ref[...], wb_ref[1024:1536, :], preferred_element_type=F32)
    y += jnp.dot(d_ref[...], wb_ref[1536:2048, :], preferred_element_type=F32)
    o_ref[...] = x_ref[...] + gate_ref[...] * _rms(y, g_ref[...])


def _out_proj(outs, w_out, li, x, mod3, g_post):
    b, s, d = x.shape
    tm = min(512, s)
    mix = lambda bi, i: (bi, i, 0)
    return pl.pallas_call(
        _out_kernel,
        out_shape=jax.ShapeDtypeStruct((b, s, d), F32),
        grid=(b, s // tm),
        in_specs=[pl.BlockSpec((None, tm, BRANCH_WIDTH), mix)] * 4
        + [pl.BlockSpec((None,) + w_out.shape[1:], lambda bi, i: (li, 0, 0), pipeline_mode=pl.Buffered(1)),
           pl.BlockSpec((None, tm, d), mix),
           pl.BlockSpec((None, 1, d), lambda bi, i: (bi, 0, 2)),
           pl.BlockSpec((1, d), lambda bi, i: (0, 0))],
        out_specs=pl.BlockSpec((None, tm, d), mix),
        scratch_shapes=[pltpu.VMEM(w_out.shape[1:], BF16)],
        compiler_params=_cparams("arbitrary", "arbitrary"),
        name="out_proj",
    )(*outs, w_out, x, mod3, g_post.reshape(1, d))


def _rope_tables(s):
    half = MLA_ROPE // 2
    inv = ROPE_THETA ** (-jnp.arange(half, dtype=F32) / half)
    ang = jnp.arange(s, dtype=F32)[:, None] * inv[None, :]
    z = jnp.zeros((s, 128 - MLA_ROPE), F32)
    cos, sin = jnp.cos(ang), jnp.sin(ang)
    return jnp.concatenate([cos, cos, z], axis=-1), jnp.concatenate([sin, sin, z], axis=-1)


def _rot_cols(w):
    half = w.shape[-1] // 2
    return jnp.concatenate([-w[..., half:], w[..., :half]], axis=-1)


IN_SPLITS = (("a_cq", 384), ("a_ckv", 256), ("a_krope", 64), ("b_q", 512), ("b_k", 512), ("b_v", 512),
             ("c_q", 512), ("c_k", 512), ("c_v", 512), ("c_qidx", 1024), ("c_kidx", 64), ("c_widx", 16),
             ("d_q", 512), ("d_k", 512), ("d_v", 512), ("gate", 2048))
IN_WIDTH = sum(width for _, width in IN_SPLITS)


def _layout_w_in_kernel(w_ref, o_ref):
    src, start = {}, 0
    for name, width in IN_SPLITS:
        src[name] = start
        start += width
    tk = w_ref.shape[1]

    def rows(name, width, offset=0):
        a = src[name] + offset
        return w_ref[a:a + width, :]

    def put(dst, val, scale=None):
        for r in range(0, val.shape[0], 512):
            piece = val[r:r + 512]
            if scale is not None:
                piece = piece * scale
            o_ref[:, dst + r:dst + r + piece.shape[0]] = piece.T.astype(BF16)

    z64 = jnp.zeros((64, tk), F32)
    half = MLA_ROPE // 2
    put(OFF_QIDX, rows("c_qidx", 1024), IDX_DIM ** -0.5)
    put(OFF_A, rows("a_cq", MLA_Q_RANK + MLA_KV_RANK))
    put(OFF_A + 640, jnp.concatenate(
        [rows("a_krope", MLA_ROPE), z64, -rows("a_krope", half, half), rows("a_krope", half), z64,
         rows("c_kidx", IDX_DIM), rows("c_kidx", IDX_DIM)], axis=0))
    for off, name, dim in ((OFF_B, "b", HEAD_DIM), (OFF_C, "c", HEAD_DIM), (OFF_D, "d", HEAD_DIM // 2)):
        put(off, rows(name + "_q", BRANCH_WIDTH), LOG2E * dim ** -0.5)
        put(off + BRANCH_WIDTH, rows(name + "_k", 2 * BRANCH_WIDTH))
    put(OFF_GATE, rows("gate", MIX_WIDTH))
    put(OFF_WIDX, jnp.concatenate([rows("c_widx", IDX_HEADS) * IDX_HEADS ** -0.5,
                                   jnp.zeros((128 - IDX_HEADS, tk), F32)], axis=0))


def _layout_w_in(w_in_t, li):
    d = w_in_t.shape[2]
    tk = 256
    return pl.pallas_call(
        _layout_w_in_kernel,
        out_shape=jax.ShapeDtypeStruct((d, PROJ_WIDTH), BF16),
        grid=(d // tk,),
        in_specs=[pl.BlockSpec((None, IN_WIDTH, tk), lambda i: (li, 0, i))],
        out_specs=pl.BlockSpec((tk, PROJ_WIDTH), lambda i: (i, 0)),
        compiler_params=_cparams("arbitrary"),
        name="layout_w_in",
    )(w_in_t)


def _layout_w_uq(w):
    r = w.shape[0]
    w = w.reshape(r, N_HEADS, MLA_NOPE + MLA_ROPE) * (LOG2E * (MLA_NOPE + MLA_ROPE) ** -0.5)
    z = jnp.zeros((r, N_HEADS, 128 - MLA_ROPE), w.dtype)
    rope = w[..., MLA_NOPE:]
    return jnp.concatenate([w[..., :MLA_NOPE], rope, z, _rot_cols(rope), z], axis=-1).reshape(r, -1).astype(BF16)


def kernel(x, c, w_ada, b_ada, g_pre, g_post, w_in, g_q_a, w_uq_a, g_kv_a, w_ukv_a,
           lam_q1, lam_k1, lam_q2, lam_k2, g_sub_d, w_out, rel_bias):
    b, s, d = x.shape
    depth = w_ada.shape[0]
    t = min(ATT_TILE, s)
    half = t // 2
    nq = s // t
    near_bias = min(nq, -(-(_first_far_diagonal(half) + 1) // 2))
    near_band = min(nq, -(-(DILATED_PATTERNS[-1][0] // half + 1) // 2))

    cos_t, sin_t = _rope_tables(s)
    bank_b = _bank(rel_bias[:, 0:N_HEADS], 2 * near_band + 1, half, -1, True)
    bank_cd = _bank(rel_bias[:, N_HEADS:3 * N_HEADS], 2 * near_bias + 1, half, -1, False)
    bank_spec = lambda n, group=0: pl.BlockSpec((N_HEADS, n, half, half), lambda bi, i: (group, 0, 0, 0),
                                                pipeline_mode=pl.Buffered(1))

    w_in_t = jnp.swapaxes(w_in, 1, 2)
    mod = _ada_mod(c, w_ada, b_ada)
    for li in range(depth):
        mod3 = mod[li].reshape(b, 1, 3 * d)
        proj = _in_proj(x, g_pre[li], mod3, _layout_w_in(w_in_t, li))

        q_a, k_a, v_a = _mla_prep(proj, cos_t, sin_t, g_q_a[li], g_kv_a[li],
                                  _layout_w_uq(w_uq_a[li]), w_ukv_a[li].astype(BF16))
        gate0 = OFF_GATE // BRANCH_WIDTH
        out_a = _attention("a", q_a, k_a, v_a, proj, 0, gate0)
        out_b = _attention("band", proj, proj, proj, proj, OFF_B // BRANCH_WIDTH, gate0 + 1,
                           extra_in=(bank_b,), extra_specs=(bank_spec(2 * near_band + 1),), near=near_band)
        sel = _select(proj)
        out_c = _attention("sel", proj, proj, proj, proj, OFF_C // BRANCH_WIDTH, gate0 + 2,
                           extra_in=(bank_cd, sel),
                           extra_specs=(bank_spec(2 * near_bias + 1, 0),
                                        pl.BlockSpec((None, t, s), lambda bi, i: (bi, i, 0))),
                           near=near_bias)
        lam_init = 0.8 - 0.6 * math.exp(-0.3 * li)
        lam_vecs = jnp.stack([lam_q1[li], lam_k1[li], lam_q2[li], lam_k2[li]])
        out_d = _attention("diff", proj, proj, proj, proj, OFF_D // BRANCH_WIDTH, gate0 + 3,
                           extra_in=(bank_cd, lam_vecs, g_sub_d[li].reshape(1, HEAD_DIM)),
                           extra_specs=(bank_spec(2 * near_bias + 1, 1),
                                        pl.BlockSpec(lam_vecs.shape, lambda bi, i: (0, 0)),
                                        pl.BlockSpec((1, HEAD_DIM), lambda bi, i: (0, 0))),
                           near=near_bias, lam_init=lam_init)
        x = _out_proj((out_a, out_b, out_c, out_d), w_out, li, x, mod3, g_post[li])
    return x
```

```python
import functools
import math

import numpy as np
import jax
import jax.numpy as jnp
from jax import lax
from jax.experimental import pallas as pl
from jax.experimental.pallas import tpu as pltpu

F32 = jnp.float32
BF16 = jnp.bfloat16

HEAD_DIM = 128
N_HEADS = 4
BRANCH_WIDTH = N_HEADS * HEAD_DIM
MIX_WIDTH = 4 * BRANCH_WIDTH
MLA_Q_RANK = 384
MLA_KV_RANK = 256
MLA_NOPE = 128
MLA_ROPE = 64
ROPE_THETA = 10000.0
DILATED_PATTERNS = ((128, 1), (512, 4), (2048, 16))
IDX_HEADS = 16
IDX_DIM = 64
TOPK_MAX = 256
REL_BUCKETS = 32
REL_MAX_DIST = 2048
NORM_EPS = 1e-6
NEG = -1e30
LOG2E = math.log2(math.e)

OFF_QIDX = 0
OFF_A = 1024
OFF_KIDX = OFF_A + 896
OFF_B = 2048
OFF_C = 3584
OFF_D = 5120
OFF_GATE = 6656
OFF_WIDX = 8704
PROJ_WIDTH = 8832

ATT_TILE = 512
V7X_VMEM_BYTES = 64 * 1024 * 1024
VMEM_LIMIT = V7X_VMEM_BYTES // 8 * 7


def _cparams(*sem):
    return pltpu.CompilerParams(dimension_semantics=sem, vmem_limit_bytes=VMEM_LIMIT)


def _split_bf16(x):
    hi = x.astype(BF16)
    return hi, (x - hi.astype(F32)).astype(BF16)


def _ada_kernel(c_ref, w_ref, b_ref, o_ref):
    c = c_ref[...]
    a_hi, a_lo = _split_bf16(c * jax.nn.sigmoid(c))
    w_hi, w_lo = _split_bf16(w_ref[...])
    dot = functools.partial(jnp.dot, preferred_element_type=F32)
    o_ref[...] = dot(a_hi, w_hi) + (dot(a_hi, w_lo) + dot(a_lo, w_hi)) + b_ref[...]


def _ada_mod(c, w_ada, b_ada):
    depth, d, n = w_ada.shape
    b = c.shape[0]
    tn = 768
    return pl.pallas_call(
        _ada_kernel,
        out_shape=jax.ShapeDtypeStruct((depth, b, n), F32),
        grid=(depth, n // tn),
        in_specs=[pl.BlockSpec((b, d), lambda l, j: (0, 0)),
                  pl.BlockSpec((None, d, tn), lambda l, j: (l, 0, j)),
                  pl.BlockSpec((None, 1, tn), lambda l, j: (l, 0, j))],
        out_specs=pl.BlockSpec((None, b, tn), lambda l, j: (l, 0, j)),
        compiler_params=_cparams("arbitrary", "arbitrary"),
        name="ada_mod",
    )(c, w_ada, b_ada.reshape(depth, 1, n))


def _in_proj_kernel(x_ref, g_ref, shift_ref, scale_ref, w_ref, o_ref):
    x = x_ref[...]
    y = x * lax.rsqrt(jnp.mean(x * x, axis=-1, keepdims=True) + NORM_EPS) * g_ref[...]
    h = (y * (1.0 + scale_ref[...]) + shift_ref[...]).astype(BF16)
    o_ref[...] = jnp.dot(h, w_ref[...], preferred_element_type=F32).astype(o_ref.dtype)


def _in_proj(x, g_pre, mod3, w):
    b, s, d = x.shape
    n = w.shape[1]
    tm = min(512, s)
    nt = s // tm
    tn = n // 3
    row = lambda j, i: (i // nt, i % nt, 0)
    return pl.pallas_call(
        _in_proj_kernel,
        out_shape=jax.ShapeDtypeStruct((b, s, n), BF16),
        grid=(n // tn, b * nt),
        in_specs=[pl.BlockSpec((None, tm, d), row),
                  pl.BlockSpec((1, d), lambda j, i: (0, 0)),
                  pl.BlockSpec((None, 1, d), lambda j, i: (i // nt, 0, 0)),
                  pl.BlockSpec((None, 1, d), lambda j, i: (i // nt, 0, 1)),
                  pl.BlockSpec((d, tn), lambda j, i: (0, j), pipeline_mode=pl.Buffered(1))],
        out_specs=pl.BlockSpec((None, tm, tn), lambda j, i: (i // nt, i % nt, j)),
        compiler_params=_cparams("arbitrary", "arbitrary"),
        name="in_proj",
    )(x, g_pre.reshape(1, d), mod3, mod3, w)


def _rms(x, g):
    return x * lax.rsqrt(jnp.mean(x * x, axis=-1, keepdims=True) + NORM_EPS) * g


def _mla_prep_kernel(p_ref, cos_ref, sin_ref, cs_ref, gq_ref, gkv_ref, wq_ref, wkv_ref, q_ref, k_ref, v_ref):
    cos = cos_ref[...]
    sin = sin_ref[...]
    cs = cs_ref[...]
    cq = _rms(p_ref[:, 0:MLA_Q_RANK].astype(F32), gq_ref[...]).astype(BF16)
    ckv = _rms(p_ref[:, MLA_Q_RANK:MLA_Q_RANK + MLA_KV_RANK].astype(F32), gkv_ref[...]).astype(BF16)
    q = jnp.dot(cq, wq_ref[...], preferred_element_type=F32)
    kv = jnp.dot(ckv, wkv_ref[...], preferred_element_type=F32)
    k_rope = (p_ref[:, 640:768].astype(F32) * cos + p_ref[:, 768:896].astype(F32) * sin).astype(BF16)
    for h in range(N_HEADS):
        q_ref[:, h * 256:h * 256 + 128] = q[:, h * 256:h * 256 + 128].astype(BF16)
        z = q[:, h * 256 + 128:(h + 1) * 256] * cs
        q_ref[:, h * 256 + 128:(h + 1) * 256] = (z + pltpu.roll(z, MLA_ROPE, axis=1)).astype(BF16)
        k_ref[:, h * 256:h * 256 + 128] = kv[:, h * 256:h * 256 + 128].astype(BF16)
        k_ref[:, h * 256 + 128:(h + 1) * 256] = k_rope
        v_ref[:, h * 128:(h + 1) * 128] = kv[:, h * 256 + 128:(h + 1) * 256].astype(BF16)


def _mla_prep(proj, rope_tabs, g_q, g_kv, wq, wkv):
    b, s, _ = proj.shape
    tm = min(512, s)
    const = lambda bi, i: (0, 0)
    return pl.pallas_call(
        _mla_prep_kernel,
        out_shape=(jax.ShapeDtypeStruct((b, s, N_HEADS * 256), BF16),
                   jax.ShapeDtypeStruct((b, s, N_HEADS * 256), BF16),
                   jax.ShapeDtypeStruct((b, s, BRANCH_WIDTH), BF16)),
        grid=(b, s // tm),
        in_specs=[pl.BlockSpec((None, tm, 1024), lambda bi, i: (bi, i, OFF_A // 1024)),
                  pl.BlockSpec((tm, 128), lambda bi, i: (i, 0)),
                  pl.BlockSpec((tm, 128), lambda bi, i: (i, 0)),
                  pl.BlockSpec((tm, 128), lambda bi, i: (i, 0)),
                  pl.BlockSpec((1, MLA_Q_RANK), const),
                  pl.BlockSpec((1, MLA_KV_RANK), const),
                  pl.BlockSpec(wq.shape, const),
                  pl.BlockSpec(wkv.shape, const)],
        out_specs=(pl.BlockSpec((None, tm, N_HEADS * 256), lambda bi, i: (bi, i, 0)),
                   pl.BlockSpec((None, tm, N_HEADS * 256), lambda bi, i: (bi, i, 0)),
                   pl.BlockSpec((None, tm, BRANCH_WIDTH), lambda bi, i: (bi, i, 0))),
        compiler_params=_cparams("arbitrary", "arbitrary"),
        name="mla_prep",
    )(proj, *rope_tabs, g_q.reshape(1, -1), g_kv.reshape(1, -1), wq, wkv)


def _bucket_np(n):
    max_exact = REL_BUCKETS // 2
    nf = np.maximum(n, max_exact).astype(np.float32)
    large = max_exact + (np.log(nf / np.float32(max_exact)) / np.float32(math.log(REL_MAX_DIST / max_exact))
                         * np.float32(REL_BUCKETS - max_exact)).astype(np.int32)
    return np.where(n < max_exact, n, np.minimum(large, REL_BUCKETS - 1)).astype(np.int32)


def _bucket_starts():
    buckets = _bucket_np(np.arange(2 * REL_MAX_DIST))
    return [int(np.argmax(buckets >= b)) for b in range(REL_BUCKETS)]


def _bank_kernel(tab_ref, o_ref, *, t, d_min, band):
    h = pl.program_id(0)
    starts = _bucket_starts()
    row = lax.broadcasted_iota(jnp.int32, (t, t), 0)
    col = lax.broadcasted_iota(jnp.int32, (t, t), 1)
    for j in range(o_ref.shape[0]):
        d = d_min + j
        if d < 0:
            o_ref[j] = jnp.full((t, t), NEG, F32)
            continue
        dist = t * d + row - col
        b_lo, b_hi = (int(x) for x in _bucket_np(np.array([max(t * d - t + 1, 0), t * d + t - 1])))
        val = jnp.full((t, t), tab_ref[b_lo, h], F32)
        for b in range(b_lo + 1, b_hi + 1):
            val = jnp.where(dist >= starts[b], tab_ref[b, h], val)
        if band:
            mult = jnp.zeros((t, t), jnp.int32)
            for window, dil in DILATED_PATTERNS:
                mult += jnp.where((dist >= 0) & (dist <= window) & ((dist & (dil - 1)) == 0), 1, 0)
            val = val + jnp.where(mult == 3, math.log(3.0), jnp.where(mult == 2, math.log(2.0), 0.0))
            keep = mult > 0
        else:
            val = val - tab_ref[REL_BUCKETS - 1, h]
            keep = dist >= 0
        o_ref[j] = jnp.where(keep, val * LOG2E, NEG)


def _bank(tab, n_tables, t, d_min, band):
    nh = tab.shape[1]
    return pl.pallas_call(
        functools.partial(_bank_kernel, t=t, d_min=d_min, band=band),
        out_shape=jax.ShapeDtypeStruct((nh, n_tables, t, t), F32),
        grid=(nh,),
        in_specs=[pl.BlockSpec(memory_space=pltpu.SMEM)],
        out_specs=pl.BlockSpec((None, n_tables, t, t), lambda h: (h, 0, 0, 0)),
        compiler_params=_cparams("arbitrary"),
        name="bank_band" if band else "bank_bias",
    )(tab)


def _first_far_diagonal(t):
    last = _bucket_starts()[REL_BUCKETS - 1]
    return -(-(last + t - 1) // t)


def _qk(q, k):
    return lax.dot_general(q, k, (((1,), (1,)), ((), ())), preferred_element_type=F32)


def _flash_init(m_ref, acc_ref):
    m_ref[...] = jnp.full(m_ref.shape, NEG, F32)
    acc_ref[...] = jnp.zeros(acc_ref.shape, F32)


def _flash_update(slot, s, v, m_ref, acc_ref, s_ref, next_scores):
    m_prev = m_ref[slot]
    m_new = jnp.maximum(m_prev, jnp.max(s, axis=-1, keepdims=True))
    alpha = jnp.exp2(m_prev - m_new)
    p = jnp.concatenate([jnp.exp2(s[:, j * 128:(j + 1) * 128] - m_new) for j in range(s.shape[1] // 128)],
                        axis=1).astype(BF16)
    if next_scores is not None:
        s_ref[slot] = next_scores()
    v_ones = jnp.concatenate([v, jnp.ones_like(v)], axis=1)
    acc_ref[slot] = (jnp.concatenate([alpha, alpha], axis=1) * acc_ref[slot]
                     + jnp.dot(p, v_ones, preferred_element_type=F32))
    m_ref[slot] = m_new


def _flash_result(slot, acc_ref):
    acc = acc_ref[slot]
    return acc[:, :HEAD_DIM] / acc[:, HEAD_DIM:]


def _silu(g):
    return g * jax.nn.sigmoid(g)


def _tile(ref, kj, t, c0, width):
    start = kj * t if isinstance(kj, int) else pl.multiple_of(kj * t, t)
    return ref[pl.ds(start, t), c0:c0 + width]


def _bank_tile(bank_ref, h, i, kj):
    base = 2 * (i - kj) + 1
    top = jnp.concatenate([bank_ref[h, base], bank_ref[h, base - 1]], axis=1)
    bot = jnp.concatenate([bank_ref[h, base + 1], bank_ref[h, base]], axis=1)
    return jnp.concatenate([top, bot], axis=0)


def _emit(o_ref, g_ref, h, o):
    gate = g_ref[:, h * 128:(h + 1) * 128].astype(F32)
    o_ref[:, h * 128:(h + 1) * 128] = (o * _silu(gate)).astype(BF16)


def _flash_walk(i, first_key, n_far, slots, score, logits, value, s_ref, m_ref, acc_ref, prep=None):
    lo = first_key(i)
    i_next = jnp.minimum(i + 1, pl.num_programs(1) - 1)
    _flash_init(m_ref, acc_ref)

    @pl.when(i == 0)
    def _():
        for slot in range(slots):
            s_ref[slot] = score(slot, False, lo)

    def step(kj, phase):
        ctx = prep(kj) if prep is not None else None
        for slot in range(slots):
            s = logits(slot, s_ref[slot], kj, phase, ctx)
            if phase == "last":
                nxt = functools.partial(score, slot, True, first_key(i_next))
            else:
                nxt = functools.partial(score, slot, False, kj + 1)
            _flash_update(slot, s, value(slot, kj), m_ref, acc_ref, s_ref, nxt)

    def walk(a, b, phase):
        def body(kj, carry):
            step(kj, phase)
            return carry
        lax.fori_loop(a, b, body, 0)

    if n_far is not None:
        walk(lo, n_far, "far")
        lo = n_far
    walk(lo, i, "near")
    step(i, "last")


def _attn_a_kernel(q_ref, qn_ref, k_ref, v_ref, g_ref, o_ref, s_ref, m_ref, acc_ref, *, t):
    i = pl.program_id(1)

    def score(h, next_q, kj):
        q = (qn_ref if next_q else q_ref)[:, h * 256:(h + 1) * 256]
        return _qk(q, _tile(k_ref, kj, t, h * 256, 256))

    def logits(h, s, kj, phase, ctx):
        if phase != "last":
            return s
        causal = lax.broadcasted_iota(jnp.int32, (t, t), 1) <= lax.broadcasted_iota(jnp.int32, (t, t), 0)
        return jnp.where(causal, s, NEG)

    def value(h, kj):
        return _tile(v_ref, kj, t, h * 128, 128)

    _flash_walk(i, lambda qi: 0, None, N_HEADS, score, logits, value, s_ref, m_ref, acc_ref)
    for h in range(N_HEADS):
        _emit(o_ref, g_ref, h, _flash_result(h, acc_ref))


def _attn_band_kernel(q_ref, qn_ref, k_ref, v_ref, g_ref, bank_ref, o_ref, s_ref, m_ref, acc_ref, *, t, near):
    i = pl.program_id(1)

    def score(h, next_q, kj):
        q = (qn_ref if next_q else q_ref)[:, h * 128:(h + 1) * 128]
        return _qk(q, _tile(k_ref, kj, t, h * 128, 128))

    def logits(h, s, kj, phase, ctx):
        return s + _bank_tile(bank_ref, h, i, kj)

    def value(h, kj):
        return _tile(v_ref, kj, t, h * 128, 128)

    _flash_walk(i, lambda qi: jnp.maximum(qi - (near - 1), 0), None, N_HEADS, score, logits, value,
                s_ref, m_ref, acc_ref)
    for h in range(N_HEADS):
        _emit(o_ref, g_ref, h, _flash_result(h, acc_ref))


def _attn_sel_kernel(q_ref, qn_ref, k_ref, v_ref, g_ref, bank_ref, sel_ref, o_ref,
                     s_ref, m_ref, acc_ref, *, t, near):
    i = pl.program_id(1)

    def score(h, next_q, kj):
        q = (qn_ref if next_q else q_ref)[:, h * 128:(h + 1) * 128]
        return _qk(q, _tile(k_ref, kj, t, h * 128, 128))

    def prep(kj):
        return sel_ref[:, pl.ds(pl.multiple_of(kj * t, t), t)].astype(F32)

    def logits(h, s, kj, phase, sel):
        return s + sel if phase == "far" else s + (sel + _bank_tile(bank_ref, h, i, kj))

    def value(h, kj):
        return _tile(v_ref, kj, t, h * 128, 128)

    _flash_walk(i, lambda qi: 0, jnp.maximum(i - (near - 1), 0), N_HEADS, score, logits, value,
                s_ref, m_ref, acc_ref, prep)
    for h in range(N_HEADS):
        _emit(o_ref, g_ref, h, _flash_result(h, acc_ref))


def _attn_diff_kernel(q_ref, qn_ref, k_ref, v_ref, g_ref, bank_ref, lam_ref, gsub_ref, o_ref,
                      s_ref, m_ref, acc_ref, *, t, near, lam_init):
    i = pl.program_id(1)
    first_half = lax.broadcasted_iota(jnp.int32, (t, HEAD_DIM), 1) < HEAD_DIM // 2

    def score(slot, next_q, kj):
        h = slot // 2
        q = (qn_ref if next_q else q_ref)[:, h * 128:(h + 1) * 128]
        keep = first_half if slot % 2 == 0 else jnp.logical_not(first_half)
        q = jnp.where(keep, q, jnp.zeros_like(q))
        return _qk(q, _tile(k_ref, kj, t, h * 128, 128))

    def logits(slot, s, kj, phase, ctx):
        return s if phase == "far" else s + _bank_tile(bank_ref, slot // 2, i, kj)

    def value(slot, kj):
        return _tile(v_ref, kj, t, (slot // 2) * 128, 128)

    _flash_walk(i, lambda qi: 0, jnp.maximum(i - (near - 1), 0), 2 * N_HEADS, score, logits, value,
                s_ref, m_ref, acc_ref)
    lam_v = lam_ref[...]
    lam = (jnp.exp(jnp.sum(lam_v[0:1] * lam_v[1:2], axis=-1, keepdims=True))
           - jnp.exp(jnp.sum(lam_v[2:3] * lam_v[3:4], axis=-1, keepdims=True)) + lam_init)
    for h in range(N_HEADS):
        o = _flash_result(2 * h, acc_ref) - lam * _flash_result(2 * h + 1, acc_ref)
        _emit(o_ref, g_ref, h, _rms(o, gsub_ref[...]) * (1.0 - lam_init))


def _attention(kind, q_arr, k_arr, v_arr, proj, q_blk, gate_blk, extra_in=(), extra_specs=(), **kw):
    b, s, _ = proj.shape
    t = min(ATT_TILE, s)
    dk = 256 if kind == "a" else 128
    qw = N_HEADS * dk
    k_blk = 0 if kind == "a" else q_blk + 1
    v_blk = 0 if kind == "a" else q_blk + 2
    body = {"a": _attn_a_kernel, "band": _attn_band_kernel, "sel": _attn_sel_kernel,
            "diff": _attn_diff_kernel}[kind]
    slots = 2 * N_HEADS if kind == "diff" else N_HEADS
    scratch = [pltpu.VMEM((slots, t, t), F32), pltpu.VMEM((slots, t, HEAD_DIM), F32),
               pltpu.VMEM((slots, t, 2 * HEAD_DIM), F32)]
    last = s // t - 1
    in_specs = [pl.BlockSpec((None, t, qw), lambda bi, i: (bi, i, q_blk)),
                pl.BlockSpec((None, t, qw), lambda bi, i: (bi, jnp.minimum(i + 1, last), q_blk)),
                pl.BlockSpec((None, s, qw), lambda bi, i: (bi, 0, k_blk)),
                pl.BlockSpec((None, s, BRANCH_WIDTH), lambda bi, i: (bi, 0, v_blk)),
                pl.BlockSpec((None, t, BRANCH_WIDTH), lambda bi, i: (bi, i, gate_blk))]
    in_specs += list(extra_specs)
    return pl.pallas_call(
        functools.partial(body, t=t, **kw),
        out_shape=jax.ShapeDtypeStruct((b, s, BRANCH_WIDTH), BF16),
        grid=(b, s // t),
        in_specs=in_specs,
        out_specs=pl.BlockSpec((None, t, BRANCH_WIDTH), lambda bi, i: (bi, i, 0)),
        scratch_shapes=scratch,
        compiler_params=_cparams("arbitrary", "arbitrary"),
        name="attn_" + kind,
    )(q_arr, q_arr, k_arr, v_arr, proj, *extra_in)


def _bit_transpose32(words):
    a = list(words)
    j, m = 16, 0x0000FFFF
    while j:
        for k in range(32):
            if not k & j:
                t = (a[k] ^ lax.shift_right_logical(a[k + j], jnp.int32(j))) & jnp.int32(m)
                a[k] = a[k] ^ t
                a[k + j] = a[k + j] ^ (t << j)
        j >>= 1
        m = (m ^ (m << j)) & 0xFFFFFFFF if j else m
    return a


def _select_kernel(qi_ref, ki_ref, wi_ref, o_ref, key_ref, plane_ref, alive_ref, *, tq, kc, n_sel):
    i = pl.program_id(1)
    s_len = o_ref.shape[1]
    n_ch = (i * tq + tq + kc - 1) // kc
    wpc = kc // 32
    int_min = jnp.int32(-2 ** 31)
    lane = lax.broadcasted_iota(jnp.int32, (tq, 128), 1)
    w_t = wi_ref[...].astype(F32).T
    q_heads = []
    for j in range(IDX_HEADS // 2):
        q2 = qi_ref[:, j * 128:(j + 1) * 128]
        q_heads.append(jnp.where(lane < IDX_DIM, q2, jnp.zeros_like(q2)))
        q_heads.append(jnp.where(lane >= IDX_DIM, q2, jnp.zeros_like(q2)))
    qpos = i * tq + lax.broadcasted_iota(jnp.int32, (kc, tq), 1)
    kiota = lax.broadcasted_iota(jnp.int32, (kc, tq), 0)

    def chunk(c):
        return pl.ds(pl.multiple_of(c * kc, kc), kc)

    def score_chunk(c, _):
        k = ki_ref[chunk(c), :]
        acc = jnp.zeros((kc, tq), F32)
        for hh in range(IDX_HEADS):
            acc = acc + jnp.maximum(_qk(k, q_heads[hh]), 0.0) * w_t[hh:hh + 1, :]
        acc = jnp.where(c * kc + kiota <= qpos, acc + 0.0, NEG)
        bits = pltpu.bitcast(acc, jnp.int32)
        keys = bits ^ ((bits >> 31) & jnp.int32(0x7FFFFFFF))
        key_ref[chunk(c), :] = keys
        ukeys = keys ^ int_min
        for blk in range(kc // 256):
            planes = _bit_transpose32([ukeys[blk * 256 + 8 * j:blk * 256 + 8 * j + 8, :] for j in range(32)])
            row = pl.multiple_of(c * wpc + blk * 8, 8)
            for b in range(32):
                plane_ref[b, pl.ds(row, 8), :] = planes[b]
        return 0

    lax.fori_loop(0, n_ch, score_chunk, 0)

    def clear_chunk(c, _):
        plane_ref[:, pl.ds(pl.multiple_of(c * wpc, wpc), wpc), :] = jnp.zeros((32, wpc, tq), jnp.int32)
        return 0

    lax.fori_loop(n_ch, s_len // kc, clear_chunk, 0)
    n_rows = s_len // 32
    word_row = lax.broadcasted_iota(jnp.int32, (n_rows, tq), 0)
    alive_ref[...] = jnp.where(word_row < n_ch * wpc, jnp.int32(-1), jnp.int32(0))

    def radix_step(bi, carry):
        thr, above = carry
        alive = alive_ref[...]
        plane = plane_ref[bi]
        ones = lax.population_count(alive & plane)
        ones = jnp.sum(jnp.sum(ones.reshape(n_rows // 8, 8, tq), axis=0), axis=0, keepdims=True)
        take = above + ones >= n_sel
        thr = jnp.where(take, thr | (jnp.int32(1) << (31 - bi)), thr)
        above = jnp.where(take, above, above + ones)
        alive_ref[...] = alive & (plane ^ jnp.where(take, jnp.int32(0), jnp.int32(-1)))
        return thr, above

    zeros = jnp.zeros((1, tq), jnp.int32)
    thr_u, above = lax.fori_loop(0, 32, radix_step, (zeros, zeros))
    thr = thr_u ^ int_min
    n_equal = lax.population_count(alive_ref[...])
    n_equal = jnp.sum(jnp.sum(n_equal.reshape(n_rows // 8, 8, tq), axis=0), axis=0, keepdims=True)
    need = n_sel - above
    masked_key = int(np.float32(NEG).view(np.int32)) ^ 0x7FFFFFFF
    tie = (n_equal > need) & (thr != masked_key)
    any_tie = jnp.max(jnp.where(tie, 1, 0)) > 0

    @pl.when(jnp.logical_not(any_tie))
    def _():
        def emit(c, _):
            keep = jnp.where(key_ref[chunk(c), :] >= thr, 0.0, NEG)
            o_ref[:, chunk(c)] = keep.T.astype(BF16)
            return 0

        lax.fori_loop(0, n_ch, emit, 0)

    @pl.when(any_tie)
    def _():
        def equal_below(bound):
            def body(c, part):
                hit = jnp.where((key_ref[chunk(c), :] == thr) & (c * kc + kiota < bound), 1, 0)
                return part + jnp.sum(hit.reshape(kc // 8, 8, tq), axis=0)
            part = lax.fori_loop(0, n_ch, body, jnp.zeros((8, tq), jnp.int32))
            return jnp.sum(part, axis=0, keepdims=True)

        n_bits = s_len.bit_length()

        def bound_step(bi, cut):
            cand = cut + (jnp.int32(1) << (n_bits - 1 - bi))
            ok = (cand <= s_len) & (equal_below(cand) <= need)
            return jnp.where(ok, cand, cut)

        cut = lax.fori_loop(0, n_bits, bound_step, zeros)

        def emit(c, _):
            keys = key_ref[chunk(c), :]
            kept = (keys > thr) | ((keys == thr) & (c * kc + kiota < cut))
            o_ref[:, chunk(c)] = jnp.where(kept, 0.0, NEG).T.astype(BF16)
            return 0

        lax.fori_loop(0, n_ch, emit, 0)

    def blank(c, _):
        o_ref[:, chunk(c)] = jnp.full((tq, kc), NEG, BF16)
        return 0

    lax.fori_loop(n_ch, s_len // kc, blank, 0)


def _select(proj):
    b, s, _ = proj.shape
    tq = min(512, s)
    kc = min(512, s)
    n_sel = min(TOPK_MAX, s // 4)
    return pl.pallas_call(
        functools.partial(_select_kernel, tq=tq, kc=kc, n_sel=n_sel),
        out_shape=jax.ShapeDtypeStruct((b, s, s), BF16),
        grid=(b, s // tq),
        in_specs=[pl.BlockSpec((None, tq, 1024), lambda bi, i: (bi, i, OFF_QIDX // 1024)),
                  pl.BlockSpec((None, s, 128), lambda bi, i: (bi, 0, OFF_KIDX // 128)),
                  pl.BlockSpec((None, tq, 128), lambda bi, i: (bi, i, OFF_WIDX // 128))],
        out_specs=pl.BlockSpec((None, tq, s), lambda bi, i: (bi, i, 0)),
        scratch_shapes=[pltpu.VMEM((s, tq), jnp.int32), pltpu.VMEM((32, s // 32, tq), jnp.int32),
                        pltpu.VMEM((s // 32, tq), jnp.int32)],
        compiler_params=_cparams("arbitrary", "arbitrary"),
        name="idx_select",
    )(proj, proj, proj)


def _out_kernel(a_ref, b_ref, c_ref, d_ref, w_ref, x_ref, gate_ref, g_ref, o_ref, wb_ref):
    @pl.when((pl.program_id(0) == 0) & (pl.program_id(1) == 0))
    def _():
        wb_ref[...] = w_ref[...].astype(BF16)

    y = jnp.dot(a_ref[...], wb_ref[0:512, :], preferred_element_type=F32)
    y += jnp.dot(b_ref[...], wb_ref[512:1024, :], preferred_element_type=F32)
    y += jnp.dot(c_ref[...], wb_ref[1024:1536, :], preferred_element_type=F32)
    y += jnp.dot(d_ref[...], wb_ref[1536:2048, :], preferred_element_type=F32)
    o_ref[...] = x_ref[...] + gate_ref[...] * _rms(y, g_ref[...])


def _out_proj(outs, w_out, li, x, mod3, g_post):
    b, s, d = x.shape
    tm = min(512, s)
    mix = lambda bi, i: (bi, i, 0)
    return pl.pallas_call(
        _out_kernel,
        out_shape=jax.ShapeDtypeStruct((b, s, d), F32),
        grid=(b, s // tm),
        in_specs=[pl.BlockSpec((None, tm, BRANCH_WIDTH), mix)] * 4
        + [pl.BlockSpec((None,) + w_out.shape[1:], lambda bi, i: (li, 0, 0), pipeline_mode=pl.Buffered(1)),
           pl.BlockSpec((None, tm, d), mix),
           pl.BlockSpec((None, 1, d), lambda bi, i: (bi, 0, 2)),
           pl.BlockSpec((1, d), lambda bi, i: (0, 0))],
        out_specs=pl.BlockSpec((None, tm, d), mix),
        scratch_shapes=[pltpu.VMEM(w_out.shape[1:], BF16)],
        compiler_params=_cparams("arbitrary", "arbitrary"),
        name="out_proj",
    )(*outs, w_out, x, mod3, g_post.reshape(1, d))


def _rope_tables(s):
    half = MLA_ROPE // 2
    inv = ROPE_THETA ** (-jnp.arange(half, dtype=F32) / half)
    ang = jnp.arange(s, dtype=F32)[:, None] * inv[None, :]
    z = jnp.zeros((s, 128 - MLA_ROPE), F32)
    cos, sin = jnp.cos(ang), jnp.sin(ang)
    cat = lambda *parts: jnp.concatenate(parts, axis=-1)
    return cat(cos, cos, z), cat(sin, sin, z), cat(cos, cos, sin, sin)


def _rot_cols(w):
    half = w.shape[-1] // 2
    return jnp.concatenate([-w[..., half:], w[..., :half]], axis=-1)


IN_SPLITS = (("a_cq", 384), ("a_ckv", 256), ("a_krope", 64), ("b_q", 512), ("b_k", 512), ("b_v", 512),
             ("c_q", 512), ("c_k", 512), ("c_v", 512), ("c_qidx", 1024), ("c_kidx", 64), ("c_widx", 16),
             ("d_q", 512), ("d_k", 512), ("d_v", 512), ("gate", 2048))
IN_WIDTH = sum(width for _, width in IN_SPLITS)


def _layout_w_in_kernel(w_ref, o_ref):
    src, start = {}, 0
    for name, width in IN_SPLITS:
        src[name] = start
        start += width
    tk = w_ref.shape[1]

    def rows(name, width, offset=0):
        a = src[name] + offset
        return w_ref[a:a + width, :]

    def put(dst, val, scale=None):
        for r in range(0, val.shape[0], 512):
            piece = val[r:r + 512]
            if scale is not None:
                piece = piece * scale
            o_ref[:, dst + r:dst + r + piece.shape[0]] = piece.T.astype(BF16)

    z64 = jnp.zeros((64, tk), F32)
    half = MLA_ROPE // 2
    put(OFF_QIDX, rows("c_qidx", 1024), IDX_DIM ** -0.5)
    put(OFF_A, rows("a_cq", MLA_Q_RANK + MLA_KV_RANK))
    put(OFF_A + 640, jnp.concatenate(
        [rows("a_krope", MLA_ROPE), z64, -rows("a_krope", half, half), rows("a_krope", half), z64,
         rows("c_kidx", IDX_DIM), rows("c_kidx", IDX_DIM)], axis=0))
    for off, name, dim in ((OFF_B, "b", HEAD_DIM), (OFF_C, "c", HEAD_DIM), (OFF_D, "d", HEAD_DIM // 2)):
        put(off, rows(name + "_q", BRANCH_WIDTH), LOG2E * dim ** -0.5)
        put(off + BRANCH_WIDTH, rows(name + "_k", 2 * BRANCH_WIDTH))
    put(OFF_GATE, rows("gate", MIX_WIDTH))
    put(OFF_WIDX, jnp.concatenate([rows("c_widx", IDX_HEADS) * IDX_HEADS ** -0.5,
                                   jnp.zeros((128 - IDX_HEADS, tk), F32)], axis=0))


def _layout_w_in(w_in_t, li):
    d = w_in_t.shape[2]
    tk = 256
    return pl.pallas_call(
        _layout_w_in_kernel,
        out_shape=jax.ShapeDtypeStruct((d, PROJ_WIDTH), BF16),
        grid=(d // tk,),
        in_specs=[pl.BlockSpec((None, IN_WIDTH, tk), lambda i: (li, 0, i))],
        out_specs=pl.BlockSpec((tk, PROJ_WIDTH), lambda i: (i, 0)),
        compiler_params=_cparams("arbitrary"),
        name="layout_w_in",
    )(w_in_t)


def _layout_w_uq(w):
    r = w.shape[0]
    w = w.reshape(r, N_HEADS, MLA_NOPE + MLA_ROPE) * (LOG2E * (MLA_NOPE + MLA_ROPE) ** -0.5)
    rope = w[..., MLA_NOPE:]
    return jnp.concatenate([w[..., :MLA_NOPE], rope, _rot_cols(rope)], axis=-1).reshape(r, -1).astype(BF16)


def kernel(x, c, w_ada, b_ada, g_pre, g_post, w_in, g_q_a, w_uq_a, g_kv_a, w_ukv_a,
           lam_q1, lam_k1, lam_q2, lam_k2, g_sub_d, w_out, rel_bias):
    b, s, d = x.shape
    depth = w_ada.shape[0]
    t = min(ATT_TILE, s)
    half = t // 2
    nq = s // t
    near_bias = min(nq, -(-(_first_far_diagonal(half) + 1) // 2))
    near_band = min(nq, -(-(DILATED_PATTERNS[-1][0] // half + 1) // 2))

    rope_tabs = _rope_tables(s)
    bank_b = _bank(rel_bias[:, 0:N_HEADS], 2 * near_band + 1, half, -1, True)
    bank_cd = _bank(rel_bias[:, N_HEADS:3 * N_HEADS], 2 * near_bias + 1, half, -1, False)
    bank_spec = lambda n, group=0: pl.BlockSpec((N_HEADS, n, half, half), lambda bi, i: (group, 0, 0, 0),
                                                pipeline_mode=pl.Buffered(1))

    w_in_t = jnp.swapaxes(w_in, 1, 2)
    mod = _ada_mod(c, w_ada, b_ada)
    for li in range(depth):
        mod3 = mod[li].reshape(b, 1, 3 * d)
        proj = _in_proj(x, g_pre[li], mod3, _layout_w_in(w_in_t, li))

        q_a, k_a, v_a = _mla_prep(proj, rope_tabs, g_q_a[li], g_kv_a[li],
                                  _layout_w_uq(w_uq_a[li]), w_ukv_a[li].astype(BF16))
        gate0 = OFF_GATE // BRANCH_WIDTH
        out_a = _attention("a", q_a, k_a, v_a, proj, 0, gate0)
        out_b = _attention("band", proj, proj, proj, proj, OFF_B // BRANCH_WIDTH, gate0 + 1,
                           extra_in=(bank_b,), extra_specs=(bank_spec(2 * near_band + 1),), near=near_band)
        sel = _select(proj)
        out_c = _attention("sel", proj, proj, proj, proj, OFF_C // BRANCH_WIDTH, gate0 + 2,
                           extra_in=(bank_cd, sel),
                           extra_specs=(bank_spec(2 * near_bias + 1, 0),
                                        pl.BlockSpec((None, t, s), lambda bi, i: (bi, i, 0))),
                           near=near_bias)
        lam_init = 0.8 - 0.6 * math.exp(-0.3 * li)
        lam_vecs = jnp.stack([lam_q1[li], lam_k1[li], lam_q2[li], lam_k2[li]])
        out_d = _attention("diff", proj, proj, proj, proj, OFF_D // BRANCH_WIDTH, gate0 + 3,
                           extra_in=(bank_cd, lam_vecs, g_sub_d[li].reshape(1, HEAD_DIM)),
                           extra_specs=(bank_spec(2 * near_bias + 1, 1),
                                        pl.BlockSpec(lam_vecs.shape, lambda bi, i: (0, 0)),
                                        pl.BlockSpec((1, HEAD_DIM), lambda bi, i: (0, 0))),
                           near=near_bias, lam_init=lam_init)
        x = _out_proj((out_a, out_b, out_c, out_d), w_out, li, x, mod3, g_post[li])
    return x
```

```python
import functools
import math

import numpy as np
import jax
import jax.numpy as jnp
from jax import lax
from jax.experimental import pallas as pl
from jax.experimental.pallas import tpu as pltpu

F32 = jnp.float32
BF16 = jnp.bfloat16

HEAD_DIM = 128
N_HEADS = 4
BRANCH_WIDTH = N_HEADS * HEAD_DIM
MIX_WIDTH = 4 * BRANCH_WIDTH
MLA_Q_RANK = 384
MLA_KV_RANK = 256
MLA_NOPE = 128
MLA_ROPE = 64
ROPE_THETA = 10000.0
DILATED_PATTERNS = ((128, 1), (512, 4), (2048, 16))
IDX_HEADS = 16
IDX_DIM = 64
TOPK_MAX = 256
REL_BUCKETS = 32
REL_MAX_DIST = 2048
NORM_EPS = 1e-6
NEG = -1e30
LOG2E = math.log2(math.e)

OFF_QIDX = 0
OFF_A = 1024
OFF_KIDX = OFF_A + 896
OFF_B = 2048
OFF_C = 3584
OFF_D = 5120
OFF_GATE = 6656
OFF_WIDX = 8704
PROJ_WIDTH = 8832

ATT_TILE = 512
V7X_VMEM_BYTES = 64 * 1024 * 1024
VMEM_LIMIT = V7X_VMEM_BYTES // 8 * 7


def _cparams(*sem):
    return pltpu.CompilerParams(dimension_semantics=sem, vmem_limit_bytes=VMEM_LIMIT)


def _split_bf16(x):
    hi = x.astype(BF16)
    return hi, (x - hi.astype(F32)).astype(BF16)


def _ada_kernel(c_ref, w_ref, b_ref, o_ref):
    c = c_ref[...]
    a_hi, a_lo = _split_bf16(c * jax.nn.sigmoid(c))
    w_hi, w_lo = _split_bf16(w_ref[...])
    dot = functools.partial(jnp.dot, preferred_element_type=F32)
    o_ref[...] = dot(a_hi, w_hi) + (dot(a_hi, w_lo) + dot(a_lo, w_hi)) + b_ref[...]


def _ada_mod(c, w_ada, b_ada):
    depth, d, n = w_ada.shape
    b = c.shape[0]
    tn = 768
    return pl.pallas_call(
        _ada_kernel,
        out_shape=jax.ShapeDtypeStruct((depth, b, n), F32),
        grid=(depth, n // tn),
        in_specs=[pl.BlockSpec((b, d), lambda l, j: (0, 0)),
                  pl.BlockSpec((None, d, tn), lambda l, j: (l, 0, j)),
                  pl.BlockSpec((None, 1, tn), lambda l, j: (l, 0, j))],
        out_specs=pl.BlockSpec((None, b, tn), lambda l, j: (l, 0, j)),
        compiler_params=_cparams("arbitrary", "arbitrary"),
        name="ada_mod",
    )(c, w_ada, b_ada.reshape(depth, 1, n))


def _in_proj_kernel(x_ref, g_ref, shift_ref, scale_ref, w_ref, o_ref):
    x = x_ref[...]
    y = x * lax.rsqrt(jnp.mean(x * x, axis=-1, keepdims=True) + NORM_EPS) * g_ref[...]
    h = (y * (1.0 + scale_ref[...]) + shift_ref[...]).astype(BF16)
    o_ref[...] = jnp.dot(h, w_ref[...], preferred_element_type=F32).astype(o_ref.dtype)


def _in_proj(x, g_pre, mod3, w):
    b, s, d = x.shape
    n = w.shape[1]
    tm = min(512, s)
    nt = s // tm
    tn = n // 3
    row = lambda j, i: (i // nt, i % nt, 0)
    return pl.pallas_call(
        _in_proj_kernel,
        out_shape=jax.ShapeDtypeStruct((b, s, n), BF16),
        grid=(n // tn, b * nt),
        in_specs=[pl.BlockSpec((None, tm, d), row),
                  pl.BlockSpec((1, d), lambda j, i: (0, 0)),
                  pl.BlockSpec((None, 1, d), lambda j, i: (i // nt, 0, 0)),
                  pl.BlockSpec((None, 1, d), lambda j, i: (i // nt, 0, 1)),
                  pl.BlockSpec((d, tn), lambda j, i: (0, j), pipeline_mode=pl.Buffered(1))],
        out_specs=pl.BlockSpec((None, tm, tn), lambda j, i: (i // nt, i % nt, j)),
        compiler_params=_cparams("arbitrary", "arbitrary"),
        name="in_proj",
    )(x, g_pre.reshape(1, d), mod3, mod3, w)


def _rms(x, g):
    return x * lax.rsqrt(jnp.mean(x * x, axis=-1, keepdims=True) + NORM_EPS) * g


def _mla_prep_kernel(p_ref, cos_ref, sin_ref, cs_ref, gq_ref, gkv_ref, wq_ref, wkv_ref, q_ref, k_ref, v_ref):
    cos = cos_ref[...]
    sin = sin_ref[...]
    cs = cs_ref[...]
    cq = _rms(p_ref[:, 0:MLA_Q_RANK].astype(F32), gq_ref[...]).astype(BF16)
    ckv = _rms(p_ref[:, MLA_Q_RANK:MLA_Q_RANK + MLA_KV_RANK].astype(F32), gkv_ref[...]).astype(BF16)
    q = jnp.dot(cq, wq_ref[...], preferred_element_type=F32)
    kv = jnp.dot(ckv, wkv_ref[...], preferred_element_type=F32)
    k_rope = (p_ref[:, 640:768].astype(F32) * cos + p_ref[:, 768:896].astype(F32) * sin).astype(BF16)
    for h in range(N_HEADS):
        q_ref[:, h * 256:h * 256 + 128] = q[:, h * 256:h * 256 + 128].astype(BF16)
        z = q[:, h * 256 + 128:(h + 1) * 256] * cs
        q_ref[:, h * 256 + 128:(h + 1) * 256] = (z + pltpu.roll(z, MLA_ROPE, axis=1)).astype(BF16)
        k_ref[:, h * 256:h * 256 + 128] = kv[:, h * 256:h * 256 + 128].astype(BF16)
        k_ref[:, h * 256 + 128:(h + 1) * 256] = k_rope
        v_ref[:, h * 128:(h + 1) * 128] = kv[:, h * 256 + 128:(h + 1) * 256].astype(BF16)


def _mla_prep(proj, rope_tabs, g_q, g_kv, wq, wkv):
    b, s, _ = proj.shape
    tm = min(512, s)
    const = lambda bi, i: (0, 0)
    return pl.pallas_call(
        _mla_prep_kernel,
        out_shape=(jax.ShapeDtypeStruct((b, s, N_HEADS * 256), BF16),
                   jax.ShapeDtypeStruct((b, s, N_HEADS * 256), BF16),
                   jax.ShapeDtypeStruct((b, s, BRANCH_WIDTH), BF16)),
        grid=(b, s // tm),
        in_specs=[pl.BlockSpec((None, tm, 1024), lambda bi, i: (bi, i, OFF_A // 1024)),
                  pl.BlockSpec((tm, 128), lambda bi, i: (i, 0)),
                  pl.BlockSpec((tm, 128), lambda bi, i: (i, 0)),
                  pl.BlockSpec((tm, 128), lambda bi, i: (i, 0)),
                  pl.BlockSpec((1, MLA_Q_RANK), const),
                  pl.BlockSpec((1, MLA_KV_RANK), const),
                  pl.BlockSpec(wq.shape, const),
                  pl.BlockSpec(wkv.shape, const)],
        out_specs=(pl.BlockSpec((None, tm, N_HEADS * 256), lambda bi, i: (bi, i, 0)),
                   pl.BlockSpec((None, tm, N_HEADS * 256), lambda bi, i: (bi, i, 0)),
                   pl.BlockSpec((None, tm, BRANCH_WIDTH), lambda bi, i: (bi, i, 0))),
        compiler_params=_cparams("arbitrary", "arbitrary"),
        name="mla_prep",
    )(proj, *rope_tabs, g_q.reshape(1, -1), g_kv.reshape(1, -1), wq, wkv)


def _bucket_np(n):
    max_exact = REL_BUCKETS // 2
    nf = np.maximum(n, max_exact).astype(np.float32)
    large = max_exact + (np.log(nf / np.float32(max_exact)) / np.float32(math.log(REL_MAX_DIST / max_exact))
                         * np.float32(REL_BUCKETS - max_exact)).astype(np.int32)
    return np.where(n < max_exact, n, np.minimum(large, REL_BUCKETS - 1)).astype(np.int32)


def _bucket_starts():
    buckets = _bucket_np(np.arange(2 * REL_MAX_DIST))
    return [int(np.argmax(buckets >= b)) for b in range(REL_BUCKETS)]


def _bank_kernel(tab_ref, o_ref, *, t, d_min, band):
    h = pl.program_id(0)
    starts = _bucket_starts()
    row = lax.broadcasted_iota(jnp.int32, (t, t), 0)
    col = lax.broadcasted_iota(jnp.int32, (t, t), 1)
    for j in range(o_ref.shape[0]):
        d = d_min + j
        if d < 0:
            o_ref[j] = jnp.full((t, t), NEG, F32)
            continue
        dist = t * d + row - col
        b_lo, b_hi = (int(x) for x in _bucket_np(np.array([max(t * d - t + 1, 0), t * d + t - 1])))
        val = jnp.full((t, t), tab_ref[b_lo, h], F32)
        for b in range(b_lo + 1, b_hi + 1):
            val = jnp.where(dist >= starts[b], tab_ref[b, h], val)
        if band:
            mult = jnp.zeros((t, t), jnp.int32)
            for window, dil in DILATED_PATTERNS:
                mult += jnp.where((dist >= 0) & (dist <= window) & ((dist & (dil - 1)) == 0), 1, 0)
            val = val + jnp.where(mult == 3, math.log(3.0), jnp.where(mult == 2, math.log(2.0), 0.0))
            keep = mult > 0
        else:
            val = val - tab_ref[REL_BUCKETS - 1, h]
            keep = dist >= 0
        o_ref[j] = jnp.where(keep, val * LOG2E, NEG)


def _bank(tab, n_tables, t, d_min, band):
    nh = tab.shape[1]
    return pl.pallas_call(
        functools.partial(_bank_kernel, t=t, d_min=d_min, band=band),
        out_shape=jax.ShapeDtypeStruct((nh, n_tables, t, t), F32),
        grid=(nh,),
        in_specs=[pl.BlockSpec(memory_space=pltpu.SMEM)],
        out_specs=pl.BlockSpec((None, n_tables, t, t), lambda h: (h, 0, 0, 0)),
        compiler_params=_cparams("arbitrary"),
        name="bank_band" if band else "bank_bias",
    )(tab)


def _first_far_diagonal(t):
    last = _bucket_starts()[REL_BUCKETS - 1]
    return -(-(last + t - 1) // t)


def _qk(q, k):
    return lax.dot_general(q, k, (((1,), (1,)), ((), ())), preferred_element_type=F32)


def _flash_init(m_ref, acc_ref):
    m_ref[...] = jnp.full(m_ref.shape, NEG, F32)
    acc_ref[...] = jnp.zeros(acc_ref.shape, F32)


def _flash_update(slot, s, v, m_ref, acc_ref, s_ref, next_scores):
    m_prev = m_ref[slot]
    m_new = jnp.maximum(m_prev, jnp.max(s, axis=-1, keepdims=True))
    alpha = jnp.exp2(m_prev - m_new)
    p = jnp.concatenate([jnp.exp2(s[:, j * 128:(j + 1) * 128] - m_new) for j in range(s.shape[1] // 128)],
                        axis=1).astype(BF16)
    if next_scores is not None:
        s_ref[slot] = next_scores()
    v_ones = jnp.concatenate([v, jnp.ones_like(v)], axis=1)
    acc_ref[slot] = (jnp.concatenate([alpha, alpha], axis=1) * acc_ref[slot]
                     + jnp.dot(p, v_ones, preferred_element_type=F32))
    m_ref[slot] = m_new


def _flash_result(slot, acc_ref):
    acc = acc_ref[slot]
    return acc[:, :HEAD_DIM] / acc[:, HEAD_DIM:]


def _silu(g):
    return g * jax.nn.sigmoid(g)


def _tile(ref, kj, t, c0, width):
    start = kj * t if isinstance(kj, int) else pl.multiple_of(kj * t, t)
    return ref[pl.ds(start, t), c0:c0 + width]


def _bank_tile(bank_ref, h, i, kj):
    base = 2 * (i - kj) + 1
    top = jnp.concatenate([bank_ref[h, base], bank_ref[h, base - 1]], axis=1)
    bot = jnp.concatenate([bank_ref[h, base + 1], bank_ref[h, base]], axis=1)
    return jnp.concatenate([top, bot], axis=0)


def _emit(o_ref, g_ref, h, o):
    gate = g_ref[:, h * 128:(h + 1) * 128].astype(F32)
    o_ref[:, h * 128:(h + 1) * 128] = (o * _silu(gate)).astype(BF16)


def _flash_walk(i, first_key, n_far, slots, score, logits, value, s_ref, m_ref, acc_ref, prep=None):
    lo = first_key(i)
    i_next = jnp.minimum(i + 1, pl.num_programs(1) - 1)

    @pl.when(i == 0)
    def _():
        _flash_init(m_ref, acc_ref)
        for slot in range(slots):
            s_ref[slot] = score(slot, False, lo)

    def step(kj, phase):
        ctx = prep(kj) if prep is not None else None
        for slot in range(slots):
            s = logits(slot, s_ref[slot], kj, phase, ctx)
            if phase == "last":
                nxt = functools.partial(score, slot, True, first_key(i_next))
            else:
                nxt = functools.partial(score, slot, False, kj + 1)
            _flash_update(slot, s, value(slot, kj), m_ref, acc_ref, s_ref, nxt)

    def walk(a, b, phase):
        def body(kj, carry):
            step(kj, phase)
            return carry
        lax.fori_loop(a, b, body, 0)

    if n_far is not None:
        walk(lo, n_far, "far")
        lo = n_far
    walk(lo, i, "near")
    step(i, "last")


def _attn_a_kernel(q_ref, qn_ref, k_ref, v_ref, g_ref, o_ref, s_ref, m_ref, acc_ref, *, t):
    i = pl.program_id(1)

    def score(h, next_q, kj):
        q = (qn_ref if next_q else q_ref)[:, h * 256:(h + 1) * 256]
        return _qk(q, _tile(k_ref, kj, t, h * 256, 256))

    def logits(h, s, kj, phase, ctx):
        if phase != "last":
            return s
        causal = lax.broadcasted_iota(jnp.int32, (t, t), 1) <= lax.broadcasted_iota(jnp.int32, (t, t), 0)
        return jnp.where(causal, s, NEG)

    def value(h, kj):
        return _tile(v_ref, kj, t, h * 128, 128)

    _flash_walk(i, lambda qi: 0, None, N_HEADS, score, logits, value, s_ref, m_ref, acc_ref)
    for h in range(N_HEADS):
        _emit(o_ref, g_ref, h, _flash_result(h, acc_ref))
    _flash_init(m_ref, acc_ref)


def _attn_band_kernel(q_ref, qn_ref, k_ref, v_ref, g_ref, bank_ref, o_ref, s_ref, m_ref, acc_ref, *, t, near):
    i = pl.program_id(1)

    def score(h, next_q, kj):
        q = (qn_ref if next_q else q_ref)[:, h * 128:(h + 1) * 128]
        return _qk(q, _tile(k_ref, kj, t, h * 128, 128))

    def logits(h, s, kj, phase, ctx):
        return s + _bank_tile(bank_ref, h, i, kj)

    def value(h, kj):
        return _tile(v_ref, kj, t, h * 128, 128)

    _flash_walk(i, lambda qi: jnp.maximum(qi - (near - 1), 0), None, N_HEADS, score, logits, value,
                s_ref, m_ref, acc_ref)
    for h in range(N_HEADS):
        _emit(o_ref, g_ref, h, _flash_result(h, acc_ref))
    _flash_init(m_ref, acc_ref)


def _attn_sel_kernel(q_ref, qn_ref, k_ref, v_ref, g_ref, bank_ref, sel_ref, o_ref,
                     s_ref, m_ref, acc_ref, *, t, near):
    i = pl.program_id(1)

    def score(h, next_q, kj):
        q = (qn_ref if next_q else q_ref)[:, h * 128:(h + 1) * 128]
        return _qk(q, _tile(k_ref, kj, t, h * 128, 128))

    def prep(kj):
        return sel_ref[:, pl.ds(pl.multiple_of(kj * t, t), t)].astype(F32)

    def logits(h, s, kj, phase, sel):
        return s + sel if phase == "far" else s + (sel + _bank_tile(bank_ref, h, i, kj))

    def value(h, kj):
        return _tile(v_ref, kj, t, h * 128, 128)

    _flash_walk(i, lambda qi: 0, jnp.maximum(i - (near - 1), 0), N_HEADS, score, logits, value,
                s_ref, m_ref, acc_ref, prep)
    for h in range(N_HEADS):
        _emit(o_ref, g_ref, h, _flash_result(h, acc_ref))
    _flash_init(m_ref, acc_ref)


def _attn_diff_kernel(q_ref, qn_ref, k_ref, v_ref, g_ref, bank_ref, lam_ref, gsub_ref, o_ref,
                      s_ref, m_ref, acc_ref, *, t, near, lam_init):
    i = pl.program_id(1)
    first_half = lax.broadcasted_iota(jnp.int32, (t, HEAD_DIM), 1) < HEAD_DIM // 2

    def score(slot, next_q, kj):
        h = slot // 2
        q = (qn_ref if next_q else q_ref)[:, h * 128:(h + 1) * 128]
        keep = first_half if slot % 2 == 0 else jnp.logical_not(first_half)
        q = jnp.where(keep, q, jnp.zeros_like(q))
        return _qk(q, _tile(k_ref, kj, t, h * 128, 128))

    def logits(slot, s, kj, phase, ctx):
        return s if phase == "far" else s + _bank_tile(bank_ref, slot // 2, i, kj)

    def value(slot, kj):
        return _tile(v_ref, kj, t, (slot // 2) * 128, 128)

    _flash_walk(i, lambda qi: 0, jnp.maximum(i - (near - 1), 0), 2 * N_HEADS, score, logits, value,
                s_ref, m_ref, acc_ref)
    lam_v = lam_ref[...]
    lam = (jnp.exp(jnp.sum(lam_v[0:1] * lam_v[1:2], axis=-1, keepdims=True))
           - jnp.exp(jnp.sum(lam_v[2:3] * lam_v[3:4], axis=-1, keepdims=True)) + lam_init)
    for h in range(N_HEADS):
        o = _flash_result(2 * h, acc_ref) - lam * _flash_result(2 * h + 1, acc_ref)
        _emit(o_ref, g_ref, h, _rms(o, gsub_ref[...]) * (1.0 - lam_init))
    _flash_init(m_ref, acc_ref)


def _attention(kind, q_arr, k_arr, v_arr, proj, q_blk, gate_blk, extra_in=(), extra_specs=(), **kw):
    b, s, _ = proj.shape
    t = min(ATT_TILE, s)
    dk = 256 if kind == "a" else 128
    qw = N_HEADS * dk
    k_blk = 0 if kind == "a" else q_blk + 1
    v_blk = 0 if kind == "a" else q_blk + 2
    body = {"a": _attn_a_kernel, "band": _attn_band_kernel, "sel": _attn_sel_kernel,
            "diff": _attn_diff_kernel}[kind]
    slots = 2 * N_HEADS if kind == "diff" else N_HEADS
    scratch = [pltpu.VMEM((slots, t, t), F32), pltpu.VMEM((slots, t, HEAD_DIM), F32),
               pltpu.VMEM((slots, t, 2 * HEAD_DIM), F32)]
    last = s // t - 1
    in_specs = [pl.BlockSpec((None, t, qw), lambda bi, i: (bi, i, q_blk)),
                pl.BlockSpec((None, t, qw), lambda bi, i: (bi, jnp.minimum(i + 1, last), q_blk)),
                pl.BlockSpec((None, s, qw), lambda bi, i: (bi, 0, k_blk)),
                pl.BlockSpec((None, s, BRANCH_WIDTH), lambda bi, i: (bi, 0, v_blk)),
                pl.BlockSpec((None, t, BRANCH_WIDTH), lambda bi, i: (bi, i, gate_blk))]
    in_specs += list(extra_specs)
    return pl.pallas_call(
        functools.partial(body, t=t, **kw),
        out_shape=jax.ShapeDtypeStruct((b, s, BRANCH_WIDTH), BF16),
        grid=(b, s // t),
        in_specs=in_specs,
        out_specs=pl.BlockSpec((None, t, BRANCH_WIDTH), lambda bi, i: (bi, i, 0)),
        scratch_shapes=scratch,
        compiler_params=_cparams("arbitrary", "arbitrary"),
        name="attn_" + kind,
    )(q_arr, q_arr, k_arr, v_arr, proj, *extra_in)


def _bit_transpose32(words):
    a = list(words)
    j, m = 16, 0x0000FFFF
    while j:
        for k in range(32):
            if not k & j:
                t = (a[k] ^ lax.shift_right_logical(a[k + j], jnp.int32(j))) & jnp.int32(m)
                a[k] = a[k] ^ t
                a[k + j] = a[k + j] ^ (t << j)
        j >>= 1
        m = (m ^ (m << j)) & 0xFFFFFFFF if j else m
    return a


def _select_kernel(qi_ref, ki_ref, wi_ref, o_ref, key_ref, plane_ref, alive_ref, *, tq, kc, n_sel):
    i = pl.program_id(1)
    s_len = o_ref.shape[1]
    n_ch = (i * tq + tq + kc - 1) // kc
    wpc = kc // 32
    int_min = jnp.int32(-2 ** 31)
    lane = lax.broadcasted_iota(jnp.int32, (tq, 128), 1)
    w_t = wi_ref[...].astype(F32).T
    q_heads = []
    for j in range(IDX_HEADS // 2):
        q2 = qi_ref[:, j * 128:(j + 1) * 128]
        q_heads.append(jnp.where(lane < IDX_DIM, q2, jnp.zeros_like(q2)))
        q_heads.append(jnp.where(lane >= IDX_DIM, q2, jnp.zeros_like(q2)))
    qpos = i * tq + lax.broadcasted_iota(jnp.int32, (kc, tq), 1)
    kiota = lax.broadcasted_iota(jnp.int32, (kc, tq), 0)

    def chunk(c):
        return pl.ds(pl.multiple_of(c * kc, kc), kc)

    def score_chunk(c, _):
        k = ki_ref[chunk(c), :]
        acc = jnp.zeros((kc, tq), F32)
        for hh in range(IDX_HEADS):
            acc = acc + jnp.maximum(_qk(k, q_heads[hh]), 0.0) * w_t[hh:hh + 1, :]
        acc = jnp.where(c * kc + kiota <= qpos, acc + 0.0, NEG)
        bits = pltpu.bitcast(acc, jnp.int32)
        keys = bits ^ ((bits >> 31) & jnp.int32(0x7FFFFFFF))
        key_ref[chunk(c), :] = keys
        ukeys = keys ^ int_min
        for blk in range(kc // 256):
            planes = _bit_transpose32([ukeys[blk * 256 + 8 * j:blk * 256 + 8 * j + 8, :] for j in range(32)])
            row = pl.multiple_of(c * wpc + blk * 8, 8)
            for b in range(32):
                plane_ref[b, pl.ds(row, 8), :] = planes[b]
        return 0

    lax.fori_loop(0, n_ch, score_chunk, 0)

    def clear_chunk(c, _):
        plane_ref[:, pl.ds(pl.multiple_of(c * wpc, wpc), wpc), :] = jnp.zeros((32, wpc, tq), jnp.int32)
        return 0

    lax.fori_loop(n_ch, s_len // kc, clear_chunk, 0)
    n_rows = s_len // 32
    word_row = lax.broadcasted_iota(jnp.int32, (n_rows, tq), 0)
    alive_ref[...] = jnp.where(word_row < n_ch * wpc, jnp.int32(-1), jnp.int32(0))

    def radix_step(bi, carry):
        thr, above = carry
        alive = alive_ref[...]
        plane = plane_ref[bi]
        ones = lax.population_count(alive & plane)
        ones = jnp.sum(jnp.sum(ones.reshape(n_rows // 8, 8, tq), axis=0), axis=0, keepdims=True)
        take = above + ones >= n_sel
        thr = jnp.where(take, thr | (jnp.int32(1) << (31 - bi)), thr)
        above = jnp.where(take, above, above + ones)
        alive_ref[...] = alive & (plane ^ jnp.where(take, jnp.int32(0), jnp.int32(-1)))
        return thr, above

    zeros = jnp.zeros((1, tq), jnp.int32)
    thr_u, above = lax.fori_loop(0, 32, radix_step, (zeros, zeros))
    thr = thr_u ^ int_min
    n_equal = lax.population_count(alive_ref[...])
    n_equal = jnp.sum(jnp.sum(n_equal.reshape(n_rows // 8, 8, tq), axis=0), axis=0, keepdims=True)
    need = n_sel - above
    masked_key = int(np.float32(NEG).view(np.int32)) ^ 0x7FFFFFFF
    tie = (n_equal > need) & (thr != masked_key)
    any_tie = jnp.max(jnp.where(tie, 1, 0)) > 0

    @pl.when(jnp.logical_not(any_tie))
    def _():
        def emit(c, _):
            keep = jnp.where(key_ref[chunk(c), :] >= thr, 0.0, NEG)
            o_ref[:, chunk(c)] = keep.T.astype(BF16)
            return 0

        lax.fori_loop(0, n_ch, emit, 0)

    @pl.when(any_tie)
    def _():
        def equal_below(bound):
            def body(c, part):
                hit = jnp.where((key_ref[chunk(c), :] == thr) & (c * kc + kiota < bound), 1, 0)
                return part + jnp.sum(hit.reshape(kc // 8, 8, tq), axis=0)
            part = lax.fori_loop(0, n_ch, body, jnp.zeros((8, tq), jnp.int32))
            return jnp.sum(part, axis=0, keepdims=True)

        n_bits = s_len.bit_length()

        def bound_step(bi, cut):
            cand = cut + (jnp.int32(1) << (n_bits - 1 - bi))
            ok = (cand <= s_len) & (equal_below(cand) <= need)
            return jnp.where(ok, cand, cut)

        cut = lax.fori_loop(0, n_bits, bound_step, zeros)

        def emit(c, _):
            keys = key_ref[chunk(c), :]
            kept = (keys > thr) | ((keys == thr) & (c * kc + kiota < cut))
            o_ref[:, chunk(c)] = jnp.where(kept, 0.0, NEG).T.astype(BF16)
            return 0

        lax.fori_loop(0, n_ch, emit, 0)

    def blank(c, _):
        o_ref[:, chunk(c)] = jnp.full((tq, kc), NEG, BF16)
        return 0

    lax.fori_loop(n_ch, s_len // kc, blank, 0)


def _select(proj):
    b, s, _ = proj.shape
    tq = min(512, s)
    kc = min(512, s)
    n_sel = min(TOPK_MAX, s // 4)
    return pl.pallas_call(
        functools.partial(_select_kernel, tq=tq, kc=kc, n_sel=n_sel),
        out_shape=jax.ShapeDtypeStruct((b, s, s), BF16),
        grid=(b, s // tq),
        in_specs=[pl.BlockSpec((None, tq, 1024), lambda bi, i: (bi, i, OFF_QIDX // 1024)),
                  pl.BlockSpec((None, s, 128), lambda bi, i: (bi, 0, OFF_KIDX // 128)),
                  pl.BlockSpec((None, tq, 128), lambda bi, i: (bi, i, OFF_WIDX // 128))],
        out_specs=pl.BlockSpec((None, tq, s), lambda bi, i: (bi, i, 0)),
        scratch_shapes=[pltpu.VMEM((s, tq), jnp.int32), pltpu.VMEM((32, s // 32, tq), jnp.int32),
                        pltpu.VMEM((s // 32, tq), jnp.int32)],
        compiler_params=_cparams("arbitrary", "arbitrary"),
        name="idx_select",
    )(proj, proj, proj)


def _out_kernel(a_ref, b_ref, c_ref, d_ref, w_ref, x_ref, gate_ref, g_ref, o_ref, wb_ref):
    @pl.when((pl.program_id(0) == 0) & (pl.program_id(1) == 0))
    def _():
        wb_ref[...] = w_ref[...].astype(BF16)

    y = jnp.dot(a_ref[...], wb_ref[0:512, :], preferred_element_type=F32)
    y += jnp.dot(b_ref[...], wb_ref[512:1024, :], preferred_element_type=F32)
    y += jnp.dot(c_ref[...], wb_ref[1024:1536, :], preferred_element_type=F32)
    y += jnp.dot(d_ref[...], wb_ref[1536:2048, :], preferred_element_type=F32)
    o_ref[...] = x_ref[...] + gate_ref[...] * _rms(y, g_ref[...])


def _out_proj(outs, w_out, li, x, mod3, g_post):
    b, s, d = x.shape
    tm = min(512, s)
    mix = lambda bi, i: (bi, i, 0)
    return pl.pallas_call(
        _out_kernel,
        out_shape=jax.ShapeDtypeStruct((b, s, d), F32),
        grid=(b, s // tm),
        in_specs=[pl.BlockSpec((None, tm, BRANCH_WIDTH), mix)] * 4
        + [pl.BlockSpec((None,) + w_out.shape[1:], lambda bi, i: (li, 0, 0), pipeline_mode=pl.Buffered(1)),
           pl.BlockSpec((None, tm, d), mix),
           pl.BlockSpec((None, 1, d), lambda bi, i: (bi, 0, 2)),
           pl.BlockSpec((1, d), lambda bi, i: (0, 0))],
        out_specs=pl.BlockSpec((None, tm, d), mix),
        scratch_shapes=[pltpu.VMEM(w_out.shape[1:], BF16)],
        compiler_params=_cparams("arbitrary", "arbitrary"),
        name="out_proj",
    )(*outs, w_out, x, mod3, g_post.reshape(1, d))


def _rope_tables(s):
    half = MLA_ROPE // 2
    inv = ROPE_THETA ** (-jnp.arange(half, dtype=F32) / half)
    ang = jnp.arange(s, dtype=F32)[:, None] * inv[None, :]
    z = jnp.zeros((s, 128 - MLA_ROPE), F32)
    cos, sin = jnp.cos(ang), jnp.sin(ang)
    cat = lambda *parts: jnp.concatenate(parts, axis=-1)
    return cat(cos, cos, z), cat(sin, sin, z), cat(cos, cos, sin, sin)


def _rot_cols(w):
    half = w.shape[-1] // 2
    return jnp.concatenate([-w[..., half:], w[..., :half]], axis=-1)


IN_SPLITS = (("a_cq", 384), ("a_ckv", 256), ("a_krope", 64), ("b_q", 512), ("b_k", 512), ("b_v", 512),
             ("c_q", 512), ("c_k", 512), ("c_v", 512), ("c_qidx", 1024), ("c_kidx", 64), ("c_widx", 16),
             ("d_q", 512), ("d_k", 512), ("d_v", 512), ("gate", 2048))
IN_WIDTH = sum(width for _, width in IN_SPLITS)


def _layout_w_in_kernel(w_ref, o_ref):
    src, start = {}, 0
    for name, width in IN_SPLITS:
        src[name] = start
        start += width
    tk = w_ref.shape[1]

    def rows(name, width, offset=0):
        a = src[name] + offset
        return w_ref[a:a + width, :]

    def put(dst, val, scale=None):
        for r in range(0, val.shape[0], 512):
            piece = val[r:r + 512]
            if scale is not None:
                piece = piece * scale
            o_ref[:, dst + r:dst + r + piece.shape[0]] = piece.T.astype(BF16)

    z64 = jnp.zeros((64, tk), F32)
    half = MLA_ROPE // 2
    put(OFF_QIDX, rows("c_qidx", 1024), IDX_DIM ** -0.5)
    put(OFF_A, rows("a_cq", MLA_Q_RANK + MLA_KV_RANK))
    put(OFF_A + 640, jnp.concatenate(
        [rows("a_krope", MLA_ROPE), z64, -rows("a_krope", half, half), rows("a_krope", half), z64,
         rows("c_kidx", IDX_DIM), rows("c_kidx", IDX_DIM)], axis=0))
    for off, name, dim in ((OFF_B, "b", HEAD_DIM), (OFF_C, "c", HEAD_DIM), (OFF_D, "d", HEAD_DIM // 2)):
        put(off, rows(name + "_q", BRANCH_WIDTH), LOG2E * dim ** -0.5)
        put(off + BRANCH_WIDTH, rows(name + "_k", 2 * BRANCH_WIDTH))
    put(OFF_GATE, rows("gate", MIX_WIDTH))
    put(OFF_WIDX, jnp.concatenate([rows("c_widx", IDX_HEADS) * IDX_HEADS ** -0.5,
                                   jnp.zeros((128 - IDX_HEADS, tk), F32)], axis=0))


def _layout_w_in(w_in_t, li):
    d = w_in_t.shape[2]
    tk = 256
    return pl.pallas_call(
        _layout_w_in_kernel,
        out_shape=jax.ShapeDtypeStruct((d, PROJ_WIDTH), BF16),
        grid=(d // tk,),
        in_specs=[pl.BlockSpec((None, IN_WIDTH, tk), lambda i: (li, 0, i))],
        out_specs=pl.BlockSpec((tk, PROJ_WIDTH), lambda i: (i, 0)),
        compiler_params=_cparams("arbitrary"),
        name="layout_w_in",
    )(w_in_t)


def _layout_w_uq(w):
    r = w.shape[0]
    w = w.reshape(r, N_HEADS, MLA_NOPE + MLA_ROPE) * (LOG2E * (MLA_NOPE + MLA_ROPE) ** -0.5)
    rope = w[..., MLA_NOPE:]
    return jnp.concatenate([w[..., :MLA_NOPE], rope, _rot_cols(rope)], axis=-1).reshape(r, -1).astype(BF16)


def kernel(x, c, w_ada, b_ada, g_pre, g_post, w_in, g_q_a, w_uq_a, g_kv_a, w_ukv_a,
           lam_q1, lam_k1, lam_q2, lam_k2, g_sub_d, w_out, rel_bias):
    b, s, d = x.shape
    depth = w_ada.shape[0]
    t = min(ATT_TILE, s)
    half = t // 2
    nq = s // t
    near_bias = min(nq, -(-(_first_far_diagonal(half) + 1) // 2))
    near_band = min(nq, -(-(DILATED_PATTERNS[-1][0] // half + 1) // 2))

    rope_tabs = _rope_tables(s)
    bank_b = _bank(rel_bias[:, 0:N_HEADS], 2 * near_band + 1, half, -1, True)
    bank_cd = _bank(rel_bias[:, N_HEADS:3 * N_HEADS], 2 * near_bias + 1, half, -1, False)
    bank_spec = lambda n, group=0: pl.BlockSpec((N_HEADS, n, half, half), lambda bi, i: (group, 0, 0, 0),
                                                pipeline_mode=pl.Buffered(1))

    w_in_t = jnp.swapaxes(w_in, 1, 2)
    mod = _ada_mod(c, w_ada, b_ada)
    for li in range(depth):
        mod3 = mod[li].reshape(b, 1, 3 * d)
        proj = _in_proj(x, g_pre[li], mod3, _layout_w_in(w_in_t, li))

        q_a, k_a, v_a = _mla_prep(proj, rope_tabs, g_q_a[li], g_kv_a[li],
                                  _layout_w_uq(w_uq_a[li]), w_ukv_a[li].astype(BF16))
        gate0 = OFF_GATE // BRANCH_WIDTH
        out_a = _attention("a", q_a, k_a, v_a, proj, 0, gate0)
        out_b = _attention("band", proj, proj, proj, proj, OFF_B // BRANCH_WIDTH, gate0 + 1,
                           extra_in=(bank_b,), extra_specs=(bank_spec(2 * near_band + 1),), near=near_band)
        sel = _select(proj)
        out_c = _attention("sel", proj, proj, proj, proj, OFF_C // BRANCH_WIDTH, gate0 + 2,
                           extra_in=(bank_cd, sel),
                           extra_specs=(bank_spec(2 * near_bias + 1, 0),
                                        pl.BlockSpec((None, t, s), lambda bi, i: (bi, i, 0))),
                           near=near_bias)
        lam_init = 0.8 - 0.6 * math.exp(-0.3 * li)
        lam_vecs = jnp.stack([lam_q1[li], lam_k1[li], lam_q2[li], lam_k2[li]])
        out_d = _attention("diff", proj, proj, proj, proj, OFF_D // BRANCH_WIDTH, gate0 + 3,
                           extra_in=(bank_cd, lam_vecs, g_sub_d[li].reshape(1, HEAD_DIM)),
                           extra_specs=(bank_spec(2 * near_bias + 1, 1),
                                        pl.BlockSpec(lam_vecs.shape, lambda bi, i: (0, 0)),
                                        pl.BlockSpec((1, HEAD_DIM), lambda bi, i: (0, 0))),
                           near=near_bias, lam_init=lam_init)
        x = _out_proj((out_a, out_b, out_c, out_d), w_out, li, x, mod3, g_post[li])
    return x
```

```python
import functools
import math

import numpy as np
import jax
import jax.numpy as jnp
from jax import lax
from jax.experimental import pallas as pl
from jax.experimental.pallas import tpu as pltpu

F32 = jnp.float32
BF16 = jnp.bfloat16

HEAD_DIM = 128
N_HEADS = 4
BRANCH_WIDTH = N_HEADS * HEAD_DIM
MIX_WIDTH = 4 * BRANCH_WIDTH
MLA_Q_RANK = 384
MLA_KV_RANK = 256
MLA_NOPE = 128
MLA_ROPE = 64
ROPE_THETA = 10000.0
DILATED_PATTERNS = ((128, 1), (512, 4), (2048, 16))
IDX_HEADS = 16
IDX_DIM = 64
TOPK_MAX = 256
REL_BUCKETS = 32
REL_MAX_DIST = 2048
NORM_EPS = 1e-6
NEG = -1e30
LOG2E = math.log2(math.e)

OFF_QIDX = 0
OFF_A = 1024
OFF_KIDX = OFF_A + 896
OFF_B = 2048
OFF_C = 3584
OFF_D = 5120
OFF_GATE = 6656
OFF_WIDX = 8704
PROJ_WIDTH = 8832

ATT_TILE = 512
V7X_VMEM_BYTES = 64 * 1024 * 1024
VMEM_LIMIT = V7X_VMEM_BYTES // 8 * 7


def _cparams(*sem):
    return pltpu.CompilerParams(dimension_semantics=sem, vmem_limit_bytes=VMEM_LIMIT)


def _split_bf16(x):
    hi = x.astype(BF16)
    return hi, (x - hi.astype(F32)).astype(BF16)


def _ada_kernel(c_ref, w_ref, b_ref, o_ref):
    c = c_ref[...]
    a_hi, a_lo = _split_bf16(c * jax.nn.sigmoid(c))
    w_hi, w_lo = _split_bf16(w_ref[...])
    dot = functools.partial(jnp.dot, preferred_element_type=F32)
    o_ref[...] = dot(a_hi, w_hi) + (dot(a_hi, w_lo) + dot(a_lo, w_hi)) + b_ref[...]


def _ada_mod(c, w_ada, b_ada):
    depth, d, n = w_ada.shape
    b = c.shape[0]
    tn = 768
    return pl.pallas_call(
        _ada_kernel,
        out_shape=jax.ShapeDtypeStruct((depth, b, n), F32),
        grid=(depth, n // tn),
        in_specs=[pl.BlockSpec((b, d), lambda l, j: (0, 0)),
                  pl.BlockSpec((None, d, tn), lambda l, j: (l, 0, j)),
                  pl.BlockSpec((None, 1, tn), lambda l, j: (l, 0, j))],
        out_specs=pl.BlockSpec((None, b, tn), lambda l, j: (l, 0, j)),
        compiler_params=_cparams("arbitrary", "arbitrary"),
        name="ada_mod",
    )(c, w_ada, b_ada.reshape(depth, 1, n))


def _in_proj_kernel(x_ref, g_ref, shift_ref, scale_ref, w_ref, o_ref):
    x = x_ref[...]
    y = x * lax.rsqrt(jnp.mean(x * x, axis=-1, keepdims=True) + NORM_EPS) * g_ref[...]
    h = (y * (1.0 + scale_ref[...]) + shift_ref[...]).astype(BF16)
    o_ref[...] = jnp.dot(h, w_ref[...], preferred_element_type=F32).astype(o_ref.dtype)


def _in_proj(x, g_pre, mod3, w):
    b, s, d = x.shape
    n = w.shape[1]
    tm = min(512, s)
    nt = s // tm
    tn = n // 3
    row = lambda j, i: (i // nt, i % nt, 0)
    return pl.pallas_call(
        _in_proj_kernel,
        out_shape=jax.ShapeDtypeStruct((b, s, n), BF16),
        grid=(n // tn, b * nt),
        in_specs=[pl.BlockSpec((None, tm, d), row),
                  pl.BlockSpec((1, d), lambda j, i: (0, 0)),
                  pl.BlockSpec((None, 1, d), lambda j, i: (i // nt, 0, 0)),
                  pl.BlockSpec((None, 1, d), lambda j, i: (i // nt, 0, 1)),
                  pl.BlockSpec((d, tn), lambda j, i: (0, j), pipeline_mode=pl.Buffered(1))],
        out_specs=pl.BlockSpec((None, tm, tn), lambda j, i: (i // nt, i % nt, j)),
        compiler_params=_cparams("arbitrary", "arbitrary"),
        name="in_proj",
    )(x, g_pre.reshape(1, d), mod3, mod3, w)


def _rms(x, g):
    return x * lax.rsqrt(jnp.mean(x * x, axis=-1, keepdims=True) + NORM_EPS) * g


def _mla_prep_kernel(p_ref, cos_ref, sin_ref, cs_ref, gq_ref, gkv_ref, wq_ref, wkv_ref, q_ref, k_ref, v_ref):
    cos = cos_ref[...]
    sin = sin_ref[...]
    cs = cs_ref[...]
    cq = _rms(p_ref[:, 0:MLA_Q_RANK].astype(F32), gq_ref[...]).astype(BF16)
    ckv = _rms(p_ref[:, MLA_Q_RANK:MLA_Q_RANK + MLA_KV_RANK].astype(F32), gkv_ref[...]).astype(BF16)
    q = jnp.dot(cq, wq_ref[...], preferred_element_type=F32)
    kv = jnp.dot(ckv, wkv_ref[...], preferred_element_type=F32)
    k_rope = (p_ref[:, 640:768].astype(F32) * cos + p_ref[:, 768:896].astype(F32) * sin).astype(BF16)
    for h in range(N_HEADS):
        q_ref[:, h * 256:h * 256 + 128] = q[:, h * 256:h * 256 + 128].astype(BF16)
        z = q[:, h * 256 + 128:(h + 1) * 256] * cs
        q_ref[:, h * 256 + 128:(h + 1) * 256] = (z + pltpu.roll(z, MLA_ROPE, axis=1)).astype(BF16)
        k_ref[:, h * 256:h * 256 + 128] = kv[:, h * 256:h * 256 + 128].astype(BF16)
        k_ref[:, h * 256 + 128:(h + 1) * 256] = k_rope
        v_ref[:, h * 128:(h + 1) * 128] = kv[:, h * 256 + 128:(h + 1) * 256].astype(BF16)


def _mla_prep(proj, rope_tabs, g_q, g_kv, wq, wkv):
    b, s, _ = proj.shape
    tm = min(512, s)
    const = lambda bi, i: (0, 0)
    return pl.pallas_call(
        _mla_prep_kernel,
        out_shape=(jax.ShapeDtypeStruct((b, s, N_HEADS * 256), BF16),
                   jax.ShapeDtypeStruct((b, s, N_HEADS * 256), BF16),
                   jax.ShapeDtypeStruct((b, s, BRANCH_WIDTH), BF16)),
        grid=(b, s // tm),
        in_specs=[pl.BlockSpec((None, tm, 1024), lambda bi, i: (bi, i, OFF_A // 1024)),
                  pl.BlockSpec((tm, 128), lambda bi, i: (i, 0)),
                  pl.BlockSpec((tm, 128), lambda bi, i: (i, 0)),
                  pl.BlockSpec((tm, 128), lambda bi, i: (i, 0)),
                  pl.BlockSpec((1, MLA_Q_RANK), const),
                  pl.BlockSpec((1, MLA_KV_RANK), const),
                  pl.BlockSpec(wq.shape, const),
                  pl.BlockSpec(wkv.shape, const)],
        out_specs=(pl.BlockSpec((None, tm, N_HEADS * 256), lambda bi, i: (bi, i, 0)),
                   pl.BlockSpec((None, tm, N_HEADS * 256), lambda bi, i: (bi, i, 0)),
                   pl.BlockSpec((None, tm, BRANCH_WIDTH), lambda bi, i: (bi, i, 0))),
        compiler_params=_cparams("arbitrary", "arbitrary"),
        name="mla_prep",
    )(proj, *rope_tabs, g_q.reshape(1, -1), g_kv.reshape(1, -1), wq, wkv)


def _bucket_np(n):
    max_exact = REL_BUCKETS // 2
    nf = np.maximum(n, max_exact).astype(np.float32)
    large = max_exact + (np.log(nf / np.float32(max_exact)) / np.float32(math.log(REL_MAX_DIST / max_exact))
                         * np.float32(REL_BUCKETS - max_exact)).astype(np.int32)
    return np.where(n < max_exact, n, np.minimum(large, REL_BUCKETS - 1)).astype(np.int32)


def _bucket_starts():
    buckets = _bucket_np(np.arange(2 * REL_MAX_DIST))
    return [int(np.argmax(buckets >= b)) for b in range(REL_BUCKETS)]


def _bank_kernel(tab_ref, o_ref, *, t, d_min, band):
    h = pl.program_id(0)
    starts = _bucket_starts()
    row = lax.broadcasted_iota(jnp.int32, (t, t), 0)
    col = lax.broadcasted_iota(jnp.int32, (t, t), 1)
    for j in range(o_ref.shape[0]):
        d = d_min + j
        if d < 0:
            o_ref[j] = jnp.full((t, t), NEG, F32)
            continue
        dist = t * d + row - col
        b_lo, b_hi = (int(x) for x in _bucket_np(np.array([max(t * d - t + 1, 0), t * d + t - 1])))
        val = jnp.full((t, t), tab_ref[b_lo, h], F32)
        for b in range(b_lo + 1, b_hi + 1):
            val = jnp.where(dist >= starts[b], tab_ref[b, h], val)
        if band:
            mult = jnp.zeros((t, t), jnp.int32)
            for window, dil in DILATED_PATTERNS:
                mult += jnp.where((dist >= 0) & (dist <= window) & ((dist & (dil - 1)) == 0), 1, 0)
            val = val + jnp.where(mult == 3, math.log(3.0), jnp.where(mult == 2, math.log(2.0), 0.0))
            keep = mult > 0
        else:
            val = val - tab_ref[REL_BUCKETS - 1, h]
            keep = dist >= 0
        o_ref[j] = jnp.where(keep, val * LOG2E, NEG)


def _bank(tab, n_tables, t, d_min, band):
    nh = tab.shape[1]
    return pl.pallas_call(
        functools.partial(_bank_kernel, t=t, d_min=d_min, band=band),
        out_shape=jax.ShapeDtypeStruct((nh, n_tables, t, t), F32),
        grid=(nh,),
        in_specs=[pl.BlockSpec(memory_space=pltpu.SMEM)],
        out_specs=pl.BlockSpec((None, n_tables, t, t), lambda h: (h, 0, 0, 0)),
        compiler_params=_cparams("arbitrary"),
        name="bank_band" if band else "bank_bias",
    )(tab)


def _first_far_diagonal(t):
    last = _bucket_starts()[REL_BUCKETS - 1]
    return -(-(last + t - 1) // t)


def _qk(q, k):
    return lax.dot_general(q, k, (((1,), (1,)), ((), ())), preferred_element_type=F32)


def _flash_init(m_ref, acc_ref):
    m_ref[...] = jnp.full(m_ref.shape, NEG, F32)
    acc_ref[...] = jnp.zeros(acc_ref.shape, F32)


def _flash_update(slot, s, v, m_ref, acc_ref, s_ref, next_scores):
    m_prev = m_ref[slot]
    m_new = jnp.maximum(m_prev, jnp.max(s, axis=-1, keepdims=True))
    alpha = jnp.exp2(m_prev - m_new)
    p = jnp.concatenate([jnp.exp2(s[:, j * 128:(j + 1) * 128] - m_new) for j in range(s.shape[1] // 128)],
                        axis=1).astype(BF16)
    if next_scores is not None:
        s_ref[slot] = next_scores()
    v_ones = jnp.concatenate([v, jnp.ones_like(v)], axis=1)
    acc_ref[slot] = (jnp.concatenate([alpha, alpha], axis=1) * acc_ref[slot]
                     + jnp.dot(p, v_ones, preferred_element_type=F32))
    m_ref[slot] = m_new


def _flash_result(slot, acc_ref):
    acc = acc_ref[slot]
    return acc[:, :HEAD_DIM] / acc[:, HEAD_DIM:]


def _silu(g):
    return g * jax.nn.sigmoid(g)


def _tile(ref, kj, t, c0, width):
    start = kj * t if isinstance(kj, int) else pl.multiple_of(kj * t, t)
    return ref[pl.ds(start, t), c0:c0 + width]


def _bank_tile(bank_ref, h, i, kj):
    base = 2 * (i - kj) + 1
    top = jnp.concatenate([bank_ref[h, base], bank_ref[h, base - 1]], axis=1)
    bot = jnp.concatenate([bank_ref[h, base + 1], bank_ref[h, base]], axis=1)
    return jnp.concatenate([top, bot], axis=0)


def _emit(o_ref, g_ref, h, o):
    gate = g_ref[:, h * 128:(h + 1) * 128].astype(F32)
    o_ref[:, h * 128:(h + 1) * 128] = (o * _silu(gate)).astype(BF16)


def _flash_walk(i, first_key, n_far, slots, score, logits, value, s_ref, m_ref, acc_ref, prep=None):
    lo = first_key(i)
    i_next = jnp.minimum(i + 1, pl.num_programs(1) - 1)
    _flash_init(m_ref, acc_ref)

    @pl.when(i == 0)
    def _():
        for slot in range(slots):
            s_ref[slot] = score(slot, False, lo)

    def step(kj, phase):
        ctx = prep(kj) if prep is not None else None
        for slot in range(slots):
            s = logits(slot, s_ref[slot], kj, phase, ctx)
            if phase == "last":
                nxt = functools.partial(score, slot, True, first_key(i_next))
            else:
                nxt = functools.partial(score, slot, False, kj + 1)
            _flash_update(slot, s, value(slot, kj), m_ref, acc_ref, s_ref, nxt)

    def walk(a, b, phase):
        def body(kj, carry):
            step(kj, phase)
            return carry
        lax.fori_loop(a, b, body, 0)

    if n_far is not None:
        walk(lo, n_far, "far")
        lo = n_far
    walk(lo, i, "near")
    step(i, "last")


def _attn_a_kernel(q_ref, qn_ref, k_ref, v_ref, g_ref, o_ref, s_ref, m_ref, acc_ref, *, t):
    i = pl.program_id(1)

    def score(h, next_q, kj):
        q = (qn_ref if next_q else q_ref)[:, h * 256:(h + 1) * 256]
        return _qk(q, _tile(k_ref, kj, t, h * 256, 256))

    def logits(h, s, kj, phase, ctx):
        if phase != "last":
            return s
        causal = lax.broadcasted_iota(jnp.int32, (t, t), 1) <= lax.broadcasted_iota(jnp.int32, (t, t), 0)
        return jnp.where(causal, s, NEG)

    def value(h, kj):
        return _tile(v_ref, kj, t, h * 128, 128)

    _flash_walk(i, lambda qi: 0, None, N_HEADS, score, logits, value, s_ref, m_ref, acc_ref)
    for h in range(N_HEADS):
        _emit(o_ref, g_ref, h, _flash_result(h, acc_ref))


def _attn_band_kernel(q_ref, qn_ref, k_ref, v_ref, g_ref, bank_ref, o_ref, s_ref, m_ref, acc_ref, *, t, near):
    i = pl.program_id(1)

    def score(h, next_q, kj):
        q = (qn_ref if next_q else q_ref)[:, h * 128:(h + 1) * 128]
        return _qk(q, _tile(k_ref, kj, t, h * 128, 128))

    def logits(h, s, kj, phase, ctx):
        return s + _bank_tile(bank_ref, h, i, kj)

    def value(h, kj):
        return _tile(v_ref, kj, t, h * 128, 128)

    _flash_walk(i, lambda qi: jnp.maximum(qi - (near - 1), 0), None, N_HEADS, score, logits, value,
                s_ref, m_ref, acc_ref)
    for h in range(N_HEADS):
        _emit(o_ref, g_ref, h, _flash_result(h, acc_ref))


def _attn_sel_kernel(q_ref, qn_ref, k_ref, v_ref, g_ref, bank_ref, sel_ref, o_ref,
                     s_ref, m_ref, acc_ref, *, t, near):
    i = pl.program_id(1)

    def score(h, next_q, kj):
        q = (qn_ref if next_q else q_ref)[:, h * 128:(h + 1) * 128]
        return _qk(q, _tile(k_ref, kj, t, h * 128, 128))

    def prep(kj):
        return sel_ref[:, pl.ds(pl.multiple_of(kj * t, t), t)].astype(F32)

    def logits(h, s, kj, phase, sel):
        return s + sel if phase == "far" else s + (sel + _bank_tile(bank_ref, h, i, kj))

    def value(h, kj):
        return _tile(v_ref, kj, t, h * 128, 128)

    _flash_walk(i, lambda qi: 0, jnp.maximum(i - (near - 1), 0), N_HEADS, score, logits, value,
                s_ref, m_ref, acc_ref, prep)
    for h in range(N_HEADS):
        _emit(o_ref, g_ref, h, _flash_result(h, acc_ref))


def _attn_diff_kernel(q_ref, qn_ref, k_ref, v_ref, g_ref, bank_ref, lam_ref, gsub_ref, o_ref,
                      s_ref, m_ref, acc_ref, *, t, near, lam_init):
    i = pl.program_id(1)
    first_half = lax.broadcasted_iota(jnp.int32, (t, HEAD_DIM), 1) < HEAD_DIM // 2

    def score(slot, next_q, kj):
        h = slot // 2
        q = (qn_ref if next_q else q_ref)[:, h * 128:(h + 1) * 128]
        keep = first_half if slot % 2 == 0 else jnp.logical_not(first_half)
        q = jnp.where(keep, q, jnp.zeros_like(q))
        return _qk(q, _tile(k_ref, kj, t, h * 128, 128))

    def logits(slot, s, kj, phase, ctx):
        return s if phase == "far" else s + _bank_tile(bank_ref, slot // 2, i, kj)

    def value(slot, kj):
        return _tile(v_ref, kj, t, (slot // 2) * 128, 128)

    _flash_walk(i, lambda qi: 0, jnp.maximum(i - (near - 1), 0), 2 * N_HEADS, score, logits, value,
                s_ref, m_ref, acc_ref)
    lam_v = lam_ref[...]
    lam = (jnp.exp(jnp.sum(lam_v[0:1] * lam_v[1:2], axis=-1, keepdims=True))
           - jnp.exp(jnp.sum(lam_v[2:3] * lam_v[3:4], axis=-1, keepdims=True)) + lam_init)
    for h in range(N_HEADS):
        o = _flash_result(2 * h, acc_ref) - lam * _flash_result(2 * h + 1, acc_ref)
        _emit(o_ref, g_ref, h, _rms(o, gsub_ref[...]) * (1.0 - lam_init))


def _attention(kind, q_arr, k_arr, v_arr, proj, q_blk, gate_blk, extra_in=(), extra_specs=(), **kw):
    b, s, _ = proj.shape
    t = min(ATT_TILE, s)
    dk = 256 if kind == "a" else 128
    qw = N_HEADS * dk
    k_blk = 0 if kind == "a" else q_blk + 1
    v_blk = 0 if kind == "a" else q_blk + 2
    body = {"a": _attn_a_kernel, "band": _attn_band_kernel, "sel": _attn_sel_kernel,
            "diff": _attn_diff_kernel}[kind]
    slots = 2 * N_HEADS if kind == "diff" else N_HEADS
    scratch = [pltpu.VMEM((slots, t, t), F32), pltpu.VMEM((slots, t, HEAD_DIM), F32),
               pltpu.VMEM((slots, t, 2 * HEAD_DIM), F32)]
    last = s // t - 1
    in_specs = [pl.BlockSpec((None, t, qw), lambda bi, i: (bi, i, q_blk)),
                pl.BlockSpec((None, t, qw), lambda bi, i: (bi, jnp.minimum(i + 1, last), q_blk)),
                pl.BlockSpec((None, s, qw), lambda bi, i: (bi, 0, k_blk)),
                pl.BlockSpec((None, s, BRANCH_WIDTH), lambda bi, i: (bi, 0, v_blk)),
                pl.BlockSpec((None, t, BRANCH_WIDTH), lambda bi, i: (bi, i, gate_blk))]
    in_specs += list(extra_specs)
    return pl.pallas_call(
        functools.partial(body, t=t, **kw),
        out_shape=jax.ShapeDtypeStruct((b, s, BRANCH_WIDTH), BF16),
        grid=(b, s // t),
        in_specs=in_specs,
        out_specs=pl.BlockSpec((None, t, BRANCH_WIDTH), lambda bi, i: (bi, i, 0)),
        scratch_shapes=scratch,
        compiler_params=_cparams("arbitrary", "arbitrary"),
        name="attn_" + kind,
    )(q_arr, q_arr, k_arr, v_arr, proj, *extra_in)


def _bit_transpose32(words):
    a = list(words)
    j, m = 16, 0x0000FFFF
    while j:
        for k in range(32):
            if not k & j:
                t = (a[k] ^ lax.shift_right_logical(a[k + j], jnp.int32(j))) & jnp.int32(m)
                a[k] = a[k] ^ t
                a[k + j] = a[k + j] ^ (t << j)
        j >>= 1
        m = (m ^ (m << j)) & 0xFFFFFFFF if j else m
    return a


def _select_kernel(qi_ref, ki_ref, wi_ref, o_ref, key_ref, plane_ref, alive_ref, *, tq, kc, n_sel):
    i = pl.program_id(1)
    s_len = o_ref.shape[1]
    n_ch = (i * tq + tq + kc - 1) // kc
    wpc = kc // 32
    int_min = jnp.int32(-2 ** 31)
    lane = lax.broadcasted_iota(jnp.int32, (tq, 128), 1)
    w_t = wi_ref[...].astype(F32).T
    q_heads = []
    for j in range(IDX_HEADS // 2):
        q2 = qi_ref[:, j * 128:(j + 1) * 128]
        q_heads.append(jnp.where(lane < IDX_DIM, q2, jnp.zeros_like(q2)))
        q_heads.append(jnp.where(lane >= IDX_DIM, q2, jnp.zeros_like(q2)))
    kiota = lax.broadcasted_iota(jnp.int32, (kc, tq), 0)

    def chunk(c):
        return pl.ds(pl.multiple_of(c * kc, kc), kc)

    def scored(k, q_from):
        acc = jnp.zeros((k.shape[0], tq - q_from), F32)
        for hh in range(IDX_HEADS):
            acc = acc + jnp.maximum(_qk(k, q_heads[hh][q_from:, :]), 0.0) * w_t[hh:hh + 1, q_from:]
        return acc + 0.0

    def causal(acc):
        key = lax.broadcasted_iota(jnp.int32, acc.shape, 0)
        return jnp.where(key <= lax.broadcasted_iota(jnp.int32, acc.shape, 1), acc, NEG)

    def store_keys(row0, acc):
        bits = pltpu.bitcast(acc, jnp.int32)
        keys = bits ^ ((bits >> 31) & jnp.int32(0x7FFFFFFF))
        key_ref[pl.ds(row0, acc.shape[0]), :] = keys
        ukeys = keys ^ int_min
        for blk in range(acc.shape[0] // 256):
            planes = _bit_transpose32([ukeys[blk * 256 + 8 * j:blk * 256 + 8 * j + 8, :] for j in range(32)])
            row = pl.multiple_of(row0 // 32 + blk * 8, 8)
            for b in range(32):
                plane_ref[b, pl.ds(row, 8), :] = planes[b]

    def score_chunk(c, _):
        store_keys(pl.multiple_of(c * kc, kc), scored(ki_ref[chunk(c), :], 0))
        return 0

    lax.fori_loop(0, i, score_chunk, 0)
    diag = pl.multiple_of(i * kc, kc)
    half = kc // 2
    if half % 256 == 0:
        store_keys(diag, causal(scored(ki_ref[pl.ds(diag, half), :], 0)))
        late = causal(scored(ki_ref[pl.ds(diag + half, half), :], half))
        store_keys(diag + half, jnp.concatenate([jnp.full((half, half), NEG, F32), late], axis=1))
    else:
        store_keys(diag, causal(scored(ki_ref[pl.ds(diag, kc), :], 0)))

    def clear_chunk(c, _):
        plane_ref[:, pl.ds(pl.multiple_of(c * wpc, wpc), wpc), :] = jnp.zeros((32, wpc, tq), jnp.int32)
        return 0

    lax.fori_loop(n_ch, s_len // kc, clear_chunk, 0)
    n_rows = s_len // 32
    word_row = lax.broadcasted_iota(jnp.int32, (n_rows, tq), 0)
    alive_ref[...] = jnp.where(word_row < n_ch * wpc, jnp.int32(-1), jnp.int32(0))

    def radix_step(bi, carry):
        thr, above = carry
        alive = alive_ref[...]
        plane = plane_ref[bi]
        ones = lax.population_count(alive & plane)
        ones = jnp.sum(jnp.sum(ones.reshape(n_rows // 8, 8, tq), axis=0), axis=0, keepdims=True)
        take = above + ones >= n_sel
        thr = jnp.where(take, thr | (jnp.int32(1) << (31 - bi)), thr)
        above = jnp.where(take, above, above + ones)
        alive_ref[...] = alive & (plane ^ jnp.where(take, jnp.int32(0), jnp.int32(-1)))
        return thr, above

    zeros = jnp.zeros((1, tq), jnp.int32)
    thr_u, above = lax.fori_loop(0, 32, radix_step, (zeros, zeros))
    thr = thr_u ^ int_min
    n_equal = lax.population_count(alive_ref[...])
    n_equal = jnp.sum(jnp.sum(n_equal.reshape(n_rows // 8, 8, tq), axis=0), axis=0, keepdims=True)
    need = n_sel - above
    masked_key = int(np.float32(NEG).view(np.int32)) ^ 0x7FFFFFFF
    tie = (n_equal > need) & (thr != masked_key)
    any_tie = jnp.max(jnp.where(tie, 1, 0)) > 0

    @pl.when(jnp.logical_not(any_tie))
    def _():
        def emit(c, _):
            keep = jnp.where(key_ref[chunk(c), :] >= thr, 0.0, NEG)
            o_ref[:, chunk(c)] = keep.T.astype(BF16)
            return 0

        lax.fori_loop(0, n_ch, emit, 0)

    @pl.when(any_tie)
    def _():
        def equal_below(bound):
            def body(c, part):
                hit = jnp.where((key_ref[chunk(c), :] == thr) & (c * kc + kiota < bound), 1, 0)
                return part + jnp.sum(hit.reshape(kc // 8, 8, tq), axis=0)
            part = lax.fori_loop(0, n_ch, body, jnp.zeros((8, tq), jnp.int32))
            return jnp.sum(part, axis=0, keepdims=True)

        n_bits = s_len.bit_length()

        def bound_step(bi, cut):
            cand = cut + (jnp.int32(1) << (n_bits - 1 - bi))
            ok = (cand <= s_len) & (equal_below(cand) <= need)
            return jnp.where(ok, cand, cut)

        cut = lax.fori_loop(0, n_bits, bound_step, zeros)

        def emit(c, _):
            keys = key_ref[chunk(c), :]
            kept = (keys > thr) | ((keys == thr) & (c * kc + kiota < cut))
            o_ref[:, chunk(c)] = jnp.where(kept, 0.0, NEG).T.astype(BF16)
            return 0

        lax.fori_loop(0, n_ch, emit, 0)

    def blank(c, _):
        o_ref[:, chunk(c)] = jnp.full((tq, kc), NEG, BF16)
        return 0

    lax.fori_loop(n_ch, s_len // kc, blank, 0)


def _select(proj):
    b, s, _ = proj.shape
    tq = kc = min(512, s)
    n_sel = min(TOPK_MAX, s // 4)
    return pl.pallas_call(
        functools.partial(_select_kernel, tq=tq, kc=kc, n_sel=n_sel),
        out_shape=jax.ShapeDtypeStruct((b, s, s), BF16),
        grid=(b, s // tq),
        in_specs=[pl.BlockSpec((None, tq, 1024), lambda bi, i: (bi, i, OFF_QIDX // 1024)),
                  pl.BlockSpec((None, s, 128), lambda bi, i: (bi, 0, OFF_KIDX // 128)),
                  pl.BlockSpec((None, tq, 128), lambda bi, i: (bi, i, OFF_WIDX // 128))],
        out_specs=pl.BlockSpec((None, tq, s), lambda bi, i: (bi, i, 0)),
        scratch_shapes=[pltpu.VMEM((s, tq), jnp.int32), pltpu.VMEM((32, s // 32, tq), jnp.int32),
                        pltpu.VMEM((s // 32, tq), jnp.int32)],
        compiler_params=_cparams("arbitrary", "arbitrary"),
        name="idx_select",
    )(proj, proj, proj)


def _out_kernel(a_ref, b_ref, c_ref, d_ref, w_ref, x_ref, gate_ref, g_ref, o_ref, wb_ref):
    @pl.when((pl.program_id(0) == 0) & (pl.program_id(1) == 0))
    def _():
        wb_ref[...] = w_ref[...].astype(BF16)

    y = jnp.dot(a_ref[...], wb_ref[0:512, :], preferred_element_type=F32)
    y += jnp.dot(b_ref[...], wb_ref[512:1024, :], preferred_element_type=F32)
    y += jnp.dot(c_ref[...], wb_ref[1024:1536, :], preferred_element_type=F32)
    y += jnp.dot(d_ref[...], wb_ref[1536:2048, :], preferred_element_type=F32)
    o_ref[...] = x_ref[...] + gate_ref[...] * _rms(y, g_ref[...])


def _out_proj(outs, w_out, li, x, mod3, g_post):
    b, s, d = x.shape
    tm = min(512, s)
    mix = lambda bi, i: (bi, i, 0)
    return pl.pallas_call(
        _out_kernel,
        out_shape=jax.ShapeDtypeStruct((b, s, d), F32),
        grid=(b, s // tm),
        in_specs=[pl.BlockSpec((None, tm, BRANCH_WIDTH), mix)] * 4
        + [pl.BlockSpec((None,) + w_out.shape[1:], lambda bi, i: (li, 0, 0), pipeline_mode=pl.Buffered(1)),
           pl.BlockSpec((None, tm, d), mix),
           pl.BlockSpec((None, 1, d), lambda bi, i: (bi, 0, 2)),
           pl.BlockSpec((1, d), lambda bi, i: (0, 0))],
        out_specs=pl.BlockSpec((None, tm, d), mix),
        scratch_shapes=[pltpu.VMEM(w_out.shape[1:], BF16)],
        compiler_params=_cparams("arbitrary", "arbitrary"),
        name="out_proj",
    )(*outs, w_out, x, mod3, g_post.reshape(1, d))


def _rope_tables(s):
    half = MLA_ROPE // 2
    inv = ROPE_THETA ** (-jnp.arange(half, dtype=F32) / half)
    ang = jnp.arange(s, dtype=F32)[:, None] * inv[None, :]
    z = jnp.zeros((s, 128 - MLA_ROPE), F32)
    cos, sin = jnp.cos(ang), jnp.sin(ang)
    cat = lambda *parts: jnp.concatenate(parts, axis=-1)
    return cat(cos, cos, z), cat(sin, sin, z), cat(cos, cos, sin, sin)


def _rot_cols(w):
    half = w.shape[-1] // 2
    return jnp.concatenate([-w[..., half:], w[..., :half]], axis=-1)


IN_SPLITS = (("a_cq", 384), ("a_ckv", 256), ("a_krope", 64), ("b_q", 512), ("b_k", 512), ("b_v", 512),
             ("c_q", 512), ("c_k", 512), ("c_v", 512), ("c_qidx", 1024), ("c_kidx", 64), ("c_widx", 16),
             ("d_q", 512), ("d_k", 512), ("d_v", 512), ("gate", 2048))
IN_WIDTH = sum(width for _, width in IN_SPLITS)


def _layout_w_in_kernel(w_ref, o_ref):
    src, start = {}, 0
    for name, width in IN_SPLITS:
        src[name] = start
        start += width
    tk = w_ref.shape[1]

    def rows(name, width, offset=0):
        a = src[name] + offset
        return w_ref[a:a + width, :]

    def put(dst, val, scale=None):
        for r in range(0, val.shape[0], 512):
            piece = val[r:r + 512]
            if scale is not None:
                piece = piece * scale
            o_ref[:, dst + r:dst + r + piece.shape[0]] = piece.T.astype(BF16)

    z64 = jnp.zeros((64, tk), F32)
    half = MLA_ROPE // 2
    put(OFF_QIDX, rows("c_qidx", 1024), IDX_DIM ** -0.5)
    put(OFF_A, rows("a_cq", MLA_Q_RANK + MLA_KV_RANK))
    put(OFF_A + 640, jnp.concatenate(
        [rows("a_krope", MLA_ROPE), z64, -rows("a_krope", half, half), rows("a_krope", half), z64,
         rows("c_kidx", IDX_DIM), rows("c_kidx", IDX_DIM)], axis=0))
    for off, name, dim in ((OFF_B, "b", HEAD_DIM), (OFF_C, "c", HEAD_DIM), (OFF_D, "d", HEAD_DIM // 2)):
        put(off, rows(name + "_q", BRANCH_WIDTH), LOG2E * dim ** -0.5)
        put(off + BRANCH_WIDTH, rows(name + "_k", 2 * BRANCH_WIDTH))
    put(OFF_GATE, rows("gate", MIX_WIDTH))
    put(OFF_WIDX, jnp.concatenate([rows("c_widx", IDX_HEADS) * IDX_HEADS ** -0.5,
                                   jnp.zeros((128 - IDX_HEADS, tk), F32)], axis=0))


def _layout_w_in(w_in_t, li):
    d = w_in_t.shape[2]
    tk = 256
    return pl.pallas_call(
        _layout_w_in_kernel,
        out_shape=jax.ShapeDtypeStruct((d, PROJ_WIDTH), BF16),
        grid=(d // tk,),
        in_specs=[pl.BlockSpec((None, IN_WIDTH, tk), lambda i: (li, 0, i))],
        out_specs=pl.BlockSpec((tk, PROJ_WIDTH), lambda i: (i, 0)),
        compiler_params=_cparams("arbitrary"),
        name="layout_w_in",
    )(w_in_t)


def _layout_w_uq(w):
    r = w.shape[0]
    w = w.reshape(r, N_HEADS, MLA_NOPE + MLA_ROPE) * (LOG2E * (MLA_NOPE + MLA_ROPE) ** -0.5)
    rope = w[..., MLA_NOPE:]
    return jnp.concatenate([w[..., :MLA_NOPE], rope, _rot_cols(rope)], axis=-1).reshape(r, -1).astype(BF16)


def kernel(x, c, w_ada, b_ada, g_pre, g_post, w_in, g_q_a, w_uq_a, g_kv_a, w_ukv_a,
           lam_q1, lam_k1, lam_q2, lam_k2, g_sub_d, w_out, rel_bias):
    b, s, d = x.shape
    depth = w_ada.shape[0]
    t = min(ATT_TILE, s)
    half = t // 2
    nq = s // t
    near_bias = min(nq, -(-(_first_far_diagonal(half) + 1) // 2))
    near_band = min(nq, -(-(DILATED_PATTERNS[-1][0] // half + 1) // 2))

    rope_tabs = _rope_tables(s)
    bank_b = _bank(rel_bias[:, 0:N_HEADS], 2 * near_band + 1, half, -1, True)
    bank_cd = _bank(rel_bias[:, N_HEADS:3 * N_HEADS], 2 * near_bias + 1, half, -1, False)
    bank_spec = lambda n, group=0: pl.BlockSpec((N_HEADS, n, half, half), lambda bi, i: (group, 0, 0, 0),
                                                pipeline_mode=pl.Buffered(1))

    w_in_t = jnp.swapaxes(w_in, 1, 2)
    mod = _ada_mod(c, w_ada, b_ada)
    for li in range(depth):
        mod3 = mod[li].reshape(b, 1, 3 * d)
        proj = _in_proj(x, g_pre[li], mod3, _layout_w_in(w_in_t, li))

        q_a, k_a, v_a = _mla_prep(proj, rope_tabs, g_q_a[li], g_kv_a[li],
                                  _layout_w_uq(w_uq_a[li]), w_ukv_a[li].astype(BF16))
        gate0 = OFF_GATE // BRANCH_WIDTH
        out_a = _attention("a", q_a, k_a, v_a, proj, 0, gate0)
        out_b = _attention("band", proj, proj, proj, proj, OFF_B // BRANCH_WIDTH, gate0 + 1,
                           extra_in=(bank_b,), extra_specs=(bank_spec(2 * near_band + 1),), near=near_band)
        sel = _select(proj)
        out_c = _attention("sel", proj, proj, proj, proj, OFF_C // BRANCH_WIDTH, gate0 + 2,
                           extra_in=(bank_cd, sel),
                           extra_specs=(bank_spec(2 * near_bias + 1, 0),
                                        pl.BlockSpec((None, t, s), lambda bi, i: (bi, i, 0))),
                           near=near_bias)
        lam_init = 0.8 - 0.6 * math.exp(-0.3 * li)
        lam_vecs = jnp.stack([lam_q1[li], lam_k1[li], lam_q2[li], lam_k2[li]])
        out_d = _attention("diff", proj, proj, proj, proj, OFF_D // BRANCH_WIDTH, gate0 + 3,
                           extra_in=(bank_cd, lam_vecs, g_sub_d[li].reshape(1, HEAD_DIM)),
                           extra_specs=(bank_spec(2 * near_bias + 1, 1),
                                        pl.BlockSpec(lam_vecs.shape, lambda bi, i: (0, 0)),
                                        pl.BlockSpec((1, HEAD_DIM), lambda bi, i: (0, 0))),
                           near=near_bias, lam_init=lam_init)
        x = _out_proj((out_a, out_b, out_c, out_d), w_out, li, x, mod3, g_post[li])
    return x
```

```python
import functools
import math

import numpy as np
import jax
import jax.numpy as jnp
from jax import lax
from jax.experimental import pallas as pl
from jax.experimental.pallas import tpu as pltpu

F32 = jnp.float32
BF16 = jnp.bfloat16

HEAD_DIM = 128
N_HEADS = 4
BRANCH_WIDTH = N_HEADS * HEAD_DIM
MIX_WIDTH = 4 * BRANCH_WIDTH
MLA_Q_RANK = 384
MLA_KV_RANK = 256
MLA_NOPE = 128
MLA_ROPE = 64
ROPE_THETA = 10000.0
DILATED_PATTERNS = ((128, 1), (512, 4), (2048, 16))
IDX_HEADS = 16
IDX_DIM = 64
TOPK_MAX = 256
REL_BUCKETS = 32
REL_MAX_DIST = 2048
NORM_EPS = 1e-6
NEG = -1e30
LOG2E = math.log2(math.e)

OFF_QIDX = 0
OFF_A = 1024
OFF_KIDX = OFF_A + 896
OFF_B = 2048
OFF_C = 3584
OFF_D = 5120
OFF_GATE = 6656
OFF_WIDX = 8704
PROJ_WIDTH = 8832

ATT_TILE = 512
V7X_VMEM_BYTES = 64 * 1024 * 1024
VMEM_LIMIT = V7X_VMEM_BYTES // 8 * 7


def _cparams(*sem):
    return pltpu.CompilerParams(dimension_semantics=sem, vmem_limit_bytes=VMEM_LIMIT)


def _split_bf16(x):
    hi = x.astype(BF16)
    return hi, (x - hi.astype(F32)).astype(BF16)


def _ada_kernel(c_ref, w_ref, b_ref, o_ref):
    c = c_ref[...]
    a_hi, a_lo = _split_bf16(c * jax.nn.sigmoid(c))
    w_hi, w_lo = _split_bf16(w_ref[...])
    dot = functools.partial(jnp.dot, preferred_element_type=F32)
    o_ref[...] = dot(a_hi, w_hi) + (dot(a_hi, w_lo) + dot(a_lo, w_hi)) + b_ref[...]


def _ada_mod(c, w_ada, b_ada):
    depth, d, n = w_ada.shape
    b = c.shape[0]
    tn = 768
    return pl.pallas_call(
        _ada_kernel,
        out_shape=jax.ShapeDtypeStruct((depth, b, n), F32),
        grid=(depth, n // tn),
        in_specs=[pl.BlockSpec((b, d), lambda l, j: (0, 0)),
                  pl.BlockSpec((None, d, tn), lambda l, j: (l, 0, j)),
                  pl.BlockSpec((None, 1, tn), lambda l, j: (l, 0, j))],
        out_specs=pl.BlockSpec((None, b, tn), lambda l, j: (l, 0, j)),
        compiler_params=_cparams("arbitrary", "arbitrary"),
        name="ada_mod",
    )(c, w_ada, b_ada.reshape(depth, 1, n))


def _in_proj_kernel(x_ref, g_ref, shift_ref, scale_ref, w_ref, o_ref):
    x = x_ref[...]
    y = x * lax.rsqrt(jnp.mean(x * x, axis=-1, keepdims=True) + NORM_EPS) * g_ref[...]
    h = (y * (1.0 + scale_ref[...]) + shift_ref[...]).astype(BF16)
    o_ref[...] = jnp.dot(h, w_ref[...], preferred_element_type=F32).astype(o_ref.dtype)


def _in_proj(x, g_pre, mod3, w):
    b, s, d = x.shape
    n = w.shape[1]
    tm = min(512, s)
    nt = s // tm
    tn = n // 3
    row = lambda j, i: (i // nt, i % nt, 0)
    return pl.pallas_call(
        _in_proj_kernel,
        out_shape=jax.ShapeDtypeStruct((b, s, n), BF16),
        grid=(n // tn, b * nt),
        in_specs=[pl.BlockSpec((None, tm, d), row),
                  pl.BlockSpec((1, d), lambda j, i: (0, 0)),
                  pl.BlockSpec((None, 1, d), lambda j, i: (i // nt, 0, 0)),
                  pl.BlockSpec((None, 1, d), lambda j, i: (i // nt, 0, 1)),
                  pl.BlockSpec((d, tn), lambda j, i: (0, j), pipeline_mode=pl.Buffered(1))],
        out_specs=pl.BlockSpec((None, tm, tn), lambda j, i: (i // nt, i % nt, j)),
        compiler_params=_cparams("arbitrary", "arbitrary"),
        name="in_proj",
    )(x, g_pre.reshape(1, d), mod3, mod3, w)


def _rms(x, g):
    return x * lax.rsqrt(jnp.mean(x * x, axis=-1, keepdims=True) + NORM_EPS) * g


def _mla_prep_kernel(p_ref, cos_ref, sin_ref, cs_ref, gq_ref, gkv_ref, wq_ref, wkv_ref, q_ref, k_ref, v_ref):
    cos = cos_ref[...]
    sin = sin_ref[...]
    cs = cs_ref[...]
    cq = _rms(p_ref[:, 0:MLA_Q_RANK].astype(F32), gq_ref[...]).astype(BF16)
    ckv = _rms(p_ref[:, MLA_Q_RANK:MLA_Q_RANK + MLA_KV_RANK].astype(F32), gkv_ref[...]).astype(BF16)
    q = jnp.dot(cq, wq_ref[...], preferred_element_type=F32)
    kv = jnp.dot(ckv, wkv_ref[...], preferred_element_type=F32)
    k_rope = (p_ref[:, 640:768].astype(F32) * cos + p_ref[:, 768:896].astype(F32) * sin).astype(BF16)
    for h in range(N_HEADS):
        q_ref[:, h * 256:h * 256 + 128] = q[:, h * 256:h * 256 + 128].astype(BF16)
        z = q[:, h * 256 + 128:(h + 1) * 256] * cs
        q_ref[:, h * 256 + 128:(h + 1) * 256] = (z + pltpu.roll(z, MLA_ROPE, axis=1)).astype(BF16)
        k_ref[:, h * 256:h * 256 + 128] = kv[:, h * 256:h * 256 + 128].astype(BF16)
        k_ref[:, h * 256 + 128:(h + 1) * 256] = k_rope
        v_ref[:, h * 128:(h + 1) * 128] = kv[:, h * 256 + 128:(h + 1) * 256].astype(BF16)


def _mla_prep(proj, rope_tabs, g_q, g_kv, wq, wkv):
    b, s, _ = proj.shape
    tm = min(512, s)
    const = lambda bi, i: (0, 0)
    return pl.pallas_call(
        _mla_prep_kernel,
        out_shape=(jax.ShapeDtypeStruct((b, s, N_HEADS * 256), BF16),
                   jax.ShapeDtypeStruct((b, s, N_HEADS * 256), BF16),
                   jax.ShapeDtypeStruct((b, s, BRANCH_WIDTH), BF16)),
        grid=(b, s // tm),
        in_specs=[pl.BlockSpec((None, tm, 1024), lambda bi, i: (bi, i, OFF_A // 1024)),
                  pl.BlockSpec((tm, 128), lambda bi, i: (i, 0)),
                  pl.BlockSpec((tm, 128), lambda bi, i: (i, 0)),
                  pl.BlockSpec((tm, 128), lambda bi, i: (i, 0)),
                  pl.BlockSpec((1, MLA_Q_RANK), const),
                  pl.BlockSpec((1, MLA_KV_RANK), const),
                  pl.BlockSpec(wq.shape, const),
                  pl.BlockSpec(wkv.shape, const)],
        out_specs=(pl.BlockSpec((None, tm, N_HEADS * 256), lambda bi, i: (bi, i, 0)),
                   pl.BlockSpec((None, tm, N_HEADS * 256), lambda bi, i: (bi, i, 0)),
                   pl.BlockSpec((None, tm, BRANCH_WIDTH), lambda bi, i: (bi, i, 0))),
        compiler_params=_cparams("arbitrary", "arbitrary"),
        name="mla_prep",
    )(proj, *rope_tabs, g_q.reshape(1, -1), g_kv.reshape(1, -1), wq, wkv)


def _bucket_np(n):
    max_exact = REL_BUCKETS // 2
    nf = np.maximum(n, max_exact).astype(np.float32)
    large = max_exact + (np.log(nf / np.float32(max_exact)) / np.float32(math.log(REL_MAX_DIST / max_exact))
                         * np.float32(REL_BUCKETS - max_exact)).astype(np.int32)
    return np.where(n < max_exact, n, np.minimum(large, REL_BUCKETS - 1)).astype(np.int32)


def _bucket_starts():
    buckets = _bucket_np(np.arange(2 * REL_MAX_DIST))
    return [int(np.argmax(buckets >= b)) for b in range(REL_BUCKETS)]


def _bank_kernel(tab_ref, o_ref, *, t, d_min, band):
    h = pl.program_id(0)
    starts = _bucket_starts()
    row = lax.broadcasted_iota(jnp.int32, (t, t), 0)
    col = lax.broadcasted_iota(jnp.int32, (t, t), 1)
    for j in range(o_ref.shape[0]):
        d = d_min + j
        if d < 0:
            o_ref[j] = jnp.full((t, t), NEG, F32)
            continue
        dist = t * d + row - col
        b_lo, b_hi = (int(x) for x in _bucket_np(np.array([max(t * d - t + 1, 0), t * d + t - 1])))
        val = jnp.full((t, t), tab_ref[b_lo, h], F32)
        for b in range(b_lo + 1, b_hi + 1):
            val = jnp.where(dist >= starts[b], tab_ref[b, h], val)
        if band:
            mult = jnp.zeros((t, t), jnp.int32)
            for window, dil in DILATED_PATTERNS:
                mult += jnp.where((dist >= 0) & (dist <= window) & ((dist & (dil - 1)) == 0), 1, 0)
            val = val + jnp.where(mult == 3, math.log(3.0), jnp.where(mult == 2, math.log(2.0), 0.0))
            keep = mult > 0
        else:
            val = val - tab_ref[REL_BUCKETS - 1, h]
            keep = dist >= 0
        o_ref[j] = jnp.where(keep, val * LOG2E, NEG)


def _bank(tab, n_tables, t, d_min, band):
    nh = tab.shape[1]
    return pl.pallas_call(
        functools.partial(_bank_kernel, t=t, d_min=d_min, band=band),
        out_shape=jax.ShapeDtypeStruct((nh, n_tables, t, t), F32),
        grid=(nh,),
        in_specs=[pl.BlockSpec(memory_space=pltpu.SMEM)],
        out_specs=pl.BlockSpec((None, n_tables, t, t), lambda h: (h, 0, 0, 0)),
        compiler_params=_cparams("arbitrary"),
        name="bank_band" if band else "bank_bias",
    )(tab)


def _first_far_diagonal(t):
    last = _bucket_starts()[REL_BUCKETS - 1]
    return -(-(last + t - 1) // t)


def _qk(q, k):
    return lax.dot_general(q, k, (((1,), (1,)), ((), ())), preferred_element_type=F32)


def _flash_init(m_ref, acc_ref):
    m_ref[...] = jnp.full(m_ref.shape, NEG, F32)
    acc_ref[...] = jnp.zeros(acc_ref.shape, F32)


def _flash_update(slot, s, v, m_ref, acc_ref, s_ref, next_scores, diagonal=False):
    t = s.shape[0]
    v_ones = jnp.concatenate([v, jnp.ones_like(v)], axis=1)
    blocks = [(0, t // 2, t // 2), (t // 2, t, t)] if diagonal else [(0, t, t)]
    for n, (r0, r1, keys) in enumerate(blocks):
        m_prev = m_ref[slot, r0:r1]
        m_new = jnp.maximum(m_prev, jnp.max(s[r0:r1, :keys], axis=-1, keepdims=True))
        alpha = jnp.exp2(m_prev - m_new)
        p = jnp.concatenate([jnp.exp2(s[r0:r1, j * 128:(j + 1) * 128] - m_new) for j in range(keys // 128)],
                            axis=1).astype(BF16)
        if n == len(blocks) - 1 and next_scores is not None:
            s_ref[slot] = next_scores()
        acc_ref[slot, r0:r1] = (jnp.concatenate([alpha, alpha], axis=1) * acc_ref[slot, r0:r1]
                                + jnp.dot(p, v_ones[:keys], preferred_element_type=F32))
        m_ref[slot, r0:r1] = m_new


def _flash_result(slot, acc_ref):
    acc = acc_ref[slot]
    return acc[:, :HEAD_DIM] / acc[:, HEAD_DIM:]


def _silu(g):
    return g * jax.nn.sigmoid(g)


def _tile(ref, kj, t, c0, width):
    start = kj * t if isinstance(kj, int) else pl.multiple_of(kj * t, t)
    return ref[pl.ds(start, t), c0:c0 + width]


def _bank_tile(bank_ref, h, i, kj):
    base = 2 * (i - kj) + 1
    top = jnp.concatenate([bank_ref[h, base], bank_ref[h, base - 1]], axis=1)
    bot = jnp.concatenate([bank_ref[h, base + 1], bank_ref[h, base]], axis=1)
    return jnp.concatenate([top, bot], axis=0)


def _emit(o_ref, g_ref, h, o):
    gate = g_ref[:, h * 128:(h + 1) * 128].astype(F32)
    o_ref[:, h * 128:(h + 1) * 128] = (o * _silu(gate)).astype(BF16)


def _flash_walk(i, first_key, n_far, slots, score, logits, value, s_ref, m_ref, acc_ref, prep=None):
    lo = first_key(i)
    i_next = jnp.minimum(i + 1, pl.num_programs(1) - 1)
    _flash_init(m_ref, acc_ref)

    @pl.when(i == 0)
    def _():
        for slot in range(slots):
            s_ref[slot] = score(slot, False, lo)

    def step(kj, phase):
        ctx = prep(kj) if prep is not None else None
        for slot in range(slots):
            s = logits(slot, s_ref[slot], kj, phase, ctx)
            if phase == "last":
                nxt = functools.partial(score, slot, True, first_key(i_next))
            else:
                nxt = functools.partial(score, slot, False, kj + 1)
            _flash_update(slot, s, value(slot, kj), m_ref, acc_ref, s_ref, nxt, diagonal=phase == "last")

    def walk(a, b, phase):
        def body(kj, carry):
            step(kj, phase)
            return carry
        lax.fori_loop(a, b, body, 0)

    if n_far is not None:
        walk(lo, n_far, "far")
        lo = n_far
    walk(lo, i, "near")
    step(i, "last")


def _attn_a_kernel(q_ref, qn_ref, k_ref, v_ref, g_ref, o_ref, s_ref, m_ref, acc_ref, *, t):
    i = pl.program_id(1)

    def score(h, next_q, kj):
        q = (qn_ref if next_q else q_ref)[:, h * 256:(h + 1) * 256]
        return _qk(q, _tile(k_ref, kj, t, h * 256, 256))

    def logits(h, s, kj, phase, ctx):
        if phase != "last":
            return s
        causal = lax.broadcasted_iota(jnp.int32, (t, t), 1) <= lax.broadcasted_iota(jnp.int32, (t, t), 0)
        return jnp.where(causal, s, NEG)

    def value(h, kj):
        return _tile(v_ref, kj, t, h * 128, 128)

    _flash_walk(i, lambda qi: 0, None, N_HEADS, score, logits, value, s_ref, m_ref, acc_ref)
    for h in range(N_HEADS):
        _emit(o_ref, g_ref, h, _flash_result(h, acc_ref))


def _attn_band_kernel(q_ref, qn_ref, k_ref, v_ref, g_ref, bank_ref, o_ref, s_ref, m_ref, acc_ref, *, t, near):
    i = pl.program_id(1)

    def score(h, next_q, kj):
        q = (qn_ref if next_q else q_ref)[:, h * 128:(h + 1) * 128]
        return _qk(q, _tile(k_ref, kj, t, h * 128, 128))

    def logits(h, s, kj, phase, ctx):
        return s + _bank_tile(bank_ref, h, i, kj)

    def value(h, kj):
        return _tile(v_ref, kj, t, h * 128, 128)

    _flash_walk(i, lambda qi: jnp.maximum(qi - (near - 1), 0), None, N_HEADS, score, logits, value,
                s_ref, m_ref, acc_ref)
    for h in range(N_HEADS):
        _emit(o_ref, g_ref, h, _flash_result(h, acc_ref))


def _attn_sel_kernel(q_ref, qn_ref, k_ref, v_ref, g_ref, bank_ref, sel_ref, o_ref,
                     s_ref, m_ref, acc_ref, *, t, near):
    i = pl.program_id(1)

    def score(h, next_q, kj):
        q = (qn_ref if next_q else q_ref)[:, h * 128:(h + 1) * 128]
        return _qk(q, _tile(k_ref, kj, t, h * 128, 128))

    def prep(kj):
        return sel_ref[:, pl.ds(pl.multiple_of(kj * t, t), t)].astype(F32)

    def logits(h, s, kj, phase, sel):
        return s + sel if phase == "far" else s + (sel + _bank_tile(bank_ref, h, i, kj))

    def value(h, kj):
        return _tile(v_ref, kj, t, h * 128, 128)

    _flash_walk(i, lambda qi: 0, jnp.maximum(i - (near - 1), 0), N_HEADS, score, logits, value,
                s_ref, m_ref, acc_ref, prep)
    for h in range(N_HEADS):
        _emit(o_ref, g_ref, h, _flash_result(h, acc_ref))


def _attn_diff_kernel(q_ref, qn_ref, k_ref, v_ref, g_ref, bank_ref, lam_ref, gsub_ref, o_ref,
                      s_ref, m_ref, acc_ref, *, t, near, lam_init):
    i = pl.program_id(1)
    first_half = lax.broadcasted_iota(jnp.int32, (t, HEAD_DIM), 1) < HEAD_DIM // 2

    def score(slot, next_q, kj):
        h = slot // 2
        q = (qn_ref if next_q else q_ref)[:, h * 128:(h + 1) * 128]
        keep = first_half if slot % 2 == 0 else jnp.logical_not(first_half)
        q = jnp.where(keep, q, jnp.zeros_like(q))
        return _qk(q, _tile(k_ref, kj, t, h * 128, 128))

    def logits(slot, s, kj, phase, ctx):
        return s if phase == "far" else s + _bank_tile(bank_ref, slot // 2, i, kj)

    def value(slot, kj):
        return _tile(v_ref, kj, t, (slot // 2) * 128, 128)

    _flash_walk(i, lambda qi: 0, jnp.maximum(i - (near - 1), 0), 2 * N_HEADS, score, logits, value,
                s_ref, m_ref, acc_ref)
    lam_v = lam_ref[...]
    lam = (jnp.exp(jnp.sum(lam_v[0:1] * lam_v[1:2], axis=-1, keepdims=True))
           - jnp.exp(jnp.sum(lam_v[2:3] * lam_v[3:4], axis=-1, keepdims=True)) + lam_init)
    for h in range(N_HEADS):
        o = _flash_result(2 * h, acc_ref) - lam * _flash_result(2 * h + 1, acc_ref)
        _emit(o_ref, g_ref, h, _rms(o, gsub_ref[...]) * (1.0 - lam_init))


def _attention(kind, q_arr, k_arr, v_arr, proj, q_blk, gate_blk, extra_in=(), extra_specs=(), **kw):
    b, s, _ = proj.shape
    t = min(ATT_TILE, s)
    dk = 256 if kind == "a" else 128
    qw = N_HEADS * dk
    k_blk = 0 if kind == "a" else q_blk + 1
    v_blk = 0 if kind == "a" else q_blk + 2
    body = {"a": _attn_a_kernel, "band": _attn_band_kernel, "sel": _attn_sel_kernel,
            "diff": _attn_diff_kernel}[kind]
    slots = 2 * N_HEADS if kind == "diff" else N_HEADS
    scratch = [pltpu.VMEM((slots, t, t), F32), pltpu.VMEM((slots, t, HEAD_DIM), F32),
               pltpu.VMEM((slots, t, 2 * HEAD_DIM), F32)]
    last = s // t - 1
    in_specs = [pl.BlockSpec((None, t, qw), lambda bi, i: (bi, i, q_blk)),
                pl.BlockSpec((None, t, qw), lambda bi, i: (bi, jnp.minimum(i + 1, last), q_blk)),
                pl.BlockSpec((None, s, qw), lambda bi, i: (bi, 0, k_blk)),
                pl.BlockSpec((None, s, BRANCH_WIDTH), lambda bi, i: (bi, 0, v_blk)),
                pl.BlockSpec((None, t, BRANCH_WIDTH), lambda bi, i: (bi, i, gate_blk))]
    in_specs += list(extra_specs)
    return pl.pallas_call(
        functools.partial(body, t=t, **kw),
        out_shape=jax.ShapeDtypeStruct((b, s, BRANCH_WIDTH), BF16),
        grid=(b, s // t),
        in_specs=in_specs,
        out_specs=pl.BlockSpec((None, t, BRANCH_WIDTH), lambda bi, i: (bi, i, 0)),
        scratch_shapes=scratch,
        compiler_params=_cparams("arbitrary", "arbitrary"),
        name="attn_" + kind,
    )(q_arr, q_arr, k_arr, v_arr, proj, *extra_in)


def _bit_transpose32(words):
    a = list(words)
    j, m = 16, 0x0000FFFF
    while j:
        for k in range(32):
            if not k & j:
                t = (a[k] ^ lax.shift_right_logical(a[k + j], jnp.int32(j))) & jnp.int32(m)
                a[k] = a[k] ^ t
                a[k + j] = a[k + j] ^ (t << j)
        j >>= 1
        m = (m ^ (m << j)) & 0xFFFFFFFF if j else m
    return a


def _select_kernel(qi_ref, ki_ref, wi_ref, o_ref, key_ref, plane_ref, alive_ref, *, tq, kc, n_sel):
    i = pl.program_id(1)
    s_len = o_ref.shape[1]
    n_ch = (i * tq + tq + kc - 1) // kc
    wpc = kc // 32
    int_min = jnp.int32(-2 ** 31)
    lane = lax.broadcasted_iota(jnp.int32, (tq, 128), 1)
    w_t = wi_ref[...].astype(F32).T
    q_heads = []
    for j in range(IDX_HEADS // 2):
        q2 = qi_ref[:, j * 128:(j + 1) * 128]
        q_heads.append(jnp.where(lane < IDX_DIM, q2, jnp.zeros_like(q2)))
        q_heads.append(jnp.where(lane >= IDX_DIM, q2, jnp.zeros_like(q2)))
    kiota = lax.broadcasted_iota(jnp.int32, (kc, tq), 0)

    def chunk(c):
        return pl.ds(pl.multiple_of(c * kc, kc), kc)

    def scored(k, q_from):
        acc = jnp.zeros((k.shape[0], tq - q_from), F32)
        for hh in range(IDX_HEADS):
            acc = acc + jnp.maximum(_qk(k, q_heads[hh][q_from:, :]), 0.0) * w_t[hh:hh + 1, q_from:]
        return acc + 0.0

    def causal(acc):
        key = lax.broadcasted_iota(jnp.int32, acc.shape, 0)
        return jnp.where(key <= lax.broadcasted_iota(jnp.int32, acc.shape, 1), acc, NEG)

    def store_keys(row0, acc):
        bits = pltpu.bitcast(acc, jnp.int32)
        keys = bits ^ ((bits >> 31) & jnp.int32(0x7FFFFFFF))
        key_ref[pl.ds(row0, acc.shape[0]), :] = keys
        ukeys = keys ^ int_min
        for blk in range(acc.shape[0] // 256):
            planes = _bit_transpose32([ukeys[blk * 256 + 8 * j:blk * 256 + 8 * j + 8, :] for j in range(32)])
            row = pl.multiple_of(row0 // 32 + blk * 8, 8)
            for b in range(32):
                plane_ref[b, pl.ds(row, 8), :] = planes[b]

    def score_chunk(c, _):
        store_keys(pl.multiple_of(c * kc, kc), scored(ki_ref[chunk(c), :], 0))
        return 0

    lax.fori_loop(0, i, score_chunk, 0)
    diag = pl.multiple_of(i * kc, kc)
    half = kc // 2
    if half % 256 == 0:
        store_keys(diag, causal(scored(ki_ref[pl.ds(diag, half), :], 0)))
        late = causal(scored(ki_ref[pl.ds(diag + half, half), :], half))
        store_keys(diag + half, jnp.concatenate([jnp.full((half, half), NEG, F32), late], axis=1))
    else:
        store_keys(diag, causal(scored(ki_ref[pl.ds(diag, kc), :], 0)))

    def clear_chunk(c, _):
        plane_ref[:, pl.ds(pl.multiple_of(c * wpc, wpc), wpc), :] = jnp.zeros((32, wpc, tq), jnp.int32)
        return 0

    lax.fori_loop(n_ch, s_len // kc, clear_chunk, 0)
    n_rows = s_len // 32
    word_row = lax.broadcasted_iota(jnp.int32, (n_rows, tq), 0)
    alive_ref[...] = jnp.where(word_row < n_ch * wpc, jnp.int32(-1), jnp.int32(0))

    def radix_step(bi, carry):
        thr, above = carry
        alive = alive_ref[...]
        plane = plane_ref[bi]
        ones = lax.population_count(alive & plane)
        ones = jnp.sum(jnp.sum(ones.reshape(n_rows // 8, 8, tq), axis=0), axis=0, keepdims=True)
        take = above + ones >= n_sel
        thr = jnp.where(take, thr | (jnp.int32(1) << (31 - bi)), thr)
        above = jnp.where(take, above, above + ones)
        alive_ref[...] = alive & (plane ^ jnp.where(take, jnp.int32(0), jnp.int32(-1)))
        return thr, above

    zeros = jnp.zeros((1, tq), jnp.int32)
    thr_u, above = lax.fori_loop(0, 32, radix_step, (zeros, zeros))
    thr = thr_u ^ int_min
    n_equal = lax.population_count(alive_ref[...])
    n_equal = jnp.sum(jnp.sum(n_equal.reshape(n_rows // 8, 8, tq), axis=0), axis=0, keepdims=True)
    need = n_sel - above
    masked_key = int(np.float32(NEG).view(np.int32)) ^ 0x7FFFFFFF
    tie = (n_equal > need) & (thr != masked_key)
    any_tie = jnp.max(jnp.where(tie, 1, 0)) > 0

    @pl.when(jnp.logical_not(any_tie))
    def _():
        def emit(c, _):
            keep = jnp.where(key_ref[chunk(c), :] >= thr, 0.0, NEG)
            o_ref[:, chunk(c)] = keep.T.astype(BF16)
            return 0

        lax.fori_loop(0, n_ch, emit, 0)

    @pl.when(any_tie)
    def _():
        def equal_below(bound):
            def body(c, part):
                hit = jnp.where((key_ref[chunk(c), :] == thr) & (c * kc + kiota < bound), 1, 0)
                return part + jnp.sum(hit.reshape(kc // 8, 8, tq), axis=0)
            part = lax.fori_loop(0, n_ch, body, jnp.zeros((8, tq), jnp.int32))
            return jnp.sum(part, axis=0, keepdims=True)

        n_bits = s_len.bit_length()

        def bound_step(bi, cut):
            cand = cut + (jnp.int32(1) << (n_bits - 1 - bi))
            ok = (cand <= s_len) & (equal_below(cand) <= need)
            return jnp.where(ok, cand, cut)

        cut = lax.fori_loop(0, n_bits, bound_step, zeros)

        def emit(c, _):
            keys = key_ref[chunk(c), :]
            kept = (keys > thr) | ((keys == thr) & (c * kc + kiota < cut))
            o_ref[:, chunk(c)] = jnp.where(kept, 0.0, NEG).T.astype(BF16)
            return 0

        lax.fori_loop(0, n_ch, emit, 0)

    def blank(c, _):
        o_ref[:, chunk(c)] = jnp.full((tq, kc), NEG, BF16)
        return 0

    lax.fori_loop(n_ch, s_len // kc, blank, 0)


def _select(proj):
    b, s, _ = proj.shape
    tq = kc = min(512, s)
    n_sel = min(TOPK_MAX, s // 4)
    return pl.pallas_call(
        functools.partial(_select_kernel, tq=tq, kc=kc, n_sel=n_sel),
        out_shape=jax.ShapeDtypeStruct((b, s, s), BF16),
        grid=(b, s // tq),
        in_specs=[pl.BlockSpec((None, tq, 1024), lambda bi, i: (bi, i, OFF_QIDX // 1024)),
                  pl.BlockSpec((None, s, 128), lambda bi, i: (bi, 0, OFF_KIDX // 128)),
                  pl.BlockSpec((None, tq, 128), lambda bi, i: (bi, i, OFF_WIDX // 128))],
        out_specs=pl.BlockSpec((None, tq, s), lambda bi, i: (bi, i, 0)),
        scratch_shapes=[pltpu.VMEM((s, tq), jnp.int32), pltpu.VMEM((32, s // 32, tq), jnp.int32),
                        pltpu.VMEM((s // 32, tq), jnp.int32)],
        compiler_params=_cparams("arbitrary", "arbitrary"),
        name="idx_select",
    )(proj, proj, proj)


def _out_kernel(a_ref, b_ref, c_ref, d_ref, w_ref, x_ref, gate_ref, g_ref, o_ref, wb_ref):
    @pl.when((pl.program_id(0) == 0) & (pl.program_id(1) == 0))
    def _():
        wb_ref[...] = w_ref[...].astype(BF16)

    y = jnp.dot(a_ref[...], wb_ref[0:512, :], preferred_element_type=F32)
    y += jnp.dot(b_ref[...], wb_ref[512:1024, :], preferred_element_type=F32)
    y += jnp.dot(c_ref[...], wb_ref[1024:1536, :], preferred_element_type=F32)
    y += jnp.dot(d_ref[...], wb_ref[1536:2048, :], preferred_element_type=F32)
    o_ref[...] = x_ref[...] + gate_ref[...] * _rms(y, g_ref[...])


def _out_proj(outs, w_out, li, x, mod3, g_post):
    b, s, d = x.shape
    tm = min(512, s)
    mix = lambda bi, i: (bi, i, 0)
    return pl.pallas_call(
        _out_kernel,
        out_shape=jax.ShapeDtypeStruct((b, s, d), F32),
        grid=(b, s // tm),
        in_specs=[pl.BlockSpec((None, tm, BRANCH_WIDTH), mix)] * 4
        + [pl.BlockSpec((None,) + w_out.shape[1:], lambda bi, i: (li, 0, 0), pipeline_mode=pl.Buffered(1)),
           pl.BlockSpec((None, tm, d), mix),
           pl.BlockSpec((None, 1, d), lambda bi, i: (bi, 0, 2)),
           pl.BlockSpec((1, d), lambda bi, i: (0, 0))],
        out_specs=pl.BlockSpec((None, tm, d), mix),
        scratch_shapes=[pltpu.VMEM(w_out.shape[1:], BF16)],
        compiler_params=_cparams("arbitrary", "arbitrary"),
        name="out_proj",
    )(*outs, w_out, x, mod3, g_post.reshape(1, d))


def _rope_tables(s):
    half = MLA_ROPE // 2
    inv = ROPE_THETA ** (-jnp.arange(half, dtype=F32) / half)
    ang = jnp.arange(s, dtype=F32)[:, None] * inv[None, :]
    z = jnp.zeros((s, 128 - MLA_ROPE), F32)
    cos, sin = jnp.cos(ang), jnp.sin(ang)
    cat = lambda *parts: jnp.concatenate(parts, axis=-1)
    return cat(cos, cos, z), cat(sin, sin, z), cat(cos, cos, sin, sin)


def _rot_cols(w):
    half = w.shape[-1] // 2
    return jnp.concatenate([-w[..., half:], w[..., :half]], axis=-1)


IN_SPLITS = (("a_cq", 384), ("a_ckv", 256), ("a_krope", 64), ("b_q", 512), ("b_k", 512), ("b_v", 512),
             ("c_q", 512), ("c_k", 512), ("c_v", 512), ("c_qidx", 1024), ("c_kidx", 64), ("c_widx", 16),
             ("d_q", 512), ("d_k", 512), ("d_v", 512), ("gate", 2048))
IN_WIDTH = sum(width for _, width in IN_SPLITS)


def _layout_w_in_kernel(w_ref, o_ref):
    src, start = {}, 0
    for name, width in IN_SPLITS:
        src[name] = start
        start += width
    tk = w_ref.shape[1]

    def rows(name, width, offset=0):
        a = src[name] + offset
        return w_ref[a:a + width, :]

    def put(dst, val, scale=None):
        for r in range(0, val.shape[0], 512):
            piece = val[r:r + 512]
            if scale is not None:
                piece = piece * scale
            o_ref[:, dst + r:dst + r + piece.shape[0]] = piece.T.astype(BF16)

    z64 = jnp.zeros((64, tk), F32)
    half = MLA_ROPE // 2
    put(OFF_QIDX, rows("c_qidx", 1024), IDX_DIM ** -0.5)
    put(OFF_A, rows("a_cq", MLA_Q_RANK + MLA_KV_RANK))
    put(OFF_A + 640, jnp.concatenate(
        [rows("a_krope", MLA_ROPE), z64, -rows("a_krope", half, half), rows("a_krope", half), z64,
         rows("c_kidx", IDX_DIM), rows("c_kidx", IDX_DIM)], axis=0))
    for off, name, dim in ((OFF_B, "b", HEAD_DIM), (OFF_C, "c", HEAD_DIM), (OFF_D, "d", HEAD_DIM // 2)):
        put(off, rows(name + "_q", BRANCH_WIDTH), LOG2E * dim ** -0.5)
        put(off + BRANCH_WIDTH, rows(name + "_k", 2 * BRANCH_WIDTH))
    put(OFF_GATE, rows("gate", MIX_WIDTH))
    put(OFF_WIDX, jnp.concatenate([rows("c_widx", IDX_HEADS) * IDX_HEADS ** -0.5,
                                   jnp.zeros((128 - IDX_HEADS, tk), F32)], axis=0))


def _layout_w_in(w_in_t, li):
    d = w_in_t.shape[2]
    tk = 256
    return pl.pallas_call(
        _layout_w_in_kernel,
        out_shape=jax.ShapeDtypeStruct((d, PROJ_WIDTH), BF16),
        grid=(d // tk,),
        in_specs=[pl.BlockSpec((None, IN_WIDTH, tk), lambda i: (li, 0, i))],
        out_specs=pl.BlockSpec((tk, PROJ_WIDTH), lambda i: (i, 0)),
        compiler_params=_cparams("arbitrary"),
        name="layout_w_in",
    )(w_in_t)


def _layout_w_uq(w):
    r = w.shape[0]
    w = w.reshape(r, N_HEADS, MLA_NOPE + MLA_ROPE) * (LOG2E * (MLA_NOPE + MLA_ROPE) ** -0.5)
    rope = w[..., MLA_NOPE:]
    return jnp.concatenate([w[..., :MLA_NOPE], rope, _rot_cols(rope)], axis=-1).reshape(r, -1).astype(BF16)


def kernel(x, c, w_ada, b_ada, g_pre, g_post, w_in, g_q_a, w_uq_a, g_kv_a, w_ukv_a,
           lam_q1, lam_k1, lam_q2, lam_k2, g_sub_d, w_out, rel_bias):
    b, s, d = x.shape
    depth = w_ada.shape[0]
    t = min(ATT_TILE, s)
    half = t // 2
    nq = s // t
    near_bias = min(nq, -(-(_first_far_diagonal(half) + 1) // 2))
    near_band = min(nq, -(-(DILATED_PATTERNS[-1][0] // half + 1) // 2))

    rope_tabs = _rope_tables(s)
    bank_b = _bank(rel_bias[:, 0:N_HEADS], 2 * near_band + 1, half, -1, True)
    bank_cd = _bank(rel_bias[:, N_HEADS:3 * N_HEADS], 2 * near_bias + 1, half, -1, False)
    bank_spec = lambda n, group=0: pl.BlockSpec((N_HEADS, n, half, half), lambda bi, i: (group, 0, 0, 0),
                                                pipeline_mode=pl.Buffered(1))

    w_in_t = jnp.swapaxes(w_in, 1, 2)
    mod = _ada_mod(c, w_ada, b_ada)
    for li in range(depth):
        mod3 = mod[li].reshape(b, 1, 3 * d)
        proj = _in_proj(x, g_pre[li], mod3, _layout_w_in(w_in_t, li))

        q_a, k_a, v_a = _mla_prep(proj, rope_tabs, g_q_a[li], g_kv_a[li],
                                  _layout_w_uq(w_uq_a[li]), w_ukv_a[li].astype(BF16))
        gate0 = OFF_GATE // BRANCH_WIDTH
        out_a = _attention("a", q_a, k_a, v_a, proj, 0, gate0)
        out_b = _attention("band", proj, proj, proj, proj, OFF_B // BRANCH_WIDTH, gate0 + 1,
                           extra_in=(bank_b,), extra_specs=(bank_spec(2 * near_band + 1),), near=near_band)
        sel = _select(proj)
        out_c = _attention("sel", proj, proj, proj, proj, OFF_C // BRANCH_WIDTH, gate0 + 2,
                           extra_in=(bank_cd, sel),
                           extra_specs=(bank_spec(2 * near_bias + 1, 0),
                                        pl.BlockSpec((None, t, s), lambda bi, i: (bi, i, 0))),
                           near=near_bias)
        lam_init = 0.8 - 0.6 * math.exp(-0.3 * li)
        lam_vecs = jnp.stack([lam_q1[li], lam_k1[li], lam_q2[li], lam_k2[li]])
        out_d = _attention("diff", proj, proj, proj, proj, OFF_D // BRANCH_WIDTH, gate0 + 3,
                           extra_in=(bank_cd, lam_vecs, g_sub_d[li].reshape(1, HEAD_DIM)),
                           extra_specs=(bank_spec(2 * near_bias + 1, 1),
                                        pl.BlockSpec(lam_vecs.shape, lambda bi, i: (0, 0)),
                                        pl.BlockSpec((1, HEAD_DIM), lambda bi, i: (0, 0))),
                           near=near_bias, lam_init=lam_init)
        x = _out_proj((out_a, out_b, out_c, out_d), w_out, li, x, mod3, g_post[li])
    return x
```

```python
import functools
import math

import numpy as np
import jax
import jax.numpy as jnp
from jax import lax
from jax.experimental import pallas as pl
from jax.experimental.pallas import tpu as pltpu

F32 = jnp.float32
BF16 = jnp.bfloat16

HEAD_DIM = 128
N_HEADS = 4
BRANCH_WIDTH = N_HEADS * HEAD_DIM
MIX_WIDTH = 4 * BRANCH_WIDTH
MLA_Q_RANK = 384
MLA_KV_RANK = 256
MLA_NOPE = 128
MLA_ROPE = 64
ROPE_THETA = 10000.0
DILATED_PATTERNS = ((128, 1), (512, 4), (2048, 16))
IDX_HEADS = 16
IDX_DIM = 64
TOPK_MAX = 256
REL_BUCKETS = 32
REL_MAX_DIST = 2048
NORM_EPS = 1e-6
NEG = -1e30
LOG2E = math.log2(math.e)

OFF_QIDX = 0
OFF_A = 1024
OFF_KIDX = OFF_A + 896
OFF_B = 2048
OFF_C = 3584
OFF_D = 5120
OFF_GATE = 6656
OFF_WIDX = 8704
PROJ_WIDTH = 8832

ATT_TILE = 512
V7X_VMEM_BYTES = 64 * 1024 * 1024
VMEM_LIMIT = V7X_VMEM_BYTES // 8 * 7


def _cparams(*sem):
    return pltpu.CompilerParams(dimension_semantics=sem, vmem_limit_bytes=VMEM_LIMIT)


def _split_bf16(x):
    hi = x.astype(BF16)
    return hi, (x - hi.astype(F32)).astype(BF16)


def _ada_kernel(c_ref, w_ref, b_ref, o_ref):
    c = c_ref[...]
    a_hi, a_lo = _split_bf16(c * jax.nn.sigmoid(c))
    w_hi, w_lo = _split_bf16(w_ref[...])
    dot = functools.partial(jnp.dot, preferred_element_type=F32)
    o_ref[...] = dot(a_hi, w_hi) + (dot(a_hi, w_lo) + dot(a_lo, w_hi)) + b_ref[...]


def _ada_mod(c, w_ada, b_ada):
    depth, d, n = w_ada.shape
    b = c.shape[0]
    tn = 768
    return pl.pallas_call(
        _ada_kernel,
        out_shape=jax.ShapeDtypeStruct((depth, b, n), F32),
        grid=(depth, n // tn),
        in_specs=[pl.BlockSpec((b, d), lambda l, j: (0, 0)),
                  pl.BlockSpec((None, d, tn), lambda l, j: (l, 0, j)),
                  pl.BlockSpec((None, 1, tn), lambda l, j: (l, 0, j))],
        out_specs=pl.BlockSpec((None, b, tn), lambda l, j: (l, 0, j)),
        compiler_params=_cparams("arbitrary", "arbitrary"),
        name="ada_mod",
    )(c, w_ada, b_ada.reshape(depth, 1, n))


def _in_proj_kernel(x_ref, g_ref, shift_ref, scale_ref, w_ref, o_ref):
    x = x_ref[...]
    y = x * lax.rsqrt(jnp.mean(x * x, axis=-1, keepdims=True) + NORM_EPS) * g_ref[...]
    h = (y * (1.0 + scale_ref[...]) + shift_ref[...]).astype(BF16)
    o_ref[...] = jnp.dot(h, w_ref[...], preferred_element_type=F32).astype(o_ref.dtype)


def _in_proj(x, g_pre, mod3, w):
    b, s, d = x.shape
    n = w.shape[1]
    tm = min(512, s)
    nt = s // tm
    tn = n // 3
    row = lambda j, i: (i // nt, i % nt, 0)
    return pl.pallas_call(
        _in_proj_kernel,
        out_shape=jax.ShapeDtypeStruct((b, s, n), BF16),
        grid=(n // tn, b * nt),
        in_specs=[pl.BlockSpec((None, tm, d), row),
                  pl.BlockSpec((1, d), lambda j, i: (0, 0)),
                  pl.BlockSpec((None, 1, d), lambda j, i: (i // nt, 0, 0)),
                  pl.BlockSpec((None, 1, d), lambda j, i: (i // nt, 0, 1)),
                  pl.BlockSpec((d, tn), lambda j, i: (0, j), pipeline_mode=pl.Buffered(1))],
        out_specs=pl.BlockSpec((None, tm, tn), lambda j, i: (i // nt, i % nt, j)),
        compiler_params=_cparams("arbitrary", "arbitrary"),
        name="in_proj",
    )(x, g_pre.reshape(1, d), mod3, mod3, w)


def _rms(x, g):
    return x * lax.rsqrt(jnp.mean(x * x, axis=-1, keepdims=True) + NORM_EPS) * g


def _mla_prep_kernel(p_ref, cos_ref, sin_ref, cs_ref, gq_ref, gkv_ref, wq_ref, wkv_ref, q_ref, k_ref, v_ref):
    cos = cos_ref[...]
    sin = sin_ref[...]
    cs = cs_ref[...]
    cq = _rms(p_ref[:, 0:MLA_Q_RANK].astype(F32), gq_ref[...]).astype(BF16)
    ckv = _rms(p_ref[:, MLA_Q_RANK:MLA_Q_RANK + MLA_KV_RANK].astype(F32), gkv_ref[...]).astype(BF16)
    q = jnp.dot(cq, wq_ref[...], preferred_element_type=F32)
    kv = jnp.dot(ckv, wkv_ref[...], preferred_element_type=F32)
    k_rope = (p_ref[:, 640:768].astype(F32) * cos + p_ref[:, 768:896].astype(F32) * sin).astype(BF16)
    for h in range(N_HEADS):
        q_ref[:, h * 256:h * 256 + 128] = q[:, h * 256:h * 256 + 128].astype(BF16)
        z = q[:, h * 256 + 128:(h + 1) * 256] * cs
        q_ref[:, h * 256 + 128:(h + 1) * 256] = (z + pltpu.roll(z, MLA_ROPE, axis=1)).astype(BF16)
        k_ref[:, h * 256:h * 256 + 128] = kv[:, h * 256:h * 256 + 128].astype(BF16)
        k_ref[:, h * 256 + 128:(h + 1) * 256] = k_rope
        v_ref[:, h * 128:(h + 1) * 128] = kv[:, h * 256 + 128:(h + 1) * 256].astype(BF16)


def _mla_prep(proj, rope_tabs, g_q, g_kv, wq, wkv):
    b, s, _ = proj.shape
    tm = min(512, s)
    const = lambda bi, i: (0, 0)
    return pl.pallas_call(
        _mla_prep_kernel,
        out_shape=(jax.ShapeDtypeStruct((b, s, N_HEADS * 256), BF16),
                   jax.ShapeDtypeStruct((b, s, N_HEADS * 256), BF16),
                   jax.ShapeDtypeStruct((b, s, BRANCH_WIDTH), BF16)),
        grid=(b, s // tm),
        in_specs=[pl.BlockSpec((None, tm, 1024), lambda bi, i: (bi, i, OFF_A // 1024)),
                  pl.BlockSpec((tm, 128), lambda bi, i: (i, 0)),
                  pl.BlockSpec((tm, 128), lambda bi, i: (i, 0)),
                  pl.BlockSpec((tm, 128), lambda bi, i: (i, 0)),
                  pl.BlockSpec((1, MLA_Q_RANK), const),
                  pl.BlockSpec((1, MLA_KV_RANK), const),
                  pl.BlockSpec(wq.shape, const),
                  pl.BlockSpec(wkv.shape, const)],
        out_specs=(pl.BlockSpec((None, tm, N_HEADS * 256), lambda bi, i: (bi, i, 0)),
                   pl.BlockSpec((None, tm, N_HEADS * 256), lambda bi, i: (bi, i, 0)),
                   pl.BlockSpec((None, tm, BRANCH_WIDTH), lambda bi, i: (bi, i, 0))),
        compiler_params=_cparams("arbitrary", "arbitrary"),
        name="mla_prep",
    )(proj, *rope_tabs, g_q.reshape(1, -1), g_kv.reshape(1, -1), wq, wkv)


def _bucket_np(n):
    max_exact = REL_BUCKETS // 2
    nf = np.maximum(n, max_exact).astype(np.float32)
    large = max_exact + (np.log(nf / np.float32(max_exact)) / np.float32(math.log(REL_MAX_DIST / max_exact))
                         * np.float32(REL_BUCKETS - max_exact)).astype(np.int32)
    return np.where(n < max_exact, n, np.minimum(large, REL_BUCKETS - 1)).astype(np.int32)


def _bucket_starts():
    buckets = _bucket_np(np.arange(2 * REL_MAX_DIST))
    return [int(np.argmax(buckets >= b)) for b in range(REL_BUCKETS)]


def _bank_kernel(tab_ref, o_ref, *, t, d_min, band):
    h = pl.program_id(0)
    starts = _bucket_starts()
    row = lax.broadcasted_iota(jnp.int32, (t, t), 0)
    col = lax.broadcasted_iota(jnp.int32, (t, t), 1)
    for j in range(o_ref.shape[0]):
        d = d_min + j
        if d < 0:
            o_ref[j] = jnp.full((t, t), NEG, F32)
            continue
        dist = t * d + row - col
        b_lo, b_hi = (int(x) for x in _bucket_np(np.array([max(t * d - t + 1, 0), t * d + t - 1])))
        val = jnp.full((t, t), tab_ref[b_lo, h], F32)
        for b in range(b_lo + 1, b_hi + 1):
            val = jnp.where(dist >= starts[b], tab_ref[b, h], val)
        if band:
            mult = jnp.zeros((t, t), jnp.int32)
            for window, dil in DILATED_PATTERNS:
                mult += jnp.where((dist >= 0) & (dist <= window) & ((dist & (dil - 1)) == 0), 1, 0)
            val = val + jnp.where(mult == 3, math.log(3.0), jnp.where(mult == 2, math.log(2.0), 0.0))
            keep = mult > 0
        else:
            val = val - tab_ref[REL_BUCKETS - 1, h]
            keep = dist >= 0
        o_ref[j] = jnp.where(keep, val * LOG2E, NEG)


def _bank(tab, n_tables, t, d_min, band):
    nh = tab.shape[1]
    return pl.pallas_call(
        functools.partial(_bank_kernel, t=t, d_min=d_min, band=band),
        out_shape=jax.ShapeDtypeStruct((nh, n_tables, t, t), F32),
        grid=(nh,),
        in_specs=[pl.BlockSpec(memory_space=pltpu.SMEM)],
        out_specs=pl.BlockSpec((None, n_tables, t, t), lambda h: (h, 0, 0, 0)),
        compiler_params=_cparams("arbitrary"),
        name="bank_band" if band else "bank_bias",
    )(tab)


def _first_far_diagonal(t):
    last = _bucket_starts()[REL_BUCKETS - 1]
    return -(-(last + t - 1) // t)


def _qk(q, k):
    return lax.dot_general(q, k, (((1,), (1,)), ((), ())), preferred_element_type=F32)


def _flash_init(m_ref, acc_ref):
    m_ref[...] = jnp.full(m_ref.shape, NEG, F32)
    acc_ref[...] = jnp.zeros(acc_ref.shape, F32)


def _flash_update(slot, s, v, m_ref, acc_ref, s_ref, next_scores, diagonal=False):
    t = s.shape[0]
    v_ones = jnp.concatenate([v, jnp.ones_like(v)], axis=1)
    blocks = [(0, t // 2, t // 2), (t // 2, t, t)] if diagonal else [(0, t, t)]
    for n, (r0, r1, keys) in enumerate(blocks):
        m_prev = m_ref[slot, r0:r1]
        m_new = jnp.maximum(m_prev, jnp.max(s[r0:r1, :keys], axis=-1, keepdims=True))
        alpha = jnp.exp2(m_prev - m_new)
        p = jnp.concatenate([jnp.exp2(s[r0:r1, j * 128:(j + 1) * 128] - m_new) for j in range(keys // 128)],
                            axis=1).astype(BF16)
        if n == len(blocks) - 1 and next_scores is not None:
            s_ref[slot] = next_scores()
        acc_ref[slot, r0:r1] = (jnp.concatenate([alpha, alpha], axis=1) * acc_ref[slot, r0:r1]
                                + jnp.dot(p, v_ones[:keys], preferred_element_type=F32))
        m_ref[slot, r0:r1] = m_new


def _flash_result(slot, acc_ref):
    acc = acc_ref[slot]
    return acc[:, :HEAD_DIM] / acc[:, HEAD_DIM:]


def _silu(g):
    return g * jax.nn.sigmoid(g)


def _tile(ref, kj, t, c0, width):
    start = kj * t if isinstance(kj, int) else pl.multiple_of(kj * t, t)
    return ref[pl.ds(start, t), c0:c0 + width]


def _bank_tile(bank_ref, h, i, kj):
    base = 2 * (i - kj) + 1
    top = jnp.concatenate([bank_ref[h, base], bank_ref[h, base - 1]], axis=1)
    bot = jnp.concatenate([bank_ref[h, base + 1], bank_ref[h, base]], axis=1)
    return jnp.concatenate([top, bot], axis=0)


def _emit(o_ref, g_ref, h, o):
    gate = g_ref[:, h * 128:(h + 1) * 128].astype(F32)
    o_ref[:, h * 128:(h + 1) * 128] = (o * _silu(gate)).astype(BF16)


def _flash_walk(i, first_key, n_far, slots, score, logits, value, s_ref, m_ref, acc_ref, prep=None):
    lo = first_key(i)
    i_next = jnp.minimum(i + 1, pl.num_programs(1) - 1)
    _flash_init(m_ref, acc_ref)

    @pl.when(i == 0)
    def _():
        for slot in range(slots):
            s_ref[slot] = score(slot, False, lo)

    def step(kj, phase):
        ctx = prep(kj) if prep is not None else None
        for slot in range(slots):
            s = logits(slot, s_ref[slot], kj, phase, ctx)
            if phase == "last":
                nxt = functools.partial(score, slot, True, first_key(i_next))
            else:
                nxt = functools.partial(score, slot, False, kj + 1)
            _flash_update(slot, s, value(slot, kj), m_ref, acc_ref, s_ref, nxt, diagonal=phase == "last")

    def walk(a, b, phase):
        def body(kj, carry):
            step(kj, phase)
            return carry
        lax.fori_loop(a, b, body, 0)

    if n_far is not None:
        walk(lo, n_far, "far")
        lo = n_far
    walk(lo, i, "near")
    step(i, "last")


def _attn_a_kernel(q_ref, qn_ref, k_ref, v_ref, g_ref, o_ref, s_ref, m_ref, acc_ref, *, t):
    i = pl.program_id(1)

    def score(h, next_q, kj):
        q = (qn_ref if next_q else q_ref)[:, h * 256:(h + 1) * 256]
        return _qk(q, _tile(k_ref, kj, t, h * 256, 256))

    def logits(h, s, kj, phase, ctx):
        if phase != "last":
            return s
        causal = lax.broadcasted_iota(jnp.int32, (t, t), 1) <= lax.broadcasted_iota(jnp.int32, (t, t), 0)
        return jnp.where(causal, s, NEG)

    def value(h, kj):
        return _tile(v_ref, kj, t, h * 128, 128)

    _flash_walk(i, lambda qi: 0, None, N_HEADS, score, logits, value, s_ref, m_ref, acc_ref)
    for h in range(N_HEADS):
        _emit(o_ref, g_ref, h, _flash_result(h, acc_ref))


def _attn_band_kernel(q_ref, qn_ref, k_ref, v_ref, g_ref, bank_ref, o_ref, s_ref, m_ref, acc_ref, *, t, near):
    i = pl.program_id(1)

    def score(h, next_q, kj):
        q = (qn_ref if next_q else q_ref)[:, h * 128:(h + 1) * 128]
        return _qk(q, _tile(k_ref, kj, t, h * 128, 128))

    def logits(h, s, kj, phase, ctx):
        return s + _bank_tile(bank_ref, h, i, kj)

    def value(h, kj):
        return _tile(v_ref, kj, t, h * 128, 128)

    _flash_walk(i, lambda qi: jnp.maximum(qi - (near - 1), 0), None, N_HEADS, score, logits, value,
                s_ref, m_ref, acc_ref)
    for h in range(N_HEADS):
        _emit(o_ref, g_ref, h, _flash_result(h, acc_ref))


def _attn_sel_kernel(q_ref, qn_ref, k_ref, v_ref, g_ref, bank_ref, sel_ref, o_ref,
                     s_ref, m_ref, acc_ref, *, t, near):
    i = pl.program_id(1)

    def score(h, next_q, kj):
        q = (qn_ref if next_q else q_ref)[:, h * 128:(h + 1) * 128]
        return _qk(q, _tile(k_ref, kj, t, h * 128, 128))

    def prep(kj):
        return sel_ref[:, pl.ds(pl.multiple_of(kj * t, t), t)].astype(F32)

    def logits(h, s, kj, phase, sel):
        return s + sel if phase == "far" else s + (sel + _bank_tile(bank_ref, h, i, kj))

    def value(h, kj):
        return _tile(v_ref, kj, t, h * 128, 128)

    _flash_walk(i, lambda qi: 0, jnp.maximum(i - (near - 1), 0), N_HEADS, score, logits, value,
                s_ref, m_ref, acc_ref, prep)
    for h in range(N_HEADS):
        _emit(o_ref, g_ref, h, _flash_result(h, acc_ref))


def _attn_diff_kernel(q_ref, qn_ref, k_ref, v_ref, g_ref, bank_ref, lam_ref, gsub_ref, o_ref,
                      s_ref, m_ref, acc_ref, *, t, near, lam_init):
    i = pl.program_id(1)
    first_half = lax.broadcasted_iota(jnp.int32, (t, HEAD_DIM), 1) < HEAD_DIM // 2

    def score(slot, next_q, kj):
        h = slot // 2
        q = (qn_ref if next_q else q_ref)[:, h * 128:(h + 1) * 128]
        keep = first_half if slot % 2 == 0 else jnp.logical_not(first_half)
        q = jnp.where(keep, q, jnp.zeros_like(q))
        return _qk(q, _tile(k_ref, kj, t, h * 128, 128))

    def logits(slot, s, kj, phase, ctx):
        return s if phase == "far" else s + _bank_tile(bank_ref, slot // 2, i, kj)

    def value(slot, kj):
        return _tile(v_ref, kj, t, (slot // 2) * 128, 128)

    _flash_walk(i, lambda qi: 0, jnp.maximum(i - (near - 1), 0), 2 * N_HEADS, score, logits, value,
                s_ref, m_ref, acc_ref)
    lam_v = lam_ref[...]
    lam = (jnp.exp(jnp.sum(lam_v[0:1] * lam_v[1:2], axis=-1, keepdims=True))
           - jnp.exp(jnp.sum(lam_v[2:3] * lam_v[3:4], axis=-1, keepdims=True)) + lam_init)
    for h in range(N_HEADS):
        o = _flash_result(2 * h, acc_ref) - lam * _flash_result(2 * h + 1, acc_ref)
        _emit(o_ref, g_ref, h, _rms(o, gsub_ref[...]) * (1.0 - lam_init))


def _attention(kind, q_arr, k_arr, v_arr, proj, q_blk, gate_blk, extra_in=(), extra_specs=(), **kw):
    b, s, _ = proj.shape
    t = min(ATT_TILE, s)
    dk = 256 if kind == "a" else 128
    qw = N_HEADS * dk
    k_blk = 0 if kind == "a" else q_blk + 1
    v_blk = 0 if kind == "a" else q_blk + 2
    body = {"a": _attn_a_kernel, "band": _attn_band_kernel, "sel": _attn_sel_kernel,
            "diff": _attn_diff_kernel}[kind]
    slots = 2 * N_HEADS if kind == "diff" else N_HEADS
    scratch = [pltpu.VMEM((slots, t, t), F32), pltpu.VMEM((slots, t, HEAD_DIM), F32),
               pltpu.VMEM((slots, t, 2 * HEAD_DIM), F32)]
    last = s // t - 1
    in_specs = [pl.BlockSpec((None, t, qw), lambda bi, i: (bi, i, q_blk)),
                pl.BlockSpec((None, t, qw), lambda bi, i: (bi, jnp.minimum(i + 1, last), q_blk)),
                pl.BlockSpec((None, s, qw), lambda bi, i: (bi, 0, k_blk)),
                pl.BlockSpec((None, s, BRANCH_WIDTH), lambda bi, i: (bi, 0, v_blk)),
                pl.BlockSpec((None, t, BRANCH_WIDTH), lambda bi, i: (bi, i, gate_blk))]
    in_specs += list(extra_specs)
    return pl.pallas_call(
        functools.partial(body, t=t, **kw),
        out_shape=jax.ShapeDtypeStruct((b, s, BRANCH_WIDTH), BF16),
        grid=(b, s // t),
        in_specs=in_specs,
        out_specs=pl.BlockSpec((None, t, BRANCH_WIDTH), lambda bi, i: (bi, i, 0)),
        scratch_shapes=scratch,
        compiler_params=_cparams("arbitrary", "arbitrary"),
        name="attn_" + kind,
    )(q_arr, q_arr, k_arr, v_arr, proj, *extra_in)


def _bit_transpose32(words):
    a = list(words)
    j, m = 16, 0x0000FFFF
    while j:
        for k in range(32):
            if not k & j:
                t = (a[k] ^ lax.shift_right_logical(a[k + j], jnp.int32(j))) & jnp.int32(m)
                a[k] = a[k] ^ t
                a[k + j] = a[k + j] ^ (t << j)
        j >>= 1
        m = (m ^ (m << j)) & 0xFFFFFFFF if j else m
    return a


def _select_kernel(qi_ref, ki_ref, wi_ref, o_ref, key_ref, plane_ref, alive_ref, stat_ref, *, tq, kc, n_sel):
    i = pl.program_id(1)
    s_len = o_ref.shape[1]
    n_ch = (i * tq + tq + kc - 1) // kc
    wpc = kc // 32
    int_min = jnp.int32(-2 ** 31)
    lane = lax.broadcasted_iota(jnp.int32, (tq, 128), 1)
    w_t = wi_ref[...].astype(F32).T
    q_heads = []
    for j in range(IDX_HEADS // 2):
        q2 = qi_ref[:, j * 128:(j + 1) * 128]
        q_heads.append(jnp.where(lane < IDX_DIM, q2, jnp.zeros_like(q2)))
        q_heads.append(jnp.where(lane >= IDX_DIM, q2, jnp.zeros_like(q2)))
    kiota = lax.broadcasted_iota(jnp.int32, (kc, tq), 0)

    def chunk(c):
        return pl.ds(pl.multiple_of(c * kc, kc), kc)

    def scored(k, q_from):
        acc = jnp.zeros((k.shape[0], tq - q_from), F32)
        for hh in range(IDX_HEADS):
            acc = acc + jnp.maximum(_qk(k, q_heads[hh][q_from:, :]), 0.0) * w_t[hh:hh + 1, q_from:]
        return acc + 0.0

    def causal(acc):
        key = lax.broadcasted_iota(jnp.int32, acc.shape, 0)
        return jnp.where(key <= lax.broadcasted_iota(jnp.int32, acc.shape, 1), acc, NEG)

    def store_keys(row0, acc):
        bits = pltpu.bitcast(acc, jnp.int32)
        keys = bits ^ ((bits >> 31) & jnp.int32(0x7FFFFFFF))
        key_ref[pl.ds(row0, acc.shape[0]), :] = keys
        ukeys = keys ^ int_min
        for blk in range(acc.shape[0] // 256):
            planes = _bit_transpose32([ukeys[blk * 256 + 8 * j:blk * 256 + 8 * j + 8, :] for j in range(32)])
            row = pl.multiple_of(row0 // 32 + blk * 8, 8)
            for b in range(32):
                plane_ref[b, pl.ds(row, 8), :] = planes[b]

    def score_chunk(c, _):
        store_keys(pl.multiple_of(c * kc, kc), scored(ki_ref[chunk(c), :], 0))
        return 0

    lax.fori_loop(0, i, score_chunk, 0)
    diag = pl.multiple_of(i * kc, kc)
    half = kc // 2
    if half % 256 == 0:
        store_keys(diag, causal(scored(ki_ref[pl.ds(diag, half), :], 0)))
        late = causal(scored(ki_ref[pl.ds(diag + half, half), :], half))
        store_keys(diag + half, jnp.concatenate([jnp.full((half, half), NEG, F32), late], axis=1))
    else:
        store_keys(diag, causal(scored(ki_ref[pl.ds(diag, kc), :], 0)))

    def clear_chunk(c, _):
        plane_ref[:, pl.ds(pl.multiple_of(c * wpc, wpc), wpc), :] = jnp.zeros((32, wpc, tq), jnp.int32)
        return 0

    @pl.when(i == 0)
    def _():
        lax.fori_loop(n_ch, s_len // kc, clear_chunk, 0)

    n_rows = s_len // 32
    word_row = lax.broadcasted_iota(jnp.int32, (n_rows, tq), 0)
    alive_ref[...] = jnp.where(word_row < n_ch * wpc, jnp.int32(-1), jnp.int32(0))
    zeros = jnp.zeros((1, tq), jnp.int32)

    def radix(rows):
        def radix_step(bi, carry):
            thr, above = carry
            alive = alive_ref[0:rows]
            plane = plane_ref[bi, 0:rows]
            ones = lax.population_count(alive & plane)
            ones = jnp.sum(jnp.sum(ones.reshape(rows // 8, 8, tq), axis=0), axis=0, keepdims=True)
            take = above + ones >= n_sel
            thr = jnp.where(take, thr | (jnp.int32(1) << (31 - bi)), thr)
            above = jnp.where(take, above, above + ones)
            alive_ref[0:rows] = alive & (plane ^ jnp.where(take, jnp.int32(0), jnp.int32(-1)))
            return thr, above

        thr, above = lax.fori_loop(0, 32, radix_step, (zeros, zeros))
        stat_ref[0:8] = jnp.broadcast_to(thr, (8, tq))
        stat_ref[8:16] = jnp.broadcast_to(above, (8, tq))

    quarter = max(n_rows // 4, 8)
    sizes = sorted({min(quarter * (n + 1), n_rows) for n in range(4)})
    for n, rows in enumerate(sizes):
        lower = sizes[n - 1] if n else 0
        pl.when((n_ch * wpc > lower) & (n_ch * wpc <= rows))(functools.partial(radix, rows))
    thr_u, above = stat_ref[0:1], stat_ref[8:9]
    thr = thr_u ^ int_min
    n_equal = lax.population_count(alive_ref[...])
    n_equal = jnp.sum(jnp.sum(n_equal.reshape(n_rows // 8, 8, tq), axis=0), axis=0, keepdims=True)
    need = n_sel - above
    masked_key = int(np.float32(NEG).view(np.int32)) ^ 0x7FFFFFFF
    tie = (n_equal > need) & (thr != masked_key)
    any_tie = jnp.max(jnp.where(tie, 1, 0)) > 0

    @pl.when(jnp.logical_not(any_tie))
    def _():
        def emit(c, _):
            keep = jnp.where(key_ref[chunk(c), :] >= thr, 0.0, NEG)
            o_ref[:, chunk(c)] = keep.T.astype(BF16)
            return 0

        lax.fori_loop(0, n_ch, emit, 0)

    @pl.when(any_tie)
    def _():
        def equal_below(bound):
            def body(c, part):
                hit = jnp.where((key_ref[chunk(c), :] == thr) & (c * kc + kiota < bound), 1, 0)
                return part + jnp.sum(hit.reshape(kc // 8, 8, tq), axis=0)
            part = lax.fori_loop(0, n_ch, body, jnp.zeros((8, tq), jnp.int32))
            return jnp.sum(part, axis=0, keepdims=True)

        n_bits = s_len.bit_length()

        def bound_step(bi, cut):
            cand = cut + (jnp.int32(1) << (n_bits - 1 - bi))
            ok = (cand <= s_len) & (equal_below(cand) <= need)
            return jnp.where(ok, cand, cut)

        cut = lax.fori_loop(0, n_bits, bound_step, zeros)

        def emit(c, _):
            keys = key_ref[chunk(c), :]
            kept = (keys > thr) | ((keys == thr) & (c * kc + kiota < cut))
            o_ref[:, chunk(c)] = jnp.where(kept, 0.0, NEG).T.astype(BF16)
            return 0

        lax.fori_loop(0, n_ch, emit, 0)

    def blank(c, _):
        o_ref[:, chunk(c)] = jnp.full((tq, kc), NEG, BF16)
        return 0

    lax.fori_loop(n_ch, s_len // kc, blank, 0)


def _select(proj):
    b, s, _ = proj.shape
    tq = kc = min(512, s)
    n_sel = min(TOPK_MAX, s // 4)
    return pl.pallas_call(
        functools.partial(_select_kernel, tq=tq, kc=kc, n_sel=n_sel),
        out_shape=jax.ShapeDtypeStruct((b, s, s), BF16),
        grid=(b, s // tq),
        in_specs=[pl.BlockSpec((None, tq, 1024), lambda bi, i: (bi, i, OFF_QIDX // 1024)),
                  pl.BlockSpec((None, s, 128), lambda bi, i: (bi, 0, OFF_KIDX // 128)),
                  pl.BlockSpec((None, tq, 128), lambda bi, i: (bi, i, OFF_WIDX // 128))],
        out_specs=pl.BlockSpec((None, tq, s), lambda bi, i: (bi, i, 0)),
        scratch_shapes=[pltpu.VMEM((s, tq), jnp.int32), pltpu.VMEM((32, s // 32, tq), jnp.int32),
                        pltpu.VMEM((s // 32, tq), jnp.int32), pltpu.VMEM((16, tq), jnp.int32)],
        compiler_params=_cparams("arbitrary", "arbitrary"),
        name="idx_select",
    )(proj, proj, proj)


def _out_kernel(a_ref, b_ref, c_ref, d_ref, w_ref, x_ref, gate_ref, g_ref, o_ref, wb_ref):
    @pl.when((pl.program_id(0) == 0) & (pl.program_id(1) == 0))
    def _():
        wb_ref[...] = w_ref[...].astype(BF16)

    y = jnp.dot(a_ref[...], wb_ref[0:512, :], preferred_element_type=F32)
    y += jnp.dot(b_ref[...], wb_ref[512:1024, :], preferred_element_type=F32)
    y += jnp.dot(c_ref[...], wb_ref[1024:1536, :], preferred_element_type=F32)
    y += jnp.dot(d_ref[...], wb_ref[1536:2048, :], preferred_element_type=F32)
    o_ref[...] = x_ref[...] + gate_ref[...] * _rms(y, g_ref[...])


def _out_proj(outs, w_out, li, x, mod3, g_post):
    b, s, d = x.shape
    tm = min(512, s)
    mix = lambda bi, i: (bi, i, 0)
    return pl.pallas_call(
        _out_kernel,
        out_shape=jax.ShapeDtypeStruct((b, s, d), F32),
        grid=(b, s // tm),
        in_specs=[pl.BlockSpec((None, tm, BRANCH_WIDTH), mix)] * 4
        + [pl.BlockSpec((None,) + w_out.shape[1:], lambda bi, i: (li, 0, 0), pipeline_mode=pl.Buffered(1)),
           pl.BlockSpec((None, tm, d), mix),
           pl.BlockSpec((None, 1, d), lambda bi, i: (bi, 0, 2)),
           pl.BlockSpec((1, d), lambda bi, i: (0, 0))],
        out_specs=pl.BlockSpec((None, tm, d), mix),
        scratch_shapes=[pltpu.VMEM(w_out.shape[1:], BF16)],
        compiler_params=_cparams("arbitrary", "arbitrary"),
        name="out_proj",
    )(*outs, w_out, x, mod3, g_post.reshape(1, d))


def _rope_tables(s):
    half = MLA_ROPE // 2
    inv = ROPE_THETA ** (-jnp.arange(half, dtype=F32) / half)
    ang = jnp.arange(s, dtype=F32)[:, None] * inv[None, :]
    z = jnp.zeros((s, 128 - MLA_ROPE), F32)
    cos, sin = jnp.cos(ang), jnp.sin(ang)
    cat = lambda *parts: jnp.concatenate(parts, axis=-1)
    return cat(cos, cos, z), cat(sin, sin, z), cat(cos, cos, sin, sin)


def _rot_cols(w):
    half = w.shape[-1] // 2
    return jnp.concatenate([-w[..., half:], w[..., :half]], axis=-1)


IN_SPLITS = (("a_cq", 384), ("a_ckv", 256), ("a_krope", 64), ("b_q", 512), ("b_k", 512), ("b_v", 512),
             ("c_q", 512), ("c_k", 512), ("c_v", 512), ("c_qidx", 1024), ("c_kidx", 64), ("c_widx", 16),
             ("d_q", 512), ("d_k", 512), ("d_v", 512), ("gate", 2048))
IN_WIDTH = sum(width for _, width in IN_SPLITS)


def _layout_w_in_kernel(w_ref, o_ref):
    src, start = {}, 0
    for name, width in IN_SPLITS:
        src[name] = start
        start += width
    tk = w_ref.shape[1]

    def rows(name, width, offset=0):
        a = src[name] + offset
        return w_ref[a:a + width, :]

    def put(dst, val, scale=None):
        for r in range(0, val.shape[0], 512):
            piece = val[r:r + 512]
            if scale is not None:
                piece = piece * scale
            o_ref[:, dst + r:dst + r + piece.shape[0]] = piece.T.astype(BF16)

    z64 = jnp.zeros((64, tk), F32)
    half = MLA_ROPE // 2
    put(OFF_QIDX, rows("c_qidx", 1024), IDX_DIM ** -0.5)
    put(OFF_A, rows("a_cq", MLA_Q_RANK + MLA_KV_RANK))
    put(OFF_A + 640, jnp.concatenate(
        [rows("a_krope", MLA_ROPE), z64, -rows("a_krope", half, half), rows("a_krope", half), z64,
         rows("c_kidx", IDX_DIM), rows("c_kidx", IDX_DIM)], axis=0))
    for off, name, dim in ((OFF_B, "b", HEAD_DIM), (OFF_C, "c", HEAD_DIM), (OFF_D, "d", HEAD_DIM // 2)):
        put(off, rows(name + "_q", BRANCH_WIDTH), LOG2E * dim ** -0.5)
        put(off + BRANCH_WIDTH, rows(name + "_k", 2 * BRANCH_WIDTH))
    put(OFF_GATE, rows("gate", MIX_WIDTH))
    put(OFF_WIDX, jnp.concatenate([rows("c_widx", IDX_HEADS) * IDX_HEADS ** -0.5,
                                   jnp.zeros((128 - IDX_HEADS, tk), F32)], axis=0))


def _layout_w_in(w_in_t, li):
    d = w_in_t.shape[2]
    tk = 256
    return pl.pallas_call(
        _layout_w_in_kernel,
        out_shape=jax.ShapeDtypeStruct((d, PROJ_WIDTH), BF16),
        grid=(d // tk,),
        in_specs=[pl.BlockSpec((None, IN_WIDTH, tk), lambda i: (li, 0, i))],
        out_specs=pl.BlockSpec((tk, PROJ_WIDTH), lambda i: (i, 0)),
        compiler_params=_cparams("arbitrary"),
        name="layout_w_in",
    )(w_in_t)


def _layout_w_uq(w):
    r = w.shape[0]
    w = w.reshape(r, N_HEADS, MLA_NOPE + MLA_ROPE) * (LOG2E * (MLA_NOPE + MLA_ROPE) ** -0.5)
    rope = w[..., MLA_NOPE:]
    return jnp.concatenate([w[..., :MLA_NOPE], rope, _rot_cols(rope)], axis=-1).reshape(r, -1).astype(BF16)


def kernel(x, c, w_ada, b_ada, g_pre, g_post, w_in, g_q_a, w_uq_a, g_kv_a, w_ukv_a,
           lam_q1, lam_k1, lam_q2, lam_k2, g_sub_d, w_out, rel_bias):
    b, s, d = x.shape
    depth = w_ada.shape[0]
    t = min(ATT_TILE, s)
    half = t // 2
    nq = s // t
    near_bias = min(nq, -(-(_first_far_diagonal(half) + 1) // 2))
    near_band = min(nq, -(-(DILATED_PATTERNS[-1][0] // half + 1) // 2))

    rope_tabs = _rope_tables(s)
    bank_b = _bank(rel_bias[:, 0:N_HEADS], 2 * near_band + 1, half, -1, True)
    bank_cd = _bank(rel_bias[:, N_HEADS:3 * N_HEADS], 2 * near_bias + 1, half, -1, False)
    bank_spec = lambda n, group=0: pl.BlockSpec((N_HEADS, n, half, half), lambda bi, i: (group, 0, 0, 0),
                                                pipeline_mode=pl.Buffered(1))

    w_in_t = jnp.swapaxes(w_in, 1, 2)
    mod = _ada_mod(c, w_ada, b_ada)
    for li in range(depth):
        mod3 = mod[li].reshape(b, 1, 3 * d)
        proj = _in_proj(x, g_pre[li], mod3, _layout_w_in(w_in_t, li))

        q_a, k_a, v_a = _mla_prep(proj, rope_tabs, g_q_a[li], g_kv_a[li],
                                  _layout_w_uq(w_uq_a[li]), w_ukv_a[li].astype(BF16))
        gate0 = OFF_GATE // BRANCH_WIDTH
        out_a = _attention("a", q_a, k_a, v_a, proj, 0, gate0)
        out_b = _attention("band", proj, proj, proj, proj, OFF_B // BRANCH_WIDTH, gate0 + 1,
                           extra_in=(bank_b,), extra_specs=(bank_spec(2 * near_band + 1),), near=near_band)
        sel = _select(proj)
        out_c = _attention("sel", proj, proj, proj, proj, OFF_C // BRANCH_WIDTH, gate0 + 2,
                           extra_in=(bank_cd, sel),
                           extra_specs=(bank_spec(2 * near_bias + 1, 0),
                                        pl.BlockSpec((None, t, s), lambda bi, i: (bi, i, 0))),
                           near=near_bias)
        lam_init = 0.8 - 0.6 * math.exp(-0.3 * li)
        lam_vecs = jnp.stack([lam_q1[li], lam_k1[li], lam_q2[li], lam_k2[li]])
        out_d = _attention("diff", proj, proj, proj, proj, OFF_D // BRANCH_WIDTH, gate0 + 3,
                           extra_in=(bank_cd, lam_vecs, g_sub_d[li].reshape(1, HEAD_DIM)),
                           extra_specs=(bank_spec(2 * near_bias + 1, 1),
                                        pl.BlockSpec(lam_vecs.shape, lambda bi, i: (0, 0)),
                                        pl.BlockSpec((1, HEAD_DIM), lambda bi, i: (0, 0))),
                           near=near_bias, lam_init=lam_init)
        x = _out_proj((out_a, out_b, out_c, out_d), w_out, li, x, mod3, g_post[li])
    return x
```

```python
import functools
import math

import numpy as np
import jax
import jax.numpy as jnp
from jax import lax
from jax.experimental import pallas as pl
from jax.experimental.pallas import tpu as pltpu

F32 = jnp.float32
BF16 = jnp.bfloat16

HEAD_DIM = 128
N_HEADS = 4
BRANCH_WIDTH = N_HEADS * HEAD_DIM
MIX_WIDTH = 4 * BRANCH_WIDTH
MLA_Q_RANK = 384
MLA_KV_RANK = 256
MLA_NOPE = 128
MLA_ROPE = 64
ROPE_THETA = 10000.0
DILATED_PATTERNS = ((128, 1), (512, 4), (2048, 16))
IDX_HEADS = 16
IDX_DIM = 64
TOPK_MAX = 256
REL_BUCKETS = 32
REL_MAX_DIST = 2048
NORM_EPS = 1e-6
NEG = -1e30
LOG2E = math.log2(math.e)

OFF_QIDX = 0
OFF_A = 1024
OFF_WIDX = OFF_A + 640
WIDX_LANE = 64
OFF_KIDX = OFF_A + 896
OFF_B = 2048
OFF_C = 3584
OFF_D = 5120
OFF_GATE = 6656
PROJ_WIDTH = 8704

ATT_TILE = 512
V7X_VMEM_BYTES = 64 * 1024 * 1024
VMEM_LIMIT = V7X_VMEM_BYTES // 8 * 7


def _cparams(*sem):
    return pltpu.CompilerParams(dimension_semantics=sem, vmem_limit_bytes=VMEM_LIMIT)


def _split_bf16(x):
    hi = x.astype(BF16)
    return hi, (x - hi.astype(F32)).astype(BF16)


def _ada_kernel(c_ref, w_ref, b_ref, o_ref):
    c = c_ref[...]
    a_hi, a_lo = _split_bf16(c * jax.nn.sigmoid(c))
    w_hi, w_lo = _split_bf16(w_ref[...])
    dot = functools.partial(jnp.dot, preferred_element_type=F32)
    o_ref[...] = dot(a_hi, w_hi) + (dot(a_hi, w_lo) + dot(a_lo, w_hi)) + b_ref[...]


def _ada_mod(c, w_ada, b_ada):
    depth, d, n = w_ada.shape
    b = c.shape[0]
    tn = 768
    return pl.pallas_call(
        _ada_kernel,
        out_shape=jax.ShapeDtypeStruct((depth, b, n), F32),
        grid=(depth, n // tn),
        in_specs=[pl.BlockSpec((b, d), lambda l, j: (0, 0)),
                  pl.BlockSpec((None, d, tn), lambda l, j: (l, 0, j)),
                  pl.BlockSpec((None, 1, tn), lambda l, j: (l, 0, j))],
        out_specs=pl.BlockSpec((None, b, tn), lambda l, j: (l, 0, j)),
        compiler_params=_cparams("arbitrary", "arbitrary"),
        name="ada_mod",
    )(c, w_ada, b_ada.reshape(depth, 1, n))


def _in_proj_kernel(x_ref, g_ref, shift_ref, scale_ref, w_ref, o_ref):
    x = x_ref[...]
    y = x * lax.rsqrt(jnp.mean(x * x, axis=-1, keepdims=True) + NORM_EPS) * g_ref[...]
    h = (y * (1.0 + scale_ref[...]) + shift_ref[...]).astype(BF16)
    o_ref[...] = jnp.dot(h, w_ref[...], preferred_element_type=F32).astype(o_ref.dtype)


def _in_proj(x, g_pre, mod3, w):
    b, s, d = x.shape
    n = w.shape[1]
    tm = min(512, s)
    nt = s // tm
    tn = n // 2
    row = lambda j, i: (i // nt, i % nt, 0)
    return pl.pallas_call(
        _in_proj_kernel,
        out_shape=jax.ShapeDtypeStruct((b, s, n), BF16),
        grid=(n // tn, b * nt),
        in_specs=[pl.BlockSpec((None, tm, d), row),
                  pl.BlockSpec((1, d), lambda j, i: (0, 0)),
                  pl.BlockSpec((None, 1, d), lambda j, i: (i // nt, 0, 0)),
                  pl.BlockSpec((None, 1, d), lambda j, i: (i // nt, 0, 1)),
                  pl.BlockSpec((d, tn), lambda j, i: (0, j), pipeline_mode=pl.Buffered(1))],
        out_specs=pl.BlockSpec((None, tm, tn), lambda j, i: (i // nt, i % nt, j)),
        compiler_params=_cparams("arbitrary", "arbitrary"),
        name="in_proj",
    )(x, g_pre.reshape(1, d), mod3, mod3, w)


def _rms(x, g):
    return x * lax.rsqrt(jnp.mean(x * x, axis=-1, keepdims=True) + NORM_EPS) * g


def _mla_prep_kernel(p_ref, cos_ref, sin_ref, cs_ref, gq_ref, gkv_ref, wq_ref, wkv_ref, q_ref, k_ref, v_ref):
    cos = cos_ref[...]
    sin = sin_ref[...]
    cs = cs_ref[...]
    cq = _rms(p_ref[:, 0:MLA_Q_RANK].astype(F32), gq_ref[...]).astype(BF16)
    ckv = _rms(p_ref[:, MLA_Q_RANK:MLA_Q_RANK + MLA_KV_RANK].astype(F32), gkv_ref[...]).astype(BF16)
    q = jnp.dot(cq, wq_ref[...], preferred_element_type=F32)
    kv = jnp.dot(ckv, wkv_ref[...], preferred_element_type=F32)
    k_rope = (p_ref[:, 640:768].astype(F32) * cos + p_ref[:, 768:896].astype(F32) * sin).astype(BF16)
    for h in range(N_HEADS):
        q_ref[:, h * 256:h * 256 + 128] = q[:, h * 256:h * 256 + 128].astype(BF16)
        z = q[:, h * 256 + 128:(h + 1) * 256] * cs
        q_ref[:, h * 256 + 128:(h + 1) * 256] = (z + pltpu.roll(z, MLA_ROPE, axis=1)).astype(BF16)
        k_ref[:, h * 256:h * 256 + 128] = kv[:, h * 256:h * 256 + 128].astype(BF16)
        k_ref[:, h * 256 + 128:(h + 1) * 256] = k_rope
        v_ref[:, h * 128:(h + 1) * 128] = kv[:, h * 256 + 128:(h + 1) * 256].astype(BF16)


def _mla_prep(proj, rope_tabs, g_q, g_kv, wq, wkv):
    b, s, _ = proj.shape
    tm = min(512, s)
    const = lambda bi, i: (0, 0)
    return pl.pallas_call(
        _mla_prep_kernel,
        out_shape=(jax.ShapeDtypeStruct((b, s, N_HEADS * 256), BF16),
                   jax.ShapeDtypeStruct((b, s, N_HEADS * 256), BF16),
                   jax.ShapeDtypeStruct((b, s, BRANCH_WIDTH), BF16)),
        grid=(b, s // tm),
        in_specs=[pl.BlockSpec((None, tm, 1024), lambda bi, i: (bi, i, OFF_A // 1024)),
                  pl.BlockSpec((tm, 128), lambda bi, i: (i, 0)),
                  pl.BlockSpec((tm, 128), lambda bi, i: (i, 0)),
                  pl.BlockSpec((tm, 128), lambda bi, i: (i, 0)),
                  pl.BlockSpec((1, MLA_Q_RANK), const),
                  pl.BlockSpec((1, MLA_KV_RANK), const),
                  pl.BlockSpec(wq.shape, const),
                  pl.BlockSpec(wkv.shape, const)],
        out_specs=(pl.BlockSpec((None, tm, N_HEADS * 256), lambda bi, i: (bi, i, 0)),
                   pl.BlockSpec((None, tm, N_HEADS * 256), lambda bi, i: (bi, i, 0)),
                   pl.BlockSpec((None, tm, BRANCH_WIDTH), lambda bi, i: (bi, i, 0))),
        compiler_params=_cparams("arbitrary", "arbitrary"),
        name="mla_prep",
    )(proj, *rope_tabs, g_q.reshape(1, -1), g_kv.reshape(1, -1), wq, wkv)


def _bucket_np(n):
    max_exact = REL_BUCKETS // 2
    nf = np.maximum(n, max_exact).astype(np.float32)
    large = max_exact + (np.log(nf / np.float32(max_exact)) / np.float32(math.log(REL_MAX_DIST / max_exact))
                         * np.float32(REL_BUCKETS - max_exact)).astype(np.int32)
    return np.where(n < max_exact, n, np.minimum(large, REL_BUCKETS - 1)).astype(np.int32)


def _bucket_starts():
    buckets = _bucket_np(np.arange(2 * REL_MAX_DIST))
    return [int(np.argmax(buckets >= b)) for b in range(REL_BUCKETS)]


def _bank_kernel(tab_ref, o_ref, *, t, d_min, band):
    h = pl.program_id(0)
    starts = _bucket_starts()
    row = lax.broadcasted_iota(jnp.int32, (t, t), 0)
    col = lax.broadcasted_iota(jnp.int32, (t, t), 1)
    for j in range(o_ref.shape[0]):
        d = d_min + j
        if d < 0:
            o_ref[j] = jnp.full((t, t), NEG, F32)
            continue
        dist = t * d + row - col
        b_lo, b_hi = (int(x) for x in _bucket_np(np.array([max(t * d - t + 1, 0), t * d + t - 1])))
        val = jnp.full((t, t), tab_ref[b_lo, h], F32)
        for b in range(b_lo + 1, b_hi + 1):
            val = jnp.where(dist >= starts[b], tab_ref[b, h], val)
        if band:
            mult = jnp.zeros((t, t), jnp.int32)
            for window, dil in DILATED_PATTERNS:
                mult += jnp.where((dist >= 0) & (dist <= window) & ((dist & (dil - 1)) == 0), 1, 0)
            val = val + jnp.where(mult == 3, math.log(3.0), jnp.where(mult == 2, math.log(2.0), 0.0))
            keep = mult > 0
        else:
            val = val - tab_ref[REL_BUCKETS - 1, h]
            keep = dist >= 0
        o_ref[j] = jnp.where(keep, val * LOG2E, NEG)


def _bank(tab, n_tables, t, d_min, band):
    nh = tab.shape[1]
    return pl.pallas_call(
        functools.partial(_bank_kernel, t=t, d_min=d_min, band=band),
        out_shape=jax.ShapeDtypeStruct((nh, n_tables, t, t), F32),
        grid=(nh,),
        in_specs=[pl.BlockSpec(memory_space=pltpu.SMEM)],
        out_specs=pl.BlockSpec((None, n_tables, t, t), lambda h: (h, 0, 0, 0)),
        compiler_params=_cparams("arbitrary"),
        name="bank_band" if band else "bank_bias",
    )(tab)


def _first_far_diagonal(t):
    last = _bucket_starts()[REL_BUCKETS - 1]
    return -(-(last + t - 1) // t)


def _qk(q, k):
    return lax.dot_general(q, k, (((1,), (1,)), ((), ())), preferred_element_type=F32)


def _flash_init(m_ref, acc_ref):
    m_ref[...] = jnp.full(m_ref.shape, NEG, F32)
    acc_ref[...] = jnp.zeros(acc_ref.shape, F32)


def _flash_update(slot, s, v, m_ref, acc_ref, s_ref, next_scores, diagonal=False):
    t = s.shape[0]
    v_ones = jnp.concatenate([v, jnp.ones_like(v)], axis=1)
    blocks = [(0, t // 2, t // 2), (t // 2, t, t)] if diagonal else [(0, t, t)]
    for n, (r0, r1, keys) in enumerate(blocks):
        m_prev = m_ref[slot, r0:r1]
        m_new = jnp.maximum(m_prev, jnp.max(s[r0:r1, :keys], axis=-1, keepdims=True))
        alpha = jnp.exp2(m_prev - m_new)
        p = jnp.concatenate([jnp.exp2(s[r0:r1, j * 128:(j + 1) * 128] - m_new) for j in range(keys // 128)],
                            axis=1).astype(BF16)
        if n == len(blocks) - 1 and next_scores is not None:
            s_ref[slot] = next_scores()
        acc_ref[slot, r0:r1] = (jnp.concatenate([alpha, alpha], axis=1) * acc_ref[slot, r0:r1]
                                + jnp.dot(p, v_ones[:keys], preferred_element_type=F32))
        m_ref[slot, r0:r1] = m_new


def _flash_result(slot, acc_ref):
    acc = acc_ref[slot]
    return acc[:, :HEAD_DIM] / acc[:, HEAD_DIM:]


def _silu(g):
    return g * jax.nn.sigmoid(g)


def _tile(ref, kj, t, c0, width):
    start = kj * t if isinstance(kj, int) else pl.multiple_of(kj * t, t)
    return ref[pl.ds(start, t), c0:c0 + width]


def _bank_tile(bank_ref, h, i, kj):
    base = 2 * (i - kj) + 1
    top = jnp.concatenate([bank_ref[h, base], bank_ref[h, base - 1]], axis=1)
    bot = jnp.concatenate([bank_ref[h, base + 1], bank_ref[h, base]], axis=1)
    return jnp.concatenate([top, bot], axis=0)


def _emit(o_ref, g_ref, h, o):
    gate = g_ref[:, h * 128:(h + 1) * 128].astype(F32)
    o_ref[:, h * 128:(h + 1) * 128] = (o * _silu(gate)).astype(BF16)


def _flash_walk(i, first_key, n_far, slots, score, logits, value, s_ref, m_ref, acc_ref, prep=None):
    lo = first_key(i)
    i_next = jnp.minimum(i + 1, pl.num_programs(1) - 1)
    _flash_init(m_ref, acc_ref)

    @pl.when(i == 0)
    def _():
        for slot in range(slots):
            s_ref[slot] = score(slot, False, lo)

    def step(kj, phase):
        ctx = prep(kj) if prep is not None else None
        for slot in range(slots):
            s = logits(slot, s_ref[slot], kj, phase, ctx)
            if phase == "last":
                nxt = functools.partial(score, slot, True, first_key(i_next))
            else:
                nxt = functools.partial(score, slot, False, kj + 1)
            _flash_update(slot, s, value(slot, kj), m_ref, acc_ref, s_ref, nxt, diagonal=phase == "last")

    def walk(a, b, phase):
        def body(kj, carry):
            step(kj, phase)
            return carry
        lax.fori_loop(a, b, body, 0)

    if n_far is not None:
        walk(lo, n_far, "far")
        lo = n_far
    walk(lo, i, "near")
    step(i, "last")


def _attn_a_kernel(q_ref, qn_ref, k_ref, v_ref, g_ref, o_ref, s_ref, m_ref, acc_ref, *, t):
    i = pl.program_id(1)

    def score(h, next_q, kj):
        q = (qn_ref if next_q else q_ref)[:, h * 256:(h + 1) * 256]
        return _qk(q, _tile(k_ref, kj, t, h * 256, 256))

    def logits(h, s, kj, phase, ctx):
        if phase != "last":
            return s
        causal = lax.broadcasted_iota(jnp.int32, (t, t), 1) <= lax.broadcasted_iota(jnp.int32, (t, t), 0)
        return jnp.where(causal, s, NEG)

    def value(h, kj):
        return _tile(v_ref, kj, t, h * 128, 128)

    _flash_walk(i, lambda qi: 0, None, N_HEADS, score, logits, value, s_ref, m_ref, acc_ref)
    for h in range(N_HEADS):
        _emit(o_ref, g_ref, h, _flash_result(h, acc_ref))


def _attn_band_kernel(q_ref, qn_ref, k_ref, v_ref, g_ref, bank_ref, o_ref, s_ref, m_ref, acc_ref, *, t, near):
    i = pl.program_id(1)

    def score(h, next_q, kj):
        q = (qn_ref if next_q else q_ref)[:, h * 128:(h + 1) * 128]
        return _qk(q, _tile(k_ref, kj, t, h * 128, 128))

    def logits(h, s, kj, phase, ctx):
        return s + _bank_tile(bank_ref, h, i, kj)

    def value(h, kj):
        return _tile(v_ref, kj, t, h * 128, 128)

    _flash_walk(i, lambda qi: jnp.maximum(qi - (near - 1), 0), None, N_HEADS, score, logits, value,
                s_ref, m_ref, acc_ref)
    for h in range(N_HEADS):
        _emit(o_ref, g_ref, h, _flash_result(h, acc_ref))


def _attn_sel_kernel(q_ref, qn_ref, k_ref, v_ref, g_ref, bank_ref, sel_ref, o_ref,
                     s_ref, m_ref, acc_ref, *, t, near):
    i = pl.program_id(1)

    def score(h, next_q, kj):
        q = (qn_ref if next_q else q_ref)[:, h * 128:(h + 1) * 128]
        return _qk(q, _tile(k_ref, kj, t, h * 128, 128))

    def prep(kj):
        return sel_ref[:, pl.ds(pl.multiple_of(kj * t, t), t)].astype(F32)

    def logits(h, s, kj, phase, sel):
        return s + sel if phase == "far" else s + (sel + _bank_tile(bank_ref, h, i, kj))

    def value(h, kj):
        return _tile(v_ref, kj, t, h * 128, 128)

    _flash_walk(i, lambda qi: 0, jnp.maximum(i - (near - 1), 0), N_HEADS, score, logits, value,
                s_ref, m_ref, acc_ref, prep)
    for h in range(N_HEADS):
        _emit(o_ref, g_ref, h, _flash_result(h, acc_ref))


def _attn_diff_kernel(q_ref, qn_ref, k_ref, v_ref, g_ref, bank_ref, lam_ref, gsub_ref, o_ref,
                      s_ref, m_ref, acc_ref, *, t, near, lam_init):
    i = pl.program_id(1)
    first_half = lax.broadcasted_iota(jnp.int32, (t, HEAD_DIM), 1) < HEAD_DIM // 2

    def score(slot, next_q, kj):
        h = slot // 2
        q = (qn_ref if next_q else q_ref)[:, h * 128:(h + 1) * 128]
        keep = first_half if slot % 2 == 0 else jnp.logical_not(first_half)
        q = jnp.where(keep, q, jnp.zeros_like(q))
        return _qk(q, _tile(k_ref, kj, t, h * 128, 128))

    def logits(slot, s, kj, phase, ctx):
        return s if phase == "far" else s + _bank_tile(bank_ref, slot // 2, i, kj)

    def value(slot, kj):
        return _tile(v_ref, kj, t, (slot // 2) * 128, 128)

    _flash_walk(i, lambda qi: 0, jnp.maximum(i - (near - 1), 0), 2 * N_HEADS, score, logits, value,
                s_ref, m_ref, acc_ref)
    lam_v = lam_ref[...]
    lam = (jnp.exp(jnp.sum(lam_v[0:1] * lam_v[1:2], axis=-1, keepdims=True))
           - jnp.exp(jnp.sum(lam_v[2:3] * lam_v[3:4], axis=-1, keepdims=True)) + lam_init)
    for h in range(N_HEADS):
        o = _flash_result(2 * h, acc_ref) - lam * _flash_result(2 * h + 1, acc_ref)
        _emit(o_ref, g_ref, h, _rms(o, gsub_ref[...]) * (1.0 - lam_init))


def _attention(kind, q_arr, k_arr, v_arr, proj, q_blk, gate_blk, extra_in=(), extra_specs=(), **kw):
    b, s, _ = proj.shape
    t = min(ATT_TILE, s)
    dk = 256 if kind == "a" else 128
    qw = N_HEADS * dk
    k_blk = 0 if kind == "a" else q_blk + 1
    v_blk = 0 if kind == "a" else q_blk + 2
    body = {"a": _attn_a_kernel, "band": _attn_band_kernel, "sel": _attn_sel_kernel,
            "diff": _attn_diff_kernel}[kind]
    slots = 2 * N_HEADS if kind == "diff" else N_HEADS
    scratch = [pltpu.VMEM((slots, t, t), F32), pltpu.VMEM((slots, t, HEAD_DIM), F32),
               pltpu.VMEM((slots, t, 2 * HEAD_DIM), F32)]
    last = s // t - 1
    in_specs = [pl.BlockSpec((None, t, qw), lambda bi, i: (bi, i, q_blk)),
                pl.BlockSpec((None, t, qw), lambda bi, i: (bi, jnp.minimum(i + 1, last), q_blk)),
                pl.BlockSpec((None, s, qw), lambda bi, i: (bi, 0, k_blk)),
                pl.BlockSpec((None, s, BRANCH_WIDTH), lambda bi, i: (bi, 0, v_blk)),
                pl.BlockSpec((None, t, BRANCH_WIDTH), lambda bi, i: (bi, i, gate_blk))]
    in_specs += list(extra_specs)
    return pl.pallas_call(
        functools.partial(body, t=t, **kw),
        out_shape=jax.ShapeDtypeStruct((b, s, BRANCH_WIDTH), BF16),
        grid=(b, s // t),
        in_specs=in_specs,
        out_specs=pl.BlockSpec((None, t, BRANCH_WIDTH), lambda bi, i: (bi, i, 0)),
        scratch_shapes=scratch,
        compiler_params=_cparams("arbitrary", "arbitrary"),
        name="attn_" + kind,
    )(q_arr, q_arr, k_arr, v_arr, proj, *extra_in)


def _bit_transpose32(words):
    a = list(words)
    j, m = 16, 0x0000FFFF
    while j:
        for k in range(32):
            if not k & j:
                t = (a[k] ^ lax.shift_right_logical(a[k + j], jnp.int32(j))) & jnp.int32(m)
                a[k] = a[k] ^ t
                a[k + j] = a[k + j] ^ (t << j)
        j >>= 1
        m = (m ^ (m << j)) & 0xFFFFFFFF if j else m
    return a


def _select_kernel(qi_ref, ki_ref, wi_ref, o_ref, key_ref, plane_ref, alive_ref, stat_ref, *, tq, kc, n_sel):
    i = pl.program_id(1)
    s_len = o_ref.shape[1]
    n_ch = (i * tq + tq + kc - 1) // kc
    wpc = kc // 32
    int_min = jnp.int32(-2 ** 31)
    lane = lax.broadcasted_iota(jnp.int32, (tq, 128), 1)
    w_t = wi_ref[...].astype(F32).T[WIDX_LANE:WIDX_LANE + IDX_HEADS]
    q_heads = []
    for j in range(IDX_HEADS // 2):
        q2 = qi_ref[:, j * 128:(j + 1) * 128]
        q_heads.append(jnp.where(lane < IDX_DIM, q2, jnp.zeros_like(q2)))
        q_heads.append(jnp.where(lane >= IDX_DIM, q2, jnp.zeros_like(q2)))
    kiota = lax.broadcasted_iota(jnp.int32, (kc, tq), 0)

    def chunk(c):
        return pl.ds(pl.multiple_of(c * kc, kc), kc)

    def scored(k, q_from):
        acc = jnp.zeros((k.shape[0], tq - q_from), F32)
        for hh in range(IDX_HEADS):
            acc = acc + jnp.maximum(_qk(k, q_heads[hh][q_from:, :]), 0.0) * w_t[hh:hh + 1, q_from:]
        return acc + 0.0

    def causal(acc):
        key = lax.broadcasted_iota(jnp.int32, acc.shape, 0)
        return jnp.where(key <= lax.broadcasted_iota(jnp.int32, acc.shape, 1), acc, NEG)

    def store_keys(row0, acc):
        bits = pltpu.bitcast(acc, jnp.int32)
        keys = bits ^ ((bits >> 31) & jnp.int32(0x7FFFFFFF))
        key_ref[pl.ds(row0, acc.shape[0]), :] = keys
        ukeys = keys ^ int_min
        for blk in range(acc.shape[0] // 256):
            planes = _bit_transpose32([ukeys[blk * 256 + 8 * j:blk * 256 + 8 * j + 8, :] for j in range(32)])
            row = pl.multiple_of(row0 // 32 + blk * 8, 8)
            for b in range(32):
                plane_ref[b, pl.ds(row, 8), :] = planes[b]

    def score_chunk(c, _):
        store_keys(pl.multiple_of(c * kc, kc), scored(ki_ref[chunk(c), :], 0))
        return 0

    lax.fori_loop(0, i, score_chunk, 0)
    diag = pl.multiple_of(i * kc, kc)
    half = kc // 2
    if half % 256 == 0:
        store_keys(diag, causal(scored(ki_ref[pl.ds(diag, half), :], 0)))
        late = causal(scored(ki_ref[pl.ds(diag + half, half), :], half))
        store_keys(diag + half, jnp.concatenate([jnp.full((half, half), NEG, F32), late], axis=1))
    else:
        store_keys(diag, causal(scored(ki_ref[pl.ds(diag, kc), :], 0)))

    def clear_chunk(c, _):
        plane_ref[:, pl.ds(pl.multiple_of(c * wpc, wpc), wpc), :] = jnp.zeros((32, wpc, tq), jnp.int32)
        return 0

    @pl.when(i == 0)
    def _():
        lax.fori_loop(n_ch, s_len // kc, clear_chunk, 0)

    n_rows = s_len // 32
    word_row = lax.broadcasted_iota(jnp.int32, (n_rows, tq), 0)
    alive_ref[...] = jnp.where(word_row < n_ch * wpc, jnp.int32(-1), jnp.int32(0))
    zeros = jnp.zeros((1, tq), jnp.int32)

    def radix(rows):
        def radix_step(bi, carry):
            thr, above = carry
            alive = alive_ref[0:rows]
            plane = plane_ref[bi, 0:rows]
            ones = lax.population_count(alive & plane)
            ones = jnp.sum(jnp.sum(ones.reshape(rows // 8, 8, tq), axis=0), axis=0, keepdims=True)
            take = above + ones >= n_sel
            thr = jnp.where(take, thr | (jnp.int32(1) << (31 - bi)), thr)
            above = jnp.where(take, above, above + ones)
            alive_ref[0:rows] = alive & (plane ^ jnp.where(take, jnp.int32(0), jnp.int32(-1)))
            return thr, above

        thr, above = lax.fori_loop(0, 32, radix_step, (zeros, zeros))
        stat_ref[0:8] = jnp.broadcast_to(thr, (8, tq))
        stat_ref[8:16] = jnp.broadcast_to(above, (8, tq))

    quarter = max(n_rows // 4, 8)
    sizes = sorted({min(quarter * (n + 1), n_rows) for n in range(4)})
    for n, rows in enumerate(sizes):
        lower = sizes[n - 1] if n else 0
        pl.when((n_ch * wpc > lower) & (n_ch * wpc <= rows))(functools.partial(radix, rows))
    thr_u, above = stat_ref[0:1], stat_ref[8:9]
    thr = thr_u ^ int_min
    n_equal = lax.population_count(alive_ref[...])
    n_equal = jnp.sum(jnp.sum(n_equal.reshape(n_rows // 8, 8, tq), axis=0), axis=0, keepdims=True)
    need = n_sel - above
    masked_key = int(np.float32(NEG).view(np.int32)) ^ 0x7FFFFFFF
    tie = (n_equal > need) & (thr != masked_key)
    any_tie = jnp.max(jnp.where(tie, 1, 0)) > 0

    @pl.when(jnp.logical_not(any_tie))
    def _():
        def emit(c, _):
            keep = jnp.where(key_ref[chunk(c), :] >= thr, 0.0, NEG)
            o_ref[:, chunk(c)] = keep.T.astype(BF16)
            return 0

        lax.fori_loop(0, n_ch, emit, 0)

    @pl.when(any_tie)
    def _():
        def equal_below(bound):
            def body(c, part):
                hit = jnp.where((key_ref[chunk(c), :] == thr) & (c * kc + kiota < bound), 1, 0)
                return part + jnp.sum(hit.reshape(kc // 8, 8, tq), axis=0)
            part = lax.fori_loop(0, n_ch, body, jnp.zeros((8, tq), jnp.int32))
            return jnp.sum(part, axis=0, keepdims=True)

        n_bits = s_len.bit_length()

        def bound_step(bi, cut):
            cand = cut + (jnp.int32(1) << (n_bits - 1 - bi))
            ok = (cand <= s_len) & (equal_below(cand) <= need)
            return jnp.where(ok, cand, cut)

        cut = lax.fori_loop(0, n_bits, bound_step, zeros)

        def emit(c, _):
            keys = key_ref[chunk(c), :]
            kept = (keys > thr) | ((keys == thr) & (c * kc + kiota < cut))
            o_ref[:, chunk(c)] = jnp.where(kept, 0.0, NEG).T.astype(BF16)
            return 0

        lax.fori_loop(0, n_ch, emit, 0)

    def blank(c, _):
        o_ref[:, chunk(c)] = jnp.full((tq, kc), NEG, BF16)
        return 0

    lax.fori_loop(n_ch, s_len // kc, blank, 0)


def _select(proj):
    b, s, _ = proj.shape
    tq = kc = min(512, s)
    n_sel = min(TOPK_MAX, s // 4)
    return pl.pallas_call(
        functools.partial(_select_kernel, tq=tq, kc=kc, n_sel=n_sel),
        out_shape=jax.ShapeDtypeStruct((b, s, s), BF16),
        grid=(b, s // tq),
        in_specs=[pl.BlockSpec((None, tq, 1024), lambda bi, i: (bi, i, OFF_QIDX // 1024)),
                  pl.BlockSpec((None, s, 128), lambda bi, i: (bi, 0, OFF_KIDX // 128)),
                  pl.BlockSpec((None, tq, 128), lambda bi, i: (bi, i, OFF_WIDX // 128))],
        out_specs=pl.BlockSpec((None, tq, s), lambda bi, i: (bi, i, 0)),
        scratch_shapes=[pltpu.VMEM((s, tq), jnp.int32), pltpu.VMEM((32, s // 32, tq), jnp.int32),
                        pltpu.VMEM((s // 32, tq), jnp.int32), pltpu.VMEM((16, tq), jnp.int32)],
        compiler_params=_cparams("arbitrary", "arbitrary"),
        name="idx_select",
    )(proj, proj, proj)


def _out_kernel(a_ref, b_ref, c_ref, d_ref, w_ref, x_ref, gate_ref, g_ref, o_ref, wb_ref):
    @pl.when((pl.program_id(0) == 0) & (pl.program_id(1) == 0))
    def _():
        wb_ref[...] = w_ref[...].astype(BF16)

    y = jnp.dot(a_ref[...], wb_ref[0:512, :], preferred_element_type=F32)
    y += jnp.dot(b_ref[...], wb_ref[512:1024, :], preferred_element_type=F32)
    y += jnp.dot(c_ref[...], wb_ref[1024:1536, :], preferred_element_type=F32)
    y += jnp.dot(d_ref[...], wb_ref[1536:2048, :], preferred_element_type=F32)
    o_ref[...] = x_ref[...] + gate_ref[...] * _rms(y, g_ref[...])


def _out_proj(outs, w_out, li, x, mod3, g_post):
    b, s, d = x.shape
    tm = min(512, s)
    mix = lambda bi, i: (bi, i, 0)
    return pl.pallas_call(
        _out_kernel,
        out_shape=jax.ShapeDtypeStruct((b, s, d), F32),
        grid=(b, s // tm),
        in_specs=[pl.BlockSpec((None, tm, BRANCH_WIDTH), mix)] * 4
        + [pl.BlockSpec((None,) + w_out.shape[1:], lambda bi, i: (li, 0, 0), pipeline_mode=pl.Buffered(1)),
           pl.BlockSpec((None, tm, d), mix),
           pl.BlockSpec((None, 1, d), lambda bi, i: (bi, 0, 2)),
           pl.BlockSpec((1, d), lambda bi, i: (0, 0))],
        out_specs=pl.BlockSpec((None, tm, d), mix),
        scratch_shapes=[pltpu.VMEM(w_out.shape[1:], BF16)],
        compiler_params=_cparams("arbitrary", "arbitrary"),
        name="out_proj",
    )(*outs, w_out, x, mod3, g_post.reshape(1, d))


def _rope_tables(s):
    half = MLA_ROPE // 2
    inv = ROPE_THETA ** (-jnp.arange(half, dtype=F32) / half)
    ang = jnp.arange(s, dtype=F32)[:, None] * inv[None, :]
    z = jnp.zeros((s, 128 - MLA_ROPE), F32)
    cos, sin = jnp.cos(ang), jnp.sin(ang)
    cat = lambda *parts: jnp.concatenate(parts, axis=-1)
    return cat(cos, cos, z), cat(sin, sin, z), cat(cos, cos, sin, sin)


def _rot_cols(w):
    half = w.shape[-1] // 2
    return jnp.concatenate([-w[..., half:], w[..., :half]], axis=-1)


IN_SPLITS = (("a_cq", 384), ("a_ckv", 256), ("a_krope", 64), ("b_q", 512), ("b_k", 512), ("b_v", 512),
             ("c_q", 512), ("c_k", 512), ("c_v", 512), ("c_qidx", 1024), ("c_kidx", 64), ("c_widx", 16),
             ("d_q", 512), ("d_k", 512), ("d_v", 512), ("gate", 2048))
IN_WIDTH = sum(width for _, width in IN_SPLITS)


def _layout_w_in_kernel(w_ref, o_ref):
    src, start = {}, 0
    for name, width in IN_SPLITS:
        src[name] = start
        start += width
    tk = w_ref.shape[1]

    def rows(name, width, offset=0):
        a = src[name] + offset
        return w_ref[a:a + width, :]

    def put(dst, val, scale=None):
        for r in range(0, val.shape[0], 512):
            piece = val[r:r + 512]
            if scale is not None:
                piece = piece * scale
            o_ref[:, dst + r:dst + r + piece.shape[0]] = piece.T.astype(BF16)

    z64 = jnp.zeros((64, tk), F32)
    half = MLA_ROPE // 2
    put(OFF_QIDX, rows("c_qidx", 1024), IDX_DIM ** -0.5)
    put(OFF_A, rows("a_cq", MLA_Q_RANK + MLA_KV_RANK))
    put(OFF_A + 640, jnp.concatenate(
        [rows("a_krope", MLA_ROPE), rows("c_widx", IDX_HEADS) * IDX_HEADS ** -0.5,
         jnp.zeros((128 - MLA_ROPE - IDX_HEADS, tk), F32),
         -rows("a_krope", half, half), rows("a_krope", half), z64,
         rows("c_kidx", IDX_DIM), rows("c_kidx", IDX_DIM)], axis=0))
    for off, name, dim in ((OFF_B, "b", HEAD_DIM), (OFF_C, "c", HEAD_DIM), (OFF_D, "d", HEAD_DIM // 2)):
        put(off, rows(name + "_q", BRANCH_WIDTH), LOG2E * dim ** -0.5)
        put(off + BRANCH_WIDTH, rows(name + "_k", 2 * BRANCH_WIDTH))
    put(OFF_GATE, rows("gate", MIX_WIDTH))


def _layout_w_in(w_in_t, li):
    d = w_in_t.shape[2]
    tk = 256
    return pl.pallas_call(
        _layout_w_in_kernel,
        out_shape=jax.ShapeDtypeStruct((d, PROJ_WIDTH), BF16),
        grid=(d // tk,),
        in_specs=[pl.BlockSpec((None, IN_WIDTH, tk), lambda i: (li, 0, i))],
        out_specs=pl.BlockSpec((tk, PROJ_WIDTH), lambda i: (i, 0)),
        compiler_params=_cparams("arbitrary"),
        name="layout_w_in",
    )(w_in_t)


def _layout_w_uq(w):
    r = w.shape[0]
    w = w.reshape(r, N_HEADS, MLA_NOPE + MLA_ROPE) * (LOG2E * (MLA_NOPE + MLA_ROPE) ** -0.5)
    rope = w[..., MLA_NOPE:]
    return jnp.concatenate([w[..., :MLA_NOPE], rope, _rot_cols(rope)], axis=-1).reshape(r, -1).astype(BF16)


def kernel(x, c, w_ada, b_ada, g_pre, g_post, w_in, g_q_a, w_uq_a, g_kv_a, w_ukv_a,
           lam_q1, lam_k1, lam_q2, lam_k2, g_sub_d, w_out, rel_bias):
    b, s, d = x.shape
    depth = w_ada.shape[0]
    t = min(ATT_TILE, s)
    half = t // 2
    nq = s // t
    near_bias = min(nq, -(-(_first_far_diagonal(half) + 1) // 2))
    near_band = min(nq, -(-(DILATED_PATTERNS[-1][0] // half + 1) // 2))

    rope_tabs = _rope_tables(s)
    bank_b = _bank(rel_bias[:, 0:N_HEADS], 2 * near_band + 1, half, -1, True)
    bank_cd = _bank(rel_bias[:, N_HEADS:3 * N_HEADS], 2 * near_bias + 1, half, -1, False)
    bank_spec = lambda n, group=0: pl.BlockSpec((N_HEADS, n, half, half), lambda bi, i: (group, 0, 0, 0),
                                                pipeline_mode=pl.Buffered(1))

    w_in_t = jnp.swapaxes(w_in, 1, 2)
    mod = _ada_mod(c, w_ada, b_ada)
    for li in range(depth):
        mod3 = mod[li].reshape(b, 1, 3 * d)
        proj = _in_proj(x, g_pre[li], mod3, _layout_w_in(w_in_t, li))

        q_a, k_a, v_a = _mla_prep(proj, rope_tabs, g_q_a[li], g_kv_a[li],
                                  _layout_w_uq(w_uq_a[li]), w_ukv_a[li].astype(BF16))
        gate0 = OFF_GATE // BRANCH_WIDTH
        out_a = _attention("a", q_a, k_a, v_a, proj, 0, gate0)
        out_b = _attention("band", proj, proj, proj, proj, OFF_B // BRANCH_WIDTH, gate0 + 1,
                           extra_in=(bank_b,), extra_specs=(bank_spec(2 * near_band + 1),), near=near_band)
        sel = _select(proj)
        out_c = _attention("sel", proj, proj, proj, proj, OFF_C // BRANCH_WIDTH, gate0 + 2,
                           extra_in=(bank_cd, sel),
                           extra_specs=(bank_spec(2 * near_bias + 1, 0),
                                        pl.BlockSpec((None, t, s), lambda bi, i: (bi, i, 0))),
                           near=near_bias)
        lam_init = 0.8 - 0.6 * math.exp(-0.3 * li)
        lam_vecs = jnp.stack([lam_q1[li], lam_k1[li], lam_q2[li], lam_k2[li]])
        out_d = _attention("diff", proj, proj, proj, proj, OFF_D // BRANCH_WIDTH, gate0 + 3,
                           extra_in=(bank_cd, lam_vecs, g_sub_d[li].reshape(1, HEAD_DIM)),
                           extra_specs=(bank_spec(2 * near_bias + 1, 1),
                                        pl.BlockSpec(lam_vecs.shape, lambda bi, i: (0, 0)),
                                        pl.BlockSpec((1, HEAD_DIM), lambda bi, i: (0, 0))),
                           near=near_bias, lam_init=lam_init)
        x = _out_proj((out_a, out_b, out_c, out_d), w_out, li, x, mod3, g_post[li])
    return x
```

```python
import functools
import math

import numpy as np
import jax
import jax.numpy as jnp
from jax import lax
from jax.experimental import pallas as pl
from jax.experimental.pallas import tpu as pltpu

F32 = jnp.float32
BF16 = jnp.bfloat16

HEAD_DIM = 128
N_HEADS = 4
BRANCH_WIDTH = N_HEADS * HEAD_DIM
MIX_WIDTH = 4 * BRANCH_WIDTH
MLA_Q_RANK = 384
MLA_KV_RANK = 256
MLA_NOPE = 128
MLA_ROPE = 64
ROPE_THETA = 10000.0
DILATED_PATTERNS = ((128, 1), (512, 4), (2048, 16))
IDX_HEADS = 16
IDX_DIM = 64
TOPK_MAX = 256
REL_BUCKETS = 32
REL_MAX_DIST = 2048
NORM_EPS = 1e-6
NEG = -1e30
LOG2E = math.log2(math.e)

OFF_QIDX = 0
OFF_A = 1024
OFF_WIDX = OFF_A + 640
WIDX_LANE = 64
OFF_KIDX = OFF_A + 896
OFF_B = 2048
OFF_C = 3584
OFF_D = 5120
OFF_GATE = 6656
PROJ_WIDTH = 8704

ATT_TILE = 512
V7X_VMEM_BYTES = 64 * 1024 * 1024
VMEM_LIMIT = V7X_VMEM_BYTES // 8 * 7


def _cparams(*sem):
    return pltpu.CompilerParams(dimension_semantics=sem, vmem_limit_bytes=VMEM_LIMIT)


def _split_bf16(x):
    hi = x.astype(BF16)
    return hi, (x - hi.astype(F32)).astype(BF16)


def _ada_kernel(c_ref, w_ref, b_ref, o_ref):
    c = c_ref[...]
    a_hi, a_lo = _split_bf16(c * jax.nn.sigmoid(c))
    w_hi, w_lo = _split_bf16(w_ref[...])
    dot = functools.partial(jnp.dot, preferred_element_type=F32)
    o_ref[...] = dot(a_hi, w_hi) + (dot(a_hi, w_lo) + dot(a_lo, w_hi)) + b_ref[...]


def _ada_mod(c, w_ada, b_ada):
    depth, d, n = w_ada.shape
    b = c.shape[0]
    tn = 768
    return pl.pallas_call(
        _ada_kernel,
        out_shape=jax.ShapeDtypeStruct((depth, b, n), F32),
        grid=(depth, n // tn),
        in_specs=[pl.BlockSpec((b, d), lambda l, j: (0, 0)),
                  pl.BlockSpec((None, d, tn), lambda l, j: (l, 0, j)),
                  pl.BlockSpec((None, 1, tn), lambda l, j: (l, 0, j))],
        out_specs=pl.BlockSpec((None, b, tn), lambda l, j: (l, 0, j)),
        compiler_params=_cparams("arbitrary", "arbitrary"),
        name="ada_mod",
    )(c, w_ada, b_ada.reshape(depth, 1, n))


def _in_proj_kernel(x_ref, g_ref, shift_ref, scale_ref, w_ref, o_ref):
    x = x_ref[...]
    y = x * lax.rsqrt(jnp.mean(x * x, axis=-1, keepdims=True) + NORM_EPS) * g_ref[...]
    h = (y * (1.0 + scale_ref[...]) + shift_ref[...]).astype(BF16)
    o_ref[...] = jnp.dot(h, w_ref[...], preferred_element_type=F32).astype(o_ref.dtype)


def _in_proj(x, g_pre, mod3, w):
    b, s, d = x.shape
    n = w.shape[1]
    tm = min(512, s)
    nt = s // tm
    tn = n // 2
    row = lambda j, i: (i // nt, i % nt, 0)
    return pl.pallas_call(
        _in_proj_kernel,
        out_shape=jax.ShapeDtypeStruct((b, s, n), BF16),
        grid=(n // tn, b * nt),
        in_specs=[pl.BlockSpec((None, tm, d), row),
                  pl.BlockSpec((1, d), lambda j, i: (0, 0)),
                  pl.BlockSpec((None, 1, d), lambda j, i: (i // nt, 0, 0)),
                  pl.BlockSpec((None, 1, d), lambda j, i: (i // nt, 0, 1)),
                  pl.BlockSpec((d, tn), lambda j, i: (0, j), pipeline_mode=pl.Buffered(1))],
        out_specs=pl.BlockSpec((None, tm, tn), lambda j, i: (i // nt, i % nt, j)),
        compiler_params=_cparams("arbitrary", "arbitrary"),
        name="in_proj",
    )(x, g_pre.reshape(1, d), mod3, mod3, w)


def _rms(x, g):
    return x * lax.rsqrt(jnp.mean(x * x, axis=-1, keepdims=True) + NORM_EPS) * g


def _mla_prep_kernel(p_ref, cos_ref, sin_ref, cs_ref, gq_ref, gkv_ref, wq_ref, wkv_ref, q_ref, k_ref, v_ref):
    cos = cos_ref[...]
    sin = sin_ref[...]
    cs = cs_ref[...]
    cq = _rms(p_ref[:, 0:MLA_Q_RANK].astype(F32), gq_ref[...]).astype(BF16)
    ckv = _rms(p_ref[:, MLA_Q_RANK:MLA_Q_RANK + MLA_KV_RANK].astype(F32), gkv_ref[...]).astype(BF16)
    q = jnp.dot(cq, wq_ref[...], preferred_element_type=F32)
    kv = jnp.dot(ckv, wkv_ref[...], preferred_element_type=F32)
    k_rope = (p_ref[:, 640:768].astype(F32) * cos + p_ref[:, 768:896].astype(F32) * sin).astype(BF16)
    for h in range(N_HEADS):
        q_ref[:, h * 256:h * 256 + 128] = q[:, h * 256:h * 256 + 128].astype(BF16)
        z = q[:, h * 256 + 128:(h + 1) * 256] * cs
        q_ref[:, h * 256 + 128:(h + 1) * 256] = (z + pltpu.roll(z, MLA_ROPE, axis=1)).astype(BF16)
        k_ref[:, h * 256:h * 256 + 128] = kv[:, h * 256:h * 256 + 128].astype(BF16)
        k_ref[:, h * 256 + 128:(h + 1) * 256] = k_rope
        v_ref[:, h * 128:(h + 1) * 128] = kv[:, h * 256 + 128:(h + 1) * 256].astype(BF16)


def _mla_prep(proj, rope_tabs, g_q, g_kv, wq, wkv):
    b, s, _ = proj.shape
    tm = min(512, s)
    const = lambda bi, i: (0, 0)
    return pl.pallas_call(
        _mla_prep_kernel,
        out_shape=(jax.ShapeDtypeStruct((b, s, N_HEADS * 256), BF16),
                   jax.ShapeDtypeStruct((b, s, N_HEADS * 256), BF16),
                   jax.ShapeDtypeStruct((b, s, BRANCH_WIDTH), BF16)),
        grid=(b, s // tm),
        in_specs=[pl.BlockSpec((None, tm, 1024), lambda bi, i: (bi, i, OFF_A // 1024)),
                  pl.BlockSpec((tm, 128), lambda bi, i: (i, 0)),
                  pl.BlockSpec((tm, 128), lambda bi, i: (i, 0)),
                  pl.BlockSpec((tm, 128), lambda bi, i: (i, 0)),
                  pl.BlockSpec((1, MLA_Q_RANK), const),
                  pl.BlockSpec((1, MLA_KV_RANK), const),
                  pl.BlockSpec(wq.shape, const),
                  pl.BlockSpec(wkv.shape, const)],
        out_specs=(pl.BlockSpec((None, tm, N_HEADS * 256), lambda bi, i: (bi, i, 0)),
                   pl.BlockSpec((None, tm, N_HEADS * 256), lambda bi, i: (bi, i, 0)),
                   pl.BlockSpec((None, tm, BRANCH_WIDTH), lambda bi, i: (bi, i, 0))),
        compiler_params=_cparams("arbitrary", "arbitrary"),
        name="mla_prep",
    )(proj, *rope_tabs, g_q.reshape(1, -1), g_kv.reshape(1, -1), wq, wkv)


def _bucket_np(n):
    max_exact = REL_BUCKETS // 2
    nf = np.maximum(n, max_exact).astype(np.float32)
    large = max_exact + (np.log(nf / np.float32(max_exact)) / np.float32(math.log(REL_MAX_DIST / max_exact))
                         * np.float32(REL_BUCKETS - max_exact)).astype(np.int32)
    return np.where(n < max_exact, n, np.minimum(large, REL_BUCKETS - 1)).astype(np.int32)


def _bucket_starts():
    buckets = _bucket_np(np.arange(2 * REL_MAX_DIST))
    return [int(np.argmax(buckets >= b)) for b in range(REL_BUCKETS)]


def _bank_kernel(tab_ref, o_ref, *, t, d_min, band):
    h = pl.program_id(0)
    starts = _bucket_starts()
    row = lax.broadcasted_iota(jnp.int32, (t, t), 0)
    col = lax.broadcasted_iota(jnp.int32, (t, t), 1)
    for j in range(o_ref.shape[0]):
        d = d_min + j
        if d < 0:
            o_ref[j] = jnp.full((t, t), NEG, F32)
            continue
        dist = t * d + row - col
        b_lo, b_hi = (int(x) for x in _bucket_np(np.array([max(t * d - t + 1, 0), t * d + t - 1])))
        val = jnp.full((t, t), tab_ref[b_lo, h], F32)
        for b in range(b_lo + 1, b_hi + 1):
            val = jnp.where(dist >= starts[b], tab_ref[b, h], val)
        if band:
            mult = jnp.zeros((t, t), jnp.int32)
            for window, dil in DILATED_PATTERNS:
                mult += jnp.where((dist >= 0) & (dist <= window) & ((dist & (dil - 1)) == 0), 1, 0)
            val = val + jnp.where(mult == 3, math.log(3.0), jnp.where(mult == 2, math.log(2.0), 0.0))
            keep = mult > 0
        else:
            val = val - tab_ref[REL_BUCKETS - 1, h]
            keep = dist >= 0
        o_ref[j] = jnp.where(keep, val * LOG2E, NEG)


def _bank(tab, n_tables, t, d_min, band):
    nh = tab.shape[1]
    return pl.pallas_call(
        functools.partial(_bank_kernel, t=t, d_min=d_min, band=band),
        out_shape=jax.ShapeDtypeStruct((nh, n_tables, t, t), F32),
        grid=(nh,),
        in_specs=[pl.BlockSpec(memory_space=pltpu.SMEM)],
        out_specs=pl.BlockSpec((None, n_tables, t, t), lambda h: (h, 0, 0, 0)),
        compiler_params=_cparams("arbitrary"),
        name="bank_band" if band else "bank_bias",
    )(tab)


def _first_far_diagonal(t):
    last = _bucket_starts()[REL_BUCKETS - 1]
    return -(-(last + t - 1) // t)


def _qk(q, k):
    return lax.dot_general(q, k, (((1,), (1,)), ((), ())), preferred_element_type=F32)


def _flash_init(m_ref, acc_ref):
    m_ref[...] = jnp.full(m_ref.shape, NEG, F32)
    acc_ref[...] = jnp.zeros(acc_ref.shape, F32)


def _flash_update(slot, s, v, m_ref, acc_ref, s_ref, next_scores, diagonal=False):
    t = s.shape[0]
    v_ones = jnp.concatenate([v, jnp.ones_like(v)], axis=1)
    blocks = [(0, t // 2, t // 2), (t // 2, t, t)] if diagonal else [(0, t, t)]
    for n, (r0, r1, keys) in enumerate(blocks):
        m_prev = m_ref[slot, r0:r1]
        m_new = jnp.maximum(m_prev, jnp.max(s[r0:r1, :keys], axis=-1, keepdims=True))
        alpha = jnp.exp2(m_prev - m_new)
        p = jnp.concatenate([jnp.exp2(s[r0:r1, j * 128:(j + 1) * 128] - m_new) for j in range(keys // 128)],
                            axis=1).astype(BF16)
        if n == len(blocks) - 1 and next_scores is not None:
            s_ref[slot] = next_scores()
        acc_ref[slot, r0:r1] = (jnp.concatenate([alpha, alpha], axis=1) * acc_ref[slot, r0:r1]
                                + jnp.dot(p, v_ones[:keys], preferred_element_type=F32))
        m_ref[slot, r0:r1] = m_new


def _flash_result(slot, acc_ref):
    acc = acc_ref[slot]
    return acc[:, :HEAD_DIM] / acc[:, HEAD_DIM:]


def _silu(g):
    return g * jax.nn.sigmoid(g)


def _tile(ref, kj, t, c0, width):
    start = kj * t if isinstance(kj, int) else pl.multiple_of(kj * t, t)
    return ref[pl.ds(start, t), c0:c0 + width]


def _bank_tile(bank_ref, h, i, kj):
    base = 2 * (i - kj) + 1
    top = jnp.concatenate([bank_ref[h, base], bank_ref[h, base - 1]], axis=1)
    bot = jnp.concatenate([bank_ref[h, base + 1], bank_ref[h, base]], axis=1)
    return jnp.concatenate([top, bot], axis=0)


def _emit(o_ref, g_ref, h, o):
    gate = g_ref[:, h * 128:(h + 1) * 128].astype(F32)
    o_ref[:, h * 128:(h + 1) * 128] = (o * _silu(gate)).astype(BF16)


def _flash_walk(i, first_key, n_far, slots, score, logits, value, s_ref, m_ref, acc_ref):
    lo = first_key(i)
    i_next = jnp.minimum(i + 1, pl.num_programs(1) - 1)
    _flash_init(m_ref, acc_ref)

    @pl.when(i == 0)
    def _():
        for slot in range(slots):
            s_ref[slot] = score(slot, False, lo)

    def step(kj, phase):
        for slot in range(slots):
            s = logits(slot, s_ref[slot], kj, phase)
            if phase == "last":
                nxt = functools.partial(score, slot, True, first_key(i_next))
            else:
                nxt = functools.partial(score, slot, False, kj + 1)
            _flash_update(slot, s, value(slot, kj), m_ref, acc_ref, s_ref, nxt, diagonal=phase == "last")

    def walk(a, b, phase):
        def body(kj, carry):
            step(kj, phase)
            return carry
        lax.fori_loop(a, b, body, 0)

    if n_far is not None:
        walk(lo, n_far, "far")
        lo = n_far
    walk(lo, i, "near")
    step(i, "last")


def _attn_a_kernel(q_ref, qn_ref, k_ref, v_ref, g_ref, o_ref, s_ref, m_ref, acc_ref, *, t):
    i = pl.program_id(1)

    def score(h, next_q, kj):
        q = (qn_ref if next_q else q_ref)[:, h * 256:(h + 1) * 256]
        return _qk(q, _tile(k_ref, kj, t, h * 256, 256))

    def logits(h, s, kj, phase):
        if phase != "last":
            return s
        causal = lax.broadcasted_iota(jnp.int32, (t, t), 1) <= lax.broadcasted_iota(jnp.int32, (t, t), 0)
        return jnp.where(causal, s, NEG)

    def value(h, kj):
        return _tile(v_ref, kj, t, h * 128, 128)

    _flash_walk(i, lambda qi: 0, None, N_HEADS, score, logits, value, s_ref, m_ref, acc_ref)
    for h in range(N_HEADS):
        _emit(o_ref, g_ref, h, _flash_result(h, acc_ref))


def _attn_band_kernel(q_ref, qn_ref, k_ref, v_ref, g_ref, bank_ref, o_ref, s_ref, m_ref, acc_ref, *, t, near):
    i = pl.program_id(1)

    def score(h, next_q, kj):
        q = (qn_ref if next_q else q_ref)[:, h * 128:(h + 1) * 128]
        return _qk(q, _tile(k_ref, kj, t, h * 128, 128))

    def logits(h, s, kj, phase):
        return s + _bank_tile(bank_ref, h, i, kj)

    def value(h, kj):
        return _tile(v_ref, kj, t, h * 128, 128)

    _flash_walk(i, lambda qi: jnp.maximum(qi - (near - 1), 0), None, N_HEADS, score, logits, value,
                s_ref, m_ref, acc_ref)
    for h in range(N_HEADS):
        _emit(o_ref, g_ref, h, _flash_result(h, acc_ref))


def _attn_sel_kernel(q_ref, qn_ref, k_ref, v_ref, g_ref, bank_ref, sel_ref, o_ref,
                     s_ref, m_ref, acc_ref, *, t, near):
    i = pl.program_id(1)

    def score(h, next_q, kj):
        q = (qn_ref if next_q else q_ref)[:, h * 128:(h + 1) * 128]
        return _qk(q, _tile(k_ref, kj, t, h * 128, 128))

    def logits(h, s, kj, phase):
        sel = sel_ref[:, pl.ds(pl.multiple_of(kj * t, t), t)].astype(F32)
        return s + sel if phase == "far" else s + (sel + _bank_tile(bank_ref, h, i, kj))

    def value(h, kj):
        return _tile(v_ref, kj, t, h * 128, 128)

    _flash_walk(i, lambda qi: 0, jnp.maximum(i - (near - 1), 0), N_HEADS, score, logits, value,
                s_ref, m_ref, acc_ref)
    for h in range(N_HEADS):
        _emit(o_ref, g_ref, h, _flash_result(h, acc_ref))


def _attn_diff_kernel(q_ref, qn_ref, k_ref, v_ref, g_ref, bank_ref, lam_ref, gsub_ref, o_ref,
                      s_ref, m_ref, acc_ref, *, t, near, lam_init):
    i = pl.program_id(1)
    first_half = lax.broadcasted_iota(jnp.int32, (t, HEAD_DIM), 1) < HEAD_DIM // 2

    def score(slot, next_q, kj):
        h = slot // 2
        q = (qn_ref if next_q else q_ref)[:, h * 128:(h + 1) * 128]
        keep = first_half if slot % 2 == 0 else jnp.logical_not(first_half)
        q = jnp.where(keep, q, jnp.zeros_like(q))
        return _qk(q, _tile(k_ref, kj, t, h * 128, 128))

    def logits(slot, s, kj, phase):
        return s if phase == "far" else s + _bank_tile(bank_ref, slot // 2, i, kj)

    def value(slot, kj):
        return _tile(v_ref, kj, t, (slot // 2) * 128, 128)

    _flash_walk(i, lambda qi: 0, jnp.maximum(i - (near - 1), 0), 2 * N_HEADS, score, logits, value,
                s_ref, m_ref, acc_ref)
    lam_v = lam_ref[...]
    lam = (jnp.exp(jnp.sum(lam_v[0:1] * lam_v[1:2], axis=-1, keepdims=True))
           - jnp.exp(jnp.sum(lam_v[2:3] * lam_v[3:4], axis=-1, keepdims=True)) + lam_init)
    for h in range(N_HEADS):
        o = _flash_result(2 * h, acc_ref) - lam * _flash_result(2 * h + 1, acc_ref)
        _emit(o_ref, g_ref, h, _rms(o, gsub_ref[...]) * (1.0 - lam_init))


def _attention(kind, q_arr, k_arr, v_arr, proj, q_blk, gate_blk, extra_in=(), extra_specs=(), **kw):
    b, s, _ = proj.shape
    t = min(ATT_TILE, s)
    dk = 256 if kind == "a" else 128
    qw = N_HEADS * dk
    k_blk = 0 if kind == "a" else q_blk + 1
    v_blk = 0 if kind == "a" else q_blk + 2
    body = {"a": _attn_a_kernel, "band": _attn_band_kernel, "sel": _attn_sel_kernel,
            "diff": _attn_diff_kernel}[kind]
    slots = 2 * N_HEADS if kind == "diff" else N_HEADS
    scratch = [pltpu.VMEM((slots, t, t), F32), pltpu.VMEM((slots, t, HEAD_DIM), F32),
               pltpu.VMEM((slots, t, 2 * HEAD_DIM), F32)]
    last = s // t - 1
    in_specs = [pl.BlockSpec((None, t, qw), lambda bi, i: (bi, i, q_blk)),
                pl.BlockSpec((None, t, qw), lambda bi, i: (bi, jnp.minimum(i + 1, last), q_blk)),
                pl.BlockSpec((None, s, qw), lambda bi, i: (bi, 0, k_blk)),
                pl.BlockSpec((None, s, BRANCH_WIDTH), lambda bi, i: (bi, 0, v_blk)),
                pl.BlockSpec((None, t, BRANCH_WIDTH), lambda bi, i: (bi, i, gate_blk))]
    in_specs += list(extra_specs)
    return pl.pallas_call(
        functools.partial(body, t=t, **kw),
        out_shape=jax.ShapeDtypeStruct((b, s, BRANCH_WIDTH), BF16),
        grid=(b, s // t),
        in_specs=in_specs,
        out_specs=pl.BlockSpec((None, t, BRANCH_WIDTH), lambda bi, i: (bi, i, 0)),
        scratch_shapes=scratch,
        compiler_params=_cparams("arbitrary", "arbitrary"),
        name="attn_" + kind,
    )(q_arr, q_arr, k_arr, v_arr, proj, *extra_in)


def _bit_transpose32(words):
    a = list(words)
    j, m = 16, 0x0000FFFF
    while j:
        for k in range(32):
            if not k & j:
                t = (a[k] ^ lax.shift_right_logical(a[k + j], jnp.int32(j))) & jnp.int32(m)
                a[k] = a[k] ^ t
                a[k + j] = a[k + j] ^ (t << j)
        j >>= 1
        m = (m ^ (m << j)) & 0xFFFFFFFF if j else m
    return a


def _select_kernel(qi_ref, ki_ref, wi_ref, o_ref, key_ref, plane_ref, alive_ref, stat_ref, *, tq, kc, n_sel):
    i = pl.program_id(1)
    s_len = o_ref.shape[1]
    n_ch = (i * tq + tq + kc - 1) // kc
    wpc = kc // 32
    int_min = jnp.int32(-2 ** 31)
    lane = lax.broadcasted_iota(jnp.int32, (tq, 128), 1)
    w_t = wi_ref[...].astype(F32).T[WIDX_LANE:WIDX_LANE + IDX_HEADS]
    q_heads = []
    for j in range(IDX_HEADS // 2):
        q2 = qi_ref[:, j * 128:(j + 1) * 128]
        q_heads.append(jnp.where(lane < IDX_DIM, q2, jnp.zeros_like(q2)))
        q_heads.append(jnp.where(lane >= IDX_DIM, q2, jnp.zeros_like(q2)))
    kiota = lax.broadcasted_iota(jnp.int32, (kc, tq), 0)

    def chunk(c):
        return pl.ds(pl.multiple_of(c * kc, kc), kc)

    def scored(k, q_from):
        acc = jnp.zeros((k.shape[0], tq - q_from), F32)
        for hh in range(IDX_HEADS):
            acc = acc + jnp.maximum(_qk(k, q_heads[hh][q_from:, :]), 0.0) * w_t[hh:hh + 1, q_from:]
        return acc + 0.0

    def causal(acc):
        key = lax.broadcasted_iota(jnp.int32, acc.shape, 0)
        return jnp.where(key <= lax.broadcasted_iota(jnp.int32, acc.shape, 1), acc, NEG)

    def store_keys(row0, acc):
        bits = pltpu.bitcast(acc, jnp.int32)
        keys = bits ^ ((bits >> 31) & jnp.int32(0x7FFFFFFF))
        key_ref[pl.ds(row0, acc.shape[0]), :] = keys
        ukeys = keys ^ int_min
        for blk in range(acc.shape[0] // 256):
            planes = _bit_transpose32([ukeys[blk * 256 + 8 * j:blk * 256 + 8 * j + 8, :] for j in range(32)])
            row = pl.multiple_of(row0 // 32 + blk * 8, 8)
            for b in range(32):
                plane_ref[b, pl.ds(row, 8), :] = planes[b]

    def score_chunk(c, _):
        store_keys(pl.multiple_of(c * kc, kc), scored(ki_ref[chunk(c), :], 0))
        return 0

    lax.fori_loop(0, i, score_chunk, 0)
    diag = pl.multiple_of(i * kc, kc)
    half = kc // 2
    if half % 256 == 0:
        store_keys(diag, causal(scored(ki_ref[pl.ds(diag, half), :], 0)))
        late = causal(scored(ki_ref[pl.ds(diag + half, half), :], half))
        store_keys(diag + half, jnp.concatenate([jnp.full((half, half), NEG, F32), late], axis=1))
    else:
        store_keys(diag, causal(scored(ki_ref[pl.ds(diag, kc), :], 0)))

    def clear_chunk(c, _):
        plane_ref[:, pl.ds(pl.multiple_of(c * wpc, wpc), wpc), :] = jnp.zeros((32, wpc, tq), jnp.int32)
        return 0

    @pl.when(i == 0)
    def _():
        lax.fori_loop(n_ch, s_len // kc, clear_chunk, 0)

    n_rows = s_len // 32
    word_row = lax.broadcasted_iota(jnp.int32, (n_rows, tq), 0)
    alive_ref[...] = jnp.where(word_row < n_ch * wpc, jnp.int32(-1), jnp.int32(0))
    zeros = jnp.zeros((1, tq), jnp.int32)

    def radix(rows):
        def radix_step(bi, carry):
            thr, above = carry
            alive = alive_ref[0:rows]
            plane = plane_ref[bi, 0:rows]
            ones = lax.population_count(alive & plane)
            ones = jnp.sum(jnp.sum(ones.reshape(rows // 8, 8, tq), axis=0), axis=0, keepdims=True)
            take = above + ones >= n_sel
            thr = jnp.where(take, thr | (jnp.int32(1) << (31 - bi)), thr)
            above = jnp.where(take, above, above + ones)
            alive_ref[0:rows] = alive & (plane ^ jnp.where(take, jnp.int32(0), jnp.int32(-1)))
            return thr, above

        thr, above = lax.fori_loop(0, 32, radix_step, (zeros, zeros))
        stat_ref[0:8] = jnp.broadcast_to(thr, (8, tq))
        stat_ref[8:16] = jnp.broadcast_to(above, (8, tq))

    sizes = sorted({min(max(wpc, 8) * (n + 1), n_rows) for n in range(s_len // kc)})
    for n, rows in enumerate(sizes):
        lower = sizes[n - 1] if n else 0
        pl.when((n_ch * wpc > lower) & (n_ch * wpc <= rows))(functools.partial(radix, rows))
    thr_u, above = stat_ref[0:1], stat_ref[8:9]
    thr = thr_u ^ int_min
    n_equal = lax.population_count(alive_ref[...])
    n_equal = jnp.sum(jnp.sum(n_equal.reshape(n_rows // 8, 8, tq), axis=0), axis=0, keepdims=True)
    need = n_sel - above
    masked_key = int(np.float32(NEG).view(np.int32)) ^ 0x7FFFFFFF
    tie = (n_equal > need) & (thr != masked_key)
    any_tie = jnp.max(jnp.where(tie, 1, 0)) > 0

    @pl.when(jnp.logical_not(any_tie))
    def _():
        def emit(c, _):
            keep = jnp.where(key_ref[chunk(c), :] >= thr, 0.0, NEG)
            o_ref[:, chunk(c)] = keep.T.astype(BF16)
            return 0

        lax.fori_loop(0, n_ch, emit, 0)

    @pl.when(any_tie)
    def _():
        def equal_below(bound):
            def body(c, part):
                hit = jnp.where((key_ref[chunk(c), :] == thr) & (c * kc + kiota < bound), 1, 0)
                return part + jnp.sum(hit.reshape(kc // 8, 8, tq), axis=0)
            part = lax.fori_loop(0, n_ch, body, jnp.zeros((8, tq), jnp.int32))
            return jnp.sum(part, axis=0, keepdims=True)

        n_bits = s_len.bit_length()

        def bound_step(bi, cut):
            cand = cut + (jnp.int32(1) << (n_bits - 1 - bi))
            ok = (cand <= s_len) & (equal_below(cand) <= need)
            return jnp.where(ok, cand, cut)

        cut = lax.fori_loop(0, n_bits, bound_step, zeros)

        def emit(c, _):
            keys = key_ref[chunk(c), :]
            kept = (keys > thr) | ((keys == thr) & (c * kc + kiota < cut))
            o_ref[:, chunk(c)] = jnp.where(kept, 0.0, NEG).T.astype(BF16)
            return 0

        lax.fori_loop(0, n_ch, emit, 0)

    def blank(c, _):
        o_ref[:, chunk(c)] = jnp.full((tq, kc), NEG, BF16)
        return 0

    lax.fori_loop(n_ch, s_len // kc, blank, 0)


def _select(proj):
    b, s, _ = proj.shape
    tq = kc = min(512, s)
    n_sel = min(TOPK_MAX, s // 4)
    return pl.pallas_call(
        functools.partial(_select_kernel, tq=tq, kc=kc, n_sel=n_sel),
        out_shape=jax.ShapeDtypeStruct((b, s, s), BF16),
        grid=(b, s // tq),
        in_specs=[pl.BlockSpec((None, tq, 1024), lambda bi, i: (bi, i, OFF_QIDX // 1024)),
                  pl.BlockSpec((None, s, 128), lambda bi, i: (bi, 0, OFF_KIDX // 128)),
                  pl.BlockSpec((None, tq, 128), lambda bi, i: (bi, i, OFF_WIDX // 128))],
        out_specs=pl.BlockSpec((None, tq, s), lambda bi, i: (bi, i, 0)),
        scratch_shapes=[pltpu.VMEM((s, tq), jnp.int32), pltpu.VMEM((32, s // 32, tq), jnp.int32),
                        pltpu.VMEM((s // 32, tq), jnp.int32), pltpu.VMEM((16, tq), jnp.int32)],
        compiler_params=_cparams("arbitrary", "arbitrary"),
        name="idx_select",
    )(proj, proj, proj)


def _out_kernel(a_ref, b_ref, c_ref, d_ref, w_ref, x_ref, gate_ref, g_ref, o_ref, wb_ref):
    @pl.when((pl.program_id(0) == 0) & (pl.program_id(1) == 0))
    def _():
        wb_ref[...] = w_ref[...].astype(BF16)

    y = jnp.dot(a_ref[...], wb_ref[0:512, :], preferred_element_type=F32)
    y += jnp.dot(b_ref[...], wb_ref[512:1024, :], preferred_element_type=F32)
    y += jnp.dot(c_ref[...], wb_ref[1024:1536, :], preferred_element_type=F32)
    y += jnp.dot(d_ref[...], wb_ref[1536:2048, :], preferred_element_type=F32)
    o_ref[...] = x_ref[...] + gate_ref[...] * _rms(y, g_ref[...])


def _out_proj(outs, w_out, li, x, mod3, g_post):
    b, s, d = x.shape
    tm = min(512, s)
    mix = lambda bi, i: (bi, i, 0)
    return pl.pallas_call(
        _out_kernel,
        out_shape=jax.ShapeDtypeStruct((b, s, d), F32),
        grid=(b, s // tm),
        in_specs=[pl.BlockSpec((None, tm, BRANCH_WIDTH), mix)] * 4
        + [pl.BlockSpec((None,) + w_out.shape[1:], lambda bi, i: (li, 0, 0), pipeline_mode=pl.Buffered(1)),
           pl.BlockSpec((None, tm, d), mix),
           pl.BlockSpec((None, 1, d), lambda bi, i: (bi, 0, 2)),
           pl.BlockSpec((1, d), lambda bi, i: (0, 0))],
        out_specs=pl.BlockSpec((None, tm, d), mix),
        scratch_shapes=[pltpu.VMEM(w_out.shape[1:], BF16)],
        compiler_params=_cparams("arbitrary", "arbitrary"),
        name="out_proj",
    )(*outs, w_out, x, mod3, g_post.reshape(1, d))


def _rope_tables(s):
    half = MLA_ROPE // 2
    inv = ROPE_THETA ** (-jnp.arange(half, dtype=F32) / half)
    ang = jnp.arange(s, dtype=F32)[:, None] * inv[None, :]
    z = jnp.zeros((s, 128 - MLA_ROPE), F32)
    cos, sin = jnp.cos(ang), jnp.sin(ang)
    cat = lambda *parts: jnp.concatenate(parts, axis=-1)
    return cat(cos, cos, z), cat(sin, sin, z), cat(cos, cos, sin, sin)


def _rot_cols(w):
    half = w.shape[-1] // 2
    return jnp.concatenate([-w[..., half:], w[..., :half]], axis=-1)


IN_SPLITS = (("a_cq", 384), ("a_ckv", 256), ("a_krope", 64), ("b_q", 512), ("b_k", 512), ("b_v", 512),
             ("c_q", 512), ("c_k", 512), ("c_v", 512), ("c_qidx", 1024), ("c_kidx", 64), ("c_widx", 16),
             ("d_q", 512), ("d_k", 512), ("d_v", 512), ("gate", 2048))
IN_WIDTH = sum(width for _, width in IN_SPLITS)


def _layout_w_in_kernel(w_ref, o_ref):
    src, start = {}, 0
    for name, width in IN_SPLITS:
        src[name] = start
        start += width
    tk = w_ref.shape[1]

    def rows(name, width, offset=0):
        a = src[name] + offset
        return w_ref[a:a + width, :]

    def put(dst, val, scale=None):
        for r in range(0, val.shape[0], 512):
            piece = val[r:r + 512]
            if scale is not None:
                piece = piece * scale
            o_ref[:, dst + r:dst + r + piece.shape[0]] = piece.T.astype(BF16)

    z64 = jnp.zeros((64, tk), F32)
    half = MLA_ROPE // 2
    put(OFF_QIDX, rows("c_qidx", 1024), IDX_DIM ** -0.5)
    put(OFF_A, rows("a_cq", MLA_Q_RANK + MLA_KV_RANK))
    put(OFF_A + 640, jnp.concatenate(
        [rows("a_krope", MLA_ROPE), rows("c_widx", IDX_HEADS) * IDX_HEADS ** -0.5,
         jnp.zeros((128 - MLA_ROPE - IDX_HEADS, tk), F32),
         -rows("a_krope", half, half), rows("a_krope", half), z64,
         rows("c_kidx", IDX_DIM), rows("c_kidx", IDX_DIM)], axis=0))
    for off, name, dim in ((OFF_B, "b", HEAD_DIM), (OFF_C, "c", HEAD_DIM), (OFF_D, "d", HEAD_DIM // 2)):
        put(off, rows(name + "_q", BRANCH_WIDTH), LOG2E * dim ** -0.5)
        put(off + BRANCH_WIDTH, rows(name + "_k", 2 * BRANCH_WIDTH))
    put(OFF_GATE, rows("gate", MIX_WIDTH))


def _layout_w_in(w_in_t, li):
    d = w_in_t.shape[2]
    tk = 256
    return pl.pallas_call(
        _layout_w_in_kernel,
        out_shape=jax.ShapeDtypeStruct((d, PROJ_WIDTH), BF16),
        grid=(d // tk,),
        in_specs=[pl.BlockSpec((None, IN_WIDTH, tk), lambda i: (li, 0, i))],
        out_specs=pl.BlockSpec((tk, PROJ_WIDTH), lambda i: (i, 0)),
        compiler_params=_cparams("arbitrary"),
        name="layout_w_in",
    )(w_in_t)


def _layout_w_uq(w):
    r = w.shape[0]
    w = w.reshape(r, N_HEADS, MLA_NOPE + MLA_ROPE) * (LOG2E * (MLA_NOPE + MLA_ROPE) ** -0.5)
    rope = w[..., MLA_NOPE:]
    return jnp.concatenate([w[..., :MLA_NOPE], rope, _rot_cols(rope)], axis=-1).reshape(r, -1).astype(BF16)


def kernel(x, c, w_ada, b_ada, g_pre, g_post, w_in, g_q_a, w_uq_a, g_kv_a, w_ukv_a,
           lam_q1, lam_k1, lam_q2, lam_k2, g_sub_d, w_out, rel_bias):
    b, s, d = x.shape
    depth = w_ada.shape[0]
    t = min(ATT_TILE, s)
    half = t // 2
    nq = s // t
    near_bias = min(nq, -(-(_first_far_diagonal(half) + 1) // 2))
    near_band = min(nq, -(-(DILATED_PATTERNS[-1][0] // half + 1) // 2))

    rope_tabs = _rope_tables(s)
    bank_b = _bank(rel_bias[:, 0:N_HEADS], 2 * near_band + 1, half, -1, True)
    bank_cd = _bank(rel_bias[:, N_HEADS:3 * N_HEADS], 2 * near_bias + 1, half, -1, False)
    bank_spec = lambda n, group=0: pl.BlockSpec((N_HEADS, n, half, half), lambda bi, i: (group, 0, 0, 0),
                                                pipeline_mode=pl.Buffered(1))

    w_in_t = jnp.swapaxes(w_in, 1, 2)
    mod = _ada_mod(c, w_ada, b_ada)
    for li in range(depth):
        mod3 = mod[li].reshape(b, 1, 3 * d)
        proj = _in_proj(x, g_pre[li], mod3, _layout_w_in(w_in_t, li))

        q_a, k_a, v_a = _mla_prep(proj, rope_tabs, g_q_a[li], g_kv_a[li],
                                  _layout_w_uq(w_uq_a[li]), w_ukv_a[li].astype(BF16))
        gate0 = OFF_GATE // BRANCH_WIDTH
        out_a = _attention("a", q_a, k_a, v_a, proj, 0, gate0)
        out_b = _attention("band", proj, proj, proj, proj, OFF_B // BRANCH_WIDTH, gate0 + 1,
                           extra_in=(bank_b,), extra_specs=(bank_spec(2 * near_band + 1),), near=near_band)
        sel = _select(proj)
        out_c = _attention("sel", proj, proj, proj, proj, OFF_C // BRANCH_WIDTH, gate0 + 2,
                           extra_in=(bank_cd, sel),
                           extra_specs=(bank_spec(2 * near_bias + 1, 0),
                                        pl.BlockSpec((None, t, s), lambda bi, i: (bi, i, 0))),
                           near=near_bias)
        lam_init = 0.8 - 0.6 * math.exp(-0.3 * li)
        lam_vecs = jnp.stack([lam_q1[li], lam_k1[li], lam_q2[li], lam_k2[li]])
        out_d = _attention("diff", proj, proj, proj, proj, OFF_D // BRANCH_WIDTH, gate0 + 3,
                           extra_in=(bank_cd, lam_vecs, g_sub_d[li].reshape(1, HEAD_DIM)),
                           extra_specs=(bank_spec(2 * near_bias + 1, 1),
                                        pl.BlockSpec(lam_vecs.shape, lambda bi, i: (0, 0)),
                                        pl.BlockSpec((1, HEAD_DIM), lambda bi, i: (0, 0))),
                           near=near_bias, lam_init=lam_init)
        x = _out_proj((out_a, out_b, out_c, out_d), w_out, li, x, mod3, g_post[li])
    return x
```

```python
import functools
import math

import numpy as np
import jax
import jax.numpy as jnp
from jax import lax
from jax.experimental import pallas as pl
from jax.experimental.pallas import tpu as pltpu

F32 = jnp.float32
BF16 = jnp.bfloat16

HEAD_DIM = 128
N_HEADS = 4
BRANCH_WIDTH = N_HEADS * HEAD_DIM
MIX_WIDTH = 4 * BRANCH_WIDTH
MLA_Q_RANK = 384
MLA_KV_RANK = 256
MLA_NOPE = 128
MLA_ROPE = 64
ROPE_THETA = 10000.0
DILATED_PATTERNS = ((128, 1), (512, 4), (2048, 16))
IDX_HEADS = 16
IDX_DIM = 64
TOPK_MAX = 256
REL_BUCKETS = 32
REL_MAX_DIST = 2048
NORM_EPS = 1e-6
NEG = -1e30
LOG2E = math.log2(math.e)

OFF_QIDX = 0
OFF_A = 1024
OFF_WIDX = OFF_A + 640
WIDX_LANE = 64
OFF_KIDX = OFF_A + 896
OFF_B = 2048
OFF_C = 3584
OFF_D = 5120
OFF_GATE = 6656
PROJ_WIDTH = 8704

ATT_TILE = 512
V7X_VMEM_BYTES = 64 * 1024 * 1024
VMEM_LIMIT = V7X_VMEM_BYTES // 8 * 7


def _cparams(*sem):
    return pltpu.CompilerParams(dimension_semantics=sem, vmem_limit_bytes=VMEM_LIMIT)


def _split_bf16(x):
    hi = x.astype(BF16)
    return hi, (x - hi.astype(F32)).astype(BF16)


def _ada_kernel(c_ref, w_ref, b_ref, o_ref):
    c = c_ref[...]
    a_hi, a_lo = _split_bf16(c * jax.nn.sigmoid(c))
    w_hi, w_lo = _split_bf16(w_ref[...])
    dot = functools.partial(jnp.dot, preferred_element_type=F32)
    o_ref[...] = dot(a_hi, w_hi) + (dot(a_hi, w_lo) + dot(a_lo, w_hi)) + b_ref[...]


def _ada_mod(c, w_ada, b_ada):
    depth, d, n = w_ada.shape
    b = c.shape[0]
    tn = 768
    return pl.pallas_call(
        _ada_kernel,
        out_shape=jax.ShapeDtypeStruct((depth, b, n), F32),
        grid=(depth, n // tn),
        in_specs=[pl.BlockSpec((b, d), lambda l, j: (0, 0)),
                  pl.BlockSpec((None, d, tn), lambda l, j: (l, 0, j)),
                  pl.BlockSpec((None, 1, tn), lambda l, j: (l, 0, j))],
        out_specs=pl.BlockSpec((None, b, tn), lambda l, j: (l, 0, j)),
        compiler_params=_cparams("arbitrary", "arbitrary"),
        name="ada_mod",
    )(c, w_ada, b_ada.reshape(depth, 1, n))


def _in_proj_kernel(x_ref, g_ref, shift_ref, scale_ref, w_ref, o_ref):
    x = x_ref[...]
    y = x * lax.rsqrt(jnp.mean(x * x, axis=-1, keepdims=True) + NORM_EPS) * g_ref[...]
    h = (y * (1.0 + scale_ref[...]) + shift_ref[...]).astype(BF16)
    o_ref[...] = jnp.dot(h, w_ref[...], preferred_element_type=F32).astype(o_ref.dtype)


def _in_proj(x, g_pre, mod3, w):
    b, s, d = x.shape
    n = w.shape[1]
    tm = min(512, s)
    nt = s // tm
    tn = n // 2
    row = lambda j, i: (i // nt, i % nt, 0)
    return pl.pallas_call(
        _in_proj_kernel,
        out_shape=jax.ShapeDtypeStruct((b, s, n), BF16),
        grid=(n // tn, b * nt),
        in_specs=[pl.BlockSpec((None, tm, d), row),
                  pl.BlockSpec((1, d), lambda j, i: (0, 0)),
                  pl.BlockSpec((None, 1, d), lambda j, i: (i // nt, 0, 0)),
                  pl.BlockSpec((None, 1, d), lambda j, i: (i // nt, 0, 1)),
                  pl.BlockSpec((d, tn), lambda j, i: (0, j), pipeline_mode=pl.Buffered(1))],
        out_specs=pl.BlockSpec((None, tm, tn), lambda j, i: (i // nt, i % nt, j)),
        compiler_params=_cparams("arbitrary", "arbitrary"),
        name="in_proj",
    )(x, g_pre.reshape(1, d), mod3, mod3, w)


def _rms(x, g):
    return x * lax.rsqrt(jnp.mean(x * x, axis=-1, keepdims=True) + NORM_EPS) * g


MLA_RING = 3


def _mla_prep_kernel(proj_hbm, cos_ref, sin_ref, cs_ref, gq_ref, gkv_ref, wq_ref, wkv_ref, q_ref, k_ref, v_ref,
                     ring_ref, sem_ref, *, tiles_per_batch):
    n = pl.program_id(0)
    n_steps = pl.num_programs(0)
    tm = ring_ref.shape[1]

    def fetch(step):
        rows = pl.ds(pl.multiple_of((step % tiles_per_batch) * tm, tm), tm)
        slot = step % MLA_RING
        return pltpu.make_async_copy(proj_hbm.at[step // tiles_per_batch, rows, pl.ds(OFF_A, ring_ref.shape[2])],
                                     ring_ref.at[slot], sem_ref.at[slot])

    @pl.when(n == 0)
    def _():
        fetch(n).start()

        @pl.when(n_steps > 1)
        def _():
            fetch(n + 1).start()

    @pl.when(n + 2 < n_steps)
    def _():
        fetch(n + 2).start()

    fetch(n).wait()
    p_ref = ring_ref.at[n % MLA_RING]
    cos = cos_ref[...]
    sin = sin_ref[...]
    cs = cs_ref[...]
    cq = _rms(p_ref[:, 0:MLA_Q_RANK].astype(F32), gq_ref[...]).astype(BF16)
    ckv = _rms(p_ref[:, MLA_Q_RANK:MLA_Q_RANK + MLA_KV_RANK].astype(F32), gkv_ref[...]).astype(BF16)
    q = jnp.dot(cq, wq_ref[...], preferred_element_type=F32)
    kv = jnp.dot(ckv, wkv_ref[...], preferred_element_type=F32)
    k_rope = (p_ref[:, 640:768].astype(F32) * cos + p_ref[:, 768:896].astype(F32) * sin).astype(BF16)
    for h in range(N_HEADS):
        q_ref[:, h * 256:h * 256 + 128] = q[:, h * 256:h * 256 + 128].astype(BF16)
        z = q[:, h * 256 + 128:(h + 1) * 256] * cs
        q_ref[:, h * 256 + 128:(h + 1) * 256] = (z + pltpu.roll(z, MLA_ROPE, axis=1)).astype(BF16)
        k_ref[:, h * 256:h * 256 + 128] = kv[:, h * 256:h * 256 + 128].astype(BF16)
        k_ref[:, h * 256 + 128:(h + 1) * 256] = k_rope
        v_ref[:, h * 128:(h + 1) * 128] = kv[:, h * 256 + 128:(h + 1) * 256].astype(BF16)


def _mla_prep(proj, rope_tabs, g_q, g_kv, wq, wkv):
    b, s, _ = proj.shape
    tm = min(512, s)
    nt = s // tm
    const = lambda n: (0, 0)
    table = pl.BlockSpec((tm, 128), lambda n: (n % nt, 0))
    out = lambda width: pl.BlockSpec((None, tm, width), lambda n: (n // nt, n % nt, 0))
    return pl.pallas_call(
        functools.partial(_mla_prep_kernel, tiles_per_batch=nt),
        out_shape=(jax.ShapeDtypeStruct((b, s, N_HEADS * 256), BF16),
                   jax.ShapeDtypeStruct((b, s, N_HEADS * 256), BF16),
                   jax.ShapeDtypeStruct((b, s, BRANCH_WIDTH), BF16)),
        grid=(b * nt,),
        in_specs=[pl.BlockSpec(memory_space=pl.ANY), table, table, table,
                  pl.BlockSpec((1, MLA_Q_RANK), const),
                  pl.BlockSpec((1, MLA_KV_RANK), const),
                  pl.BlockSpec(wq.shape, const),
                  pl.BlockSpec(wkv.shape, const)],
        out_specs=(out(N_HEADS * 256), out(N_HEADS * 256), out(BRANCH_WIDTH)),
        scratch_shapes=[pltpu.VMEM((MLA_RING, tm, 1024), BF16), pltpu.SemaphoreType.DMA((MLA_RING,))],
        compiler_params=_cparams("arbitrary"),
        name="mla_prep",
    )(proj, *rope_tabs, g_q.reshape(1, -1), g_kv.reshape(1, -1), wq, wkv)


def _bucket_np(n):
    max_exact = REL_BUCKETS // 2
    nf = np.maximum(n, max_exact).astype(np.float32)
    large = max_exact + (np.log(nf / np.float32(max_exact)) / np.float32(math.log(REL_MAX_DIST / max_exact))
                         * np.float32(REL_BUCKETS - max_exact)).astype(np.int32)
    return np.where(n < max_exact, n, np.minimum(large, REL_BUCKETS - 1)).astype(np.int32)


def _bucket_starts():
    buckets = _bucket_np(np.arange(2 * REL_MAX_DIST))
    return [int(np.argmax(buckets >= b)) for b in range(REL_BUCKETS)]


def _bank_kernel(tab_ref, o_ref, *, t, d_min, band):
    h = pl.program_id(0)
    starts = _bucket_starts()
    row = lax.broadcasted_iota(jnp.int32, (t, t), 0)
    col = lax.broadcasted_iota(jnp.int32, (t, t), 1)
    for j in range(o_ref.shape[0]):
        d = d_min + j
        if d < 0:
            o_ref[j] = jnp.full((t, t), NEG, F32)
            continue
        dist = t * d + row - col
        b_lo, b_hi = (int(x) for x in _bucket_np(np.array([max(t * d - t + 1, 0), t * d + t - 1])))
        val = jnp.full((t, t), tab_ref[b_lo, h], F32)
        for b in range(b_lo + 1, b_hi + 1):
            val = jnp.where(dist >= starts[b], tab_ref[b, h], val)
        if band:
            mult = jnp.zeros((t, t), jnp.int32)
            for window, dil in DILATED_PATTERNS:
                mult += jnp.where((dist >= 0) & (dist <= window) & ((dist & (dil - 1)) == 0), 1, 0)
            val = val + jnp.where(mult == 3, math.log(3.0), jnp.where(mult == 2, math.log(2.0), 0.0))
            keep = mult > 0
        else:
            val = val - tab_ref[REL_BUCKETS - 1, h]
            keep = dist >= 0
        o_ref[j] = jnp.where(keep, val * LOG2E, NEG)


def _bank(tab, n_tables, t, d_min, band):
    nh = tab.shape[1]
    return pl.pallas_call(
        functools.partial(_bank_kernel, t=t, d_min=d_min, band=band),
        out_shape=jax.ShapeDtypeStruct((nh, n_tables, t, t), F32),
        grid=(nh,),
        in_specs=[pl.BlockSpec(memory_space=pltpu.SMEM)],
        out_specs=pl.BlockSpec((None, n_tables, t, t), lambda h: (h, 0, 0, 0)),
        compiler_params=_cparams("arbitrary"),
        name="bank_band" if band else "bank_bias",
    )(tab)


def _first_far_diagonal(t):
    last = _bucket_starts()[REL_BUCKETS - 1]
    return -(-(last + t - 1) // t)


def _qk(q, k):
    return lax.dot_general(q, k, (((1,), (1,)), ((), ())), preferred_element_type=F32)


def _flash_init(m_ref, acc_ref):
    m_ref[...] = jnp.full(m_ref.shape, NEG, F32)
    acc_ref[...] = jnp.zeros(acc_ref.shape, F32)


def _flash_update(slot, s, v, m_ref, acc_ref, s_ref, next_scores, diagonal=False):
    t = s.shape[0]
    v_ones = jnp.concatenate([v, jnp.ones_like(v)], axis=1)
    blocks = [(0, t // 2, t // 2), (t // 2, t, t)] if diagonal else [(0, t, t)]
    for n, (r0, r1, keys) in enumerate(blocks):
        m_prev = m_ref[slot, r0:r1]
        m_new = jnp.maximum(m_prev, jnp.max(s[r0:r1, :keys], axis=-1, keepdims=True))
        alpha = jnp.exp2(m_prev - m_new)
        p = jnp.concatenate([jnp.exp2(s[r0:r1, j * 128:(j + 1) * 128] - m_new) for j in range(keys // 128)],
                            axis=1).astype(BF16)
        if n == len(blocks) - 1 and next_scores is not None:
            s_ref[slot] = next_scores()
        acc_ref[slot, r0:r1] = (jnp.concatenate([alpha, alpha], axis=1) * acc_ref[slot, r0:r1]
                                + jnp.dot(p, v_ones[:keys], preferred_element_type=F32))
        m_ref[slot, r0:r1] = m_new


def _flash_result(slot, acc_ref):
    acc = acc_ref[slot]
    return acc[:, :HEAD_DIM] / acc[:, HEAD_DIM:]


def _silu(g):
    return g * jax.nn.sigmoid(g)


def _tile(ref, kj, t, c0, width):
    start = kj * t if isinstance(kj, int) else pl.multiple_of(kj * t, t)
    return ref[pl.ds(start, t), c0:c0 + width]


def _bank_tile(bank_ref, h, i, kj):
    base = 2 * (i - kj) + 1
    top = jnp.concatenate([bank_ref[h, base], bank_ref[h, base - 1]], axis=1)
    bot = jnp.concatenate([bank_ref[h, base + 1], bank_ref[h, base]], axis=1)
    return jnp.concatenate([top, bot], axis=0)


def _emit(o_ref, g_ref, h, o):
    gate = g_ref[:, h * 128:(h + 1) * 128].astype(F32)
    o_ref[:, h * 128:(h + 1) * 128] = (o * _silu(gate)).astype(BF16)


def _flash_walk(i, first_key, n_far, slots, score, logits, value, s_ref, m_ref, acc_ref, prep=None):
    lo = first_key(i)
    i_next = jnp.minimum(i + 1, pl.num_programs(1) - 1)
    _flash_init(m_ref, acc_ref)

    @pl.when(i == 0)
    def _():
        for slot in range(slots):
            s_ref[slot] = score(slot, False, lo)

    def step(kj, phase):
        ctx = prep(kj) if prep is not None else None
        for slot in range(slots):
            s = logits(slot, s_ref[slot], kj, phase, ctx)
            if phase == "last":
                nxt = functools.partial(score, slot, True, first_key(i_next))
            else:
                nxt = functools.partial(score, slot, False, kj + 1)
            _flash_update(slot, s, value(slot, kj), m_ref, acc_ref, s_ref, nxt, diagonal=phase == "last")

    def walk(a, b, phase):
        def body(kj, carry):
            step(kj, phase)
            return carry
        lax.fori_loop(a, b, body, 0)

    if n_far is not None:
        walk(lo, n_far, "far")
        lo = n_far
    walk(lo, i, "near")
    step(i, "last")


def _attn_a_kernel(q_ref, qn_ref, k_ref, v_ref, g_ref, o_ref, s_ref, m_ref, acc_ref, *, t):
    i = pl.program_id(1)

    def score(h, next_q, kj):
        q = (qn_ref if next_q else q_ref)[:, h * 256:(h + 1) * 256]
        return _qk(q, _tile(k_ref, kj, t, h * 256, 256))

    def logits(h, s, kj, phase, ctx):
        if phase != "last":
            return s
        causal = lax.broadcasted_iota(jnp.int32, (t, t), 1) <= lax.broadcasted_iota(jnp.int32, (t, t), 0)
        return jnp.where(causal, s, NEG)

    def value(h, kj):
        return _tile(v_ref, kj, t, h * 128, 128)

    _flash_walk(i, lambda qi: 0, None, N_HEADS, score, logits, value, s_ref, m_ref, acc_ref)
    for h in range(N_HEADS):
        _emit(o_ref, g_ref, h, _flash_result(h, acc_ref))


def _attn_band_kernel(q_ref, qn_ref, k_ref, v_ref, g_ref, bank_ref, o_ref, s_ref, m_ref, acc_ref, *, t, near):
    i = pl.program_id(1)

    def score(h, next_q, kj):
        q = (qn_ref if next_q else q_ref)[:, h * 128:(h + 1) * 128]
        return _qk(q, _tile(k_ref, kj, t, h * 128, 128))

    def logits(h, s, kj, phase, ctx):
        return s + _bank_tile(bank_ref, h, i, kj)

    def value(h, kj):
        return _tile(v_ref, kj, t, h * 128, 128)

    _flash_walk(i, lambda qi: jnp.maximum(qi - (near - 1), 0), None, N_HEADS, score, logits, value,
                s_ref, m_ref, acc_ref)
    for h in range(N_HEADS):
        _emit(o_ref, g_ref, h, _flash_result(h, acc_ref))


def _attn_sel_kernel(q_ref, qn_ref, k_ref, v_ref, g_ref, bank_ref, sel_ref, o_ref,
                     s_ref, m_ref, acc_ref, *, t, near):
    i = pl.program_id(1)

    def score(h, next_q, kj):
        q = (qn_ref if next_q else q_ref)[:, h * 128:(h + 1) * 128]
        return _qk(q, _tile(k_ref, kj, t, h * 128, 128))

    def prep(kj):
        return sel_ref[:, pl.ds(pl.multiple_of(kj * t, t), t)].astype(F32)

    def logits(h, s, kj, phase, sel):
        return s + sel if phase == "far" else s + (sel + _bank_tile(bank_ref, h, i, kj))

    def value(h, kj):
        return _tile(v_ref, kj, t, h * 128, 128)

    _flash_walk(i, lambda qi: 0, jnp.maximum(i - (near - 1), 0), N_HEADS, score, logits, value,
                s_ref, m_ref, acc_ref, prep)
    for h in range(N_HEADS):
        _emit(o_ref, g_ref, h, _flash_result(h, acc_ref))


def _attn_diff_kernel(q_ref, qn_ref, k_ref, v_ref, g_ref, bank_ref, lam_ref, gsub_ref, o_ref,
                      s_ref, m_ref, acc_ref, *, t, near, lam_init):
    i = pl.program_id(1)
    first_half = lax.broadcasted_iota(jnp.int32, (t, HEAD_DIM), 1) < HEAD_DIM // 2

    def score(slot, next_q, kj):
        h = slot // 2
        q = (qn_ref if next_q else q_ref)[:, h * 128:(h + 1) * 128]
        keep = first_half if slot % 2 == 0 else jnp.logical_not(first_half)
        q = jnp.where(keep, q, jnp.zeros_like(q))
        return _qk(q, _tile(k_ref, kj, t, h * 128, 128))

    def logits(slot, s, kj, phase, ctx):
        return s if phase == "far" else s + _bank_tile(bank_ref, slot // 2, i, kj)

    def value(slot, kj):
        return _tile(v_ref, kj, t, (slot // 2) * 128, 128)

    _flash_walk(i, lambda qi: 0, jnp.maximum(i - (near - 1), 0), 2 * N_HEADS, score, logits, value,
                s_ref, m_ref, acc_ref)
    lam_v = lam_ref[...]
    lam = (jnp.exp(jnp.sum(lam_v[0:1] * lam_v[1:2], axis=-1, keepdims=True))
           - jnp.exp(jnp.sum(lam_v[2:3] * lam_v[3:4], axis=-1, keepdims=True)) + lam_init)
    for h in range(N_HEADS):
        o = _flash_result(2 * h, acc_ref) - lam * _flash_result(2 * h + 1, acc_ref)
        _emit(o_ref, g_ref, h, _rms(o, gsub_ref[...]) * (1.0 - lam_init))


def _attention(kind, q_arr, k_arr, v_arr, proj, q_blk, gate_blk, extra_in=(), extra_specs=(), **kw):
    b, s, _ = proj.shape
    t = min(ATT_TILE, s)
    dk = 256 if kind == "a" else 128
    qw = N_HEADS * dk
    k_blk = 0 if kind == "a" else q_blk + 1
    v_blk = 0 if kind == "a" else q_blk + 2
    body = {"a": _attn_a_kernel, "band": _attn_band_kernel, "sel": _attn_sel_kernel,
            "diff": _attn_diff_kernel}[kind]
    slots = 2 * N_HEADS if kind == "diff" else N_HEADS
    scratch = [pltpu.VMEM((slots, t, t), F32), pltpu.VMEM((slots, t, HEAD_DIM), F32),
               pltpu.VMEM((slots, t, 2 * HEAD_DIM), F32)]
    last = s // t - 1
    in_specs = [pl.BlockSpec((None, t, qw), lambda bi, i: (bi, i, q_blk)),
                pl.BlockSpec((None, t, qw), lambda bi, i: (bi, jnp.minimum(i + 1, last), q_blk)),
                pl.BlockSpec((None, s, qw), lambda bi, i: (bi, 0, k_blk)),
                pl.BlockSpec((None, s, BRANCH_WIDTH), lambda bi, i: (bi, 0, v_blk)),
                pl.BlockSpec((None, t, BRANCH_WIDTH), lambda bi, i: (bi, i, gate_blk))]
    in_specs += list(extra_specs)
    return pl.pallas_call(
        functools.partial(body, t=t, **kw),
        out_shape=jax.ShapeDtypeStruct((b, s, BRANCH_WIDTH), BF16),
        grid=(b, s // t),
        in_specs=in_specs,
        out_specs=pl.BlockSpec((None, t, BRANCH_WIDTH), lambda bi, i: (bi, i, 0)),
        scratch_shapes=scratch,
        compiler_params=_cparams("arbitrary", "arbitrary"),
        name="attn_" + kind,
    )(q_arr, q_arr, k_arr, v_arr, proj, *extra_in)


def _bit_transpose32(words):
    a = list(words)
    j, m = 16, 0x0000FFFF
    while j:
        for k in range(32):
            if not k & j:
                t = (a[k] ^ lax.shift_right_logical(a[k + j], jnp.int32(j))) & jnp.int32(m)
                a[k] = a[k] ^ t
                a[k + j] = a[k + j] ^ (t << j)
        j >>= 1
        m = (m ^ (m << j)) & 0xFFFFFFFF if j else m
    return a


def _select_kernel(qi_ref, ki_ref, wi_ref, o_ref, key_ref, plane_ref, alive_ref, stat_ref, *, tq, kc, n_sel):
    i = pl.program_id(1)
    s_len = o_ref.shape[1]
    n_ch = (i * tq + tq + kc - 1) // kc
    wpc = kc // 32
    int_min = jnp.int32(-2 ** 31)
    lane = lax.broadcasted_iota(jnp.int32, (tq, 128), 1)
    w_t = wi_ref[...].astype(F32).T[WIDX_LANE:WIDX_LANE + IDX_HEADS]
    q_heads = []
    for j in range(IDX_HEADS // 2):
        q2 = qi_ref[:, j * 128:(j + 1) * 128]
        q_heads.append(jnp.where(lane < IDX_DIM, q2, jnp.zeros_like(q2)))
        q_heads.append(jnp.where(lane >= IDX_DIM, q2, jnp.zeros_like(q2)))
    kiota = lax.broadcasted_iota(jnp.int32, (kc, tq), 0)

    def chunk(c):
        return pl.ds(pl.multiple_of(c * kc, kc), kc)

    def scored(k, q_from):
        acc = jnp.zeros((k.shape[0], tq - q_from), F32)
        for hh in range(IDX_HEADS):
            acc = acc + jnp.maximum(_qk(k, q_heads[hh][q_from:, :]), 0.0) * w_t[hh:hh + 1, q_from:]
        return acc + 0.0

    def causal(acc):
        key = lax.broadcasted_iota(jnp.int32, acc.shape, 0)
        return jnp.where(key <= lax.broadcasted_iota(jnp.int32, acc.shape, 1), acc, NEG)

    def store_keys(row0, acc):
        bits = pltpu.bitcast(acc, jnp.int32)
        keys = bits ^ ((bits >> 31) & jnp.int32(0x7FFFFFFF))
        key_ref[pl.ds(row0, acc.shape[0]), :] = keys
        ukeys = keys ^ int_min
        for blk in range(acc.shape[0] // 256):
            planes = _bit_transpose32([ukeys[blk * 256 + 8 * j:blk * 256 + 8 * j + 8, :] for j in range(32)])
            row = pl.multiple_of(row0 // 32 + blk * 8, 8)
            for b in range(32):
                plane_ref[b, pl.ds(row, 8), :] = planes[b]

    def score_chunk(c, _):
        store_keys(pl.multiple_of(c * kc, kc), scored(ki_ref[chunk(c), :], 0))
        return 0

    lax.fori_loop(0, i, score_chunk, 0)
    diag = pl.multiple_of(i * kc, kc)
    half = kc // 2
    if half % 256 == 0:
        store_keys(diag, causal(scored(ki_ref[pl.ds(diag, half), :], 0)))
        late = causal(scored(ki_ref[pl.ds(diag + half, half), :], half))
        store_keys(diag + half, jnp.concatenate([jnp.full((half, half), NEG, F32), late], axis=1))
    else:
        store_keys(diag, causal(scored(ki_ref[pl.ds(diag, kc), :], 0)))

    def clear_chunk(c, _):
        plane_ref[:, pl.ds(pl.multiple_of(c * wpc, wpc), wpc), :] = jnp.zeros((32, wpc, tq), jnp.int32)
        return 0

    @pl.when(i == 0)
    def _():
        lax.fori_loop(n_ch, s_len // kc, clear_chunk, 0)

    n_rows = s_len // 32
    word_row = lax.broadcasted_iota(jnp.int32, (n_rows, tq), 0)
    alive_ref[...] = jnp.where(word_row < n_ch * wpc, jnp.int32(-1), jnp.int32(0))
    zeros = jnp.zeros((1, tq), jnp.int32)

    def radix(rows):
        def radix_step(bi, carry):
            thr, above = carry
            alive = alive_ref[0:rows]
            plane = plane_ref[bi, 0:rows]
            ones = lax.population_count(alive & plane)
            ones = jnp.sum(jnp.sum(ones.reshape(rows // 8, 8, tq), axis=0), axis=0, keepdims=True)
            take = above + ones >= n_sel
            thr = jnp.where(take, thr | (jnp.int32(1) << (31 - bi)), thr)
            above = jnp.where(take, above, above + ones)
            alive_ref[0:rows] = alive & (plane ^ jnp.where(take, jnp.int32(0), jnp.int32(-1)))
            return thr, above

        thr, above = lax.fori_loop(0, 32, radix_step, (zeros, zeros))
        stat_ref[0:8] = jnp.broadcast_to(thr, (8, tq))
        stat_ref[8:16] = jnp.broadcast_to(above, (8, tq))

    quarter = max(n_rows // 4, 8)
    sizes = sorted({min(quarter * (n + 1), n_rows) for n in range(4)})
    for n, rows in enumerate(sizes):
        lower = sizes[n - 1] if n else 0
        pl.when((n_ch * wpc > lower) & (n_ch * wpc <= rows))(functools.partial(radix, rows))
    thr_u, above = stat_ref[0:1], stat_ref[8:9]
    thr = thr_u ^ int_min
    n_equal = lax.population_count(alive_ref[...])
    n_equal = jnp.sum(jnp.sum(n_equal.reshape(n_rows // 8, 8, tq), axis=0), axis=0, keepdims=True)
    need = n_sel - above
    masked_key = int(np.float32(NEG).view(np.int32)) ^ 0x7FFFFFFF
    tie = (n_equal > need) & (thr != masked_key)
    any_tie = jnp.max(jnp.where(tie, 1, 0)) > 0

    @pl.when(jnp.logical_not(any_tie))
    def _():
        def emit(c, _):
            keep = jnp.where(key_ref[chunk(c), :] >= thr, 0.0, NEG)
            o_ref[:, chunk(c)] = keep.T.astype(BF16)
            return 0

        lax.fori_loop(0, n_ch, emit, 0)

    @pl.when(any_tie)
    def _():
        def equal_below(bound):
            def body(c, part):
                hit = jnp.where((key_ref[chunk(c), :] == thr) & (c * kc + kiota < bound), 1, 0)
                return part + jnp.sum(hit.reshape(kc // 8, 8, tq), axis=0)
            part = lax.fori_loop(0, n_ch, body, jnp.zeros((8, tq), jnp.int32))
            return jnp.sum(part, axis=0, keepdims=True)

        n_bits = s_len.bit_length()

        def bound_step(bi, cut):
            cand = cut + (jnp.int32(1) << (n_bits - 1 - bi))
            ok = (cand <= s_len) & (equal_below(cand) <= need)
            return jnp.where(ok, cand, cut)

        cut = lax.fori_loop(0, n_bits, bound_step, zeros)

        def emit(c, _):
            keys = key_ref[chunk(c), :]
            kept = (keys > thr) | ((keys == thr) & (c * kc + kiota < cut))
            o_ref[:, chunk(c)] = jnp.where(kept, 0.0, NEG).T.astype(BF16)
            return 0

        lax.fori_loop(0, n_ch, emit, 0)

    def blank(c, _):
        o_ref[:, chunk(c)] = jnp.full((tq, kc), NEG, BF16)
        return 0

    lax.fori_loop(n_ch, s_len // kc, blank, 0)


def _select(proj):
    b, s, _ = proj.shape
    tq = kc = min(512, s)
    n_sel = min(TOPK_MAX, s // 4)
    return pl.pallas_call(
        functools.partial(_select_kernel, tq=tq, kc=kc, n_sel=n_sel),
        out_shape=jax.ShapeDtypeStruct((b, s, s), BF16),
        grid=(b, s // tq),
        in_specs=[pl.BlockSpec((None, tq, 1024), lambda bi, i: (bi, i, OFF_QIDX // 1024)),
                  pl.BlockSpec((None, s, 128), lambda bi, i: (bi, 0, OFF_KIDX // 128)),
                  pl.BlockSpec((None, tq, 128), lambda bi, i: (bi, i, OFF_WIDX // 128))],
        out_specs=pl.BlockSpec((None, tq, s), lambda bi, i: (bi, i, 0)),
        scratch_shapes=[pltpu.VMEM((s, tq), jnp.int32), pltpu.VMEM((32, s // 32, tq), jnp.int32),
                        pltpu.VMEM((s // 32, tq), jnp.int32), pltpu.VMEM((16, tq), jnp.int32)],
        compiler_params=_cparams("arbitrary", "arbitrary"),
        name="idx_select",
    )(proj, proj, proj)


def _out_kernel(a_ref, b_ref, c_ref, d_ref, w_ref, x_ref, gate_ref, g_ref, o_ref, wb_ref):
    @pl.when((pl.program_id(0) == 0) & (pl.program_id(1) == 0))
    def _():
        wb_ref[...] = w_ref[...].astype(BF16)

    y = jnp.dot(a_ref[...], wb_ref[0:512, :], preferred_element_type=F32)
    y += jnp.dot(b_ref[...], wb_ref[512:1024, :], preferred_element_type=F32)
    y += jnp.dot(c_ref[...], wb_ref[1024:1536, :], preferred_element_type=F32)
    y += jnp.dot(d_ref[...], wb_ref[1536:2048, :], preferred_element_type=F32)
    o_ref[...] = x_ref[...] + gate_ref[...] * _rms(y, g_ref[...])


def _out_proj(outs, w_out, li, x, mod3, g_post):
    b, s, d = x.shape
    tm = min(512, s)
    mix = lambda bi, i: (bi, i, 0)
    return pl.pallas_call(
        _out_kernel,
        out_shape=jax.ShapeDtypeStruct((b, s, d), F32),
        grid=(b, s // tm),
        in_specs=[pl.BlockSpec((None, tm, BRANCH_WIDTH), mix)] * 4
        + [pl.BlockSpec((None,) + w_out.shape[1:], lambda bi, i: (li, 0, 0), pipeline_mode=pl.Buffered(1)),
           pl.BlockSpec((None, tm, d), mix),
           pl.BlockSpec((None, 1, d), lambda bi, i: (bi, 0, 2)),
           pl.BlockSpec((1, d), lambda bi, i: (0, 0))],
        out_specs=pl.BlockSpec((None, tm, d), mix),
        scratch_shapes=[pltpu.VMEM(w_out.shape[1:], BF16)],
        compiler_params=_cparams("arbitrary", "arbitrary"),
        name="out_proj",
    )(*outs, w_out, x, mod3, g_post.reshape(1, d))


def _rope_tables(s):
    half = MLA_ROPE // 2
    inv = ROPE_THETA ** (-jnp.arange(half, dtype=F32) / half)
    ang = jnp.arange(s, dtype=F32)[:, None] * inv[None, :]
    z = jnp.zeros((s, 128 - MLA_ROPE), F32)
    cos, sin = jnp.cos(ang), jnp.sin(ang)
    cat = lambda *parts: jnp.concatenate(parts, axis=-1)
    return cat(cos, cos, z), cat(sin, sin, z), cat(cos, cos, sin, sin)


def _rot_cols(w):
    half = w.shape[-1] // 2
    return jnp.concatenate([-w[..., half:], w[..., :half]], axis=-1)


IN_SPLITS = (("a_cq", 384), ("a_ckv", 256), ("a_krope", 64), ("b_q", 512), ("b_k", 512), ("b_v", 512),
             ("c_q", 512), ("c_k", 512), ("c_v", 512), ("c_qidx", 1024), ("c_kidx", 64), ("c_widx", 16),
             ("d_q", 512), ("d_k", 512), ("d_v", 512), ("gate", 2048))
IN_WIDTH = sum(width for _, width in IN_SPLITS)


def _layout_w_in_kernel(w_ref, o_ref):
    src, start = {}, 0
    for name, width in IN_SPLITS:
        src[name] = start
        start += width
    tk = w_ref.shape[1]

    def rows(name, width, offset=0):
        a = src[name] + offset
        return w_ref[a:a + width, :]

    def put(dst, val, scale=None):
        for r in range(0, val.shape[0], 512):
            piece = val[r:r + 512]
            if scale is not None:
                piece = piece * scale
            o_ref[:, dst + r:dst + r + piece.shape[0]] = piece.T.astype(BF16)

    z64 = jnp.zeros((64, tk), F32)
    half = MLA_ROPE // 2
    put(OFF_QIDX, rows("c_qidx", 1024), IDX_DIM ** -0.5)
    put(OFF_A, rows("a_cq", MLA_Q_RANK + MLA_KV_RANK))
    put(OFF_A + 640, jnp.concatenate(
        [rows("a_krope", MLA_ROPE), rows("c_widx", IDX_HEADS) * IDX_HEADS ** -0.5,
         jnp.zeros((128 - MLA_ROPE - IDX_HEADS, tk), F32),
         -rows("a_krope", half, half), rows("a_krope", half), z64,
         rows("c_kidx", IDX_DIM), rows("c_kidx", IDX_DIM)], axis=0))
    for off, name, dim in ((OFF_B, "b", HEAD_DIM), (OFF_C, "c", HEAD_DIM), (OFF_D, "d", HEAD_DIM // 2)):
        put(off, rows(name + "_q", BRANCH_WIDTH), LOG2E * dim ** -0.5)
        put(off + BRANCH_WIDTH, rows(name + "_k", 2 * BRANCH_WIDTH))
    put(OFF_GATE, rows("gate", MIX_WIDTH))


def _layout_w_in(w_in_t, li):
    d = w_in_t.shape[2]
    tk = 256
    return pl.pallas_call(
        _layout_w_in_kernel,
        out_shape=jax.ShapeDtypeStruct((d, PROJ_WIDTH), BF16),
        grid=(d // tk,),
        in_specs=[pl.BlockSpec((None, IN_WIDTH, tk), lambda i: (li, 0, i))],
        out_specs=pl.BlockSpec((tk, PROJ_WIDTH), lambda i: (i, 0)),
        compiler_params=_cparams("arbitrary"),
        name="layout_w_in",
    )(w_in_t)


def _layout_w_uq(w):
    r = w.shape[0]
    w = w.reshape(r, N_HEADS, MLA_NOPE + MLA_ROPE) * (LOG2E * (MLA_NOPE + MLA_ROPE) ** -0.5)
    rope = w[..., MLA_NOPE:]
    return jnp.concatenate([w[..., :MLA_NOPE], rope, _rot_cols(rope)], axis=-1).reshape(r, -1).astype(BF16)


def kernel(x, c, w_ada, b_ada, g_pre, g_post, w_in, g_q_a, w_uq_a, g_kv_a, w_ukv_a,
           lam_q1, lam_k1, lam_q2, lam_k2, g_sub_d, w_out, rel_bias):
    b, s, d = x.shape
    depth = w_ada.shape[0]
    t = min(ATT_TILE, s)
    half = t // 2
    nq = s // t
    near_bias = min(nq, -(-(_first_far_diagonal(half) + 1) // 2))
    near_band = min(nq, -(-(DILATED_PATTERNS[-1][0] // half + 1) // 2))

    rope_tabs = _rope_tables(s)
    bank_b = _bank(rel_bias[:, 0:N_HEADS], 2 * near_band + 1, half, -1, True)
    bank_cd = _bank(rel_bias[:, N_HEADS:3 * N_HEADS], 2 * near_bias + 1, half, -1, False)
    bank_spec = lambda n, group=0: pl.BlockSpec((N_HEADS, n, half, half), lambda bi, i: (group, 0, 0, 0),
                                                pipeline_mode=pl.Buffered(1))

    w_in_t = jnp.swapaxes(w_in, 1, 2)
    mod = _ada_mod(c, w_ada, b_ada)
    for li in range(depth):
        mod3 = mod[li].reshape(b, 1, 3 * d)
        proj = _in_proj(x, g_pre[li], mod3, _layout_w_in(w_in_t, li))

        q_a, k_a, v_a = _mla_prep(proj, rope_tabs, g_q_a[li], g_kv_a[li],
                                  _layout_w_uq(w_uq_a[li]), w_ukv_a[li].astype(BF16))
        gate0 = OFF_GATE // BRANCH_WIDTH
        out_a = _attention("a", q_a, k_a, v_a, proj, 0, gate0)
        out_b = _attention("band", proj, proj, proj, proj, OFF_B // BRANCH_WIDTH, gate0 + 1,
                           extra_in=(bank_b,), extra_specs=(bank_spec(2 * near_band + 1),), near=near_band)
        sel = _select(proj)
        out_c = _attention("sel", proj, proj, proj, proj, OFF_C // BRANCH_WIDTH, gate0 + 2,
                           extra_in=(bank_cd, sel),
                           extra_specs=(bank_spec(2 * near_bias + 1, 0),
                                        pl.BlockSpec((None, t, s), lambda bi, i: (bi, i, 0))),
                           near=near_bias)
        lam_init = 0.8 - 0.6 * math.exp(-0.3 * li)
        lam_vecs = jnp.stack([lam_q1[li], lam_k1[li], lam_q2[li], lam_k2[li]])
        out_d = _attention("diff", proj, proj, proj, proj, OFF_D // BRANCH_WIDTH, gate0 + 3,
                           extra_in=(bank_cd, lam_vecs, g_sub_d[li].reshape(1, HEAD_DIM)),
                           extra_specs=(bank_spec(2 * near_bias + 1, 1),
                                        pl.BlockSpec(lam_vecs.shape, lambda bi, i: (0, 0)),
                                        pl.BlockSpec((1, HEAD_DIM), lambda bi, i: (0, 0))),
                           near=near_bias, lam_init=lam_init)
        x = _out_proj((out_a, out_b, out_c, out_d), w_out, li, x, mod3, g_post[li])
    return x
```

```python
import functools
import math

import numpy as np
import jax
import jax.numpy as jnp
from jax import lax
from jax.experimental import pallas as pl
from jax.experimental.pallas import tpu as pltpu

F32 = jnp.float32
BF16 = jnp.bfloat16

HEAD_DIM = 128
N_HEADS = 4
BRANCH_WIDTH = N_HEADS * HEAD_DIM
MIX_WIDTH = 4 * BRANCH_WIDTH
MLA_Q_RANK = 384
MLA_KV_RANK = 256
MLA_NOPE = 128
MLA_ROPE = 64
ROPE_THETA = 10000.0
DILATED_PATTERNS = ((128, 1), (512, 4), (2048, 16))
IDX_HEADS = 16
IDX_DIM = 64
TOPK_MAX = 256
REL_BUCKETS = 32
REL_MAX_DIST = 2048
NORM_EPS = 1e-6
NEG = -1e30
LOG2E = math.log2(math.e)

OFF_QIDX = 0
OFF_A = 1024
OFF_WIDX = OFF_A + 640
WIDX_LANE = 64
OFF_KIDX = OFF_A + 896
OFF_B = 2048
OFF_C = 3584
OFF_D = 5120
OFF_GATE = 6656
PROJ_WIDTH = 8704

ATT_TILE = 512
V7X_VMEM_BYTES = 64 * 1024 * 1024
VMEM_LIMIT = V7X_VMEM_BYTES // 8 * 7


def _cparams(*sem):
    return pltpu.CompilerParams(dimension_semantics=sem, vmem_limit_bytes=VMEM_LIMIT)


def _split_bf16(x):
    hi = x.astype(BF16)
    return hi, (x - hi.astype(F32)).astype(BF16)


def _ada_kernel(c_ref, w_ref, b_ref, o_ref):
    c = c_ref[...]
    a_hi, a_lo = _split_bf16(c * jax.nn.sigmoid(c))
    w_hi, w_lo = _split_bf16(w_ref[...])
    dot = functools.partial(jnp.dot, preferred_element_type=F32)
    o_ref[...] = dot(a_hi, w_hi) + (dot(a_hi, w_lo) + dot(a_lo, w_hi)) + b_ref[...]


def _ada_mod(c, w_ada, b_ada):
    depth, d, n = w_ada.shape
    b = c.shape[0]
    tn = 768
    return pl.pallas_call(
        _ada_kernel,
        out_shape=jax.ShapeDtypeStruct((depth, b, n), F32),
        grid=(depth, n // tn),
        in_specs=[pl.BlockSpec((b, d), lambda l, j: (0, 0)),
                  pl.BlockSpec((None, d, tn), lambda l, j: (l, 0, j)),
                  pl.BlockSpec((None, 1, tn), lambda l, j: (l, 0, j))],
        out_specs=pl.BlockSpec((None, b, tn), lambda l, j: (l, 0, j)),
        compiler_params=_cparams("arbitrary", "arbitrary"),
        name="ada_mod",
    )(c, w_ada, b_ada.reshape(depth, 1, n))


def _in_proj_kernel(x_ref, g_ref, shift_ref, scale_ref, w_ref, o_ref):
    x = x_ref[...]
    y = x * lax.rsqrt(jnp.mean(x * x, axis=-1, keepdims=True) + NORM_EPS) * g_ref[...]
    h = (y * (1.0 + scale_ref[...]) + shift_ref[...]).astype(BF16)
    o_ref[...] = jnp.dot(h, w_ref[...], preferred_element_type=F32).astype(o_ref.dtype)


def _in_proj(x, g_pre, mod3, w):
    b, s, d = x.shape
    n = w.shape[1]
    tm = min(512, s)
    nt = s // tm
    tn = n // 2
    row = lambda j, i: (i // nt, i % nt, 0)
    return pl.pallas_call(
        _in_proj_kernel,
        out_shape=jax.ShapeDtypeStruct((b, s, n), BF16),
        grid=(n // tn, b * nt),
        in_specs=[pl.BlockSpec((None, tm, d), row),
                  pl.BlockSpec((1, d), lambda j, i: (0, 0)),
                  pl.BlockSpec((None, 1, d), lambda j, i: (i // nt, 0, 0)),
                  pl.BlockSpec((None, 1, d), lambda j, i: (i // nt, 0, 1)),
                  pl.BlockSpec((d, tn), lambda j, i: (0, j), pipeline_mode=pl.Buffered(1))],
        out_specs=pl.BlockSpec((None, tm, tn), lambda j, i: (i // nt, i % nt, j)),
        compiler_params=_cparams("arbitrary", "arbitrary"),
        name="in_proj",
    )(x, g_pre.reshape(1, d), mod3, mod3, w)


def _rms(x, g):
    return x * lax.rsqrt(jnp.mean(x * x, axis=-1, keepdims=True) + NORM_EPS) * g


MLA_RING = 3


def _mla_prep_kernel(proj_hbm, cos_ref, sin_ref, cs_ref, gq_ref, gkv_ref, wq_ref, wkv_ref, q_ref, k_ref, v_ref,
                     ring_ref, sem_ref, *, tiles_per_batch):
    n = pl.program_id(0)
    n_steps = pl.num_programs(0)
    tm = ring_ref.shape[1]

    def fetch(step):
        rows = pl.ds(pl.multiple_of((step % tiles_per_batch) * tm, tm), tm)
        slot = step % MLA_RING
        return pltpu.make_async_copy(proj_hbm.at[step // tiles_per_batch, rows, pl.ds(OFF_A, ring_ref.shape[2])],
                                     ring_ref.at[slot], sem_ref.at[slot])

    @pl.when(n == 0)
    def _():
        fetch(n).start()

        @pl.when(n_steps > 1)
        def _():
            fetch(n + 1).start()

    @pl.when(n + 2 < n_steps)
    def _():
        fetch(n + 2).start()

    fetch(n).wait()
    p_ref = ring_ref.at[n % MLA_RING]
    cos = cos_ref[...]
    sin = sin_ref[...]
    cs = cs_ref[...]
    cq = _rms(p_ref[:, 0:MLA_Q_RANK].astype(F32), gq_ref[...]).astype(BF16)
    ckv = _rms(p_ref[:, MLA_Q_RANK:MLA_Q_RANK + MLA_KV_RANK].astype(F32), gkv_ref[...]).astype(BF16)
    q = jnp.dot(cq, wq_ref[...], preferred_element_type=F32)
    kv = jnp.dot(ckv, wkv_ref[...], preferred_element_type=F32)
    k_rope = (p_ref[:, 640:768].astype(F32) * cos + p_ref[:, 768:896].astype(F32) * sin).astype(BF16)
    for h in range(N_HEADS):
        q_ref[:, h * 256:h * 256 + 128] = q[:, h * 256:h * 256 + 128].astype(BF16)
        z = q[:, h * 256 + 128:(h + 1) * 256] * cs
        q_ref[:, h * 256 + 128:(h + 1) * 256] = (z + pltpu.roll(z, MLA_ROPE, axis=1)).astype(BF16)
        k_ref[:, h * 256:h * 256 + 128] = kv[:, h * 256:h * 256 + 128].astype(BF16)
        k_ref[:, h * 256 + 128:(h + 1) * 256] = k_rope
        v_ref[:, h * 128:(h + 1) * 128] = kv[:, h * 256 + 128:(h + 1) * 256].astype(BF16)


def _mla_prep(proj, rope_tabs, g_q, g_kv, wq, wkv):
    b, s, _ = proj.shape
    tm = min(512, s)
    nt = s // tm
    const = lambda n: (0, 0)
    table = pl.BlockSpec((tm, 128), lambda n: (n % nt, 0))
    out = lambda width: pl.BlockSpec((None, tm, width), lambda n: (n // nt, n % nt, 0))
    return pl.pallas_call(
        functools.partial(_mla_prep_kernel, tiles_per_batch=nt),
        out_shape=(jax.ShapeDtypeStruct((b, s, N_HEADS * 256), BF16),
                   jax.ShapeDtypeStruct((b, s, N_HEADS * 256), BF16),
                   jax.ShapeDtypeStruct((b, s, BRANCH_WIDTH), BF16)),
        grid=(b * nt,),
        in_specs=[pl.BlockSpec(memory_space=pl.ANY), table, table, table,
                  pl.BlockSpec((1, MLA_Q_RANK), const),
                  pl.BlockSpec((1, MLA_KV_RANK), const),
                  pl.BlockSpec(wq.shape, const),
                  pl.BlockSpec(wkv.shape, const)],
        out_specs=(out(N_HEADS * 256), out(N_HEADS * 256), out(BRANCH_WIDTH)),
        scratch_shapes=[pltpu.VMEM((MLA_RING, tm, 1024), BF16), pltpu.SemaphoreType.DMA((MLA_RING,))],
        compiler_params=_cparams("arbitrary"),
        name="mla_prep",
    )(proj, *rope_tabs, g_q.reshape(1, -1), g_kv.reshape(1, -1), wq, wkv)


def _bucket_np(n):
    max_exact = REL_BUCKETS // 2
    nf = np.maximum(n, max_exact).astype(np.float32)
    large = max_exact + (np.log(nf / np.float32(max_exact)) / np.float32(math.log(REL_MAX_DIST / max_exact))
                         * np.float32(REL_BUCKETS - max_exact)).astype(np.int32)
    return np.where(n < max_exact, n, np.minimum(large, REL_BUCKETS - 1)).astype(np.int32)


def _bucket_starts():
    buckets = _bucket_np(np.arange(2 * REL_MAX_DIST))
    return [int(np.argmax(buckets >= b)) for b in range(REL_BUCKETS)]


def _bank_kernel(tab_ref, o_ref, *, t, d_min, band):
    h = pl.program_id(0)
    starts = _bucket_starts()
    row = lax.broadcasted_iota(jnp.int32, (t, t), 0)
    col = lax.broadcasted_iota(jnp.int32, (t, t), 1)
    for j in range(o_ref.shape[0]):
        d = d_min + j
        if d < 0:
            o_ref[j] = jnp.full((t, t), NEG, F32)
            continue
        dist = t * d + row - col
        b_lo, b_hi = (int(x) for x in _bucket_np(np.array([max(t * d - t + 1, 0), t * d + t - 1])))
        val = jnp.full((t, t), tab_ref[b_lo, h], F32)
        for b in range(b_lo + 1, b_hi + 1):
            val = jnp.where(dist >= starts[b], tab_ref[b, h], val)
        if band:
            mult = jnp.zeros((t, t), jnp.int32)
            for window, dil in DILATED_PATTERNS:
                mult += jnp.where((dist >= 0) & (dist <= window) & ((dist & (dil - 1)) == 0), 1, 0)
            val = val + jnp.where(mult == 3, math.log(3.0), jnp.where(mult == 2, math.log(2.0), 0.0))
            keep = mult > 0
        else:
            val = val - tab_ref[REL_BUCKETS - 1, h]
            keep = dist >= 0
        o_ref[j] = jnp.where(keep, val * LOG2E, NEG)


def _bank(tab, n_tables, t, d_min, band):
    nh = tab.shape[1]
    return pl.pallas_call(
        functools.partial(_bank_kernel, t=t, d_min=d_min, band=band),
        out_shape=jax.ShapeDtypeStruct((nh, n_tables, t, t), F32),
        grid=(nh,),
        in_specs=[pl.BlockSpec(memory_space=pltpu.SMEM)],
        out_specs=pl.BlockSpec((None, n_tables, t, t), lambda h: (h, 0, 0, 0)),
        compiler_params=_cparams("arbitrary"),
        name="bank_band" if band else "bank_bias",
    )(tab)


def _first_far_diagonal(t):
    last = _bucket_starts()[REL_BUCKETS - 1]
    return -(-(last + t - 1) // t)


def _qk(q, k):
    return lax.dot_general(q, k, (((1,), (1,)), ((), ())), preferred_element_type=F32)


def _flash_init(m_ref, acc_ref):
    m_ref[...] = jnp.full(m_ref.shape, NEG, F32)
    acc_ref[...] = jnp.zeros(acc_ref.shape, F32)


def _flash_update(slot, s, v, m_ref, acc_ref, s_ref, next_scores, diagonal=False):
    t = s.shape[0]
    v_ones = jnp.concatenate([v, jnp.ones_like(v)], axis=1)
    blocks = [(0, t // 2, t // 2), (t // 2, t, t)] if diagonal else [(0, t, t)]
    for n, (r0, r1, keys) in enumerate(blocks):
        m_prev = m_ref[slot, r0:r1]
        m_new = jnp.maximum(m_prev, jnp.max(s[r0:r1, :keys], axis=-1, keepdims=True))
        alpha = jnp.exp2(m_prev - m_new)
        p = jnp.concatenate([jnp.exp2(s[r0:r1, j * 128:(j + 1) * 128] - m_new) for j in range(keys // 128)],
                            axis=1).astype(BF16)
        if n == len(blocks) - 1 and next_scores is not None:
            s_ref[slot] = next_scores()
        acc_ref[slot, r0:r1] = (jnp.concatenate([alpha, alpha], axis=1) * acc_ref[slot, r0:r1]
                                + jnp.dot(p, v_ones[:keys], preferred_element_type=F32))
        m_ref[slot, r0:r1] = m_new


def _flash_result(slot, acc_ref):
    acc = acc_ref[slot]
    return acc[:, :HEAD_DIM] / acc[:, HEAD_DIM:]


def _silu(g):
    return g * jax.nn.sigmoid(g)


def _tile(ref, kj, t, c0, width):
    start = kj * t if isinstance(kj, int) else pl.multiple_of(kj * t, t)
    return ref[pl.ds(start, t), c0:c0 + width]


def _bank_tile(bank_ref, h, i, kj):
    base = 2 * (i - kj) + 1
    top = jnp.concatenate([bank_ref[h, base], bank_ref[h, base - 1]], axis=1)
    bot = jnp.concatenate([bank_ref[h, base + 1], bank_ref[h, base]], axis=1)
    return jnp.concatenate([top, bot], axis=0)


def _emit(o_ref, g_ref, h, o):
    gate = g_ref[:, h * 128:(h + 1) * 128].astype(F32)
    o_ref[:, h * 128:(h + 1) * 128] = (o * _silu(gate)).astype(BF16)


def _flash_walk(i, first_key, n_far, slots, score, logits, value, s_ref, m_ref, acc_ref, prep=None):
    lo = first_key(i)
    i_next = jnp.minimum(i + 1, pl.num_programs(1) - 1)
    _flash_init(m_ref, acc_ref)

    @pl.when(i == 0)
    def _():
        for slot in range(slots):
            s_ref[slot] = score(slot, False, lo)

    def step(kj, phase):
        ctx = prep(kj) if prep is not None else None
        for slot in range(slots):
            s = logits(slot, s_ref[slot], kj, phase, ctx)
            if phase == "last":
                nxt = functools.partial(score, slot, True, first_key(i_next))
            else:
                nxt = functools.partial(score, slot, False, kj + 1)
            _flash_update(slot, s, value(slot, kj), m_ref, acc_ref, s_ref, nxt, diagonal=phase == "last")

    def walk(a, b, phase):
        def body(kj, carry):
            step(kj, phase)
            return carry
        lax.fori_loop(a, b, body, 0)

    if n_far is not None:
        walk(lo, n_far, "far")
        lo = n_far
    walk(lo, i, "near")
    step(i, "last")


def _attn_a_kernel(q_ref, qn_ref, k_ref, v_ref, g_ref, o_ref, s_ref, m_ref, acc_ref, *, t):
    i = pl.program_id(1)

    def score(h, next_q, kj):
        q = (qn_ref if next_q else q_ref)[:, h * 256:(h + 1) * 256]
        return _qk(q, _tile(k_ref, kj, t, h * 256, 256))

    def logits(h, s, kj, phase, ctx):
        if phase != "last":
            return s
        causal = lax.broadcasted_iota(jnp.int32, (t, t), 1) <= lax.broadcasted_iota(jnp.int32, (t, t), 0)
        return jnp.where(causal, s, NEG)

    def value(h, kj):
        return _tile(v_ref, kj, t, h * 128, 128)

    _flash_walk(i, lambda qi: 0, None, N_HEADS, score, logits, value, s_ref, m_ref, acc_ref)
    for h in range(N_HEADS):
        _emit(o_ref, g_ref, h, _flash_result(h, acc_ref))


def _attn_band_kernel(q_ref, qn_ref, k_ref, v_ref, g_ref, bank_ref, o_ref, s_ref, m_ref, acc_ref, *, t, near):
    i = pl.program_id(1)

    def score(h, next_q, kj):
        q = (qn_ref if next_q else q_ref)[:, h * 128:(h + 1) * 128]
        return _qk(q, _tile(k_ref, kj, t, h * 128, 128))

    def logits(h, s, kj, phase, ctx):
        return s + _bank_tile(bank_ref, h, i, kj)

    def value(h, kj):
        return _tile(v_ref, kj, t, h * 128, 128)

    _flash_walk(i, lambda qi: jnp.maximum(qi - (near - 1), 0), None, N_HEADS, score, logits, value,
                s_ref, m_ref, acc_ref)
    for h in range(N_HEADS):
        _emit(o_ref, g_ref, h, _flash_result(h, acc_ref))


def _attn_sel_kernel(q_ref, qn_ref, k_ref, v_ref, g_ref, bank_ref, sel_ref, o_ref,
                     s_ref, m_ref, acc_ref, *, t, near):
    i = pl.program_id(1)

    def score(h, next_q, kj):
        q = (qn_ref if next_q else q_ref)[:, h * 128:(h + 1) * 128]
        return _qk(q, _tile(k_ref, kj, t, h * 128, 128))

    def prep(kj):
        return sel_ref[:, pl.ds(pl.multiple_of(kj * t, t), t)].astype(F32)

    def logits(h, s, kj, phase, sel):
        return s + sel if phase == "far" else s + (sel + _bank_tile(bank_ref, h, i, kj))

    def value(h, kj):
        return _tile(v_ref, kj, t, h * 128, 128)

    _flash_walk(i, lambda qi: 0, jnp.maximum(i - (near - 1), 0), N_HEADS, score, logits, value,
                s_ref, m_ref, acc_ref, prep)
    for h in range(N_HEADS):
        _emit(o_ref, g_ref, h, _flash_result(h, acc_ref))


def _attn_diff_kernel(q_ref, qn_ref, k_ref, v_ref, g_ref, bank_ref, lam_ref, gsub_ref, o_ref,
                      s_ref, m_ref, acc_ref, *, t, near, lam_init):
    i = pl.program_id(1)
    first_half = lax.broadcasted_iota(jnp.int32, (t, HEAD_DIM), 1) < HEAD_DIM // 2

    def score(slot, next_q, kj):
        h = slot // 2
        q = (qn_ref if next_q else q_ref)[:, h * 128:(h + 1) * 128]
        keep = first_half if slot % 2 == 0 else jnp.logical_not(first_half)
        q = jnp.where(keep, q, jnp.zeros_like(q))
        return _qk(q, _tile(k_ref, kj, t, h * 128, 128))

    def logits(slot, s, kj, phase, ctx):
        return s if phase == "far" else s + _bank_tile(bank_ref, slot // 2, i, kj)

    def value(slot, kj):
        return _tile(v_ref, kj, t, (slot // 2) * 128, 128)

    _flash_walk(i, lambda qi: 0, jnp.maximum(i - (near - 1), 0), 2 * N_HEADS, score, logits, value,
                s_ref, m_ref, acc_ref)
    lam_v = lam_ref[...]
    lam = (jnp.exp(jnp.sum(lam_v[0:1] * lam_v[1:2], axis=-1, keepdims=True))
           - jnp.exp(jnp.sum(lam_v[2:3] * lam_v[3:4], axis=-1, keepdims=True)) + lam_init)
    for h in range(N_HEADS):
        o = _flash_result(2 * h, acc_ref) - lam * _flash_result(2 * h + 1, acc_ref)
        _emit(o_ref, g_ref, h, _rms(o, gsub_ref[...]) * (1.0 - lam_init))


def _attention(kind, q_arr, k_arr, v_arr, proj, q_blk, gate_blk, extra_in=(), extra_specs=(), **kw):
    b, s, _ = proj.shape
    t = min(ATT_TILE, s)
    dk = 256 if kind == "a" else 128
    qw = N_HEADS * dk
    k_blk = 0 if kind == "a" else q_blk + 1
    v_blk = 0 if kind == "a" else q_blk + 2
    body = {"a": _attn_a_kernel, "band": _attn_band_kernel, "sel": _attn_sel_kernel,
            "diff": _attn_diff_kernel}[kind]
    slots = 2 * N_HEADS if kind == "diff" else N_HEADS
    scratch = [pltpu.VMEM((slots, t, t), F32), pltpu.VMEM((slots, t, HEAD_DIM), F32),
               pltpu.VMEM((slots, t, 2 * HEAD_DIM), F32)]
    last = s // t - 1
    in_specs = [pl.BlockSpec((None, t, qw), lambda bi, i: (bi, i, q_blk)),
                pl.BlockSpec((None, t, qw), lambda bi, i: (bi, jnp.minimum(i + 1, last), q_blk)),
                pl.BlockSpec((None, s, qw), lambda bi, i: (bi, 0, k_blk)),
                pl.BlockSpec((None, s, BRANCH_WIDTH), lambda bi, i: (bi, 0, v_blk)),
                pl.BlockSpec((None, t, BRANCH_WIDTH), lambda bi, i: (bi, i, gate_blk))]
    in_specs += list(extra_specs)
    return pl.pallas_call(
        functools.partial(body, t=t, **kw),
        out_shape=jax.ShapeDtypeStruct((b, s, BRANCH_WIDTH), BF16),
        grid=(b, s // t),
        in_specs=in_specs,
        out_specs=pl.BlockSpec((None, t, BRANCH_WIDTH), lambda bi, i: (bi, i, 0)),
        scratch_shapes=scratch,
        compiler_params=_cparams("arbitrary", "arbitrary"),
        name="attn_" + kind,
    )(q_arr, q_arr, k_arr, v_arr, proj, *extra_in)


def _bit_transpose32(words):
    a = list(words)
    j, m = 16, 0x0000FFFF
    while j:
        for k in range(32):
            if not k & j:
                t = (a[k] ^ lax.shift_right_logical(a[k + j], jnp.int32(j))) & jnp.int32(m)
                a[k] = a[k] ^ t
                a[k + j] = a[k + j] ^ (t << j)
        j >>= 1
        m = (m ^ (m << j)) & 0xFFFFFFFF if j else m
    return a


def _select_kernel(qi_ref, ki_ref, wi_ref, o_ref, key_ref, plane_ref, alive_ref, stat_ref, *, tq, kc, n_sel):
    i = pl.program_id(1)
    s_len = o_ref.shape[1]
    n_ch = (i * tq + tq + kc - 1) // kc
    wpc = kc // 32
    int_min = jnp.int32(-2 ** 31)
    lane = lax.broadcasted_iota(jnp.int32, (tq, 128), 1)
    w_t = wi_ref[...].astype(F32).T[WIDX_LANE:WIDX_LANE + IDX_HEADS]
    q_heads = []
    for j in range(IDX_HEADS // 2):
        q2 = qi_ref[:, j * 128:(j + 1) * 128]
        q_heads.append(jnp.where(lane < IDX_DIM, q2, jnp.zeros_like(q2)))
        q_heads.append(jnp.where(lane >= IDX_DIM, q2, jnp.zeros_like(q2)))
    kiota = lax.broadcasted_iota(jnp.int32, (kc, tq), 0)

    def chunk(c):
        return pl.ds(pl.multiple_of(c * kc, kc), kc)

    def scored(k, q_from):
        acc = jnp.zeros((k.shape[0], tq - q_from), F32)
        for hh in range(IDX_HEADS):
            acc = acc + jnp.maximum(_qk(k, q_heads[hh][q_from:, :]), 0.0) * w_t[hh:hh + 1, q_from:]
        return acc + 0.0

    def causal(acc):
        key = lax.broadcasted_iota(jnp.int32, acc.shape, 0)
        return jnp.where(key <= lax.broadcasted_iota(jnp.int32, acc.shape, 1), acc, NEG)

    def store_keys(row0, acc):
        bits = pltpu.bitcast(acc, jnp.int32)
        keys = bits ^ ((bits >> 31) & jnp.int32(0x7FFFFFFF))
        key_ref[pl.ds(row0, acc.shape[0]), :] = keys
        ukeys = keys ^ int_min
        for blk in range(acc.shape[0] // 256):
            planes = _bit_transpose32([ukeys[blk * 256 + 8 * j:blk * 256 + 8 * j + 8, :] for j in range(32)])
            row = pl.multiple_of(row0 // 32 + blk * 8, 8)
            for b in range(32):
                plane_ref[b, pl.ds(row, 8), :] = planes[b]

    def score_chunk(c, _):
        store_keys(pl.multiple_of(c * kc, kc), scored(ki_ref[chunk(c), :], 0))
        return 0

    lax.fori_loop(0, i, score_chunk, 0)
    diag = pl.multiple_of(i * kc, kc)
    half = kc // 2
    if half % 256 == 0:
        store_keys(diag, causal(scored(ki_ref[pl.ds(diag, half), :], 0)))
        late = causal(scored(ki_ref[pl.ds(diag + half, half), :], half))
        store_keys(diag + half, jnp.concatenate([jnp.full((half, half), NEG, F32), late], axis=1))
    else:
        store_keys(diag, causal(scored(ki_ref[pl.ds(diag, kc), :], 0)))

    def clear_chunk(c, _):
        plane_ref[:, pl.ds(pl.multiple_of(c * wpc, wpc), wpc), :] = jnp.zeros((32, wpc, tq), jnp.int32)
        return 0

    @pl.when(i == 0)
    def _():
        lax.fori_loop(n_ch, s_len // kc, clear_chunk, 0)

    n_rows = s_len // 32
    word_row = lax.broadcasted_iota(jnp.int32, (n_rows, tq), 0)
    alive_ref[...] = jnp.where(word_row < n_ch * wpc, jnp.int32(-1), jnp.int32(0))
    zeros = jnp.zeros((1, tq), jnp.int32)

    def radix(rows):
        def radix_step(bi, carry):
            thr, above = carry
            alive = alive_ref[0:rows]
            plane = plane_ref[bi, 0:rows]
            ones = lax.population_count(alive & plane)
            ones = jnp.sum(jnp.sum(ones.reshape(rows // 8, 8, tq), axis=0), axis=0, keepdims=True)
            take = above + ones >= n_sel
            thr = jnp.where(take, thr | (jnp.int32(1) << (31 - bi)), thr)
            above = jnp.where(take, above, above + ones)
            alive_ref[0:rows] = alive & (plane ^ jnp.where(take, jnp.int32(0), jnp.int32(-1)))
            return thr, above

        thr, above = lax.fori_loop(0, 32, radix_step, (zeros, zeros))
        stat_ref[0:8] = jnp.broadcast_to(thr, (8, tq))
        stat_ref[8:16] = jnp.broadcast_to(above, (8, tq))

    quarter = max(n_rows // 4, 8)
    sizes = sorted({min(quarter * (n + 1), n_rows) for n in range(4)})
    for n, rows in enumerate(sizes):
        lower = sizes[n - 1] if n else 0
        pl.when((n_ch * wpc > lower) & (n_ch * wpc <= rows))(functools.partial(radix, rows))
    thr_u, above = stat_ref[0:1], stat_ref[8:9]
    thr = thr_u ^ int_min
    n_equal = lax.population_count(alive_ref[...])
    n_equal = jnp.sum(jnp.sum(n_equal.reshape(n_rows // 8, 8, tq), axis=0), axis=0, keepdims=True)
    need = n_sel - above
    masked_key = int(np.float32(NEG).view(np.int32)) ^ 0x7FFFFFFF
    tie = (n_equal > need) & (thr != masked_key)
    any_tie = jnp.max(jnp.where(tie, 1, 0)) > 0

    @pl.when(jnp.logical_not(any_tie))
    def _():
        def emit(c, _):
            keep = jnp.where(key_ref[chunk(c), :] >= thr, 0.0, NEG)
            o_ref[:, chunk(c)] = keep.T.astype(BF16)
            return 0

        lax.fori_loop(0, n_ch, emit, 0)

    @pl.when(any_tie)
    def _():
        def equal_below(bound):
            def body(c, part):
                hit = jnp.where((key_ref[chunk(c), :] == thr) & (c * kc + kiota < bound), 1, 0)
                return part + jnp.sum(hit.reshape(kc // 8, 8, tq), axis=0)
            part = lax.fori_loop(0, n_ch, body, jnp.zeros((8, tq), jnp.int32))
            return jnp.sum(part, axis=0, keepdims=True)

        n_bits = s_len.bit_length()

        def bound_step(bi, cut):
            cand = cut + (jnp.int32(1) << (n_bits - 1 - bi))
            ok = (cand <= s_len) & (equal_below(cand) <= need)
            return jnp.where(ok, cand, cut)

        cut = lax.fori_loop(0, n_bits, bound_step, zeros)

        def emit(c, _):
            keys = key_ref[chunk(c), :]
            kept = (keys > thr) | ((keys == thr) & (c * kc + kiota < cut))
            o_ref[:, chunk(c)] = jnp.where(kept, 0.0, NEG).T.astype(BF16)
            return 0

        lax.fori_loop(0, n_ch, emit, 0)

    def blank(c, _):
        o_ref[:, chunk(c)] = jnp.full((tq, kc), NEG, BF16)
        return 0

    lax.fori_loop(n_ch, s_len // kc, blank, 0)


def _select(proj):
    b, s, _ = proj.shape
    tq = kc = min(512, s)
    n_sel = min(TOPK_MAX, s // 4)
    return pl.pallas_call(
        functools.partial(_select_kernel, tq=tq, kc=kc, n_sel=n_sel),
        out_shape=jax.ShapeDtypeStruct((b, s, s), BF16),
        grid=(b, s // tq),
        in_specs=[pl.BlockSpec((None, tq, 1024), lambda bi, i: (bi, i, OFF_QIDX // 1024)),
                  pl.BlockSpec((None, s, 128), lambda bi, i: (bi, 0, OFF_KIDX // 128)),
                  pl.BlockSpec((None, tq, 128), lambda bi, i: (bi, i, OFF_WIDX // 128))],
        out_specs=pl.BlockSpec((None, tq, s), lambda bi, i: (bi, i, 0)),
        scratch_shapes=[pltpu.VMEM((s, tq), jnp.int32), pltpu.VMEM((32, s // 32, tq), jnp.int32),
                        pltpu.VMEM((s // 32, tq), jnp.int32), pltpu.VMEM((16, tq), jnp.int32)],
        compiler_params=_cparams("arbitrary", "arbitrary"),
        name="idx_select",
    )(proj, proj, proj)


def _out_kernel(a_ref, b_ref, c_ref, d_ref, w_ref, x_hbm, gate_ref, g_ref, o_ref, wb_ref, ring_ref, sem_ref,
                *, tiles_per_batch):
    n = pl.program_id(0)
    n_steps = pl.num_programs(0)
    tm = ring_ref.shape[1]

    def fetch(step):
        rows = pl.ds(pl.multiple_of((step % tiles_per_batch) * tm, tm), tm)
        slot = step % MLA_RING
        return pltpu.make_async_copy(x_hbm.at[step // tiles_per_batch, rows], ring_ref.at[slot], sem_ref.at[slot])

    @pl.when(n == 0)
    def _():
        wb_ref[...] = w_ref[...].astype(BF16)
        fetch(n).start()

        @pl.when(n_steps > 1)
        def _():
            fetch(n + 1).start()

    @pl.when(n + 2 < n_steps)
    def _():
        fetch(n + 2).start()

    y = jnp.dot(a_ref[...], wb_ref[0:512, :], preferred_element_type=F32)
    y += jnp.dot(b_ref[...], wb_ref[512:1024, :], preferred_element_type=F32)
    y += jnp.dot(c_ref[...], wb_ref[1024:1536, :], preferred_element_type=F32)
    y += jnp.dot(d_ref[...], wb_ref[1536:2048, :], preferred_element_type=F32)
    fetch(n).wait()
    o_ref[...] = ring_ref[n % MLA_RING] + gate_ref[...] * _rms(y, g_ref[...])


def _out_proj(outs, w_out, li, x, mod3, g_post):
    b, s, d = x.shape
    tm = min(512, s)
    nt = s // tm
    mix = lambda n: (n // nt, n % nt, 0)
    return pl.pallas_call(
        functools.partial(_out_kernel, tiles_per_batch=nt),
        out_shape=jax.ShapeDtypeStruct((b, s, d), F32),
        grid=(b * nt,),
        in_specs=[pl.BlockSpec((None, tm, BRANCH_WIDTH), mix)] * 4
        + [pl.BlockSpec((None,) + w_out.shape[1:], lambda n: (li, 0, 0), pipeline_mode=pl.Buffered(1)),
           pl.BlockSpec(memory_space=pl.ANY),
           pl.BlockSpec((None, 1, d), lambda n: (n // nt, 0, 2)),
           pl.BlockSpec((1, d), lambda n: (0, 0))],
        out_specs=pl.BlockSpec((None, tm, d), mix),
        scratch_shapes=[pltpu.VMEM(w_out.shape[1:], BF16), pltpu.VMEM((MLA_RING, tm, d), F32),
                        pltpu.SemaphoreType.DMA((MLA_RING,))],
        compiler_params=_cparams("arbitrary"),
        name="out_proj",
    )(*outs, w_out, x, mod3, g_post.reshape(1, d))


def _rope_tables(s):
    half = MLA_ROPE // 2
    inv = ROPE_THETA ** (-jnp.arange(half, dtype=F32) / half)
    ang = jnp.arange(s, dtype=F32)[:, None] * inv[None, :]
    z = jnp.zeros((s, 128 - MLA_ROPE), F32)
    cos, sin = jnp.cos(ang), jnp.sin(ang)
    cat = lambda *parts: jnp.concatenate(parts, axis=-1)
    return cat(cos, cos, z), cat(sin, sin, z), cat(cos, cos, sin, sin)


def _rot_cols(w):
    half = w.shape[-1] // 2
    return jnp.concatenate([-w[..., half:], w[..., :half]], axis=-1)


IN_SPLITS = (("a_cq", 384), ("a_ckv", 256), ("a_krope", 64), ("b_q", 512), ("b_k", 512), ("b_v", 512),
             ("c_q", 512), ("c_k", 512), ("c_v", 512), ("c_qidx", 1024), ("c_kidx", 64), ("c_widx", 16),
             ("d_q", 512), ("d_k", 512), ("d_v", 512), ("gate", 2048))
IN_WIDTH = sum(width for _, width in IN_SPLITS)


def _layout_w_in_kernel(w_ref, o_ref):
    src, start = {}, 0
    for name, width in IN_SPLITS:
        src[name] = start
        start += width
    tk = w_ref.shape[1]

    def rows(name, width, offset=0):
        a = src[name] + offset
        return w_ref[a:a + width, :]

    def put(dst, val, scale=None):
        for r in range(0, val.shape[0], 512):
            piece = val[r:r + 512]
            if scale is not None:
                piece = piece * scale
            o_ref[:, dst + r:dst + r + piece.shape[0]] = piece.T.astype(BF16)

    z64 = jnp.zeros((64, tk), F32)
    half = MLA_ROPE // 2
    put(OFF_QIDX, rows("c_qidx", 1024), IDX_DIM ** -0.5)
    put(OFF_A, rows("a_cq", MLA_Q_RANK + MLA_KV_RANK))
    put(OFF_A + 640, jnp.concatenate(
        [rows("a_krope", MLA_ROPE), rows("c_widx", IDX_HEADS) * IDX_HEADS ** -0.5,
         jnp.zeros((128 - MLA_ROPE - IDX_HEADS, tk), F32),
         -rows("a_krope", half, half), rows("a_krope", half), z64,
         rows("c_kidx", IDX_DIM), rows("c_kidx", IDX_DIM)], axis=0))
    for off, name, dim in ((OFF_B, "b", HEAD_DIM), (OFF_C, "c", HEAD_DIM), (OFF_D, "d", HEAD_DIM // 2)):
        put(off, rows(name + "_q", BRANCH_WIDTH), LOG2E * dim ** -0.5)
        put(off + BRANCH_WIDTH, rows(name + "_k", 2 * BRANCH_WIDTH))
    put(OFF_GATE, rows("gate", MIX_WIDTH))


def _layout_w_in(w_in_t, li):
    d = w_in_t.shape[2]
    tk = 256
    return pl.pallas_call(
        _layout_w_in_kernel,
        out_shape=jax.ShapeDtypeStruct((d, PROJ_WIDTH), BF16),
        grid=(d // tk,),
        in_specs=[pl.BlockSpec((None, IN_WIDTH, tk), lambda i: (li, 0, i))],
        out_specs=pl.BlockSpec((tk, PROJ_WIDTH), lambda i: (i, 0)),
        compiler_params=_cparams("arbitrary"),
        name="layout_w_in",
    )(w_in_t)


def _layout_w_uq(w):
    r = w.shape[0]
    w = w.reshape(r, N_HEADS, MLA_NOPE + MLA_ROPE) * (LOG2E * (MLA_NOPE + MLA_ROPE) ** -0.5)
    rope = w[..., MLA_NOPE:]
    return jnp.concatenate([w[..., :MLA_NOPE], rope, _rot_cols(rope)], axis=-1).reshape(r, -1).astype(BF16)


def kernel(x, c, w_ada, b_ada, g_pre, g_post, w_in, g_q_a, w_uq_a, g_kv_a, w_ukv_a,
           lam_q1, lam_k1, lam_q2, lam_k2, g_sub_d, w_out, rel_bias):
    b, s, d = x.shape
    depth = w_ada.shape[0]
    t = min(ATT_TILE, s)
    half = t // 2
    nq = s // t
    near_bias = min(nq, -(-(_first_far_diagonal(half) + 1) // 2))
    near_band = min(nq, -(-(DILATED_PATTERNS[-1][0] // half + 1) // 2))

    rope_tabs = _rope_tables(s)
    bank_b = _bank(rel_bias[:, 0:N_HEADS], 2 * near_band + 1, half, -1, True)
    bank_cd = _bank(rel_bias[:, N_HEADS:3 * N_HEADS], 2 * near_bias + 1, half, -1, False)
    bank_spec = lambda n, group=0: pl.BlockSpec((N_HEADS, n, half, half), lambda bi, i: (group, 0, 0, 0),
                                                pipeline_mode=pl.Buffered(1))

    w_in_t = jnp.swapaxes(w_in, 1, 2)
    mod = _ada_mod(c, w_ada, b_ada)
    for li in range(depth):
        mod3 = mod[li].reshape(b, 1, 3 * d)
        proj = _in_proj(x, g_pre[li], mod3, _layout_w_in(w_in_t, li))

        q_a, k_a, v_a = _mla_prep(proj, rope_tabs, g_q_a[li], g_kv_a[li],
                                  _layout_w_uq(w_uq_a[li]), w_ukv_a[li].astype(BF16))
        gate0 = OFF_GATE // BRANCH_WIDTH
        out_a = _attention("a", q_a, k_a, v_a, proj, 0, gate0)
        out_b = _attention("band", proj, proj, proj, proj, OFF_B // BRANCH_WIDTH, gate0 + 1,
                           extra_in=(bank_b,), extra_specs=(bank_spec(2 * near_band + 1),), near=near_band)
        sel = _select(proj)
        out_c = _attention("sel", proj, proj, proj, proj, OFF_C // BRANCH_WIDTH, gate0 + 2,
                           extra_in=(bank_cd, sel),
                           extra_specs=(bank_spec(2 * near_bias + 1, 0),
                                        pl.BlockSpec((None, t, s), lambda bi, i: (bi, i, 0))),
                           near=near_bias)
        lam_init = 0.8 - 0.6 * math.exp(-0.3 * li)
        lam_vecs = jnp.stack([lam_q1[li], lam_k1[li], lam_q2[li], lam_k2[li]])
        out_d = _attention("diff", proj, proj, proj, proj, OFF_D // BRANCH_WIDTH, gate0 + 3,
                           extra_in=(bank_cd, lam_vecs, g_sub_d[li].reshape(1, HEAD_DIM)),
                           extra_specs=(bank_spec(2 * near_bias + 1, 1),
                                        pl.BlockSpec(lam_vecs.shape, lambda bi, i: (0, 0)),
                                        pl.BlockSpec((1, HEAD_DIM), lambda bi, i: (0, 0))),
                           near=near_bias, lam_init=lam_init)
        x = _out_proj((out_a, out_b, out_c, out_d), w_out, li, x, mod3, g_post[li])
    return x
```

```python
import functools
import math

import numpy as np
import jax
import jax.numpy as jnp
from jax import lax
from jax.experimental import pallas as pl
from jax.experimental.pallas import tpu as pltpu

F32 = jnp.float32
BF16 = jnp.bfloat16

HEAD_DIM = 128
N_HEADS = 4
BRANCH_WIDTH = N_HEADS * HEAD_DIM
MIX_WIDTH = 4 * BRANCH_WIDTH
MLA_Q_RANK = 384
MLA_KV_RANK = 256
MLA_NOPE = 128
MLA_ROPE = 64
ROPE_THETA = 10000.0
DILATED_PATTERNS = ((128, 1), (512, 4), (2048, 16))
IDX_HEADS = 16
IDX_DIM = 64
TOPK_MAX = 256
REL_BUCKETS = 32
REL_MAX_DIST = 2048
NORM_EPS = 1e-6
NEG = -1e30
LOG2E = math.log2(math.e)

OFF_QIDX = 0
OFF_A = 1024
OFF_WIDX = OFF_A + 640
WIDX_LANE = 64
OFF_KIDX = OFF_A + 896
OFF_B = 2048
OFF_C = 3584
OFF_D = 5120
OFF_GATE = 6656
PROJ_WIDTH = 8704

ATT_TILE = 512
V7X_VMEM_BYTES = 64 * 1024 * 1024
VMEM_LIMIT = V7X_VMEM_BYTES // 8 * 7


def _cparams(*sem):
    return pltpu.CompilerParams(dimension_semantics=sem, vmem_limit_bytes=VMEM_LIMIT)


def _split_bf16(x):
    hi = x.astype(BF16)
    return hi, (x - hi.astype(F32)).astype(BF16)


def _ada_kernel(c_ref, w_ref, b_ref, o_ref):
    c = c_ref[...]
    a_hi, a_lo = _split_bf16(c * jax.nn.sigmoid(c))
    w_hi, w_lo = _split_bf16(w_ref[...])
    dot = functools.partial(jnp.dot, preferred_element_type=F32)
    o_ref[...] = dot(a_hi, w_hi) + (dot(a_hi, w_lo) + dot(a_lo, w_hi)) + b_ref[...]


def _ada_mod(c, w_ada, b_ada):
    depth, d, n = w_ada.shape
    b = c.shape[0]
    tn = 768
    return pl.pallas_call(
        _ada_kernel,
        out_shape=jax.ShapeDtypeStruct((depth, b, n), F32),
        grid=(depth, n // tn),
        in_specs=[pl.BlockSpec((b, d), lambda l, j: (0, 0)),
                  pl.BlockSpec((None, d, tn), lambda l, j: (l, 0, j)),
                  pl.BlockSpec((None, 1, tn), lambda l, j: (l, 0, j))],
        out_specs=pl.BlockSpec((None, b, tn), lambda l, j: (l, 0, j)),
        compiler_params=_cparams("arbitrary", "arbitrary"),
        name="ada_mod",
    )(c, w_ada, b_ada.reshape(depth, 1, n))


def _in_proj_kernel(x_ref, g_ref, shift_ref, scale_ref, w_ref, o_ref):
    x = x_ref[...]
    y = x * lax.rsqrt(jnp.mean(x * x, axis=-1, keepdims=True) + NORM_EPS) * g_ref[...]
    h = (y * (1.0 + scale_ref[...]) + shift_ref[...]).astype(BF16)
    o_ref[...] = jnp.dot(h, w_ref[...], preferred_element_type=F32).astype(o_ref.dtype)


def _in_proj(x, g_pre, mod3, w):
    b, s, d = x.shape
    n = w.shape[1]
    tm = min(512, s)
    nt = s // tm
    tn = n // 2
    row = lambda j, i: (i // nt, i % nt, 0)
    return pl.pallas_call(
        _in_proj_kernel,
        out_shape=jax.ShapeDtypeStruct((b, s, n), BF16),
        grid=(n // tn, b * nt),
        in_specs=[pl.BlockSpec((None, tm, d), row),
                  pl.BlockSpec((1, d), lambda j, i: (0, 0)),
                  pl.BlockSpec((None, 1, d), lambda j, i: (i // nt, 0, 0)),
                  pl.BlockSpec((None, 1, d), lambda j, i: (i // nt, 0, 1)),
                  pl.BlockSpec((d, tn), lambda j, i: (0, j), pipeline_mode=pl.Buffered(1))],
        out_specs=pl.BlockSpec((None, tm, tn), lambda j, i: (i // nt, i % nt, j)),
        compiler_params=_cparams("parallel", "parallel"),
        name="in_proj",
    )(x, g_pre.reshape(1, d), mod3, mod3, w)


def _rms(x, g):
    return x * lax.rsqrt(jnp.mean(x * x, axis=-1, keepdims=True) + NORM_EPS) * g


def _mla_prep_kernel(p_ref, cos_ref, sin_ref, cs_ref, gq_ref, gkv_ref, wq_ref, wkv_ref, q_ref, k_ref, v_ref):
    cos = cos_ref[...]
    sin = sin_ref[...]
    cs = cs_ref[...]
    cq = _rms(p_ref[:, 0:MLA_Q_RANK].astype(F32), gq_ref[...]).astype(BF16)
    ckv = _rms(p_ref[:, MLA_Q_RANK:MLA_Q_RANK + MLA_KV_RANK].astype(F32), gkv_ref[...]).astype(BF16)
    q = jnp.dot(cq, wq_ref[...], preferred_element_type=F32)
    kv = jnp.dot(ckv, wkv_ref[...], preferred_element_type=F32)
    k_rope = (p_ref[:, 640:768].astype(F32) * cos + p_ref[:, 768:896].astype(F32) * sin).astype(BF16)
    for h in range(N_HEADS):
        q_ref[:, h * 256:h * 256 + 128] = q[:, h * 256:h * 256 + 128].astype(BF16)
        z = q[:, h * 256 + 128:(h + 1) * 256] * cs
        q_ref[:, h * 256 + 128:(h + 1) * 256] = (z + pltpu.roll(z, MLA_ROPE, axis=1)).astype(BF16)
        k_ref[:, h * 256:h * 256 + 128] = kv[:, h * 256:h * 256 + 128].astype(BF16)
        k_ref[:, h * 256 + 128:(h + 1) * 256] = k_rope
        v_ref[:, h * 128:(h + 1) * 128] = kv[:, h * 256 + 128:(h + 1) * 256].astype(BF16)


def _mla_prep(proj, rope_tabs, g_q, g_kv, wq, wkv):
    b, s, _ = proj.shape
    tm = min(512, s)
    const = lambda bi, i: (0, 0)
    return pl.pallas_call(
        _mla_prep_kernel,
        out_shape=(jax.ShapeDtypeStruct((b, s, N_HEADS * 256), BF16),
                   jax.ShapeDtypeStruct((b, s, N_HEADS * 256), BF16),
                   jax.ShapeDtypeStruct((b, s, BRANCH_WIDTH), BF16)),
        grid=(b, s // tm),
        in_specs=[pl.BlockSpec((None, tm, 1024), lambda bi, i: (bi, i, OFF_A // 1024)),
                  pl.BlockSpec((tm, 128), lambda bi, i: (i, 0)),
                  pl.BlockSpec((tm, 128), lambda bi, i: (i, 0)),
                  pl.BlockSpec((tm, 128), lambda bi, i: (i, 0)),
                  pl.BlockSpec((1, MLA_Q_RANK), const),
                  pl.BlockSpec((1, MLA_KV_RANK), const),
                  pl.BlockSpec(wq.shape, const),
                  pl.BlockSpec(wkv.shape, const)],
        out_specs=(pl.BlockSpec((None, tm, N_HEADS * 256), lambda bi, i: (bi, i, 0)),
                   pl.BlockSpec((None, tm, N_HEADS * 256), lambda bi, i: (bi, i, 0)),
                   pl.BlockSpec((None, tm, BRANCH_WIDTH), lambda bi, i: (bi, i, 0))),
        compiler_params=_cparams("parallel", "parallel"),
        name="mla_prep",
    )(proj, *rope_tabs, g_q.reshape(1, -1), g_kv.reshape(1, -1), wq, wkv)


def _bucket_np(n):
    max_exact = REL_BUCKETS // 2
    nf = np.maximum(n, max_exact).astype(np.float32)
    large = max_exact + (np.log(nf / np.float32(max_exact)) / np.float32(math.log(REL_MAX_DIST / max_exact))
                         * np.float32(REL_BUCKETS - max_exact)).astype(np.int32)
    return np.where(n < max_exact, n, np.minimum(large, REL_BUCKETS - 1)).astype(np.int32)


def _bucket_starts():
    buckets = _bucket_np(np.arange(2 * REL_MAX_DIST))
    return [int(np.argmax(buckets >= b)) for b in range(REL_BUCKETS)]


def _bank_kernel(tab_ref, o_ref, *, t, d_min, band):
    h = pl.program_id(0)
    starts = _bucket_starts()
    row = lax.broadcasted_iota(jnp.int32, (t, t), 0)
    col = lax.broadcasted_iota(jnp.int32, (t, t), 1)
    for j in range(o_ref.shape[0]):
        d = d_min + j
        if d < 0:
            o_ref[j] = jnp.full((t, t), NEG, F32)
            continue
        dist = t * d + row - col
        b_lo, b_hi = (int(x) for x in _bucket_np(np.array([max(t * d - t + 1, 0), t * d + t - 1])))
        val = jnp.full((t, t), tab_ref[b_lo, h], F32)
        for b in range(b_lo + 1, b_hi + 1):
            val = jnp.where(dist >= starts[b], tab_ref[b, h], val)
        if band:
            mult = jnp.zeros((t, t), jnp.int32)
            for window, dil in DILATED_PATTERNS:
                mult += jnp.where((dist >= 0) & (dist <= window) & ((dist & (dil - 1)) == 0), 1, 0)
            val = val + jnp.where(mult == 3, math.log(3.0), jnp.where(mult == 2, math.log(2.0), 0.0))
            keep = mult > 0
        else:
            val = val - tab_ref[REL_BUCKETS - 1, h]
            keep = dist >= 0
        o_ref[j] = jnp.where(keep, val * LOG2E, NEG)


def _bank(tab, n_tables, t, d_min, band):
    nh = tab.shape[1]
    return pl.pallas_call(
        functools.partial(_bank_kernel, t=t, d_min=d_min, band=band),
        out_shape=jax.ShapeDtypeStruct((nh, n_tables, t, t), F32),
        grid=(nh,),
        in_specs=[pl.BlockSpec(memory_space=pltpu.SMEM)],
        out_specs=pl.BlockSpec((None, n_tables, t, t), lambda h: (h, 0, 0, 0)),
        compiler_params=_cparams("arbitrary"),
        name="bank_band" if band else "bank_bias",
    )(tab)


def _first_far_diagonal(t):
    last = _bucket_starts()[REL_BUCKETS - 1]
    return -(-(last + t - 1) // t)


def _qk(q, k):
    return lax.dot_general(q, k, (((1,), (1,)), ((), ())), preferred_element_type=F32)


def _flash_init(m_ref, acc_ref):
    m_ref[...] = jnp.full(m_ref.shape, NEG, F32)
    acc_ref[...] = jnp.zeros(acc_ref.shape, F32)


def _flash_update(slot, s, v, m_ref, acc_ref, s_ref, next_scores, diagonal=False):
    t = s.shape[0]
    v_ones = jnp.concatenate([v, jnp.ones_like(v)], axis=1)
    blocks = [(0, t // 2, t // 2), (t // 2, t, t)] if diagonal else [(0, t, t)]
    for n, (r0, r1, keys) in enumerate(blocks):
        m_prev = m_ref[slot, r0:r1]
        m_new = jnp.maximum(m_prev, jnp.max(s[r0:r1, :keys], axis=-1, keepdims=True))
        alpha = jnp.exp2(m_prev - m_new)
        p = jnp.concatenate([jnp.exp2(s[r0:r1, j * 128:(j + 1) * 128] - m_new) for j in range(keys // 128)],
                            axis=1).astype(BF16)
        if n == len(blocks) - 1 and next_scores is not None:
            s_ref[slot] = next_scores()
        acc_ref[slot, r0:r1] = (jnp.concatenate([alpha, alpha], axis=1) * acc_ref[slot, r0:r1]
                                + jnp.dot(p, v_ones[:keys], preferred_element_type=F32))
        m_ref[slot, r0:r1] = m_new


def _flash_result(slot, acc_ref):
    acc = acc_ref[slot]
    return acc[:, :HEAD_DIM] / acc[:, HEAD_DIM:]


def _silu(g):
    return g * jax.nn.sigmoid(g)


def _tile(ref, kj, t, c0, width):
    start = kj * t if isinstance(kj, int) else pl.multiple_of(kj * t, t)
    return ref[pl.ds(start, t), c0:c0 + width]


def _bank_tile(bank_ref, h, i, kj):
    base = 2 * (i - kj) + 1
    top = jnp.concatenate([bank_ref[h, base], bank_ref[h, base - 1]], axis=1)
    bot = jnp.concatenate([bank_ref[h, base + 1], bank_ref[h, base]], axis=1)
    return jnp.concatenate([top, bot], axis=0)


def _emit(o_ref, g_ref, h, o):
    gate = g_ref[:, h * 128:(h + 1) * 128].astype(F32)
    o_ref[:, h * 128:(h + 1) * 128] = (o * _silu(gate)).astype(BF16)


def _flash_walk(i, first_key, n_far, slots, score, logits, value, s_ref, m_ref, acc_ref, prep=None):
    lo = first_key(i)
    i_next = jnp.minimum(i + 1, pl.num_programs(1) - 1)
    _flash_init(m_ref, acc_ref)

    @pl.when(i == 0)
    def _():
        for slot in range(slots):
            s_ref[slot] = score(slot, False, lo)

    def step(kj, phase):
        ctx = prep(kj) if prep is not None else None
        for slot in range(slots):
            s = logits(slot, s_ref[slot], kj, phase, ctx)
            if phase == "last":
                nxt = functools.partial(score, slot, True, first_key(i_next))
            else:
                nxt = functools.partial(score, slot, False, kj + 1)
            _flash_update(slot, s, value(slot, kj), m_ref, acc_ref, s_ref, nxt, diagonal=phase == "last")

    def walk(a, b, phase):
        def body(kj, carry):
            step(kj, phase)
            return carry
        lax.fori_loop(a, b, body, 0)

    if n_far is not None:
        walk(lo, n_far, "far")
        lo = n_far
    walk(lo, i, "near")
    step(i, "last")


def _attn_a_kernel(q_ref, qn_ref, k_ref, v_ref, g_ref, o_ref, s_ref, m_ref, acc_ref, *, t):
    i = pl.program_id(1)

    def score(h, next_q, kj):
        q = (qn_ref if next_q else q_ref)[:, h * 256:(h + 1) * 256]
        return _qk(q, _tile(k_ref, kj, t, h * 256, 256))

    def logits(h, s, kj, phase, ctx):
        if phase != "last":
            return s
        causal = lax.broadcasted_iota(jnp.int32, (t, t), 1) <= lax.broadcasted_iota(jnp.int32, (t, t), 0)
        return jnp.where(causal, s, NEG)

    def value(h, kj):
        return _tile(v_ref, kj, t, h * 128, 128)

    _flash_walk(i, lambda qi: 0, None, N_HEADS, score, logits, value, s_ref, m_ref, acc_ref)
    for h in range(N_HEADS):
        _emit(o_ref, g_ref, h, _flash_result(h, acc_ref))


def _attn_band_kernel(q_ref, qn_ref, k_ref, v_ref, g_ref, bank_ref, o_ref, s_ref, m_ref, acc_ref, *, t, near):
    i = pl.program_id(1)

    def score(h, next_q, kj):
        q = (qn_ref if next_q else q_ref)[:, h * 128:(h + 1) * 128]
        return _qk(q, _tile(k_ref, kj, t, h * 128, 128))

    def logits(h, s, kj, phase, ctx):
        return s + _bank_tile(bank_ref, h, i, kj)

    def value(h, kj):
        return _tile(v_ref, kj, t, h * 128, 128)

    _flash_walk(i, lambda qi: jnp.maximum(qi - (near - 1), 0), None, N_HEADS, score, logits, value,
                s_ref, m_ref, acc_ref)
    for h in range(N_HEADS):
        _emit(o_ref, g_ref, h, _flash_result(h, acc_ref))


def _attn_sel_kernel(q_ref, qn_ref, k_ref, v_ref, g_ref, bank_ref, sel_ref, o_ref,
                     s_ref, m_ref, acc_ref, *, t, near):
    i = pl.program_id(1)

    def score(h, next_q, kj):
        q = (qn_ref if next_q else q_ref)[:, h * 128:(h + 1) * 128]
        return _qk(q, _tile(k_ref, kj, t, h * 128, 128))

    def prep(kj):
        return sel_ref[:, pl.ds(pl.multiple_of(kj * t, t), t)].astype(F32)

    def logits(h, s, kj, phase, sel):
        return s + sel if phase == "far" else s + (sel + _bank_tile(bank_ref, h, i, kj))

    def value(h, kj):
        return _tile(v_ref, kj, t, h * 128, 128)

    _flash_walk(i, lambda qi: 0, jnp.maximum(i - (near - 1), 0), N_HEADS, score, logits, value,
                s_ref, m_ref, acc_ref, prep)
    for h in range(N_HEADS):
        _emit(o_ref, g_ref, h, _flash_result(h, acc_ref))


def _attn_diff_kernel(q_ref, qn_ref, k_ref, v_ref, g_ref, bank_ref, lam_ref, gsub_ref, o_ref,
                      s_ref, m_ref, acc_ref, *, t, near, lam_init):
    i = pl.program_id(1)
    first_half = lax.broadcasted_iota(jnp.int32, (t, HEAD_DIM), 1) < HEAD_DIM // 2

    def score(slot, next_q, kj):
        h = slot // 2
        q = (qn_ref if next_q else q_ref)[:, h * 128:(h + 1) * 128]
        keep = first_half if slot % 2 == 0 else jnp.logical_not(first_half)
        q = jnp.where(keep, q, jnp.zeros_like(q))
        return _qk(q, _tile(k_ref, kj, t, h * 128, 128))

    def logits(slot, s, kj, phase, ctx):
        return s if phase == "far" else s + _bank_tile(bank_ref, slot // 2, i, kj)

    def value(slot, kj):
        return _tile(v_ref, kj, t, (slot // 2) * 128, 128)

    _flash_walk(i, lambda qi: 0, jnp.maximum(i - (near - 1), 0), 2 * N_HEADS, score, logits, value,
                s_ref, m_ref, acc_ref)
    lam_v = lam_ref[...]
    lam = (jnp.exp(jnp.sum(lam_v[0:1] * lam_v[1:2], axis=-1, keepdims=True))
           - jnp.exp(jnp.sum(lam_v[2:3] * lam_v[3:4], axis=-1, keepdims=True)) + lam_init)
    for h in range(N_HEADS):
        o = _flash_result(2 * h, acc_ref) - lam * _flash_result(2 * h + 1, acc_ref)
        _emit(o_ref, g_ref, h, _rms(o, gsub_ref[...]) * (1.0 - lam_init))


def _attention(kind, q_arr, k_arr, v_arr, proj, q_blk, gate_blk, extra_in=(), extra_specs=(), **kw):
    b, s, _ = proj.shape
    t = min(ATT_TILE, s)
    dk = 256 if kind == "a" else 128
    qw = N_HEADS * dk
    k_blk = 0 if kind == "a" else q_blk + 1
    v_blk = 0 if kind == "a" else q_blk + 2
    body = {"a": _attn_a_kernel, "band": _attn_band_kernel, "sel": _attn_sel_kernel,
            "diff": _attn_diff_kernel}[kind]
    slots = 2 * N_HEADS if kind == "diff" else N_HEADS
    scratch = [pltpu.VMEM((slots, t, t), F32), pltpu.VMEM((slots, t, HEAD_DIM), F32),
               pltpu.VMEM((slots, t, 2 * HEAD_DIM), F32)]
    last = s // t - 1
    in_specs = [pl.BlockSpec((None, t, qw), lambda bi, i: (bi, i, q_blk)),
                pl.BlockSpec((None, t, qw), lambda bi, i: (bi, jnp.minimum(i + 1, last), q_blk)),
                pl.BlockSpec((None, s, qw), lambda bi, i: (bi, 0, k_blk)),
                pl.BlockSpec((None, s, BRANCH_WIDTH), lambda bi, i: (bi, 0, v_blk)),
                pl.BlockSpec((None, t, BRANCH_WIDTH), lambda bi, i: (bi, i, gate_blk))]
    in_specs += list(extra_specs)
    return pl.pallas_call(
        functools.partial(body, t=t, **kw),
        out_shape=jax.ShapeDtypeStruct((b, s, BRANCH_WIDTH), BF16),
        grid=(b, s // t),
        in_specs=in_specs,
        out_specs=pl.BlockSpec((None, t, BRANCH_WIDTH), lambda bi, i: (bi, i, 0)),
        scratch_shapes=scratch,
        compiler_params=_cparams("parallel", "arbitrary"),
        name="attn_" + kind,
    )(q_arr, q_arr, k_arr, v_arr, proj, *extra_in)


def _bit_transpose32(words):
    a = list(words)
    j, m = 16, 0x0000FFFF
    while j:
        for k in range(32):
            if not k & j:
                t = (a[k] ^ lax.shift_right_logical(a[k + j], jnp.int32(j))) & jnp.int32(m)
                a[k] = a[k] ^ t
                a[k + j] = a[k + j] ^ (t << j)
        j >>= 1
        m = (m ^ (m << j)) & 0xFFFFFFFF if j else m
    return a


def _select_kernel(qi_ref, ki_ref, wi_ref, o_ref, key_ref, plane_ref, alive_ref, stat_ref, *, tq, kc, n_sel):
    i = pl.program_id(1)
    s_len = o_ref.shape[1]
    n_ch = (i * tq + tq + kc - 1) // kc
    wpc = kc // 32
    int_min = jnp.int32(-2 ** 31)
    lane = lax.broadcasted_iota(jnp.int32, (tq, 128), 1)
    w_t = wi_ref[...].astype(F32).T[WIDX_LANE:WIDX_LANE + IDX_HEADS]
    q_heads = []
    for j in range(IDX_HEADS // 2):
        q2 = qi_ref[:, j * 128:(j + 1) * 128]
        q_heads.append(jnp.where(lane < IDX_DIM, q2, jnp.zeros_like(q2)))
        q_heads.append(jnp.where(lane >= IDX_DIM, q2, jnp.zeros_like(q2)))
    kiota = lax.broadcasted_iota(jnp.int32, (kc, tq), 0)

    def chunk(c):
        return pl.ds(pl.multiple_of(c * kc, kc), kc)

    def scored(k, q_from):
        acc = jnp.zeros((k.shape[0], tq - q_from), F32)
        for hh in range(IDX_HEADS):
            acc = acc + jnp.maximum(_qk(k, q_heads[hh][q_from:, :]), 0.0) * w_t[hh:hh + 1, q_from:]
        return acc + 0.0

    def causal(acc):
        key = lax.broadcasted_iota(jnp.int32, acc.shape, 0)
        return jnp.where(key <= lax.broadcasted_iota(jnp.int32, acc.shape, 1), acc, NEG)

    def store_keys(row0, acc):
        bits = pltpu.bitcast(acc, jnp.int32)
        keys = bits ^ ((bits >> 31) & jnp.int32(0x7FFFFFFF))
        key_ref[pl.ds(row0, acc.shape[0]), :] = keys
        ukeys = keys ^ int_min
        for blk in range(acc.shape[0] // 256):
            planes = _bit_transpose32([ukeys[blk * 256 + 8 * j:blk * 256 + 8 * j + 8, :] for j in range(32)])
            row = pl.multiple_of(row0 // 32 + blk * 8, 8)
            for b in range(32):
                plane_ref[b, pl.ds(row, 8), :] = planes[b]

    def score_chunk(c, _):
        store_keys(pl.multiple_of(c * kc, kc), scored(ki_ref[chunk(c), :], 0))
        return 0

    lax.fori_loop(0, i, score_chunk, 0)
    diag = pl.multiple_of(i * kc, kc)
    half = kc // 2
    if half % 256 == 0:
        store_keys(diag, causal(scored(ki_ref[pl.ds(diag, half), :], 0)))
        late = causal(scored(ki_ref[pl.ds(diag + half, half), :], half))
        store_keys(diag + half, jnp.concatenate([jnp.full((half, half), NEG, F32), late], axis=1))
    else:
        store_keys(diag, causal(scored(ki_ref[pl.ds(diag, kc), :], 0)))

    def clear_chunk(c, _):
        plane_ref[:, pl.ds(pl.multiple_of(c * wpc, wpc), wpc), :] = jnp.zeros((32, wpc, tq), jnp.int32)
        return 0

    @pl.when(i == 0)
    def _():
        lax.fori_loop(n_ch, s_len // kc, clear_chunk, 0)

    n_rows = s_len // 32
    word_row = lax.broadcasted_iota(jnp.int32, (n_rows, tq), 0)
    alive_ref[...] = jnp.where(word_row < n_ch * wpc, jnp.int32(-1), jnp.int32(0))
    zeros = jnp.zeros((1, tq), jnp.int32)

    def radix(rows):
        def radix_step(bi, carry):
            thr, above = carry
            alive = alive_ref[0:rows]
            plane = plane_ref[bi, 0:rows]
            ones = lax.population_count(alive & plane)
            ones = jnp.sum(jnp.sum(ones.reshape(rows // 8, 8, tq), axis=0), axis=0, keepdims=True)
            take = above + ones >= n_sel
            thr = jnp.where(take, thr | (jnp.int32(1) << (31 - bi)), thr)
            above = jnp.where(take, above, above + ones)
            alive_ref[0:rows] = alive & (plane ^ jnp.where(take, jnp.int32(0), jnp.int32(-1)))
            return thr, above

        thr, above = lax.fori_loop(0, 32, radix_step, (zeros, zeros))
        stat_ref[0:8] = jnp.broadcast_to(thr, (8, tq))
        stat_ref[8:16] = jnp.broadcast_to(above, (8, tq))

    quarter = max(n_rows // 4, 8)
    sizes = sorted({min(quarter * (n + 1), n_rows) for n in range(4)})
    for n, rows in enumerate(sizes):
        lower = sizes[n - 1] if n else 0
        pl.when((n_ch * wpc > lower) & (n_ch * wpc <= rows))(functools.partial(radix, rows))
    thr_u, above = stat_ref[0:1], stat_ref[8:9]
    thr = thr_u ^ int_min
    n_equal = lax.population_count(alive_ref[...])
    n_equal = jnp.sum(jnp.sum(n_equal.reshape(n_rows // 8, 8, tq), axis=0), axis=0, keepdims=True)
    need = n_sel - above
    masked_key = int(np.float32(NEG).view(np.int32)) ^ 0x7FFFFFFF
    tie = (n_equal > need) & (thr != masked_key)
    any_tie = jnp.max(jnp.where(tie, 1, 0)) > 0

    @pl.when(jnp.logical_not(any_tie))
    def _():
        def emit(c, _):
            keep = jnp.where(key_ref[chunk(c), :] >= thr, 0.0, NEG)
            o_ref[:, chunk(c)] = keep.T.astype(BF16)
            return 0

        lax.fori_loop(0, n_ch, emit, 0)

    @pl.when(any_tie)
    def _():
        def equal_below(bound):
            def body(c, part):
                hit = jnp.where((key_ref[chunk(c), :] == thr) & (c * kc + kiota < bound), 1, 0)
                return part + jnp.sum(hit.reshape(kc // 8, 8, tq), axis=0)
            part = lax.fori_loop(0, n_ch, body, jnp.zeros((8, tq), jnp.int32))
            return jnp.sum(part, axis=0, keepdims=True)

        n_bits = s_len.bit_length()

        def bound_step(bi, cut):
            cand = cut + (jnp.int32(1) << (n_bits - 1 - bi))
            ok = (cand <= s_len) & (equal_below(cand) <= need)
            return jnp.where(ok, cand, cut)

        cut = lax.fori_loop(0, n_bits, bound_step, zeros)

        def emit(c, _):
            keys = key_ref[chunk(c), :]
            kept = (keys > thr) | ((keys == thr) & (c * kc + kiota < cut))
            o_ref[:, chunk(c)] = jnp.where(kept, 0.0, NEG).T.astype(BF16)
            return 0

        lax.fori_loop(0, n_ch, emit, 0)

    def blank(c, _):
        o_ref[:, chunk(c)] = jnp.full((tq, kc), NEG, BF16)
        return 0

    lax.fori_loop(n_ch, s_len // kc, blank, 0)


def _select(proj):
    b, s, _ = proj.shape
    tq = kc = min(512, s)
    n_sel = min(TOPK_MAX, s // 4)
    return pl.pallas_call(
        functools.partial(_select_kernel, tq=tq, kc=kc, n_sel=n_sel),
        out_shape=jax.ShapeDtypeStruct((b, s, s), BF16),
        grid=(b, s // tq),
        in_specs=[pl.BlockSpec((None, tq, 1024), lambda bi, i: (bi, i, OFF_QIDX // 1024)),
                  pl.BlockSpec((None, s, 128), lambda bi, i: (bi, 0, OFF_KIDX // 128)),
                  pl.BlockSpec((None, tq, 128), lambda bi, i: (bi, i, OFF_WIDX // 128))],
        out_specs=pl.BlockSpec((None, tq, s), lambda bi, i: (bi, i, 0)),
        scratch_shapes=[pltpu.VMEM((s, tq), jnp.int32), pltpu.VMEM((32, s // 32, tq), jnp.int32),
                        pltpu.VMEM((s // 32, tq), jnp.int32), pltpu.VMEM((16, tq), jnp.int32)],
        compiler_params=_cparams("parallel", "arbitrary"),
        name="idx_select",
    )(proj, proj, proj)


def _out_kernel(a_ref, b_ref, c_ref, d_ref, w_ref, x_ref, gate_ref, g_ref, o_ref, wb_ref):
    @pl.when((pl.program_id(0) == 0) & (pl.program_id(1) == 0))
    def _():
        wb_ref[...] = w_ref[...].astype(BF16)

    y = jnp.dot(a_ref[...], wb_ref[0:512, :], preferred_element_type=F32)
    y += jnp.dot(b_ref[...], wb_ref[512:1024, :], preferred_element_type=F32)
    y += jnp.dot(c_ref[...], wb_ref[1024:1536, :], preferred_element_type=F32)
    y += jnp.dot(d_ref[...], wb_ref[1536:2048, :], preferred_element_type=F32)
    o_ref[...] = x_ref[...] + gate_ref[...] * _rms(y, g_ref[...])


def _out_proj(outs, w_out, li, x, mod3, g_post):
    b, s, d = x.shape
    tm = min(512, s)
    mix = lambda bi, i: (bi, i, 0)
    return pl.pallas_call(
        _out_kernel,
        out_shape=jax.ShapeDtypeStruct((b, s, d), F32),
        grid=(b, s // tm),
        in_specs=[pl.BlockSpec((None, tm, BRANCH_WIDTH), mix)] * 4
        + [pl.BlockSpec((None,) + w_out.shape[1:], lambda bi, i: (li, 0, 0), pipeline_mode=pl.Buffered(1)),
           pl.BlockSpec((None, tm, d), mix),
           pl.BlockSpec((None, 1, d), lambda bi, i: (bi, 0, 2)),
           pl.BlockSpec((1, d), lambda bi, i: (0, 0))],
        out_specs=pl.BlockSpec((None, tm, d), mix),
        scratch_shapes=[pltpu.VMEM(w_out.shape[1:], BF16)],
        compiler_params=_cparams("arbitrary", "arbitrary"),
        name="out_proj",
    )(*outs, w_out, x, mod3, g_post.reshape(1, d))


def _rope_tables(s):
    half = MLA_ROPE // 2
    inv = ROPE_THETA ** (-jnp.arange(half, dtype=F32) / half)
    ang = jnp.arange(s, dtype=F32)[:, None] * inv[None, :]
    z = jnp.zeros((s, 128 - MLA_ROPE), F32)
    cos, sin = jnp.cos(ang), jnp.sin(ang)
    cat = lambda *parts: jnp.concatenate(parts, axis=-1)
    return cat(cos, cos, z), cat(sin, sin, z), cat(cos, cos, sin, sin)


def _rot_cols(w):
    half = w.shape[-1] // 2
    return jnp.concatenate([-w[..., half:], w[..., :half]], axis=-1)


IN_SPLITS = (("a_cq", 384), ("a_ckv", 256), ("a_krope", 64), ("b_q", 512), ("b_k", 512), ("b_v", 512),
             ("c_q", 512), ("c_k", 512), ("c_v", 512), ("c_qidx", 1024), ("c_kidx", 64), ("c_widx", 16),
             ("d_q", 512), ("d_k", 512), ("d_v", 512), ("gate", 2048))
IN_WIDTH = sum(width for _, width in IN_SPLITS)


def _layout_w_in_kernel(w_ref, o_ref):
    src, start = {}, 0
    for name, width in IN_SPLITS:
        src[name] = start
        start += width
    tk = w_ref.shape[1]

    def rows(name, width, offset=0):
        a = src[name] + offset
        return w_ref[a:a + width, :]

    def put(dst, val, scale=None):
        for r in range(0, val.shape[0], 512):
            piece = val[r:r + 512]
            if scale is not None:
                piece = piece * scale
            o_ref[:, dst + r:dst + r + piece.shape[0]] = piece.T.astype(BF16)

    z64 = jnp.zeros((64, tk), F32)
    half = MLA_ROPE // 2
    put(OFF_QIDX, rows("c_qidx", 1024), IDX_DIM ** -0.5)
    put(OFF_A, rows("a_cq", MLA_Q_RANK + MLA_KV_RANK))
    put(OFF_A + 640, jnp.concatenate(
        [rows("a_krope", MLA_ROPE), rows("c_widx", IDX_HEADS) * IDX_HEADS ** -0.5,
         jnp.zeros((128 - MLA_ROPE - IDX_HEADS, tk), F32),
         -rows("a_krope", half, half), rows("a_krope", half), z64,
         rows("c_kidx", IDX_DIM), rows("c_kidx", IDX_DIM)], axis=0))
    for off, name, dim in ((OFF_B, "b", HEAD_DIM), (OFF_C, "c", HEAD_DIM), (OFF_D, "d", HEAD_DIM // 2)):
        put(off, rows(name + "_q", BRANCH_WIDTH), LOG2E * dim ** -0.5)
        put(off + BRANCH_WIDTH, rows(name + "_k", 2 * BRANCH_WIDTH))
    put(OFF_GATE, rows("gate", MIX_WIDTH))


def _layout_w_in(w_in_t, li):
    d = w_in_t.shape[2]
    tk = 256
    return pl.pallas_call(
        _layout_w_in_kernel,
        out_shape=jax.ShapeDtypeStruct((d, PROJ_WIDTH), BF16),
        grid=(d // tk,),
        in_specs=[pl.BlockSpec((None, IN_WIDTH, tk), lambda i: (li, 0, i))],
        out_specs=pl.BlockSpec((tk, PROJ_WIDTH), lambda i: (i, 0)),
        compiler_params=_cparams("arbitrary"),
        name="layout_w_in",
    )(w_in_t)


def _layout_w_uq(w):
    r = w.shape[0]
    w = w.reshape(r, N_HEADS, MLA_NOPE + MLA_ROPE) * (LOG2E * (MLA_NOPE + MLA_ROPE) ** -0.5)
    rope = w[..., MLA_NOPE:]
    return jnp.concatenate([w[..., :MLA_NOPE], rope, _rot_cols(rope)], axis=-1).reshape(r, -1).astype(BF16)


def kernel(x, c, w_ada, b_ada, g_pre, g_post, w_in, g_q_a, w_uq_a, g_kv_a, w_ukv_a,
           lam_q1, lam_k1, lam_q2, lam_k2, g_sub_d, w_out, rel_bias):
    b, s, d = x.shape
    depth = w_ada.shape[0]
    t = min(ATT_TILE, s)
    half = t // 2
    nq = s // t
    near_bias = min(nq, -(-(_first_far_diagonal(half) + 1) // 2))
    near_band = min(nq, -(-(DILATED_PATTERNS[-1][0] // half + 1) // 2))

    rope_tabs = _rope_tables(s)
    bank_b = _bank(rel_bias[:, 0:N_HEADS], 2 * near_band + 1, half, -1, True)
    bank_cd = _bank(rel_bias[:, N_HEADS:3 * N_HEADS], 2 * near_bias + 1, half, -1, False)
    bank_spec = lambda n, group=0: pl.BlockSpec((N_HEADS, n, half, half), lambda bi, i: (group, 0, 0, 0),
                                                pipeline_mode=pl.Buffered(1))

    w_in_t = jnp.swapaxes(w_in, 1, 2)
    mod = _ada_mod(c, w_ada, b_ada)
    for li in range(depth):
        mod3 = mod[li].reshape(b, 1, 3 * d)
        proj = _in_proj(x, g_pre[li], mod3, _layout_w_in(w_in_t, li))

        q_a, k_a, v_a = _mla_prep(proj, rope_tabs, g_q_a[li], g_kv_a[li],
                                  _layout_w_uq(w_uq_a[li]), w_ukv_a[li].astype(BF16))
        gate0 = OFF_GATE // BRANCH_WIDTH
        out_a = _attention("a", q_a, k_a, v_a, proj, 0, gate0)
        out_b = _attention("band", proj, proj, proj, proj, OFF_B // BRANCH_WIDTH, gate0 + 1,
                           extra_in=(bank_b,), extra_specs=(bank_spec(2 * near_band + 1),), near=near_band)
        sel = _select(proj)
        out_c = _attention("sel", proj, proj, proj, proj, OFF_C // BRANCH_WIDTH, gate0 + 2,
                           extra_in=(bank_cd, sel),
                           extra_specs=(bank_spec(2 * near_bias + 1, 0),
                                        pl.BlockSpec((None, t, s), lambda bi, i: (bi, i, 0))),
                           near=near_bias)
        lam_init = 0.8 - 0.6 * math.exp(-0.3 * li)
        lam_vecs = jnp.stack([lam_q1[li], lam_k1[li], lam_q2[li], lam_k2[li]])
        out_d = _attention("diff", proj, proj, proj, proj, OFF_D // BRANCH_WIDTH, gate0 + 3,
                           extra_in=(bank_cd, lam_vecs, g_sub_d[li].reshape(1, HEAD_DIM)),
                           extra_specs=(bank_spec(2 * near_bias + 1, 1),
                                        pl.BlockSpec(lam_vecs.shape, lambda bi, i: (0, 0)),
                                        pl.BlockSpec((1, HEAD_DIM), lambda bi, i: (0, 0))),
                           near=near_bias, lam_init=lam_init)
        x = _out_proj((out_a, out_b, out_c, out_d), w_out, li, x, mod3, g_post[li])
    return x
```
